```python
import math
import jax, jax.numpy as jnp
from jax import lax
import numpy as np

D_MODEL = 1024
BATCH = 8
SEQ = 2048
DEPTH = 2

D_FF = 2816
EPS = 1e-6
FOURIER_GROUPS = 8
FOURIER_GROUP_DIM = 128
FOURIER_WIDTH = FOURIER_GROUPS * FOURIER_GROUP_DIM
DIL_PAIRS = ((128, 1), (512, 4), (2048, 16))
DIL_HEADS_PER_GROUP = 2
DIL_HEAD_DIM = 64
DIL_HEADS = DIL_HEADS_PER_GROUP * len(DIL_PAIRS)
DIL_WIDTH = DIL_HEADS * DIL_HEAD_DIM
DIL_OUT_WIDTH = DIL_HEADS_PER_GROUP * DIL_HEAD_DIM
DIFF_HEADS = 4
DIFF_HEAD_DIM = 64
DIFF_QK_WIDTH = DIFF_HEADS * 2 * DIFF_HEAD_DIM
DIFF_V_DIM = 2 * DIFF_HEAD_DIM
DIFF_V_WIDTH = DIFF_HEADS * DIFF_V_DIM
SUBLN_EPS = 1e-5
Q_BLOCK = 128
IN_WIDTH = FOURIER_WIDTH + 3 * DIL_WIDTH + 2 * DIFF_QK_WIDTH + DIFF_V_WIDTH
N_BRANCHES = 3
NUM_BUCKETS = 32
MAX_DISTANCE = 1024
N_BIAS_HEADS = DIL_HEADS + DIFF_HEADS
NEG_INF = -1e30

kernel_name = "hybrid_gated_fourier_dilated_diff_encoder"


def rmsnorm(x, g, eps=EPS):
    xf = x.astype(jnp.float32)
    y = xf * lax.rsqrt(jnp.mean(xf * xf, axis=-1, keepdims=True) + eps)
    return (y * g.astype(jnp.float32)).astype(x.dtype)


def swiglu(h, w_gate, w_up, w_down):
    return (jax.nn.silu(h @ w_gate) * (h @ w_up)) @ w_down


def rel_bucket(rel):
    half = NUM_BUCKETS // 2
    max_exact = half // 2
    ret = jnp.where(rel > 0, half, 0)
    n = jnp.abs(rel)
    nf = jnp.maximum(n, 1).astype(jnp.float32)
    large = max_exact + (jnp.log(nf / max_exact) / math.log(MAX_DISTANCE / max_exact)
                         * (half - max_exact)).astype(jnp.int32)
    large = jnp.minimum(large, half - 1)
    return ret + jnp.where(n < max_exact, n, large)


def fourier_mix(a):
    Bsz, S, _ = a.shape
    ag = a.astype(jnp.float32).reshape(Bsz, S, FOURIER_GROUPS, FOURIER_GROUP_DIM)
    y = jnp.fft.fftn(ag, axes=(1, 3), norm="ortho").real
    return y.reshape(Bsz, S, FOURIER_WIDTH).astype(a.dtype)


def dilated_window_attention(q, k, v, bias_cols, window, dil):
    Bsz, S, H, dh = q.shape
    radius = window // (2 * dil)
    blk = radius
    L = S // dil
    nb = -(-L // blk)
    Lp = nb * blk

    def to_sub(t):
        t = t.reshape(Bsz, L, dil, H, dh).transpose(0, 2, 3, 1, 4)
        return jnp.pad(t, ((0, 0), (0, 0), (0, 0), (0, Lp - L), (0, 0)))

    def windows(t):
        tp = jnp.pad(to_sub(t), ((0, 0), (0, 0), (0, 0), (blk, blk), (0, 0)))
        tp = tp.reshape(Bsz, dil, H, nb + 2, blk, dh)
        return jnp.concatenate([tp[:, :, :, :-2], tp[:, :, :, 1:-1], tp[:, :, :, 2:]], axis=4)

    qs = to_sub(q).reshape(Bsz, dil, H, nb, blk, dh).astype(jnp.float32)
    kw = windows(k).astype(jnp.float32)
    vw = windows(v).astype(jnp.float32)

    i = jnp.arange(blk)[:, None]
    m = jnp.arange(3 * blk)[None, :]
    rel_t = m - blk - i
    bias = bias_cols[rel_bucket(rel_t * dil)].astype(jnp.float32)
    bias = jnp.transpose(bias, (2, 0, 1))
    tk = (jnp.arange(nb)[:, None, None] - 1) * blk + m[None]
    valid = (jnp.abs(rel_t) <= radius)[None] & (tk >= 0) & (tk < L)

    logits = jnp.einsum('bdhjqc,bdhjkc->bdhjqk', qs, kw) * (dh ** -0.5)
    logits = logits + bias[None, None, :, None]
    logits = jnp.where(valid, logits, NEG_INF)
    lse = jax.nn.logsumexp(logits, axis=-1)
    p = jnp.exp(logits - lse[..., None])
    o = jnp.einsum('bdhjqk,bdhjkc->bdhjqc', p, vw)

    o = o.reshape(Bsz, dil, H, Lp, dh)[:, :, :, :L].transpose(0, 3, 1, 2, 4).reshape(Bsz, S, H, dh)
    lse = lse.reshape(Bsz, dil, H, Lp)[:, :, :, :L].transpose(0, 3, 1, 2).reshape(Bsz, S, H)
    return o, lse


def dilated_mixture(qb, kb, vb, rel_bias):
    Bsz, S, _ = qb.shape
    q = qb.reshape(Bsz, S, DIL_HEADS, DIL_HEAD_DIM)
    k = kb.reshape(Bsz, S, DIL_HEADS, DIL_HEAD_DIM)
    v = vb.reshape(Bsz, S, DIL_HEADS, DIL_HEAD_DIM)
    outs, lses = [], []
    for g, (window, dil) in enumerate(DIL_PAIRS):
        hs = slice(g * DIL_HEADS_PER_GROUP, (g + 1) * DIL_HEADS_PER_GROUP)
        o, lse = dilated_window_attention(q[:, :, hs], k[:, :, hs], v[:, :, hs],
                                          rel_bias[:, hs], window, dil)
        outs.append(o)
        lses.append(lse)
    o = jnp.stack(outs, axis=0)
    w = jax.nn.softmax(jnp.stack(lses, axis=0), axis=0)
    y = jnp.sum(w[..., None] * o, axis=0)
    return y.reshape(Bsz, S, DIL_OUT_WIDTH).astype(qb.dtype)


def diff_attention(qc, kc, vc, bias_cols, lam, subln_g, lam_init):
    Bsz, S, _ = qc.shape
    H, dh = DIFF_HEADS, DIFF_HEAD_DIM
    nq = S // Q_BLOCK
    qb = qc.reshape(Bsz, nq, Q_BLOCK, H, 2, dh).transpose(1, 0, 2, 3, 4, 5).astype(jnp.float32)
    kf = kc.reshape(Bsz, S, H, 2, dh).astype(jnp.float32)
    vf = vc.reshape(Bsz, S, H, DIFF_V_DIM).astype(jnp.float32)
    kpos = jnp.arange(S)
    scale = dh ** -0.5

    def one_block(args):
        j, qj = args
        qpos = j * Q_BLOCK + jnp.arange(Q_BLOCK)
        bias = bias_cols[rel_bucket(kpos[None, :] - qpos[:, None])].astype(jnp.float32)
        bias = jnp.transpose(bias, (2, 0, 1))
        logits = jnp.einsum('bqhmc,bkhmc->bhmqk', qj, kf) * scale + bias[None, :, None]
        p = jax.nn.softmax(logits, axis=-1)
        a = p[:, :, 0] - lam * p[:, :, 1]
        return jnp.einsum('bhqk,bkhe->bqhe', a, vf)

    o = lax.map(one_block, (jnp.arange(nq), qb))
    o = o.transpose(1, 0, 2, 3, 4).reshape(Bsz, S, H, DIFF_V_DIM)
    o = rmsnorm(o, subln_g, SUBLN_EPS) * (1.0 - lam_init)
    return o.reshape(Bsz, S, DIFF_V_WIDTH).astype(qc.dtype)


def setup_inputs(seed: int = 0) -> dict:
    key = jax.random.key(seed)
    ks = jax.random.split(key, 26)

    def nrm(k, shape, fan_in):
        return jax.random.normal(k, shape, jnp.float32) * (fan_in ** -0.5)

    def gain(k, shape):
        return 1.0 + 0.05 * jax.random.normal(k, shape, jnp.float32)

    L = DEPTH
    return {
        "x": jax.random.normal(ks[0], (BATCH, SEQ, D_MODEL), jnp.float32),
        "g_ffn1": gain(ks[1], (L, D_MODEL)),
        "w_ffn1_gate": nrm(ks[2], (L, D_MODEL, D_FF), D_MODEL),
        "w_ffn1_up": nrm(ks[3], (L, D_MODEL, D_FF), D_MODEL),
        "w_ffn1_down": nrm(ks[4], (L, D_FF, D_MODEL), D_FF),
        "g_mix": gain(ks[5], (L, D_MODEL)),
        "w_in": nrm(ks[6], (L, D_MODEL, IN_WIDTH), D_MODEL),
        "w_gate": nrm(ks[7], (L, D_MODEL, N_BRANCHES * D_MODEL), D_MODEL),
        "b_gate": 0.02 * jax.random.normal(ks[8], (L, N_BRANCHES * D_MODEL), jnp.float32),
        "w_br_a": nrm(ks[9], (L, FOURIER_WIDTH, D_MODEL), FOURIER_WIDTH),
        "w_br_b": nrm(ks[10], (L, DIL_OUT_WIDTH, D_MODEL), DIL_OUT_WIDTH),
        "w_br_c": nrm(ks[11], (L, DIFF_V_WIDTH, D_MODEL), DIFF_V_WIDTH),
        "w_out": nrm(ks[12], (L, D_MODEL, D_MODEL), D_MODEL),
        "lam_q1": 0.1 * jax.random.normal(ks[13], (L, DIFF_HEAD_DIM), jnp.float32),
        "lam_k1": 0.1 * jax.random.normal(ks[14], (L, DIFF_HEAD_DIM), jnp.float32),
        "lam_q2": 0.1 * jax.random.normal(ks[15], (L, DIFF_HEAD_DIM), jnp.float32),
        "lam_k2": 0.1 * jax.random.normal(ks[16], (L, DIFF_HEAD_DIM), jnp.float32),
        "subln_g": gain(ks[17], (L, DIFF_V_DIM)),
        "rel_bias": 0.5 * jax.random.normal(ks[18], (NUM_BUCKETS, N_BIAS_HEADS), jnp.float32),
        "g_ffn2": gain(ks[19], (L, D_MODEL)),
        "w_ffn2_gate": nrm(ks[20], (L, D_MODEL, D_FF), D_MODEL),
        "w_ffn2_up": nrm(ks[21], (L, D_MODEL, D_FF), D_MODEL),
        "w_ffn2_down": nrm(ks[22], (L, D_FF, D_MODEL), D_FF),
        "g_final": gain(ks[23], (D_MODEL,)),
    }


def reference(x, g_ffn1, w_ffn1_gate, w_ffn1_up, w_ffn1_down, g_mix, w_in, w_gate, b_gate,
              w_br_a, w_br_b, w_br_c, w_out, lam_q1, lam_k1, lam_q2, lam_k2, subln_g,
              rel_bias, g_ffn2, w_ffn2_gate, w_ffn2_up, w_ffn2_down, g_final):
    Bsz, S, D = x.shape
    c0 = FOURIER_WIDTH
    c1 = c0 + DIL_WIDTH
    c2 = c1 + DIL_WIDTH
    c3 = c2 + DIL_WIDTH
    c4 = c3 + DIFF_QK_WIDTH
    c5 = c4 + DIFF_QK_WIDTH
    for l in range(DEPTH):
        x = x + 0.5 * swiglu(rmsnorm(x, g_ffn1[l]), w_ffn1_gate[l], w_ffn1_up[l], w_ffn1_down[l])

        u = rmsnorm(x, g_mix[l])
        p = u @ w_in[l]
        a_in = p[..., :c0]
        qb, kb, vb = p[..., c0:c1], p[..., c1:c2], p[..., c2:c3]
        qc, kc, vc = p[..., c3:c4], p[..., c4:c5], p[..., c5:]

        y_a = fourier_mix(a_in) @ w_br_a[l]
        y_b = dilated_mixture(qb, kb, vb, rel_bias[:, :DIL_HEADS]) @ w_br_b[l]
        lam_init = 0.8 - 0.6 * math.exp(-0.3 * l)
        lam = (jnp.exp(jnp.sum(lam_q1[l].astype(jnp.float32) * lam_k1[l].astype(jnp.float32)))
               - jnp.exp(jnp.sum(lam_q2[l].astype(jnp.float32) * lam_k2[l].astype(jnp.float32)))
               + lam_init)
        y_c = diff_attention(qc, kc, vc, rel_bias[:, DIL_HEADS:], lam, subln_g[l], lam_init) @ w_br_c[l]

        gates = jax.nn.sigmoid(u @ w_gate[l] + b_gate[l]).reshape(Bsz, S, N_BRANCHES, D)
        merged = gates[:, :, 0] * y_a + gates[:, :, 1] * y_b + gates[:, :, 2] * y_c
        x = x + merged @ w_out[l]

        x = x + 0.5 * swiglu(rmsnorm(x, g_ffn2[l]), w_ffn2_gate[l], w_ffn2_up[l], w_ffn2_down[l])
    return rmsnorm(x, g_final)
```

```python
import functools
import math

import numpy as np
import jax
import jax.numpy as jnp
from jax import lax
from jax.experimental import pallas as pl
from jax.experimental.pallas import tpu as pltpu

F32 = jnp.float32
BF16 = jnp.bfloat16

D_MODEL = 1024
BATCH = 8
SEQ = 2048
DEPTH = 2
TOKENS = BATCH * SEQ
D_FF = 2816
EPS = 1e-6
FOURIER_GROUPS = 8
FOURIER_GROUP_DIM = 128
FOURIER_WIDTH = FOURIER_GROUPS * FOURIER_GROUP_DIM
DIL_PAIRS = ((128, 1), (512, 4), (2048, 16))
DIL_GROUPS = len(DIL_PAIRS)
DIL_HEAD_DIM = 64
DIL_HEADS = 2 * DIL_GROUPS
DIL_WIDTH = DIL_HEADS * DIL_HEAD_DIM
DIL_OUT_WIDTH = 2 * DIL_HEAD_DIM
DIL_RADIUS = 64
DIFF_HEADS = 4
DIFF_HEAD_DIM = 64
DIFF_QK_WIDTH = DIFF_HEADS * 2 * DIFF_HEAD_DIM
DIFF_V_DIM = 2 * DIFF_HEAD_DIM
DIFF_V_WIDTH = DIFF_HEADS * DIFF_V_DIM
SUBLN_EPS = 1e-5
NUM_BUCKETS = 32
MAX_DISTANCE = 1024
N_BIAS_HEADS = DIL_HEADS + DIFF_HEADS
NEG_INF = -1e30
QK_SCALE = DIL_HEAD_DIM ** -0.5
B_WIDTH = 3 * DIL_WIDTH
C_WIDTH = 2 * DIFF_QK_WIDTH + DIFF_V_WIDTH
GATE_WIDTH = 3 * D_MODEL

LANES = 128
VMEM_BYTES_V7X = 64 * 1024 * 1024
VMEM_LIMIT = 56 * 1024 * 1024

FFN_TM = 512
FF_CHUNK = 256
PROJ_TM = 512
PROJ_CHUNK = 512
MERGE_TM = 512
HALF_SEQ = SEQ // 2
FOURIER_CN = 256
DIL_QB = 128
DIL_KW = DIL_QB + 2 * DIL_RADIUS
DIFF_TQ = 256
DIFF_STRIP = 2 * SEQ - DIFF_TQ


def _params(*semantics):
    return pltpu.CompilerParams(dimension_semantics=semantics, vmem_limit_bytes=VMEM_LIMIT)


def _resident(shape):
    nd = len(shape)
    return pl.BlockSpec(shape, lambda *_: (0,) * nd, pipeline_mode=pl.Buffered(1))


def _rms(x, g, eps):
    return x * lax.rsqrt(jnp.mean(x * x, axis=-1, keepdims=True) + eps) * g


def _bucket_breakpoints():
    rel = np.arange(-(SEQ - 1), SEQ)
    half = NUM_BUCKETS // 2
    max_exact = half // 2
    n = np.abs(rel)
    nf = np.maximum(n, 1).astype(np.float64)
    large = max_exact + (np.log(nf / max_exact) / math.log(MAX_DISTANCE / max_exact)
                         * (half - max_exact)).astype(np.int32)
    large = np.minimum(large, half - 1)
    b = np.where(rel > 0, half, 0) + np.where(n < max_exact, n, large)
    pts = [(int(rel[i]), int(b[i])) for i in range(1, len(rel)) if b[i] != b[i - 1]]
    return int(b[0]), pts


def _bias_kernel(tbl_ref, o_ref, *, head0, col0, dil, band):
    head = head0 + pl.program_id(0)
    rows, cols = o_ref.shape[1], o_ref.shape[2]
    rel_t = (lax.broadcasted_iota(jnp.int32, (rows, cols), 1)
             - lax.broadcasted_iota(jnp.int32, (rows, cols), 0) + col0)
    rel = rel_t * dil
    first, pts = _bucket_breakpoints()
    val = jnp.full((rows, cols), tbl_ref[first, head], F32)
    for thr, bucket in pts:
        val = jnp.where(rel >= thr, tbl_ref[bucket, head], val)
    if band is not None:
        val = jnp.where(jnp.abs(rel_t) <= band, val, NEG_INF)
    o_ref[0] = val


def _bias_tiles(rel_bias, *, nheads, head0, rows, cols, col0, dil, band):
    return pl.pallas_call(
        functools.partial(_bias_kernel, head0=head0, col0=col0, dil=dil, band=band),
        grid=(nheads,),
        in_specs=[pl.BlockSpec(memory_space=pltpu.SMEM)],
        out_specs=pl.BlockSpec((1, rows, cols), lambda h: (h, 0, 0)),
        out_shape=jax.ShapeDtypeStruct((nheads, rows, cols), F32),
        compiler_params=_params("arbitrary"),
        name="rel_bias_tiles",
    )(rel_bias)


def _fold_kernel(w_ref, dc_ref, ds_ref, oc_ref, os_ref):
    w = w_ref[...]
    oc_ref[...] = jnp.dot(w, dc_ref[...], preferred_element_type=F32,
                          precision=lax.Precision.HIGHEST).astype(BF16)
    os_ref[...] = jnp.dot(w, ds_ref[...], preferred_element_type=F32,
                          precision=lax.Precision.HIGHEST).astype(BF16)


def _fold_channel_dft(w_a, dc, ds):
    gd = FOURIER_GROUP_DIM
    col = pl.BlockSpec((D_MODEL, gd), lambda g: (0, g))
    mat = pl.BlockSpec((gd, gd), lambda g: (0, 0))
    return pl.pallas_call(
        _fold_kernel,
        grid=(FOURIER_GROUPS,),
        in_specs=[col, mat, mat],
        out_specs=[col, col],
        out_shape=[jax.ShapeDtypeStruct((D_MODEL, FOURIER_WIDTH), BF16)] * 2,
        compiler_params=_params("arbitrary"),
        name="fold_channel_dft",
    )(w_a, dc, ds)


def _ffn_kernel(x_ref, g_ref, wg_ref, wu_ref, wd_ref, *rest, final_norm):
    if final_norm:
        gf_ref, o_ref, a_ref = rest
    else:
        o_ref, a_ref = rest
    x = x_ref[...]
    h = _rms(x, g_ref[...], EPS).astype(BF16)
    for c in range(D_FF // FF_CHUNK):
        sl = slice(c * FF_CHUNK, (c + 1) * FF_CHUNK)
        gate = jnp.dot(h, wg_ref[:, sl], preferred_element_type=F32)
        up = jnp.dot(h, wu_ref[:, sl], preferred_element_type=F32)
        a_ref[:, sl] = (gate * jax.nn.sigmoid(gate) * up).astype(BF16)
    y = x + 0.5 * jnp.dot(a_ref[...], wd_ref[...], preferred_element_type=F32)
    if final_norm:
        y = _rms(y, gf_ref[...], EPS)
    o_ref[...] = y


def _ffn(x, g, wg, wu, wd, g_final=None):
    final_norm = g_final is not None
    tile = pl.BlockSpec((FFN_TM, D_MODEL), lambda i: (i, 0))
    in_specs = [tile, _resident((1, D_MODEL)), _resident((D_MODEL, D_FF)),
                _resident((D_MODEL, D_FF)), _resident((D_FF, D_MODEL))]
    args = [x, g.reshape(1, D_MODEL), wg, wu, wd]
    if final_norm:
        in_specs.append(_resident((1, D_MODEL)))
        args.append(g_final.reshape(1, D_MODEL))
    return pl.pallas_call(
        functools.partial(_ffn_kernel, final_norm=final_norm),
        grid=(TOKENS // FFN_TM,),
        in_specs=in_specs,
        out_specs=tile,
        out_shape=jax.ShapeDtypeStruct((TOKENS, D_MODEL), F32),
        scratch_shapes=[pltpu.VMEM((FFN_TM, D_FF), BF16)],
        compiler_params=_params("parallel"),
        name="ffn_final" if final_norm else "ffn",
    )(*args)


def _proj_kernel(x_ref, g_ref, wac_ref, was_ref, wb_ref, wc_ref, wg_ref, bg_ref,
                 pac_ref, pas_ref, pb_ref, pc_ref, gate_ref):
    u = _rms(x_ref[...], g_ref[...], EPS).astype(BF16)

    def project(w_ref, o_ref, bias_ref=None):
        width = w_ref.shape[1]
        for c0 in range(0, width, PROJ_CHUNK):
            sl = slice(c0, min(c0 + PROJ_CHUNK, width))
            y = jnp.dot(u, w_ref[:, sl], preferred_element_type=F32)
            if bias_ref is not None:
                y = jax.nn.sigmoid(y + bias_ref[:, sl])
            o_ref[:, sl] = y.astype(o_ref.dtype)

    project(wac_ref, pac_ref)
    project(was_ref, pas_ref)
    project(wb_ref, pb_ref)
    project(wc_ref, pc_ref)
    project(wg_ref, gate_ref, bg_ref)


def _proj(x, g, wac, was, wb, wc, wg, bg):
    widths = (FOURIER_WIDTH, FOURIER_WIDTH, B_WIDTH, C_WIDTH, GATE_WIDTH)

    def tile(w):
        return pl.BlockSpec((PROJ_TM, w), lambda i: (i, 0))

    return pl.pallas_call(
        _proj_kernel,
        grid=(TOKENS // PROJ_TM,),
        in_specs=[tile(D_MODEL), _resident((1, D_MODEL))]
        + [_resident((D_MODEL, w)) for w in widths] + [_resident((1, GATE_WIDTH))],
        out_specs=[tile(w) for w in widths],
        out_shape=[jax.ShapeDtypeStruct((TOKENS, w), BF16) for w in widths],
        compiler_params=_params("parallel"),
        name="proj_in",
    )(x, g.reshape(1, D_MODEL), wac, was, wb, wc, wg, bg.reshape(1, GATE_WIDTH))


def _fourier_kernel(pc_ref, ps_ref, me_ref, mo_ref, o_ref):
    pc_lo, pc_hi = pc_ref[0, :HALF_SEQ, :], pc_ref[0, HALF_SEQ:, :]
    ps_lo, ps_hi = ps_ref[0, :HALF_SEQ, :], ps_ref[0, HALF_SEQ:, :]
    z_even = jnp.concatenate([pc_lo + pc_hi, ps_lo + ps_hi], axis=0)
    z_odd = jnp.concatenate([pc_lo - pc_hi, ps_lo - ps_hi], axis=0)
    o_ref[0, 0] = jnp.dot(me_ref[...], z_even, preferred_element_type=F32).astype(BF16)
    o_ref[0, 1] = jnp.dot(mo_ref[...], z_odd, preferred_element_type=F32).astype(BF16)


def _fourier(pac, pas, m_even, m_odd):
    cn = FOURIER_CN
    inp = pl.BlockSpec((1, SEQ, cn), lambda b, j: (b, 0, j))
    return pl.pallas_call(
        _fourier_kernel,
        grid=(BATCH, FOURIER_WIDTH // cn),
        in_specs=[inp, inp, _resident((HALF_SEQ, SEQ)), _resident((HALF_SEQ, SEQ))],
        out_specs=pl.BlockSpec((1, 2, HALF_SEQ, cn), lambda b, j: (b, 0, 0, j)),
        out_shape=jax.ShapeDtypeStruct((BATCH, 2, HALF_SEQ, FOURIER_WIDTH), BF16),
        compiler_params=_params("parallel", "parallel"),
        name="fourier_mix",
    )(pac, pas, m_even, m_odd)


def _dft_matrices():
    half = HALF_SEQ
    m = lax.broadcasted_iota(jnp.int32, (half, half), 0)
    n = lax.broadcasted_iota(jnp.int32, (half, half), 1)
    ang_e = ((m * n) % half).astype(F32) * (2.0 * math.pi / half)
    ang_o = (((2 * m + 1) * n) % SEQ).astype(F32) * (2.0 * math.pi / SEQ)
    m_even = jnp.concatenate([jnp.cos(ang_e), -jnp.sin(ang_e)], axis=1).astype(BF16)
    m_odd = jnp.concatenate([jnp.cos(ang_o), -jnp.sin(ang_o)], axis=1).astype(BF16)
    return m_even, m_odd


def _channel_dft():
    gd = FOURIER_GROUP_DIM
    idx = np.arange(gd)
    ang = 2.0 * np.pi * ((idx[:, None] * idx[None, :]) % gd) / gd
    scale = 1.0 / math.sqrt(SEQ * gd)
    return (jnp.asarray(np.cos(ang) * scale, F32), jnp.asarray(np.sin(ang) * scale, F32))


def _dil_kernel(q_ref, k_ref, v_ref, bias_ref, o_ref, lse_ref, kpad_ref, vpad_ref, *, sub_len):
    pad = DIL_RADIUS
    zeros = jnp.zeros((pad, LANES), BF16)
    for ref, src in ((kpad_ref, k_ref), (vpad_ref, v_ref)):
        ref[:pad, :] = zeros
        ref[pad + sub_len:, :] = zeros
        ref[pad:pad + sub_len, :] = src[0]

    col = lax.broadcasted_iota(jnp.int32, (DIL_QB, DIL_KW), 1)

    def block(i, carry):
        r0 = pl.multiple_of(i * DIL_QB, DIL_QB)
        q = q_ref[0, pl.ds(r0, DIL_QB), :]
        kw = kpad_ref[pl.ds(r0, DIL_KW), :]
        vw = vpad_ref[pl.ds(r0, DIL_KW), :]
        key_pos = col + (r0 - pad)
        in_seq = (key_pos >= 0) & (key_pos < sub_len)
        outs, lses = [], []
        for hh in range(2):
            hs = slice(hh * DIL_HEAD_DIM, (hh + 1) * DIL_HEAD_DIM)
            logits = lax.dot_general(q[:, hs], kw[:, hs], (((1,), (1,)), ((), ())),
                                     preferred_element_type=F32)
            logits = jnp.where(in_seq, logits + bias_ref[hh], NEG_INF)
            mx = jnp.max(logits, axis=-1, keepdims=True)
            e = jnp.exp(logits - mx)
            s = jnp.sum(e, axis=-1, keepdims=True)
            pv = jnp.dot(e.astype(BF16), vw[:, hs], preferred_element_type=F32)
            outs.append(pv / s)
            lses.append(jnp.broadcast_to(mx + jnp.log(s), (DIL_QB, DIL_HEAD_DIM)))
        o_ref[0, pl.ds(r0, DIL_QB), :] = jnp.concatenate(outs, axis=-1).astype(o_ref.dtype)
        lse_ref[0, pl.ds(r0, DIL_QB), :] = jnp.concatenate(lses, axis=-1)
        return carry

    lax.fori_loop(0, sub_len // DIL_QB, block, 0)


def _dilated_group(pb, bias, group, dil):
    sub_len = SEQ // dil
    nblk = B_WIDTH // LANES
    pbv = pb.reshape(BATCH, sub_len, dil * B_WIDTH)

    def qkv(which):
        return pl.BlockSpec((1, sub_len, LANES),
                            lambda b, r: (b, 0, r * nblk + which * DIL_GROUPS + group))

    out = pl.BlockSpec((1, sub_len, LANES), lambda b, r: (b, 0, r))
    o, lse = pl.pallas_call(
        functools.partial(_dil_kernel, sub_len=sub_len),
        grid=(BATCH, dil),
        in_specs=[qkv(0), qkv(1), qkv(2), _resident((2, DIL_QB, DIL_KW))],
        out_specs=[out, out],
        out_shape=[jax.ShapeDtypeStruct((BATCH, sub_len, dil * LANES), BF16),
                   jax.ShapeDtypeStruct((BATCH, sub_len, dil * LANES), F32)],
        scratch_shapes=[pltpu.VMEM((sub_len + 2 * DIL_RADIUS, LANES), BF16)] * 2,
        compiler_params=_params("parallel", "parallel"),
        name=f"dilated_attn_d{dil}",
    )(pbv, pbv, pbv, bias)
    return o.reshape(TOKENS, LANES), lse.reshape(TOKENS, LANES)


def _diff_kernel(q_ref, k_ref, v_ref, bias_ref, lq1_ref, lk1_ref, lq2_ref, lk2_ref, sg_ref,
                 o_ref, *, lam_init):
    j = pl.program_id(2)
    nq = pl.num_programs(2)
    off = pl.multiple_of((nq - 1 - j) * DIFF_TQ, DIFF_TQ)
    bias = bias_ref[0, :, pl.ds(off, SEQ)]
    lam = (jnp.exp(jnp.sum(lq1_ref[...] * lk1_ref[...], axis=-1, keepdims=True))
           - jnp.exp(jnp.sum(lq2_ref[...] * lk2_ref[...], axis=-1, keepdims=True))
           + lam_init)
    q = q_ref[0]
    k = k_ref[0]
    probs = []
    for m in range(2):
        hs = slice(m * DIFF_HEAD_DIM, (m + 1) * DIFF_HEAD_DIM)
        logits = lax.dot_general(q[:, hs], k[:, hs], (((1,), (1,)), ((), ())),
                                 preferred_element_type=F32) + bias
        e = jnp.exp(logits - jnp.max(logits, axis=-1, keepdims=True))
        probs.append(e / jnp.sum(e, axis=-1, keepdims=True))
    a = (probs[0] - lam * probs[1]).astype(BF16)
    o = jnp.dot(a, v_ref[0], preferred_element_type=F32)
    o = _rms(o, sg_ref[...], SUBLN_EPS) * (1.0 - lam_init)
    o_ref[0] = o.astype(o_ref.dtype)


def _diff_attention(pc, strip, lq1, lk1, lq2, lk2, subln_g, lam_init):
    pcv = pc.reshape(BATCH, SEQ, C_WIDTH)
    vec = _resident((1, DIFF_HEAD_DIM))
    return pl.pallas_call(
        functools.partial(_diff_kernel, lam_init=lam_init),
        grid=(DIFF_HEADS, BATCH, SEQ // DIFF_TQ),
        in_specs=[pl.BlockSpec((1, DIFF_TQ, LANES), lambda h, b, j: (b, j, h)),
                  pl.BlockSpec((1, SEQ, LANES), lambda h, b, j: (b, 0, DIFF_HEADS + h)),
                  pl.BlockSpec((1, SEQ, LANES), lambda h, b, j: (b, 0, 2 * DIFF_HEADS + h)),
                  pl.BlockSpec((1, DIFF_TQ, DIFF_STRIP), lambda h, b, j: (h, 0, 0)),
                  vec, vec, vec, vec, _resident((1, DIFF_V_DIM))],
        out_specs=pl.BlockSpec((1, DIFF_TQ, LANES), lambda h, b, j: (b, j, h)),
        out_shape=jax.ShapeDtypeStruct((BATCH, SEQ, DIFF_V_WIDTH), BF16),
        compiler_params=_params("parallel", "parallel", "parallel"),
        name="diff_attn",
    )(pcv, pcv, pcv, strip, lq1.reshape(1, -1), lk1.reshape(1, -1), lq2.reshape(1, -1),
      lk2.reshape(1, -1), subln_g.reshape(1, -1)).reshape(TOKENS, DIFF_V_WIDTH)


def _merge_kernel(x_ref, f_ref, o0_ref, o1_ref, o2_ref, l0_ref, l1_ref, l2_ref, c_ref, gate_ref,
                  wa_ref, wb_ref, wc_ref, wo_ref, out_ref):
    y_a = jnp.dot(f_ref[...], wa_ref[...], preferred_element_type=F32)
    l0, l1, l2 = l0_ref[...], l1_ref[...], l2_ref[...]
    mx = jnp.maximum(jnp.maximum(l0, l1), l2)
    e0, e1, e2 = jnp.exp(l0 - mx), jnp.exp(l1 - mx), jnp.exp(l2 - mx)
    mixed = (e0 * o0_ref[...].astype(F32) + e1 * o1_ref[...].astype(F32)
             + e2 * o2_ref[...].astype(F32)) / (e0 + e1 + e2)
    y_b = jnp.dot(mixed.astype(BF16), wb_ref[...], preferred_element_type=F32)
    y_c = jnp.dot(c_ref[...], wc_ref[...], preferred_element_type=F32)
    merged = (gate_ref[:, :D_MODEL].astype(F32) * y_a
              + gate_ref[:, D_MODEL:2 * D_MODEL].astype(F32) * y_b
              + gate_ref[:, 2 * D_MODEL:].astype(F32) * y_c)
    out_ref[...] = x_ref[...] + jnp.dot(merged.astype(BF16), wo_ref[...],
                                        preferred_element_type=F32)


def _merge(x, f, dil_o, dil_lse, c, gates, wa, wb, wc, wo):
    pairs = TOKENS // 2

    def by_parity(width):
        return pl.BlockSpec((MERGE_TM, width), lambda i, e: (i, e))

    tiles_per_batch = HALF_SEQ // MERGE_TM
    f_spec = pl.BlockSpec((None, None, MERGE_TM, FOURIER_WIDTH),
                          lambda i, e: (i // tiles_per_batch, e, i % tiles_per_batch, 0))
    args = [x.reshape(pairs, 2 * D_MODEL), f]
    specs = [by_parity(D_MODEL), f_spec]
    for arr in (*dil_o, *dil_lse):
        args.append(arr.reshape(pairs, 2 * LANES))
        specs.append(by_parity(LANES))
    args += [c.reshape(pairs, 2 * DIFF_V_WIDTH), gates.reshape(pairs, 2 * GATE_WIDTH),
             wa, wb, wc, wo]
    specs += [by_parity(DIFF_V_WIDTH), by_parity(GATE_WIDTH),
              _resident((FOURIER_WIDTH, D_MODEL)), _resident((DIL_OUT_WIDTH, D_MODEL)),
              _resident((DIFF_V_WIDTH, D_MODEL)), _resident((D_MODEL, D_MODEL))]
    out = pl.pallas_call(
        _merge_kernel,
        grid=(pairs // MERGE_TM, 2),
        in_specs=specs,
        out_specs=by_parity(D_MODEL),
        out_shape=jax.ShapeDtypeStruct((pairs, 2 * D_MODEL), F32),
        compiler_params=_params("parallel", "parallel"),
        name="gated_merge",
    )(*args)
    return out.reshape(TOKENS, D_MODEL)


def kernel(x, g_ffn1, w_ffn1_gate, w_ffn1_up, w_ffn1_down, g_mix, w_in, w_gate, b_gate,
           w_br_a, w_br_b, w_br_c, w_out, lam_q1, lam_k1, lam_q2, lam_k2, subln_g,
           rel_bias, g_ffn2, w_ffn2_gate, w_ffn2_up, w_ffn2_down, g_final):
    assert x.shape == (BATCH, SEQ, D_MODEL) and x.dtype == F32
    bf = lambda w: w.astype(BF16)
    c0 = FOURIER_WIDTH
    c3 = c0 + B_WIDTH

    dil_bias = [
        _bias_tiles(rel_bias, nheads=2, head0=2 * g, rows=DIL_QB, cols=DIL_KW,
                    col0=-DIL_RADIUS, dil=dil, band=DIL_RADIUS)
        for g, (_, dil) in enumerate(DIL_PAIRS)]
    diff_strip = _bias_tiles(rel_bias, nheads=DIFF_HEADS, head0=DIL_HEADS, rows=DIFF_TQ,
                             cols=DIFF_STRIP, col0=-(SEQ - DIFF_TQ), dil=1, band=None)
    m_even, m_odd = _dft_matrices()
    dc, ds = _channel_dft()

    q_scale_b = jnp.concatenate([jnp.full((DIL_WIDTH,), QK_SCALE, F32),
                                 jnp.ones((2 * DIL_WIDTH,), F32)])
    q_scale_c = jnp.concatenate([jnp.full((DIFF_QK_WIDTH,), QK_SCALE, F32),
                                 jnp.ones((DIFF_QK_WIDTH + DIFF_V_WIDTH,), F32)])

    h = x.reshape(TOKENS, D_MODEL)
    for l in range(DEPTH):
        h = _ffn(h, g_ffn1[l], bf(w_ffn1_gate[l]), bf(w_ffn1_up[l]), bf(w_ffn1_down[l]))

        wac, was = _fold_channel_dft(w_in[l, :, :c0], dc, ds)
        pac, pas, pb, pc, gates = _proj(
            h, g_mix[l], wac, was, bf(w_in[l, :, c0:c3] * q_scale_b),
            bf(w_in[l, :, c3:] * q_scale_c), bf(w_gate[l]), b_gate[l])

        f = _fourier(pac.reshape(BATCH, SEQ, FOURIER_WIDTH),
                     pas.reshape(BATCH, SEQ, FOURIER_WIDTH), m_even, m_odd)
        dil_o, dil_lse = [], []
        for g, (_, dil) in enumerate(DIL_PAIRS):
            o, lse = _dilated_group(pb, dil_bias[g], g, dil)
            dil_o.append(o)
            dil_lse.append(lse)
        lam_init = 0.8 - 0.6 * math.exp(-0.3 * l)
        c = _diff_attention(pc, diff_strip, lam_q1[l], lam_k1[l], lam_q2[l], lam_k2[l],
                            subln_g[l], lam_init)

        h = _merge(h, f, dil_o, dil_lse, c, gates,
                   bf(w_br_a[l]), bf(w_br_b[l]), bf(w_br_c[l]), bf(w_out[l]))

        h = _ffn(h, g_ffn2[l], bf(w_ffn2_gate[l]), bf(w_ffn2_up[l]), bf(w_ffn2_down[l]),
                 g_final if l == DEPTH - 1 else None)
    return h.reshape(BATCH, SEQ, D_MODEL)
```

```python
import functools
import math

import numpy as np
import jax
import jax.numpy as jnp
from jax import lax
from jax.experimental import pallas as pl
from jax.experimental.pallas import tpu as pltpu

F32 = jnp.float32
BF16 = jnp.bfloat16

D_MODEL = 1024
BATCH = 8
SEQ = 2048
DEPTH = 2
TOKENS = BATCH * SEQ
D_FF = 2816
EPS = 1e-6
FOURIER_GROUPS = 8
FOURIER_GROUP_DIM = 128
FOURIER_WIDTH = FOURIER_GROUPS * FOURIER_GROUP_DIM
DIL_PAIRS = ((128, 1), (512, 4), (2048, 16))
DIL_GROUPS = len(DIL_PAIRS)
DIL_HEAD_DIM = 64
DIL_HEADS = 2 * DIL_GROUPS
DIL_WIDTH = DIL_HEADS * DIL_HEAD_DIM
DIL_OUT_WIDTH = 2 * DIL_HEAD_DIM
DIL_RADIUS = 64
DIFF_HEADS = 4
DIFF_HEAD_DIM = 64
DIFF_QK_WIDTH = DIFF_HEADS * 2 * DIFF_HEAD_DIM
DIFF_V_DIM = 2 * DIFF_HEAD_DIM
DIFF_V_WIDTH = DIFF_HEADS * DIFF_V_DIM
SUBLN_EPS = 1e-5
NUM_BUCKETS = 32
MAX_DISTANCE = 1024
N_BIAS_HEADS = DIL_HEADS + DIFF_HEADS
NEG_INF = -1e30
QK_SCALE = DIL_HEAD_DIM ** -0.5
C_WIDTH = 2 * DIFF_QK_WIDTH + DIFF_V_WIDTH
GATE_WIDTH = 3 * D_MODEL

LANES = 128
VMEM_BYTES_V7X = 64 * 1024 * 1024
VMEM_LIMIT = 56 * 1024 * 1024

FFN_TM = 512
FF_CHUNK = 256
PROJ_TM = 512
PROJ_CHUNK = 512
MERGE_TM = 512
HALF_SEQ = SEQ // 2
FOURIER_CN = 256
DIL_QKV = 3 * LANES
DIL_QB = 128
DIL_KW = DIL_QB + 2 * DIL_RADIUS
DIL_UNROLL = 4
DIFF_TQ = 256
DIFF_STRIP = 2 * SEQ - DIFF_TQ


def _params(*semantics):
    return pltpu.CompilerParams(dimension_semantics=semantics, vmem_limit_bytes=VMEM_LIMIT)


def _resident(shape):
    nd = len(shape)
    return pl.BlockSpec(shape, lambda *_: (0,) * nd, pipeline_mode=pl.Buffered(1))


def _rms(x, g, eps):
    return x * lax.rsqrt(jnp.mean(x * x, axis=-1, keepdims=True) + eps) * g


def _bucket_breakpoints():
    rel = np.arange(-(SEQ - 1), SEQ)
    half = NUM_BUCKETS // 2
    max_exact = half // 2
    n = np.abs(rel)
    nf = np.maximum(n, 1).astype(np.float64)
    large = max_exact + (np.log(nf / max_exact) / math.log(MAX_DISTANCE / max_exact)
                         * (half - max_exact)).astype(np.int32)
    large = np.minimum(large, half - 1)
    b = np.where(rel > 0, half, 0) + np.where(n < max_exact, n, large)
    pts = [(int(rel[i]), int(b[i])) for i in range(1, len(rel)) if b[i] != b[i - 1]]
    return int(b[0]), pts


def _bias_kernel(tbl_ref, o_ref, *, head0, col0, dil, band):
    head = head0 + pl.program_id(0)
    rows, cols = o_ref.shape[1], o_ref.shape[2]
    rel_t = (lax.broadcasted_iota(jnp.int32, (rows, cols), 1)
             - lax.broadcasted_iota(jnp.int32, (rows, cols), 0) + col0)
    rel = rel_t * dil
    first, pts = _bucket_breakpoints()
    val = jnp.full((rows, cols), tbl_ref[first, head], F32)
    for thr, bucket in pts:
        val = jnp.where(rel >= thr, tbl_ref[bucket, head], val)
    if band is not None:
        val = jnp.where(jnp.abs(rel_t) <= band, val, NEG_INF)
    o_ref[0] = val


def _bias_tiles(rel_bias, *, nheads, head0, rows, cols, col0, dil, band):
    return pl.pallas_call(
        functools.partial(_bias_kernel, head0=head0, col0=col0, dil=dil, band=band),
        grid=(nheads,),
        in_specs=[pl.BlockSpec(memory_space=pltpu.SMEM)],
        out_specs=pl.BlockSpec((1, rows, cols), lambda h: (h, 0, 0)),
        out_shape=jax.ShapeDtypeStruct((nheads, rows, cols), F32),
        compiler_params=_params("arbitrary"),
        name="rel_bias_tiles",
    )(rel_bias)


def _fold_kernel(w_ref, dc_ref, ds_ref, o_ref):
    w = w_ref[...]
    o_ref[0] = jnp.dot(w, dc_ref[...], preferred_element_type=F32,
                       precision=lax.Precision.HIGHEST).astype(BF16)
    o_ref[1] = jnp.dot(w, ds_ref[...], preferred_element_type=F32,
                       precision=lax.Precision.HIGHEST).astype(BF16)


def _fold_channel_dft(w_a, dc, ds):
    gd = FOURIER_GROUP_DIM
    mat = pl.BlockSpec((gd, gd), lambda g: (0, 0))
    out = pl.pallas_call(
        _fold_kernel,
        grid=(FOURIER_GROUPS,),
        in_specs=[pl.BlockSpec((D_MODEL, gd), lambda g: (0, g)), mat, mat],
        out_specs=pl.BlockSpec((2, D_MODEL, gd), lambda g: (0, 0, g)),
        out_shape=jax.ShapeDtypeStruct((2, D_MODEL, FOURIER_WIDTH), BF16),
        compiler_params=_params("arbitrary"),
        name="fold_channel_dft",
    )(w_a, dc, ds)
    return jnp.concatenate([out[0], out[1]], axis=1)


def _ffn_kernel(x_ref, g_ref, wg_ref, wu_ref, wd_ref, *rest, final_norm):
    if final_norm:
        gf_ref, o_ref, a_ref = rest
    else:
        o_ref, a_ref = rest
    x = x_ref[...]
    h = _rms(x, g_ref[...], EPS).astype(BF16)
    for c in range(D_FF // FF_CHUNK):
        sl = slice(c * FF_CHUNK, (c + 1) * FF_CHUNK)
        gate = jnp.dot(h, wg_ref[:, sl], preferred_element_type=F32)
        up = jnp.dot(h, wu_ref[:, sl], preferred_element_type=F32)
        a_ref[:, sl] = (gate * jax.nn.sigmoid(gate) * up).astype(BF16)
    y = x + 0.5 * jnp.dot(a_ref[...], wd_ref[...], preferred_element_type=F32)
    if final_norm:
        y = _rms(y, gf_ref[...], EPS)
    o_ref[...] = y


def _ffn(x, g, wg, wu, wd, g_final=None):
    final_norm = g_final is not None
    tile = pl.BlockSpec((FFN_TM, D_MODEL), lambda i: (i, 0))
    in_specs = [tile, _resident((1, D_MODEL)), _resident((D_MODEL, D_FF)),
                _resident((D_MODEL, D_FF)), _resident((D_FF, D_MODEL))]
    args = [x, g.reshape(1, D_MODEL), wg, wu, wd]
    if final_norm:
        in_specs.append(_resident((1, D_MODEL)))
        args.append(g_final.reshape(1, D_MODEL))
    return pl.pallas_call(
        functools.partial(_ffn_kernel, final_norm=final_norm),
        grid=(TOKENS // FFN_TM,),
        in_specs=in_specs,
        out_specs=tile,
        out_shape=jax.ShapeDtypeStruct((TOKENS, D_MODEL), F32),
        scratch_shapes=[pltpu.VMEM((FFN_TM, D_FF), BF16)],
        compiler_params=_params("parallel"),
        name="ffn_final" if final_norm else "ffn",
    )(*args)


def _proj_kernel(x_ref, g_ref, wf_ref, wb0_ref, wb1_ref, wb2_ref, wc_ref, wg_ref, bg_ref,
                 pf_ref, pb0_ref, pb1_ref, pb2_ref, pc_ref, gate_ref, y_ref):
    u = _rms(x_ref[...], g_ref[...], EPS).astype(BF16)
    rows = x_ref.shape[0]

    def project(w_ref, store, bias_ref=None):
        width = w_ref.shape[1]
        for c0 in range(0, width, PROJ_CHUNK):
            sl = slice(c0, min(c0 + PROJ_CHUNK, width))
            y = jnp.dot(u, w_ref[:, sl], preferred_element_type=F32)
            if bias_ref is not None:
                y = jax.nn.sigmoid(y + bias_ref[:, sl])
            store(sl, y)

    def natural(o_ref):
        def store(sl, y):
            o_ref[:, sl] = y.astype(o_ref.dtype)
        return store

    def by_residue(o_ref, dil):
        def store(sl, y):
            for j in range((sl.stop - sl.start) // LANES):
                y_ref[j] = y[:, j * LANES:(j + 1) * LANES]
                cols = slice(sl.start + j * LANES, sl.start + (j + 1) * LANES)
                for r in range(dil):
                    o_ref[0, r, :, cols] = y_ref[j, pl.ds(r, rows // dil, stride=dil), :].astype(o_ref.dtype)
        return store

    project(wf_ref, by_residue(pf_ref, 2))
    project(wb0_ref, natural(pb0_ref))
    project(wb1_ref, by_residue(pb1_ref, DIL_PAIRS[1][1]))
    project(wb2_ref, by_residue(pb2_ref, DIL_PAIRS[2][1]))
    project(wc_ref, natural(pc_ref))
    project(wg_ref, natural(gate_ref), bg_ref)


def _proj(x, g, wf, wb, wc, wg, bg):
    tiles_per_seq = SEQ // PROJ_TM

    def tile(w):
        return pl.BlockSpec((PROJ_TM, w), lambda i: (i, 0))

    def residue_tile(dil, w):
        return pl.BlockSpec((1, dil, PROJ_TM // dil, w),
                            lambda i: (i // tiles_per_seq, 0, i % tiles_per_seq, 0))

    def residue_shape(dil, w):
        return jax.ShapeDtypeStruct((BATCH, dil, SEQ // dil, w), BF16)

    d1, d2 = DIL_PAIRS[1][1], DIL_PAIRS[2][1]
    weights = [wf, *wb, wc, wg]
    return pl.pallas_call(
        _proj_kernel,
        grid=(TOKENS // PROJ_TM,),
        in_specs=[tile(D_MODEL), _resident((1, D_MODEL))]
        + [_resident(w.shape) for w in weights] + [_resident((1, GATE_WIDTH))],
        out_specs=[residue_tile(2, 2 * FOURIER_WIDTH), tile(DIL_QKV), residue_tile(d1, DIL_QKV),
                   residue_tile(d2, DIL_QKV), tile(C_WIDTH), tile(GATE_WIDTH)],
        out_shape=[residue_shape(2, 2 * FOURIER_WIDTH),
                   jax.ShapeDtypeStruct((TOKENS, DIL_QKV), BF16),
                   residue_shape(d1, DIL_QKV), residue_shape(d2, DIL_QKV),
                   jax.ShapeDtypeStruct((TOKENS, C_WIDTH), BF16),
                   jax.ShapeDtypeStruct((TOKENS, GATE_WIDTH), BF16)],
        scratch_shapes=[pltpu.VMEM((PROJ_CHUNK // LANES, PROJ_TM, LANES), F32)],
        compiler_params=_params("parallel"),
        name="proj_in",
    )(x, g.reshape(1, D_MODEL), *weights, bg.reshape(1, GATE_WIDTH))


def _fourier_kernel(pc_ref, ps_ref, me_ref, mo_ref, o_ref):
    h = HALF_SEQ
    even = (jnp.dot(me_ref[:, :h], pc_ref[0, 0], preferred_element_type=F32)
            + jnp.dot(me_ref[:, h:], ps_ref[0, 0], preferred_element_type=F32))
    odd = (jnp.dot(mo_ref[:, :h], pc_ref[0, 1], preferred_element_type=F32)
           + jnp.dot(mo_ref[:, h:], ps_ref[0, 1], preferred_element_type=F32))
    o_ref[0, :h, :] = (even + odd).astype(o_ref.dtype)
    o_ref[0, h:, :] = (even - odd).astype(o_ref.dtype)


def _fourier(pf, m_even, m_odd):
    cn = FOURIER_CN
    nblk = FOURIER_WIDTH // cn
    return pl.pallas_call(
        _fourier_kernel,
        grid=(BATCH, nblk),
        in_specs=[pl.BlockSpec((1, 2, HALF_SEQ, cn), lambda b, j: (b, 0, 0, j)),
                  pl.BlockSpec((1, 2, HALF_SEQ, cn), lambda b, j: (b, 0, 0, nblk + j)),
                  _resident((HALF_SEQ, SEQ)), _resident((HALF_SEQ, SEQ))],
        out_specs=pl.BlockSpec((1, SEQ, cn), lambda b, j: (b, 0, j)),
        out_shape=jax.ShapeDtypeStruct((BATCH, SEQ, FOURIER_WIDTH), BF16),
        compiler_params=_params("parallel", "parallel"),
        name="fourier_mix",
    )(pf, pf, m_even, m_odd)


def _dft_matrices():
    half = HALF_SEQ
    k = lax.broadcasted_iota(jnp.int32, (half, half), 0)
    m = lax.broadcasted_iota(jnp.int32, (half, half), 1)
    ang_e = ((k * m) % half).astype(F32) * (2.0 * math.pi / half)
    ang_o = ((k * (2 * m + 1)) % SEQ).astype(F32) * (2.0 * math.pi / SEQ)
    m_even = jnp.concatenate([jnp.cos(ang_e), -jnp.sin(ang_e)], axis=1).astype(BF16)
    m_odd = jnp.concatenate([jnp.cos(ang_o), -jnp.sin(ang_o)], axis=1).astype(BF16)
    return m_even, m_odd


def _channel_dft():
    gd = FOURIER_GROUP_DIM
    idx = np.arange(gd)
    ang = 2.0 * np.pi * ((idx[:, None] * idx[None, :]) % gd) / gd
    scale = 1.0 / math.sqrt(SEQ * gd)
    return (jnp.asarray(np.cos(ang) * scale, F32), jnp.asarray(np.sin(ang) * scale, F32))


def _dil_block(q, kw, vw, bias_ref, key_pos0, sub_len, head0_lanes):
    key_pos = lax.broadcasted_iota(jnp.int32, (DIL_QB, DIL_KW), 1) + key_pos0
    in_seq = (key_pos >= 0) & (key_pos < sub_len)
    zero = jnp.zeros_like(q)
    pv, lse = [], []
    for hh in range(2):
        q_h = jnp.where(head0_lanes, q, zero) if hh == 0 else jnp.where(head0_lanes, zero, q)
        logits = lax.dot_general(q_h, kw, (((1,), (1,)), ((), ())), preferred_element_type=F32)
        logits = jnp.where(in_seq, logits + bias_ref[hh], NEG_INF)
        mx = jnp.max(logits, axis=-1, keepdims=True)
        e = jnp.exp(logits - mx)
        s = jnp.sum(e, axis=-1, keepdims=True)
        pv.append(jnp.dot(e.astype(BF16), vw, preferred_element_type=F32) / s)
        lse.append(mx + jnp.log(s))
    return jnp.where(head0_lanes, pv[0], pv[1]), jnp.where(head0_lanes, lse[0], lse[1])


def _dil_kernel(g0_ref, g1_ref, g2_ref, b0_ref, b1_ref, b2_ref, out_ref,
                kpad_ref, vpad_ref, o_ref, lse_ref):
    pad = DIL_RADIUS
    head0_lanes = lax.broadcasted_iota(jnp.int32, (DIL_QB, LANES), 1) < DIL_HEAD_DIM
    zeros = jnp.zeros((pad, LANES), BF16)
    q_sl, k_sl, v_sl = (slice(i * LANES, (i + 1) * LANES) for i in range(3))

    def fill_padded(src, sub_len):
        for ref, sl in ((kpad_ref, k_sl), (vpad_ref, v_sl)):
            ref[:pad, :] = zeros
            ref[pl.ds(pad, sub_len), :] = src[:, sl]
            ref[pl.ds(pad + sub_len, pad), :] = zeros

    def run_blocks(group, q_src, bias_ref, sub_len, dil, residue, first_block, count):
        for n in range(count):
            r0 = pl.multiple_of((first_block + n) * DIL_QB, DIL_QB)
            o, lse = _dil_block(q_src[pl.ds(r0, DIL_QB), q_sl], kpad_ref[pl.ds(r0, DIL_KW), :],
                                vpad_ref[pl.ds(r0, DIL_KW), :], bias_ref, r0 - pad, sub_len,
                                head0_lanes)
            rows = pl.ds(residue + r0 * dil, DIL_QB, stride=dil) if dil > 1 else pl.ds(r0, DIL_QB)
            o_ref[group, rows, :] = o
            lse_ref[group, rows, :] = lse

    fill_padded(g0_ref.at[0], SEQ)

    def g0_body(i, carry):
        run_blocks(0, g0_ref.at[0], b0_ref, SEQ, 1, 0, i * DIL_UNROLL, DIL_UNROLL)
        return carry

    lax.fori_loop(0, SEQ // (DIL_QB * DIL_UNROLL), g0_body, 0)

    for group, g_ref, bias_ref in ((1, g1_ref, b1_ref), (2, g2_ref, b2_ref)):
        dil = DIL_PAIRS[group][1]
        sub_len = SEQ // dil

        def body(r, carry, group=group, g_ref=g_ref, bias_ref=bias_ref, dil=dil, sub_len=sub_len):
            src = g_ref.at[0, r]
            fill_padded(src, sub_len)
            run_blocks(group, src, bias_ref, sub_len, dil, r, 0, sub_len // DIL_QB)
            return carry

        lax.fori_loop(0, dil, body, 0)

    chunk = 256
    for c in range(SEQ // chunk):
        rows = slice(c * chunk, (c + 1) * chunk)
        l0, l1, l2 = lse_ref[0, rows, :], lse_ref[1, rows, :], lse_ref[2, rows, :]
        mx = jnp.maximum(jnp.maximum(l0, l1), l2)
        e0, e1, e2 = jnp.exp(l0 - mx), jnp.exp(l1 - mx), jnp.exp(l2 - mx)
        mixed = (e0 * o_ref[0, rows, :] + e1 * o_ref[1, rows, :] + e2 * o_ref[2, rows, :]) / (e0 + e1 + e2)
        out_ref[0, rows, :] = mixed.astype(out_ref.dtype)


def _dilated_mixture(pb0, pb1, pb2, biases):
    d1, d2 = DIL_PAIRS[1][1], DIL_PAIRS[2][1]
    bias_spec = _resident((2, DIL_QB, DIL_KW))
    max_len = SEQ + 2 * DIL_RADIUS
    return pl.pallas_call(
        _dil_kernel,
        grid=(BATCH,),
        in_specs=[pl.BlockSpec((1, SEQ, DIL_QKV), lambda b: (b, 0, 0)),
                  pl.BlockSpec((1, d1, SEQ // d1, DIL_QKV), lambda b: (b, 0, 0, 0)),
                  pl.BlockSpec((1, d2, SEQ // d2, DIL_QKV), lambda b: (b, 0, 0, 0)),
                  bias_spec, bias_spec, bias_spec],
        out_specs=pl.BlockSpec((1, SEQ, LANES), lambda b: (b, 0, 0)),
        out_shape=jax.ShapeDtypeStruct((BATCH, SEQ, DIL_OUT_WIDTH), BF16),
        scratch_shapes=[pltpu.VMEM((max_len, LANES), BF16), pltpu.VMEM((max_len, LANES), BF16),
                        pltpu.VMEM((DIL_GROUPS, SEQ, LANES), F32),
                        pltpu.VMEM((DIL_GROUPS, SEQ, LANES), F32)],
        compiler_params=_params("parallel"),
        name="dilated_attn",
    )(pb0, pb1, pb2, *biases)


def _diff_kernel(q_ref, k_ref, v_ref, bias_ref, lq1_ref, lk1_ref, lq2_ref, lk2_ref, sg_ref,
                 o_ref, *, lam_init):
    j = pl.program_id(2)
    nq = pl.num_programs(2)
    off = pl.multiple_of((nq - 1 - j) * DIFF_TQ, DIFF_TQ)
    bias = bias_ref[0, :, pl.ds(off, SEQ)]
    lam = (jnp.exp(jnp.sum(lq1_ref[...] * lk1_ref[...], axis=-1, keepdims=True))
           - jnp.exp(jnp.sum(lq2_ref[...] * lk2_ref[...], axis=-1, keepdims=True))
           + lam_init)
    q = q_ref[0]
    k = k_ref[0]
    probs = []
    for m in range(2):
        hs = slice(m * DIFF_HEAD_DIM, (m + 1) * DIFF_HEAD_DIM)
        logits = lax.dot_general(q[:, hs], k[:, hs], (((1,), (1,)), ((), ())),
                                 preferred_element_type=F32) + bias
        e = jnp.exp(logits - jnp.max(logits, axis=-1, keepdims=True))
        probs.append(e / jnp.sum(e, axis=-1, keepdims=True))
    a = (probs[0] - lam * probs[1]).astype(BF16)
    o = jnp.dot(a, v_ref[0], preferred_element_type=F32)
    o = _rms(o, sg_ref[...], SUBLN_EPS) * (1.0 - lam_init)
    o_ref[0] = o.astype(o_ref.dtype)


def _diff_attention(pc, strip, lq1, lk1, lq2, lk2, subln_g, lam_init):
    pcv = pc.reshape(BATCH, SEQ, C_WIDTH)
    vec = _resident((1, DIFF_HEAD_DIM))
    return pl.pallas_call(
        functools.partial(_diff_kernel, lam_init=lam_init),
        grid=(DIFF_HEADS, BATCH, SEQ // DIFF_TQ),
        in_specs=[pl.BlockSpec((1, DIFF_TQ, LANES), lambda h, b, j: (b, j, h)),
                  pl.BlockSpec((1, SEQ, LANES), lambda h, b, j: (b, 0, DIFF_HEADS + h)),
                  pl.BlockSpec((1, SEQ, LANES), lambda h, b, j: (b, 0, 2 * DIFF_HEADS + h)),
                  pl.BlockSpec((1, DIFF_TQ, DIFF_STRIP), lambda h, b, j: (h, 0, 0)),
                  vec, vec, vec, vec, _resident((1, DIFF_V_DIM))],
        out_specs=pl.BlockSpec((1, DIFF_TQ, LANES), lambda h, b, j: (b, j, h)),
        out_shape=jax.ShapeDtypeStruct((BATCH, SEQ, DIFF_V_WIDTH), BF16),
        compiler_params=_params("parallel", "parallel", "parallel"),
        name="diff_attn",
    )(pcv, pcv, pcv, strip, lq1.reshape(1, -1), lk1.reshape(1, -1), lq2.reshape(1, -1),
      lk2.reshape(1, -1), subln_g.reshape(1, -1)).reshape(TOKENS, DIFF_V_WIDTH)


def _merge_kernel(x_ref, f_ref, b_ref, c_ref, gate_ref, wa_ref, wb_ref, wc_ref, wo_ref, out_ref):
    y_a = jnp.dot(f_ref[...], wa_ref[...], preferred_element_type=F32)
    y_b = jnp.dot(b_ref[...], wb_ref[...], preferred_element_type=F32)
    y_c = jnp.dot(c_ref[...], wc_ref[...], preferred_element_type=F32)
    merged = (gate_ref[:, :D_MODEL].astype(F32) * y_a
              + gate_ref[:, D_MODEL:2 * D_MODEL].astype(F32) * y_b
              + gate_ref[:, 2 * D_MODEL:].astype(F32) * y_c)
    out_ref[...] = x_ref[...] + jnp.dot(merged.astype(BF16), wo_ref[...],
                                        preferred_element_type=F32)


def _merge(x, f, b, c, gates, wa, wb, wc, wo):
    def tile(w):
        return pl.BlockSpec((MERGE_TM, w), lambda i: (i, 0))

    return pl.pallas_call(
        _merge_kernel,
        grid=(TOKENS // MERGE_TM,),
        in_specs=[tile(D_MODEL), tile(FOURIER_WIDTH), tile(DIL_OUT_WIDTH), tile(DIFF_V_WIDTH),
                  tile(GATE_WIDTH), _resident(wa.shape), _resident(wb.shape),
                  _resident(wc.shape), _resident(wo.shape)],
        out_specs=tile(D_MODEL),
        out_shape=jax.ShapeDtypeStruct((TOKENS, D_MODEL), F32),
        compiler_params=_params("parallel"),
        name="gated_merge",
    )(x, f, b, c, gates, wa, wb, wc, wo)


def kernel(x, g_ffn1, w_ffn1_gate, w_ffn1_up, w_ffn1_down, g_mix, w_in, w_gate, b_gate,
           w_br_a, w_br_b, w_br_c, w_out, lam_q1, lam_k1, lam_q2, lam_k2, subln_g,
           rel_bias, g_ffn2, w_ffn2_gate, w_ffn2_up, w_ffn2_down, g_final):
    assert x.shape == (BATCH, SEQ, D_MODEL) and x.dtype == F32
    bf = lambda w: w.astype(BF16)
    c0 = FOURIER_WIDTH
    c3 = c0 + 3 * DIL_WIDTH

    dil_bias = [
        _bias_tiles(rel_bias, nheads=2, head0=2 * g, rows=DIL_QB, cols=DIL_KW,
                    col0=-DIL_RADIUS, dil=dil, band=DIL_RADIUS)
        for g, (_, dil) in enumerate(DIL_PAIRS)]
    diff_strip = _bias_tiles(rel_bias, nheads=DIFF_HEADS, head0=DIL_HEADS, rows=DIFF_TQ,
                             cols=DIFF_STRIP, col0=-(SEQ - DIFF_TQ), dil=1, band=None)
    m_even, m_odd = _dft_matrices()
    dc, ds = _channel_dft()

    def dil_weights(w_in_l, g):
        lo = c0 + g * LANES
        q, k, v = (w_in_l[:, lo + i * DIL_WIDTH: lo + i * DIL_WIDTH + LANES] for i in range(3))
        return bf(jnp.concatenate([q * QK_SCALE, k, v], axis=1))

    q_scale_c = jnp.concatenate([jnp.full((DIFF_QK_WIDTH,), QK_SCALE, F32),
                                 jnp.ones((DIFF_QK_WIDTH + DIFF_V_WIDTH,), F32)])

    h = x.reshape(TOKENS, D_MODEL)
    for l in range(DEPTH):
        h = _ffn(h, g_ffn1[l], bf(w_ffn1_gate[l]), bf(w_ffn1_up[l]), bf(w_ffn1_down[l]))

        wf = _fold_channel_dft(w_in[l, :, :c0], dc, ds)
        pf, pb0, pb1, pb2, pc, gates = _proj(
            h, g_mix[l], wf, [dil_weights(w_in[l], g) for g in range(DIL_GROUPS)],
            bf(w_in[l, :, c3:] * q_scale_c), bf(w_gate[l]), b_gate[l])

        f = _fourier(pf, m_even, m_odd).reshape(TOKENS, FOURIER_WIDTH)
        mixed = _dilated_mixture(pb0.reshape(BATCH, SEQ, DIL_QKV), pb1, pb2, dil_bias)
        lam_init = 0.8 - 0.6 * math.exp(-0.3 * l)
        c = _diff_attention(pc, diff_strip, lam_q1[l], lam_k1[l], lam_q2[l], lam_k2[l],
                            subln_g[l], lam_init)

        h = _merge(h, f, mixed.reshape(TOKENS, DIL_OUT_WIDTH), c, gates,
                   bf(w_br_a[l]), bf(w_br_b[l]), bf(w_br_c[l]), bf(w_out[l]))

        h = _ffn(h, g_ffn2[l], bf(w_ffn2_gate[l]), bf(w_ffn2_up[l]), bf(w_ffn2_down[l]),
                 g_final if l == DEPTH - 1 else None)
    return h.reshape(BATCH, SEQ, D_MODEL)
```

```python
import functools
import math

import numpy as np
import jax
import jax.numpy as jnp
from jax import lax
from jax.experimental import pallas as pl
from jax.experimental.pallas import tpu as pltpu

F32 = jnp.float32
BF16 = jnp.bfloat16

D_MODEL = 1024
BATCH = 8
SEQ = 2048
DEPTH = 2
TOKENS = BATCH * SEQ
D_FF = 2816
EPS = 1e-6
FOURIER_GROUPS = 8
FOURIER_GROUP_DIM = 128
FOURIER_WIDTH = FOURIER_GROUPS * FOURIER_GROUP_DIM
DIL_PAIRS = ((128, 1), (512, 4), (2048, 16))
DIL_GROUPS = len(DIL_PAIRS)
DIL_HEAD_DIM = 64
DIL_HEADS = 2 * DIL_GROUPS
DIL_WIDTH = DIL_HEADS * DIL_HEAD_DIM
DIL_OUT_WIDTH = 2 * DIL_HEAD_DIM
DIL_RADIUS = 64
DIFF_HEADS = 4
DIFF_HEAD_DIM = 64
DIFF_QK_WIDTH = DIFF_HEADS * 2 * DIFF_HEAD_DIM
DIFF_V_DIM = 2 * DIFF_HEAD_DIM
DIFF_V_WIDTH = DIFF_HEADS * DIFF_V_DIM
SUBLN_EPS = 1e-5
NUM_BUCKETS = 32
MAX_DISTANCE = 1024
N_BIAS_HEADS = DIL_HEADS + DIFF_HEADS
NEG_INF = -1e30
QK_SCALE = DIL_HEAD_DIM ** -0.5
LOG2_E = math.log2(math.e)
C_WIDTH = 2 * DIFF_QK_WIDTH + DIFF_V_WIDTH
GATE_WIDTH = 3 * D_MODEL

LANES = 128
VMEM_BYTES_V7X = 64 * 1024 * 1024
VMEM_LIMIT = 56 * 1024 * 1024

FFN_TM = 512
FF_CHUNK = 256
PROJ_TM = 512
PROJ_CHUNK = 512
MERGE_TM = 512
HALF_SEQ = SEQ // 2
FOURIER_CN = 256
DIL_QKV = 3 * LANES
DIL_QB = 128
DIL_KW = DIL_QB + 2 * DIL_RADIUS
DIL_UNROLL = 4
DIFF_TQ = 256
DIFF_STRIP = 2 * SEQ - DIFF_TQ
DIFF_KT = 512


def _params(*semantics):
    return pltpu.CompilerParams(dimension_semantics=semantics, vmem_limit_bytes=VMEM_LIMIT)


def _resident(shape):
    nd = len(shape)
    return pl.BlockSpec(shape, lambda *_: (0,) * nd, pipeline_mode=pl.Buffered(1))


def _rms(x, g, eps):
    return x * lax.rsqrt(jnp.mean(x * x, axis=-1, keepdims=True) + eps) * g


def _bucket_breakpoints():
    rel = np.arange(-(SEQ - 1), SEQ)
    half = NUM_BUCKETS // 2
    max_exact = half // 2
    n = np.abs(rel)
    nf = np.maximum(n, 1).astype(np.float64)
    large = max_exact + (np.log(nf / max_exact) / math.log(MAX_DISTANCE / max_exact)
                         * (half - max_exact)).astype(np.int32)
    large = np.minimum(large, half - 1)
    b = np.where(rel > 0, half, 0) + np.where(n < max_exact, n, large)
    pts = [(int(rel[i]), int(b[i])) for i in range(1, len(rel)) if b[i] != b[i - 1]]
    return int(b[0]), pts


def _bias_kernel(tbl_ref, o_ref, *, head0, col0, dil, band, scale):
    head = head0 + pl.program_id(0)
    rows, cols = o_ref.shape[1], o_ref.shape[2]
    rel_t = (lax.broadcasted_iota(jnp.int32, (rows, cols), 1)
             - lax.broadcasted_iota(jnp.int32, (rows, cols), 0) + col0)
    rel = rel_t * dil
    first, pts = _bucket_breakpoints()
    val = jnp.full((rows, cols), tbl_ref[first, head], F32)
    for thr, bucket in pts:
        val = jnp.where(rel >= thr, tbl_ref[bucket, head], val)
    if scale != 1.0:
        val = val * scale
    if band is not None:
        val = jnp.where(jnp.abs(rel_t) <= band, val, NEG_INF)
    o_ref[0] = val


def _bias_tiles(rel_bias, *, nheads, head0, rows, cols, col0, dil, band, scale=1.0):
    return pl.pallas_call(
        functools.partial(_bias_kernel, head0=head0, col0=col0, dil=dil, band=band, scale=scale),
        grid=(nheads,),
        in_specs=[pl.BlockSpec(memory_space=pltpu.SMEM)],
        out_specs=pl.BlockSpec((1, rows, cols), lambda h: (h, 0, 0)),
        out_shape=jax.ShapeDtypeStruct((nheads, rows, cols), F32),
        compiler_params=_params("arbitrary"),
        name="rel_bias_tiles",
    )(rel_bias)


def _fold_kernel(w_ref, dc_ref, ds_ref, o_ref):
    w = w_ref[...]
    o_ref[0] = jnp.dot(w, dc_ref[...], preferred_element_type=F32,
                       precision=lax.Precision.HIGHEST).astype(BF16)
    o_ref[1] = jnp.dot(w, ds_ref[...], preferred_element_type=F32,
                       precision=lax.Precision.HIGHEST).astype(BF16)


def _fold_channel_dft(w_a, dc, ds):
    gd = FOURIER_GROUP_DIM
    mat = pl.BlockSpec((gd, gd), lambda g: (0, 0))
    out = pl.pallas_call(
        _fold_kernel,
        grid=(FOURIER_GROUPS,),
        in_specs=[pl.BlockSpec((D_MODEL, gd), lambda g: (0, g)), mat, mat],
        out_specs=pl.BlockSpec((2, D_MODEL, gd), lambda g: (0, 0, g)),
        out_shape=jax.ShapeDtypeStruct((2, D_MODEL, FOURIER_WIDTH), BF16),
        compiler_params=_params("arbitrary"),
        name="fold_channel_dft",
    )(w_a, dc, ds)
    return jnp.concatenate([out[0], out[1]], axis=1)


def _ffn_kernel(x_ref, g_ref, wg_ref, wu_ref, wd_ref, *rest, final_norm):
    if final_norm:
        gf_ref, o_ref, a_ref = rest
    else:
        o_ref, a_ref = rest
    x = x_ref[...]
    h = _rms(x, g_ref[...], EPS).astype(BF16)
    for c in range(D_FF // FF_CHUNK):
        sl = slice(c * FF_CHUNK, (c + 1) * FF_CHUNK)
        gate = jnp.dot(h, wg_ref[:, sl], preferred_element_type=F32)
        up = jnp.dot(h, wu_ref[:, sl], preferred_element_type=F32)
        a_ref[:, sl] = (gate * jax.nn.sigmoid(gate) * up).astype(BF16)
    y = x + 0.5 * jnp.dot(a_ref[...], wd_ref[...], preferred_element_type=F32)
    if final_norm:
        y = _rms(y, gf_ref[...], EPS)
    o_ref[...] = y


def _ffn(x, g, wg, wu, wd, g_final=None):
    final_norm = g_final is not None
    tile = pl.BlockSpec((FFN_TM, D_MODEL), lambda i: (i, 0))
    in_specs = [tile, _resident((1, D_MODEL)), _resident((D_MODEL, D_FF)),
                _resident((D_MODEL, D_FF)), _resident((D_FF, D_MODEL))]
    args = [x, g.reshape(1, D_MODEL), wg, wu, wd]
    if final_norm:
        in_specs.append(_resident((1, D_MODEL)))
        args.append(g_final.reshape(1, D_MODEL))
    return pl.pallas_call(
        functools.partial(_ffn_kernel, final_norm=final_norm),
        grid=(TOKENS // FFN_TM,),
        in_specs=in_specs,
        out_specs=tile,
        out_shape=jax.ShapeDtypeStruct((TOKENS, D_MODEL), F32),
        scratch_shapes=[pltpu.VMEM((FFN_TM, D_FF), BF16)],
        compiler_params=_params("parallel"),
        name="ffn_final" if final_norm else "ffn",
    )(*args)


def _proj_kernel(x_ref, g_ref, wf_ref, wb0_ref, wb1_ref, wb2_ref, wc_ref, wg_ref, bg_ref,
                 pf_ref, pb0_ref, pb1_ref, pb2_ref, pc_ref, gate_ref, y_ref):
    u = _rms(x_ref[...], g_ref[...], EPS).astype(BF16)
    rows = x_ref.shape[0]

    def project(w_ref, store, bias_ref=None):
        width = w_ref.shape[1]
        for c0 in range(0, width, PROJ_CHUNK):
            sl = slice(c0, min(c0 + PROJ_CHUNK, width))
            y = jnp.dot(u, w_ref[:, sl], preferred_element_type=F32)
            if bias_ref is not None:
                y = jax.nn.sigmoid(y + bias_ref[:, sl])
            store(sl, y)

    def natural(o_ref):
        def store(sl, y):
            o_ref[:, sl] = y.astype(o_ref.dtype)
        return store

    def by_residue(o_ref, dil):
        def store(sl, y):
            for j in range((sl.stop - sl.start) // LANES):
                y_ref[j] = y[:, j * LANES:(j + 1) * LANES]
                cols = slice(sl.start + j * LANES, sl.start + (j + 1) * LANES)
                for r in range(dil):
                    o_ref[0, r, :, cols] = y_ref[j, pl.ds(r, rows // dil, stride=dil), :].astype(o_ref.dtype)
        return store

    project(wf_ref, by_residue(pf_ref, 2))
    project(wb0_ref, natural(pb0_ref))
    project(wb1_ref, by_residue(pb1_ref, DIL_PAIRS[1][1]))
    project(wb2_ref, by_residue(pb2_ref, DIL_PAIRS[2][1]))
    project(wc_ref, natural(pc_ref))
    project(wg_ref, natural(gate_ref), bg_ref)


def _proj(x, g, wf, wb, wc, wg, bg):
    tiles_per_seq = SEQ // PROJ_TM

    def tile(w):
        return pl.BlockSpec((PROJ_TM, w), lambda i: (i, 0))

    def residue_tile(dil, w):
        return pl.BlockSpec((1, dil, PROJ_TM // dil, w),
                            lambda i: (i // tiles_per_seq, 0, i % tiles_per_seq, 0))

    def residue_shape(dil, w):
        return jax.ShapeDtypeStruct((BATCH, dil, SEQ // dil, w), BF16)

    d1, d2 = DIL_PAIRS[1][1], DIL_PAIRS[2][1]
    weights = [wf, *wb, wc, wg]
    return pl.pallas_call(
        _proj_kernel,
        grid=(TOKENS // PROJ_TM,),
        in_specs=[tile(D_MODEL), _resident((1, D_MODEL))]
        + [_resident(w.shape) for w in weights] + [_resident((1, GATE_WIDTH))],
        out_specs=[residue_tile(2, 2 * FOURIER_WIDTH), tile(DIL_QKV), residue_tile(d1, DIL_QKV),
                   residue_tile(d2, DIL_QKV), tile(C_WIDTH), tile(GATE_WIDTH)],
        out_shape=[residue_shape(2, 2 * FOURIER_WIDTH),
                   jax.ShapeDtypeStruct((TOKENS, DIL_QKV), BF16),
                   residue_shape(d1, DIL_QKV), residue_shape(d2, DIL_QKV),
                   jax.ShapeDtypeStruct((TOKENS, C_WIDTH), BF16),
                   jax.ShapeDtypeStruct((TOKENS, GATE_WIDTH), BF16)],
        scratch_shapes=[pltpu.VMEM((PROJ_CHUNK // LANES, PROJ_TM, LANES), F32)],
        compiler_params=_params("parallel"),
        name="proj_in",
    )(x, g.reshape(1, D_MODEL), *weights, bg.reshape(1, GATE_WIDTH))


def _fourier_kernel(pc_ref, ps_ref, me_ref, mo_ref, o_ref):
    h = HALF_SEQ
    even = (jnp.dot(me_ref[:, :h], pc_ref[0, 0], preferred_element_type=F32)
            + jnp.dot(me_ref[:, h:], ps_ref[0, 0], preferred_element_type=F32))
    odd = (jnp.dot(mo_ref[:, :h], pc_ref[0, 1], preferred_element_type=F32)
           + jnp.dot(mo_ref[:, h:], ps_ref[0, 1], preferred_element_type=F32))
    o_ref[0, :h, :] = (even + odd).astype(o_ref.dtype)
    o_ref[0, h:, :] = (even - odd).astype(o_ref.dtype)


def _fourier(pf, m_even, m_odd):
    cn = FOURIER_CN
    nblk = FOURIER_WIDTH // cn
    return pl.pallas_call(
        _fourier_kernel,
        grid=(BATCH, nblk),
        in_specs=[pl.BlockSpec((1, 2, HALF_SEQ, cn), lambda b, j: (b, 0, 0, j)),
                  pl.BlockSpec((1, 2, HALF_SEQ, cn), lambda b, j: (b, 0, 0, nblk + j)),
                  _resident((HALF_SEQ, SEQ)), _resident((HALF_SEQ, SEQ))],
        out_specs=pl.BlockSpec((1, SEQ, cn), lambda b, j: (b, 0, j)),
        out_shape=jax.ShapeDtypeStruct((BATCH, SEQ, FOURIER_WIDTH), BF16),
        compiler_params=_params("parallel", "parallel"),
        name="fourier_mix",
    )(pf, pf, m_even, m_odd)


def _dft_matrices():
    half = HALF_SEQ
    k = lax.broadcasted_iota(jnp.int32, (half, half), 0)
    m = lax.broadcasted_iota(jnp.int32, (half, half), 1)
    ang_e = ((k * m) % half).astype(F32) * (2.0 * math.pi / half)
    ang_o = ((k * (2 * m + 1)) % SEQ).astype(F32) * (2.0 * math.pi / SEQ)
    m_even = jnp.concatenate([jnp.cos(ang_e), -jnp.sin(ang_e)], axis=1).astype(BF16)
    m_odd = jnp.concatenate([jnp.cos(ang_o), -jnp.sin(ang_o)], axis=1).astype(BF16)
    return m_even, m_odd


def _channel_dft():
    gd = FOURIER_GROUP_DIM
    idx = np.arange(gd)
    ang = 2.0 * np.pi * ((idx[:, None] * idx[None, :]) % gd) / gd
    scale = 1.0 / math.sqrt(SEQ * gd)
    return (jnp.asarray(np.cos(ang) * scale, F32), jnp.asarray(np.sin(ang) * scale, F32))


def _dil_block(q, kw, vw, bias_ref, key_pos0, sub_len, head0_lanes):
    key_pos = lax.broadcasted_iota(jnp.int32, (DIL_QB, DIL_KW), 1) + key_pos0
    in_seq = (key_pos >= 0) & (key_pos < sub_len)
    zero = jnp.zeros_like(q)
    pv, lse = [], []
    for hh in range(2):
        q_h = jnp.where(head0_lanes, q, zero) if hh == 0 else jnp.where(head0_lanes, zero, q)
        logits = lax.dot_general(q_h, kw, (((1,), (1,)), ((), ())), preferred_element_type=F32)
        logits = jnp.where(in_seq, logits + bias_ref[hh], NEG_INF)
        mx = jnp.max(logits, axis=-1, keepdims=True)
        e = jnp.exp(logits - mx)
        s = jnp.sum(e, axis=-1, keepdims=True)
        pv.append(jnp.dot(e.astype(BF16), vw, preferred_element_type=F32) / s)
        lse.append(mx + jnp.log(s))
    return jnp.where(head0_lanes, pv[0], pv[1]), jnp.where(head0_lanes, lse[0], lse[1])


def _dil_kernel(g0_ref, g1_ref, g2_ref, b0_ref, b1_ref, b2_ref, out_ref,
                kpad_ref, vpad_ref, o_ref, lse_ref):
    pad = DIL_RADIUS
    head0_lanes = lax.broadcasted_iota(jnp.int32, (DIL_QB, LANES), 1) < DIL_HEAD_DIM
    zeros = jnp.zeros((pad, LANES), BF16)
    q_sl, k_sl, v_sl = (slice(i * LANES, (i + 1) * LANES) for i in range(3))

    def fill_padded(src, sub_len):
        for ref, sl in ((kpad_ref, k_sl), (vpad_ref, v_sl)):
            ref[:pad, :] = zeros
            ref[pl.ds(pad, sub_len), :] = src[:, sl]
            ref[pl.ds(pad + sub_len, pad), :] = zeros

    def run_blocks(group, q_src, bias_ref, sub_len, dil, residue, first_block, count):
        for n in range(count):
            r0 = pl.multiple_of((first_block + n) * DIL_QB, DIL_QB)
            o, lse = _dil_block(q_src[pl.ds(r0, DIL_QB), q_sl], kpad_ref[pl.ds(r0, DIL_KW), :],
                                vpad_ref[pl.ds(r0, DIL_KW), :], bias_ref, r0 - pad, sub_len,
                                head0_lanes)
            rows = pl.ds(residue + r0 * dil, DIL_QB, stride=dil) if dil > 1 else pl.ds(r0, DIL_QB)
            o_ref[group, rows, :] = o
            lse_ref[group, rows, :] = lse

    fill_padded(g0_ref.at[0], SEQ)

    def g0_body(i, carry):
        run_blocks(0, g0_ref.at[0], b0_ref, SEQ, 1, 0, i * DIL_UNROLL, DIL_UNROLL)
        return carry

    lax.fori_loop(0, SEQ // (DIL_QB * DIL_UNROLL), g0_body, 0)

    for group, g_ref, bias_ref in ((1, g1_ref, b1_ref), (2, g2_ref, b2_ref)):
        dil = DIL_PAIRS[group][1]
        sub_len = SEQ // dil

        def body(r, carry, group=group, g_ref=g_ref, bias_ref=bias_ref, dil=dil, sub_len=sub_len):
            src = g_ref.at[0, r]
            fill_padded(src, sub_len)
            run_blocks(group, src, bias_ref, sub_len, dil, r, 0, sub_len // DIL_QB)
            return carry

        lax.fori_loop(0, dil, body, 0)

    chunk = 256
    for c in range(SEQ // chunk):
        rows = slice(c * chunk, (c + 1) * chunk)
        l0, l1, l2 = lse_ref[0, rows, :], lse_ref[1, rows, :], lse_ref[2, rows, :]
        mx = jnp.maximum(jnp.maximum(l0, l1), l2)
        e0, e1, e2 = jnp.exp(l0 - mx), jnp.exp(l1 - mx), jnp.exp(l2 - mx)
        mixed = (e0 * o_ref[0, rows, :] + e1 * o_ref[1, rows, :] + e2 * o_ref[2, rows, :]) / (e0 + e1 + e2)
        out_ref[0, rows, :] = mixed.astype(out_ref.dtype)


def _dilated_mixture(pb0, pb1, pb2, biases):
    d1, d2 = DIL_PAIRS[1][1], DIL_PAIRS[2][1]
    bias_spec = _resident((2, DIL_QB, DIL_KW))
    max_len = SEQ + 2 * DIL_RADIUS
    return pl.pallas_call(
        _dil_kernel,
        grid=(BATCH,),
        in_specs=[pl.BlockSpec((1, SEQ, DIL_QKV), lambda b: (b, 0, 0)),
                  pl.BlockSpec((1, d1, SEQ // d1, DIL_QKV), lambda b: (b, 0, 0, 0)),
                  pl.BlockSpec((1, d2, SEQ // d2, DIL_QKV), lambda b: (b, 0, 0, 0)),
                  bias_spec, bias_spec, bias_spec],
        out_specs=pl.BlockSpec((1, SEQ, LANES), lambda b: (b, 0, 0)),
        out_shape=jax.ShapeDtypeStruct((BATCH, SEQ, DIL_OUT_WIDTH), BF16),
        scratch_shapes=[pltpu.VMEM((max_len, LANES), BF16), pltpu.VMEM((max_len, LANES), BF16),
                        pltpu.VMEM((DIL_GROUPS, SEQ, LANES), F32),
                        pltpu.VMEM((DIL_GROUPS, SEQ, LANES), F32)],
        compiler_params=_params("parallel"),
        name="dilated_attn",
    )(pb0, pb1, pb2, *biases)


def _diff_kernel(q_ref, k_ref, v_ref, bias_ref, lq1_ref, lk1_ref, lq2_ref, lk2_ref, sg_ref,
                 o_ref, vaug_ref, logit_ref, *, lam_init):
    j = pl.program_id(2)
    nq = pl.num_programs(2)

    @pl.when(j == 0)
    def _():
        vaug_ref[:, :LANES] = v_ref[0]
        vaug_ref[:, LANES:] = jnp.ones((SEQ, LANES), BF16)

    off = pl.multiple_of((nq - 1 - j) * DIFF_TQ, DIFF_TQ)
    lam = (jnp.exp(jnp.sum(lq1_ref[...] * lk1_ref[...], axis=-1, keepdims=True))
           - jnp.exp(jnp.sum(lq2_ref[...] * lk2_ref[...], axis=-1, keepdims=True))
           + lam_init)
    q = q_ref[0]
    first_half = lax.broadcasted_iota(jnp.int32, q.shape, 1) < DIFF_HEAD_DIM
    zero = jnp.zeros_like(q)
    n_tiles = SEQ // DIFF_KT

    def key_tile(t):
        return slice(t * DIFF_KT, (t + 1) * DIFF_KT)

    row_max = []
    for m in range(2):
        q_m = jnp.where(first_half, q, zero) if m == 0 else jnp.where(first_half, zero, q)
        lane_max = None
        for t in range(n_tiles):
            bias = bias_ref[0, :, pl.ds(pl.multiple_of(off + t * DIFF_KT, LANES), DIFF_KT)]
            logits = lax.dot_general(q_m, k_ref[0, key_tile(t), :], (((1,), (1,)), ((), ())),
                                     preferred_element_type=F32) + bias
            logit_ref[m, :, key_tile(t)] = logits
            for c in range(DIFF_KT // LANES):
                part = logits[:, c * LANES:(c + 1) * LANES]
                lane_max = part if lane_max is None else jnp.maximum(lane_max, part)
        row_max.append(jnp.max(lane_max, axis=-1, keepdims=True))

    outs = []
    for m in range(2):
        pv = jnp.zeros((DIFF_TQ, 2 * LANES), F32)
        for t in range(n_tiles):
            e = jnp.exp2(logit_ref[m, :, key_tile(t)] - row_max[m]).astype(BF16)
            pv = pv + jnp.dot(e, vaug_ref[key_tile(t), :], preferred_element_type=F32)
        outs.append(pv[:, :LANES] / pv[:, LANES:])
    o = outs[0] - lam * outs[1]
    o = _rms(o, sg_ref[...], SUBLN_EPS) * (1.0 - lam_init)
    o_ref[0] = o.astype(o_ref.dtype)


def _diff_attention(pc, strip, lq1, lk1, lq2, lk2, subln_g, lam_init):
    pcv = pc.reshape(BATCH, SEQ, C_WIDTH)
    vec = _resident((1, DIFF_HEAD_DIM))
    return pl.pallas_call(
        functools.partial(_diff_kernel, lam_init=lam_init),
        grid=(DIFF_HEADS, BATCH, SEQ // DIFF_TQ),
        in_specs=[pl.BlockSpec((1, DIFF_TQ, LANES), lambda h, b, j: (b, j, h)),
                  pl.BlockSpec((1, SEQ, LANES), lambda h, b, j: (b, 0, DIFF_HEADS + h)),
                  pl.BlockSpec((1, SEQ, LANES), lambda h, b, j: (b, 0, 2 * DIFF_HEADS + h)),
                  pl.BlockSpec((1, DIFF_TQ, DIFF_STRIP), lambda h, b, j: (h, 0, 0)),
                  vec, vec, vec, vec, _resident((1, DIFF_V_DIM))],
        out_specs=pl.BlockSpec((1, DIFF_TQ, LANES), lambda h, b, j: (b, j, h)),
        out_shape=jax.ShapeDtypeStruct((BATCH, SEQ, DIFF_V_WIDTH), BF16),
        scratch_shapes=[pltpu.VMEM((SEQ, 2 * LANES), BF16),
                        pltpu.VMEM((2, DIFF_TQ, SEQ), F32)],
        compiler_params=_params("parallel", "parallel", "arbitrary"),
        name="diff_attn",
    )(pcv, pcv, pcv, strip, lq1.reshape(1, -1), lk1.reshape(1, -1), lq2.reshape(1, -1),
      lk2.reshape(1, -1), subln_g.reshape(1, -1)).reshape(TOKENS, DIFF_V_WIDTH)


def _merge_kernel(x_ref, f_ref, b_ref, c_ref, gate_ref, wa_ref, wb_ref, wc_ref, wo_ref, out_ref):
    y_a = jnp.dot(f_ref[...], wa_ref[...], preferred_element_type=F32)
    y_b = jnp.dot(b_ref[...], wb_ref[...], preferred_element_type=F32)
    y_c = jnp.dot(c_ref[...], wc_ref[...], preferred_element_type=F32)
    merged = (gate_ref[:, :D_MODEL].astype(F32) * y_a
              + gate_ref[:, D_MODEL:2 * D_MODEL].astype(F32) * y_b
              + gate_ref[:, 2 * D_MODEL:].astype(F32) * y_c)
    out_ref[...] = x_ref[...] + jnp.dot(merged.astype(BF16), wo_ref[...],
                                        preferred_element_type=F32)


def _merge(x, f, b, c, gates, wa, wb, wc, wo):
    def tile(w):
        return pl.BlockSpec((MERGE_TM, w), lambda i: (i, 0))

    return pl.pallas_call(
        _merge_kernel,
        grid=(TOKENS // MERGE_TM,),
        in_specs=[tile(D_MODEL), tile(FOURIER_WIDTH), tile(DIL_OUT_WIDTH), tile(DIFF_V_WIDTH),
                  tile(GATE_WIDTH), _resident(wa.shape), _resident(wb.shape),
                  _resident(wc.shape), _resident(wo.shape)],
        out_specs=tile(D_MODEL),
        out_shape=jax.ShapeDtypeStruct((TOKENS, D_MODEL), F32),
        compiler_params=_params("parallel"),
        name="gated_merge",
    )(x, f, b, c, gates, wa, wb, wc, wo)


def kernel(x, g_ffn1, w_ffn1_gate, w_ffn1_up, w_ffn1_down, g_mix, w_in, w_gate, b_gate,
           w_br_a, w_br_b, w_br_c, w_out, lam_q1, lam_k1, lam_q2, lam_k2, subln_g,
           rel_bias, g_ffn2, w_ffn2_gate, w_ffn2_up, w_ffn2_down, g_final):
    assert x.shape == (BATCH, SEQ, D_MODEL) and x.dtype == F32
    bf = lambda w: w.astype(BF16)
    c0 = FOURIER_WIDTH
    c3 = c0 + 3 * DIL_WIDTH

    dil_bias = [
        _bias_tiles(rel_bias, nheads=2, head0=2 * g, rows=DIL_QB, cols=DIL_KW,
                    col0=-DIL_RADIUS, dil=dil, band=DIL_RADIUS)
        for g, (_, dil) in enumerate(DIL_PAIRS)]
    diff_strip = _bias_tiles(rel_bias, nheads=DIFF_HEADS, head0=DIL_HEADS, rows=DIFF_TQ,
                             cols=DIFF_STRIP, col0=-(SEQ - DIFF_TQ), dil=1, band=None,
                             scale=LOG2_E)
    m_even, m_odd = _dft_matrices()
    dc, ds = _channel_dft()

    def dil_weights(w_in_l, g):
        lo = c0 + g * LANES
        q, k, v = (w_in_l[:, lo + i * DIL_WIDTH: lo + i * DIL_WIDTH + LANES] for i in range(3))
        return bf(jnp.concatenate([q * QK_SCALE, k, v], axis=1))

    q_scale_c = jnp.concatenate([jnp.full((DIFF_QK_WIDTH,), QK_SCALE * LOG2_E, F32),
                                 jnp.ones((DIFF_QK_WIDTH + DIFF_V_WIDTH,), F32)])

    h = x.reshape(TOKENS, D_MODEL)
    for l in range(DEPTH):
        h = _ffn(h, g_ffn1[l], bf(w_ffn1_gate[l]), bf(w_ffn1_up[l]), bf(w_ffn1_down[l]))

        wf = _fold_channel_dft(w_in[l, :, :c0], dc, ds)
        pf, pb0, pb1, pb2, pc, gates = _proj(
            h, g_mix[l], wf, [dil_weights(w_in[l], g) for g in range(DIL_GROUPS)],
            bf(w_in[l, :, c3:] * q_scale_c), bf(w_gate[l]), b_gate[l])

        f = _fourier(pf, m_even, m_odd).reshape(TOKENS, FOURIER_WIDTH)
        mixed = _dilated_mixture(pb0.reshape(BATCH, SEQ, DIL_QKV), pb1, pb2, dil_bias)
        lam_init = 0.8 - 0.6 * math.exp(-0.3 * l)
        c = _diff_attention(pc, diff_strip, lam_q1[l], lam_k1[l], lam_q2[l], lam_k2[l],
                            subln_g[l], lam_init)

        h = _merge(h, f, mixed.reshape(TOKENS, DIL_OUT_WIDTH), c, gates,
                   bf(w_br_a[l]), bf(w_br_b[l]), bf(w_br_c[l]), bf(w_out[l]))

        h = _ffn(h, g_ffn2[l], bf(w_ffn2_gate[l]), bf(w_ffn2_up[l]), bf(w_ffn2_down[l]),
                 g_final if l == DEPTH - 1 else None)
    return h.reshape(BATCH, SEQ, D_MODEL)
```

```python
import functools
import math

import numpy as np
import jax
import jax.numpy as jnp
from jax import lax
from jax.experimental import pallas as pl
from jax.experimental.pallas import tpu as pltpu

F32 = jnp.float32
BF16 = jnp.bfloat16

D_MODEL = 1024
BATCH = 8
SEQ = 2048
DEPTH = 2
TOKENS = BATCH * SEQ
D_FF = 2816
EPS = 1e-6
FOURIER_GROUPS = 8
FOURIER_GROUP_DIM = 128
FOURIER_WIDTH = FOURIER_GROUPS * FOURIER_GROUP_DIM
DIL_PAIRS = ((128, 1), (512, 4), (2048, 16))
DIL_GROUPS = len(DIL_PAIRS)
DIL_HEAD_DIM = 64
DIL_HEADS = 2 * DIL_GROUPS
DIL_WIDTH = DIL_HEADS * DIL_HEAD_DIM
DIL_OUT_WIDTH = 2 * DIL_HEAD_DIM
DIL_RADIUS = 64
DIFF_HEADS = 4
DIFF_HEAD_DIM = 64
DIFF_QK_WIDTH = DIFF_HEADS * 2 * DIFF_HEAD_DIM
DIFF_V_DIM = 2 * DIFF_HEAD_DIM
DIFF_V_WIDTH = DIFF_HEADS * DIFF_V_DIM
SUBLN_EPS = 1e-5
NUM_BUCKETS = 32
MAX_DISTANCE = 1024
N_BIAS_HEADS = DIL_HEADS + DIFF_HEADS
NEG_INF = -1e30
QK_SCALE = DIL_HEAD_DIM ** -0.5
LOG2_E = math.log2(math.e)
C_WIDTH = 2 * DIFF_QK_WIDTH + DIFF_V_WIDTH
GATE_WIDTH = 3 * D_MODEL

LANES = 128
VMEM_BYTES_V7X = 64 * 1024 * 1024
VMEM_LIMIT = 56 * 1024 * 1024

FFN_TM = 512
FF_CHUNK = 256
PROJ_TM = 512
PROJ_CHUNK = 512
MERGE_TM = 512
HALF_SEQ = SEQ // 2
FOURIER_CN = 256
DIL_QKV = 3 * LANES
DIL_QB = 128
DIL_KW = DIL_QB + 2 * DIL_RADIUS
DIL_UNROLL = 4
DIFF_TQ = 256
DIFF_STRIP = 2 * SEQ - DIFF_TQ
DIFF_KT = 512


def _params(*semantics):
    return pltpu.CompilerParams(dimension_semantics=semantics, vmem_limit_bytes=VMEM_LIMIT)


def _resident(shape):
    nd = len(shape)
    return pl.BlockSpec(shape, lambda *_: (0,) * nd, pipeline_mode=pl.Buffered(1))


def _rms(x, g, eps):
    return x * lax.rsqrt(jnp.mean(x * x, axis=-1, keepdims=True) + eps) * g


def _bucket_breakpoints():
    rel = np.arange(-(SEQ - 1), SEQ)
    half = NUM_BUCKETS // 2
    max_exact = half // 2
    n = np.abs(rel)
    nf = np.maximum(n, 1).astype(np.float64)
    large = max_exact + (np.log(nf / max_exact) / math.log(MAX_DISTANCE / max_exact)
                         * (half - max_exact)).astype(np.int32)
    large = np.minimum(large, half - 1)
    b = np.where(rel > 0, half, 0) + np.where(n < max_exact, n, large)
    pts = [(int(rel[i]), int(b[i])) for i in range(1, len(rel)) if b[i] != b[i - 1]]
    return int(b[0]), pts


def _bias_values(tbl_ref, head, rel):
    first, pts = _bucket_breakpoints()
    val = jnp.full(rel.shape, tbl_ref[first, head], F32)
    for thr, bucket in pts:
        val = jnp.where(rel >= thr, tbl_ref[bucket, head], val)
    return val * LOG2_E


def _diff_bias_kernel(tbl_ref, o_ref):
    rows, cols = o_ref.shape[1], o_ref.shape[2]
    rel = (lax.broadcasted_iota(jnp.int32, (rows, cols), 1)
           - lax.broadcasted_iota(jnp.int32, (rows, cols), 0) - (SEQ - rows))
    o_ref[0] = _bias_values(tbl_ref, DIL_HEADS + pl.program_id(0), rel)


def _diff_bias_strip(rel_bias):
    return pl.pallas_call(
        _diff_bias_kernel,
        grid=(DIFF_HEADS,),
        in_specs=[pl.BlockSpec(memory_space=pltpu.SMEM)],
        out_specs=pl.BlockSpec((1, DIFF_TQ, DIFF_STRIP), lambda h: (h, 0, 0)),
        out_shape=jax.ShapeDtypeStruct((DIFF_HEADS, DIFF_TQ, DIFF_STRIP), F32),
        compiler_params=_params("arbitrary"),
        name="diff_bias_strip",
    )(rel_bias)


EDGE_FIRST, EDGE_NONE, EDGE_LAST, EDGE_BOTH = range(4)


def _dil_bias_kernel(tbl_ref, o_ref, *, head0, dil):
    rows, cols = o_ref.shape[2], o_ref.shape[3]
    col = lax.broadcasted_iota(jnp.int32, (rows, cols), 1)
    rel_t = col - lax.broadcasted_iota(jnp.int32, (rows, cols), 0) - DIL_RADIUS
    val = _bias_values(tbl_ref, head0 + pl.program_id(0), rel_t * dil)
    val = jnp.where(jnp.abs(rel_t) <= DIL_RADIUS, val, NEG_INF)
    lo = jnp.where(col >= DIL_RADIUS, val, NEG_INF)
    o_ref[EDGE_FIRST, 0] = lo
    o_ref[EDGE_NONE, 0] = val
    o_ref[EDGE_LAST, 0] = jnp.where(col < cols - DIL_RADIUS, val, NEG_INF)
    o_ref[EDGE_BOTH, 0] = jnp.where(col < cols - DIL_RADIUS, lo, NEG_INF)


def _dil_bias_tiles(rel_bias, group, dil):
    return pl.pallas_call(
        functools.partial(_dil_bias_kernel, head0=2 * group, dil=dil),
        grid=(2,),
        in_specs=[pl.BlockSpec(memory_space=pltpu.SMEM)],
        out_specs=pl.BlockSpec((4, 1, DIL_QB, DIL_KW), lambda h: (0, h, 0, 0)),
        out_shape=jax.ShapeDtypeStruct((4, 2, DIL_QB, DIL_KW), F32),
        compiler_params=_params("arbitrary"),
        name="dil_bias_tiles",
    )(rel_bias)


def _fold_kernel(w_ref, dc_ref, ds_ref, o_ref):
    w = w_ref[...]
    o_ref[0] = jnp.dot(w, dc_ref[...], preferred_element_type=F32,
                       precision=lax.Precision.HIGHEST).astype(BF16)
    o_ref[1] = jnp.dot(w, ds_ref[...], preferred_element_type=F32,
                       precision=lax.Precision.HIGHEST).astype(BF16)


def _fold_channel_dft(w_a, dc, ds):
    gd = FOURIER_GROUP_DIM
    mat = pl.BlockSpec((gd, gd), lambda g: (0, 0))
    out = pl.pallas_call(
        _fold_kernel,
        grid=(FOURIER_GROUPS,),
        in_specs=[pl.BlockSpec((D_MODEL, gd), lambda g: (0, g)), mat, mat],
        out_specs=pl.BlockSpec((2, D_MODEL, gd), lambda g: (0, 0, g)),
        out_shape=jax.ShapeDtypeStruct((2, D_MODEL, FOURIER_WIDTH), BF16),
        compiler_params=_params("arbitrary"),
        name="fold_channel_dft",
    )(w_a, dc, ds)
    return jnp.concatenate([out[0], out[1]], axis=1)


def _ffn_kernel(x_ref, g_ref, wg_ref, wu_ref, wd_ref, *rest, final_norm):
    if final_norm:
        gf_ref, o_ref, a_ref = rest
    else:
        o_ref, a_ref = rest
    x = x_ref[...]
    h = _rms(x, g_ref[...], EPS).astype(BF16)
    for c in range(D_FF // FF_CHUNK):
        sl = slice(c * FF_CHUNK, (c + 1) * FF_CHUNK)
        gate = jnp.dot(h, wg_ref[:, sl], preferred_element_type=F32)
        up = jnp.dot(h, wu_ref[:, sl], preferred_element_type=F32)
        a_ref[:, sl] = (gate * jax.nn.sigmoid(gate) * up).astype(BF16)
    y = x + 0.5 * jnp.dot(a_ref[...], wd_ref[...], preferred_element_type=F32)
    if final_norm:
        y = _rms(y, gf_ref[...], EPS)
    o_ref[...] = y


def _ffn(x, g, wg, wu, wd, g_final=None):
    final_norm = g_final is not None
    tile = pl.BlockSpec((FFN_TM, D_MODEL), lambda i: (i, 0))
    in_specs = [tile, _resident((1, D_MODEL)), _resident((D_MODEL, D_FF)),
                _resident((D_MODEL, D_FF)), _resident((D_FF, D_MODEL))]
    args = [x, g.reshape(1, D_MODEL), wg, wu, wd]
    if final_norm:
        in_specs.append(_resident((1, D_MODEL)))
        args.append(g_final.reshape(1, D_MODEL))
    return pl.pallas_call(
        functools.partial(_ffn_kernel, final_norm=final_norm),
        grid=(TOKENS // FFN_TM,),
        in_specs=in_specs,
        out_specs=tile,
        out_shape=jax.ShapeDtypeStruct((TOKENS, D_MODEL), F32),
        scratch_shapes=[pltpu.VMEM((FFN_TM, D_FF), BF16)],
        compiler_params=_params("parallel"),
        name="ffn_final" if final_norm else "ffn",
    )(*args)


def _proj_kernel(x_ref, g_ref, wf_ref, wb0_ref, wb1_ref, wb2_ref, wc_ref, wg_ref, bg_ref,
                 pf_ref, pb0_ref, pb1_ref, pb2_ref, pc_ref, gate_ref, y_ref):
    u = _rms(x_ref[...], g_ref[...], EPS).astype(BF16)
    rows = x_ref.shape[0]

    def project(w_ref, store, bias_ref=None):
        width = w_ref.shape[1]
        for c0 in range(0, width, PROJ_CHUNK):
            sl = slice(c0, min(c0 + PROJ_CHUNK, width))
            y = jnp.dot(u, w_ref[:, sl], preferred_element_type=F32)
            if bias_ref is not None:
                y = jax.nn.sigmoid(y + bias_ref[:, sl])
            store(sl, y)

    def natural(o_ref):
        def store(sl, y):
            o_ref[:, sl] = y.astype(o_ref.dtype)
        return store

    def by_residue(o_ref, dil):
        def store(sl, y):
            for j in range((sl.stop - sl.start) // LANES):
                y_ref[j] = y[:, j * LANES:(j + 1) * LANES]
                cols = slice(sl.start + j * LANES, sl.start + (j + 1) * LANES)
                for r in range(dil):
                    o_ref[0, r, :, cols] = y_ref[j, pl.ds(r, rows // dil, stride=dil), :].astype(o_ref.dtype)
        return store

    project(wf_ref, by_residue(pf_ref, 2))
    project(wb0_ref, natural(pb0_ref))
    project(wb1_ref, by_residue(pb1_ref, DIL_PAIRS[1][1]))
    project(wb2_ref, by_residue(pb2_ref, DIL_PAIRS[2][1]))
    project(wc_ref, natural(pc_ref))
    project(wg_ref, natural(gate_ref), bg_ref)


def _proj(x, g, wf, wb, wc, wg, bg):
    tiles_per_seq = SEQ // PROJ_TM

    def tile(w):
        return pl.BlockSpec((PROJ_TM, w), lambda i: (i, 0))

    def residue_tile(dil, w):
        return pl.BlockSpec((1, dil, PROJ_TM // dil, w),
                            lambda i: (i // tiles_per_seq, 0, i % tiles_per_seq, 0))

    def residue_shape(dil, w):
        return jax.ShapeDtypeStruct((BATCH, dil, SEQ // dil, w), BF16)

    d1, d2 = DIL_PAIRS[1][1], DIL_PAIRS[2][1]
    weights = [wf, *wb, wc, wg]
    return pl.pallas_call(
        _proj_kernel,
        grid=(TOKENS // PROJ_TM,),
        in_specs=[tile(D_MODEL), _resident((1, D_MODEL))]
        + [_resident(w.shape) for w in weights] + [_resident((1, GATE_WIDTH))],
        out_specs=[residue_tile(2, 2 * FOURIER_WIDTH), tile(DIL_QKV), residue_tile(d1, DIL_QKV),
                   residue_tile(d2, DIL_QKV), tile(C_WIDTH), tile(GATE_WIDTH)],
        out_shape=[residue_shape(2, 2 * FOURIER_WIDTH),
                   jax.ShapeDtypeStruct((TOKENS, DIL_QKV), BF16),
                   residue_shape(d1, DIL_QKV), residue_shape(d2, DIL_QKV),
                   jax.ShapeDtypeStruct((TOKENS, C_WIDTH), BF16),
                   jax.ShapeDtypeStruct((TOKENS, GATE_WIDTH), BF16)],
        scratch_shapes=[pltpu.VMEM((PROJ_CHUNK // LANES, PROJ_TM, LANES), F32)],
        compiler_params=_params("parallel"),
        name="proj_in",
    )(x, g.reshape(1, D_MODEL), *weights, bg.reshape(1, GATE_WIDTH))


def _fourier_kernel(pc_ref, ps_ref, me_ref, mo_ref, o_ref):
    h = HALF_SEQ
    even = (jnp.dot(me_ref[:, :h], pc_ref[0, 0], preferred_element_type=F32)
            + jnp.dot(me_ref[:, h:], ps_ref[0, 0], preferred_element_type=F32))
    odd = (jnp.dot(mo_ref[:, :h], pc_ref[0, 1], preferred_element_type=F32)
           + jnp.dot(mo_ref[:, h:], ps_ref[0, 1], preferred_element_type=F32))
    o_ref[0, :h, :] = (even + odd).astype(o_ref.dtype)
    o_ref[0, h:, :] = (even - odd).astype(o_ref.dtype)


def _fourier(pf, m_even, m_odd):
    cn = FOURIER_CN
    nblk = FOURIER_WIDTH // cn
    return pl.pallas_call(
        _fourier_kernel,
        grid=(BATCH, nblk),
        in_specs=[pl.BlockSpec((1, 2, HALF_SEQ, cn), lambda b, j: (b, 0, 0, j)),
                  pl.BlockSpec((1, 2, HALF_SEQ, cn), lambda b, j: (b, 0, 0, nblk + j)),
                  _resident((HALF_SEQ, SEQ)), _resident((HALF_SEQ, SEQ))],
        out_specs=pl.BlockSpec((1, SEQ, cn), lambda b, j: (b, 0, j)),
        out_shape=jax.ShapeDtypeStruct((BATCH, SEQ, FOURIER_WIDTH), BF16),
        compiler_params=_params("parallel", "parallel"),
        name="fourier_mix",
    )(pf, pf, m_even, m_odd)


def _dft_matrices():
    half = HALF_SEQ
    k = lax.broadcasted_iota(jnp.int32, (half, half), 0)
    m = lax.broadcasted_iota(jnp.int32, (half, half), 1)
    ang_e = ((k * m) % half).astype(F32) * (2.0 * math.pi / half)
    ang_o = ((k * (2 * m + 1)) % SEQ).astype(F32) * (2.0 * math.pi / SEQ)
    m_even = jnp.concatenate([jnp.cos(ang_e), -jnp.sin(ang_e)], axis=1).astype(BF16)
    m_odd = jnp.concatenate([jnp.cos(ang_o), -jnp.sin(ang_o)], axis=1).astype(BF16)
    return m_even, m_odd


def _channel_dft():
    gd = FOURIER_GROUP_DIM
    idx = np.arange(gd)
    ang = 2.0 * np.pi * ((idx[:, None] * idx[None, :]) % gd) / gd
    scale = 1.0 / math.sqrt(SEQ * gd)
    return (jnp.asarray(np.cos(ang) * scale, F32), jnp.asarray(np.sin(ang) * scale, F32))


def _dil_block(q, kw, vw_aug, bias, head0_lanes):
    zero = jnp.zeros_like(q)
    pv, lse = [], []
    for hh in range(2):
        q_h = jnp.where(head0_lanes, q, zero) if hh == 0 else jnp.where(head0_lanes, zero, q)
        logits = lax.dot_general(q_h, kw, (((1,), (1,)), ((), ())),
                                 preferred_element_type=F32) + bias[hh]
        mx = jnp.max(logits, axis=-1, keepdims=True)
        e = jnp.exp2(logits - mx).astype(BF16)
        r = jnp.dot(e, vw_aug, preferred_element_type=F32)
        s = r[:, LANES:]
        pv.append(r[:, :LANES] / s)
        lse.append(mx + jnp.log2(s))
    return jnp.where(head0_lanes, pv[0], pv[1]), jnp.where(head0_lanes, lse[0], lse[1])


def _dil_kernel(g0_ref, g1_ref, g2_ref, b0_ref, b1_ref, b2_ref, out_ref,
                kpad_ref, vpad_ref, o_ref, lse_ref):
    pad = DIL_RADIUS
    head0_lanes = lax.broadcasted_iota(jnp.int32, (DIL_QB, LANES), 1) < DIL_HEAD_DIM
    zeros_k = jnp.zeros((pad, LANES), BF16)
    zeros_v = jnp.zeros((pad, 2 * LANES), BF16)
    q_sl, k_sl, v_sl = (slice(i * LANES, (i + 1) * LANES) for i in range(3))

    vpad_ref[pl.ds(pad, SEQ), LANES:] = jnp.ones((SEQ, LANES), BF16)

    def fill_padded(src, sub_len):
        kpad_ref[:pad, :] = zeros_k
        kpad_ref[pl.ds(pad, sub_len), :] = src[:, k_sl]
        kpad_ref[pl.ds(pad + sub_len, pad), :] = zeros_k
        vpad_ref[:pad, :] = zeros_v
        vpad_ref[pl.ds(pad, sub_len), :LANES] = src[:, v_sl]
        vpad_ref[pl.ds(pad + sub_len, pad), :] = zeros_v

    def store(group, rows, o, lse):
        o_ref[group, rows, :] = o
        lse_ref[group, rows, :] = lse

    def padded_block(group, q_src, bias_ref, blk, edge, dil, residue):
        r0 = pl.multiple_of(blk * DIL_QB, DIL_QB)
        o, lse = _dil_block(q_src[pl.ds(r0, DIL_QB), q_sl], kpad_ref[pl.ds(r0, DIL_KW), :],
                            vpad_ref[pl.ds(r0, DIL_KW), :], [bias_ref[edge, 0], bias_ref[edge, 1]],
                            head0_lanes)
        rows = pl.ds(residue + r0 * dil, DIL_QB, stride=dil) if dil > 1 else pl.ds(r0, DIL_QB)
        store(group, rows, o, lse)

    fill_padded(g0_ref.at[0], SEQ)
    n_blocks0 = SEQ // DIL_QB

    def g0_body(i, carry):
        for n in range(DIL_UNROLL):
            blk = i * DIL_UNROLL + n
            edge = jnp.where(blk == 0, EDGE_FIRST,
                             jnp.where(blk == n_blocks0 - 1, EDGE_LAST, EDGE_NONE))
            padded_block(0, g0_ref.at[0], b0_ref, blk, edge, 1, 0)
        return carry

    lax.fori_loop(0, n_blocks0 // DIL_UNROLL, g0_body, 0)

    dil1 = DIL_PAIRS[1][1]
    n_blocks1 = SEQ // dil1 // DIL_QB

    def g1_body(r, carry):
        src = g1_ref.at[0, r]
        fill_padded(src, SEQ // dil1)
        for blk in range(n_blocks1):
            edge = EDGE_FIRST if blk == 0 else EDGE_LAST if blk == n_blocks1 - 1 else EDGE_NONE
            padded_block(1, src, b1_ref, blk, edge, dil1, r)
        return carry

    lax.fori_loop(0, dil1, g1_body, 0)

    dil2 = DIL_PAIRS[2][1]
    assert SEQ // dil2 == DIL_QB
    for n in range(DIL_UNROLL):
        base = n * DIL_KW
        for ref, zeros in ((kpad_ref, zeros_k), (vpad_ref, zeros_v)):
            ref[base:base + pad, :] = zeros
            ref[base + pad + DIL_QB:base + DIL_KW, :] = zeros
        vpad_ref[base + pad:base + pad + DIL_QB, LANES:] = jnp.ones((DIL_QB, LANES), BF16)

    def g2_body(i, carry):
        for n in range(DIL_UNROLL):
            r = i * DIL_UNROLL + n
            src = g2_ref.at[0, r]
            base = n * DIL_KW
            kpad_ref[base + pad:base + pad + DIL_QB, :] = src[:, k_sl]
            vpad_ref[base + pad:base + pad + DIL_QB, :LANES] = src[:, v_sl]
            o, lse = _dil_block(src[:, q_sl], kpad_ref[base:base + DIL_KW, :],
                                vpad_ref[base:base + DIL_KW, :],
                                [b2_ref[EDGE_BOTH, 0], b2_ref[EDGE_BOTH, 1]], head0_lanes)
            store(2, pl.ds(r, DIL_QB, stride=dil2), o, lse)
        return carry

    lax.fori_loop(0, dil2 // DIL_UNROLL, g2_body, 0)

    chunk = 256
    for c in range(SEQ // chunk):
        rows = slice(c * chunk, (c + 1) * chunk)
        l0, l1, l2 = lse_ref[0, rows, :], lse_ref[1, rows, :], lse_ref[2, rows, :]
        mx = jnp.maximum(jnp.maximum(l0, l1), l2)
        e0, e1, e2 = jnp.exp2(l0 - mx), jnp.exp2(l1 - mx), jnp.exp2(l2 - mx)
        mixed = (e0 * o_ref[0, rows, :] + e1 * o_ref[1, rows, :] + e2 * o_ref[2, rows, :]) / (e0 + e1 + e2)
        out_ref[0, rows, :] = mixed.astype(out_ref.dtype)


def _dilated_mixture(pb0, pb1, pb2, biases):
    d1, d2 = DIL_PAIRS[1][1], DIL_PAIRS[2][1]
    bias_spec = _resident((4, 2, DIL_QB, DIL_KW))
    max_len = SEQ + 2 * DIL_RADIUS
    return pl.pallas_call(
        _dil_kernel,
        grid=(BATCH,),
        in_specs=[pl.BlockSpec((1, SEQ, DIL_QKV), lambda b: (b, 0, 0)),
                  pl.BlockSpec((1, d1, SEQ // d1, DIL_QKV), lambda b: (b, 0, 0, 0)),
                  pl.BlockSpec((1, d2, SEQ // d2, DIL_QKV), lambda b: (b, 0, 0, 0)),
                  bias_spec, bias_spec, bias_spec],
        out_specs=pl.BlockSpec((1, SEQ, LANES), lambda b: (b, 0, 0)),
        out_shape=jax.ShapeDtypeStruct((BATCH, SEQ, DIL_OUT_WIDTH), BF16),
        scratch_shapes=[pltpu.VMEM((max_len, LANES), BF16), pltpu.VMEM((max_len, 2 * LANES), BF16),
                        pltpu.VMEM((DIL_GROUPS, SEQ, LANES), F32),
                        pltpu.VMEM((DIL_GROUPS, SEQ, LANES), F32)],
        compiler_params=_params("parallel"),
        name="dilated_attn",
    )(pb0, pb1, pb2, *biases)


def _diff_kernel(q_ref, k_ref, v_ref, bias_ref, lq1_ref, lk1_ref, lq2_ref, lk2_ref, sg_ref,
                 o_ref, vaug_ref, logit_ref, *, lam_init):
    j = pl.program_id(2)
    nq = pl.num_programs(2)

    @pl.when(j == 0)
    def _():
        vaug_ref[:, :LANES] = v_ref[0]
        vaug_ref[:, LANES:] = jnp.ones((SEQ, LANES), BF16)

    off = pl.multiple_of((nq - 1 - j) * DIFF_TQ, DIFF_TQ)
    lam = (jnp.exp(jnp.sum(lq1_ref[...] * lk1_ref[...], axis=-1, keepdims=True))
           - jnp.exp(jnp.sum(lq2_ref[...] * lk2_ref[...], axis=-1, keepdims=True))
           + lam_init)
    q = q_ref[0]
    first_half = lax.broadcasted_iota(jnp.int32, q.shape, 1) < DIFF_HEAD_DIM
    zero = jnp.zeros_like(q)
    n_tiles = SEQ // DIFF_KT

    def key_tile(t):
        return slice(t * DIFF_KT, (t + 1) * DIFF_KT)

    row_max = []
    for m in range(2):
        q_m = jnp.where(first_half, q, zero) if m == 0 else jnp.where(first_half, zero, q)
        lane_max = None
        for t in range(n_tiles):
            bias = bias_ref[0, :, pl.ds(pl.multiple_of(off + t * DIFF_KT, LANES), DIFF_KT)]
            logits = lax.dot_general(q_m, k_ref[0, key_tile(t), :], (((1,), (1,)), ((), ())),
                                     preferred_element_type=F32) + bias
            logit_ref[m, :, key_tile(t)] = logits
            for c in range(DIFF_KT // LANES):
                part = logits[:, c * LANES:(c + 1) * LANES]
                lane_max = part if lane_max is None else jnp.maximum(lane_max, part)
        row_max.append(jnp.max(lane_max, axis=-1, keepdims=True))

    outs = []
    for m in range(2):
        pv = jnp.zeros((DIFF_TQ, 2 * LANES), F32)
        for t in range(n_tiles):
            e = jnp.exp2(logit_ref[m, :, key_tile(t)] - row_max[m]).astype(BF16)
            pv = pv + jnp.dot(e, vaug_ref[key_tile(t), :], preferred_element_type=F32)
        outs.append(pv[:, :LANES] / pv[:, LANES:])
    o = outs[0] - lam * outs[1]
    o = _rms(o, sg_ref[...], SUBLN_EPS) * (1.0 - lam_init)
    o_ref[0] = o.astype(o_ref.dtype)


def _diff_attention(pc, strip, lq1, lk1, lq2, lk2, subln_g, lam_init):
    pcv = pc.reshape(BATCH, SEQ, C_WIDTH)
    vec = _resident((1, DIFF_HEAD_DIM))
    return pl.pallas_call(
        functools.partial(_diff_kernel, lam_init=lam_init),
        grid=(DIFF_HEADS, BATCH, SEQ // DIFF_TQ),
        in_specs=[pl.BlockSpec((1, DIFF_TQ, LANES), lambda h, b, j: (b, j, h)),
                  pl.BlockSpec((1, SEQ, LANES), lambda h, b, j: (b, 0, DIFF_HEADS + h)),
                  pl.BlockSpec((1, SEQ, LANES), lambda h, b, j: (b, 0, 2 * DIFF_HEADS + h)),
                  pl.BlockSpec((1, DIFF_TQ, DIFF_STRIP), lambda h, b, j: (h, 0, 0)),
                  vec, vec, vec, vec, _resident((1, DIFF_V_DIM))],
        out_specs=pl.BlockSpec((1, DIFF_TQ, LANES), lambda h, b, j: (b, j, h)),
        out_shape=jax.ShapeDtypeStruct((BATCH, SEQ, DIFF_V_WIDTH), BF16),
        scratch_shapes=[pltpu.VMEM((SEQ, 2 * LANES), BF16),
                        pltpu.VMEM((2, DIFF_TQ, SEQ), F32)],
        compiler_params=_params("parallel", "parallel", "arbitrary"),
        name="diff_attn",
    )(pcv, pcv, pcv, strip, lq1.reshape(1, -1), lk1.reshape(1, -1), lq2.reshape(1, -1),
      lk2.reshape(1, -1), subln_g.reshape(1, -1)).reshape(TOKENS, DIFF_V_WIDTH)


def _merge_kernel(x_ref, f_ref, b_ref, c_ref, gate_ref, wa_ref, wb_ref, wc_ref, wo_ref, out_ref):
    y_a = jnp.dot(f_ref[...], wa_ref[...], preferred_element_type=F32)
    y_b = jnp.dot(b_ref[...], wb_ref[...], preferred_element_type=F32)
    y_c = jnp.dot(c_ref[...], wc_ref[...], preferred_element_type=F32)
    merged = (gate_ref[:, :D_MODEL].astype(F32) * y_a
              + gate_ref[:, D_MODEL:2 * D_MODEL].astype(F32) * y_b
              + gate_ref[:, 2 * D_MODEL:].astype(F32) * y_c)
    out_ref[...] = x_ref[...] + jnp.dot(merged.astype(BF16), wo_ref[...],
                                        preferred_element_type=F32)


def _merge(x, f, b, c, gates, wa, wb, wc, wo):
    def tile(w):
        return pl.BlockSpec((MERGE_TM, w), lambda i: (i, 0))

    return pl.pallas_call(
        _merge_kernel,
        grid=(TOKENS // MERGE_TM,),
        in_specs=[tile(D_MODEL), tile(FOURIER_WIDTH), tile(DIL_OUT_WIDTH), tile(DIFF_V_WIDTH),
                  tile(GATE_WIDTH), _resident(wa.shape), _resident(wb.shape),
                  _resident(wc.shape), _resident(wo.shape)],
        out_specs=tile(D_MODEL),
        out_shape=jax.ShapeDtypeStruct((TOKENS, D_MODEL), F32),
        compiler_params=_params("parallel"),
        name="gated_merge",
    )(x, f, b, c, gates, wa, wb, wc, wo)


def kernel(x, g_ffn1, w_ffn1_gate, w_ffn1_up, w_ffn1_down, g_mix, w_in, w_gate, b_gate,
           w_br_a, w_br_b, w_br_c, w_out, lam_q1, lam_k1, lam_q2, lam_k2, subln_g,
           rel_bias, g_ffn2, w_ffn2_gate, w_ffn2_up, w_ffn2_down, g_final):
    assert x.shape == (BATCH, SEQ, D_MODEL) and x.dtype == F32
    bf = lambda w: w.astype(BF16)
    c0 = FOURIER_WIDTH
    c3 = c0 + 3 * DIL_WIDTH

    dil_bias = [_dil_bias_tiles(rel_bias, g, dil) for g, (_, dil) in enumerate(DIL_PAIRS)]
    diff_strip = _diff_bias_strip(rel_bias)
    m_even, m_odd = _dft_matrices()
    dc, ds = _channel_dft()

    def dil_weights(w_in_l, g):
        lo = c0 + g * LANES
        q, k, v = (w_in_l[:, lo + i * DIL_WIDTH: lo + i * DIL_WIDTH + LANES] for i in range(3))
        return bf(jnp.concatenate([q * (QK_SCALE * LOG2_E), k, v], axis=1))

    q_scale_c = jnp.concatenate([jnp.full((DIFF_QK_WIDTH,), QK_SCALE * LOG2_E, F32),
                                 jnp.ones((DIFF_QK_WIDTH + DIFF_V_WIDTH,), F32)])

    h = x.reshape(TOKENS, D_MODEL)
    for l in range(DEPTH):
        h = _ffn(h, g_ffn1[l], bf(w_ffn1_gate[l]), bf(w_ffn1_up[l]), bf(w_ffn1_down[l]))

        wf = _fold_channel_dft(w_in[l, :, :c0], dc, ds)
        pf, pb0, pb1, pb2, pc, gates = _proj(
            h, g_mix[l], wf, [dil_weights(w_in[l], g) for g in range(DIL_GROUPS)],
            bf(w_in[l, :, c3:] * q_scale_c), bf(w_gate[l]), b_gate[l])

        f = _fourier(pf, m_even, m_odd).reshape(TOKENS, FOURIER_WIDTH)
        mixed = _dilated_mixture(pb0.reshape(BATCH, SEQ, DIL_QKV), pb1, pb2, dil_bias)
        lam_init = 0.8 - 0.6 * math.exp(-0.3 * l)
        c = _diff_attention(pc, diff_strip, lam_q1[l], lam_k1[l], lam_q2[l], lam_k2[l],
                            subln_g[l], lam_init)

        h = _merge(h, f, mixed.reshape(TOKENS, DIL_OUT_WIDTH), c, gates,
                   bf(w_br_a[l]), bf(w_br_b[l]), bf(w_br_c[l]), bf(w_out[l]))

        h = _ffn(h, g_ffn2[l], bf(w_ffn2_gate[l]), bf(w_ffn2_up[l]), bf(w_ffn2_down[l]),
                 g_final if l == DEPTH - 1 else None)
    return h.reshape(BATCH, SEQ, D_MODEL)
```

```python
import functools
import math

import numpy as np
import jax
import jax.numpy as jnp
from jax import lax
from jax.experimental import pallas as pl
from jax.experimental.pallas import tpu as pltpu

F32 = jnp.float32
BF16 = jnp.bfloat16

D_MODEL = 1024
BATCH = 8
SEQ = 2048
DEPTH = 2
TOKENS = BATCH * SEQ
D_FF = 2816
EPS = 1e-6
FOURIER_GROUPS = 8
FOURIER_GROUP_DIM = 128
FOURIER_WIDTH = FOURIER_GROUPS * FOURIER_GROUP_DIM
DIL_PAIRS = ((128, 1), (512, 4), (2048, 16))
DIL_GROUPS = len(DIL_PAIRS)
DIL_HEAD_DIM = 64
DIL_HEADS = 2 * DIL_GROUPS
DIL_WIDTH = DIL_HEADS * DIL_HEAD_DIM
DIL_OUT_WIDTH = 2 * DIL_HEAD_DIM
DIL_RADIUS = 64
DIFF_HEADS = 4
DIFF_HEAD_DIM = 64
DIFF_QK_WIDTH = DIFF_HEADS * 2 * DIFF_HEAD_DIM
DIFF_V_DIM = 2 * DIFF_HEAD_DIM
DIFF_V_WIDTH = DIFF_HEADS * DIFF_V_DIM
SUBLN_EPS = 1e-5
NUM_BUCKETS = 32
MAX_DISTANCE = 1024
N_BIAS_HEADS = DIL_HEADS + DIFF_HEADS
NEG_INF = -1e30
QK_SCALE = DIL_HEAD_DIM ** -0.5
LOG2_E = math.log2(math.e)
C_WIDTH = 2 * DIFF_QK_WIDTH + DIFF_V_WIDTH
GATE_WIDTH = 3 * D_MODEL

LANES = 128
VMEM_BYTES_V7X = 64 * 1024 * 1024
VMEM_LIMIT = 56 * 1024 * 1024

FFN_TM = 512
FF_CHUNK = 256
PROJ_TM = 512
PROJ_CHUNK = 512
MERGE_TM = 512
HALF_SEQ = SEQ // 2
FOURIER_CN = 256
DIL_QKV = 3 * LANES
DIL_QB = 128
DIL_KW = DIL_QB + 2 * DIL_RADIUS
DIL_UNROLL = 4
DIFF_TQ = 256
DIFF_STRIP = 2 * SEQ - DIFF_TQ
DIFF_KT = 512


def _params(*semantics):
    return pltpu.CompilerParams(dimension_semantics=semantics, vmem_limit_bytes=VMEM_LIMIT)


def _resident(shape):
    nd = len(shape)
    return pl.BlockSpec(shape, lambda *_: (0,) * nd, pipeline_mode=pl.Buffered(1))


def _rms(x, g, eps):
    return x * lax.rsqrt(jnp.mean(x * x, axis=-1, keepdims=True) + eps) * g


def _bucket_breakpoints():
    rel = np.arange(-(SEQ - 1), SEQ)
    half = NUM_BUCKETS // 2
    max_exact = half // 2
    n = np.abs(rel)
    nf = np.maximum(n, 1).astype(np.float64)
    large = max_exact + (np.log(nf / max_exact) / math.log(MAX_DISTANCE / max_exact)
                         * (half - max_exact)).astype(np.int32)
    large = np.minimum(large, half - 1)
    b = np.where(rel > 0, half, 0) + np.where(n < max_exact, n, large)
    pts = [(int(rel[i]), int(b[i])) for i in range(1, len(rel)) if b[i] != b[i - 1]]
    return int(b[0]), pts


def _bias_values(tbl_ref, head, rel):
    first, pts = _bucket_breakpoints()
    val = jnp.full(rel.shape, tbl_ref[first, head], F32)
    for thr, bucket in pts:
        val = jnp.where(rel >= thr, tbl_ref[bucket, head], val)
    return val * LOG2_E


def _diff_bias_kernel(tbl_ref, o_ref):
    rows, cols = o_ref.shape[1], o_ref.shape[2]
    rel = (lax.broadcasted_iota(jnp.int32, (rows, cols), 1)
           - lax.broadcasted_iota(jnp.int32, (rows, cols), 0) - (SEQ - rows))
    o_ref[0] = _bias_values(tbl_ref, DIL_HEADS + pl.program_id(0), rel)


def _diff_bias_strip(rel_bias):
    return pl.pallas_call(
        _diff_bias_kernel,
        grid=(DIFF_HEADS,),
        in_specs=[pl.BlockSpec(memory_space=pltpu.SMEM)],
        out_specs=pl.BlockSpec((1, DIFF_TQ, DIFF_STRIP), lambda h: (h, 0, 0)),
        out_shape=jax.ShapeDtypeStruct((DIFF_HEADS, DIFF_TQ, DIFF_STRIP), F32),
        compiler_params=_params("arbitrary"),
        name="diff_bias_strip",
    )(rel_bias)


EDGE_FIRST, EDGE_NONE, EDGE_LAST, EDGE_BOTH = range(4)


def _dil_bias_kernel(tbl_ref, o_ref, *, head0, dil):
    rows, cols = o_ref.shape[2], o_ref.shape[3]
    col = lax.broadcasted_iota(jnp.int32, (rows, cols), 1)
    rel_t = col - lax.broadcasted_iota(jnp.int32, (rows, cols), 0) - DIL_RADIUS
    val = _bias_values(tbl_ref, head0 + pl.program_id(0), rel_t * dil)
    val = jnp.where(jnp.abs(rel_t) <= DIL_RADIUS, val, NEG_INF)
    lo = jnp.where(col >= DIL_RADIUS, val, NEG_INF)
    o_ref[EDGE_FIRST, 0] = lo
    o_ref[EDGE_NONE, 0] = val
    o_ref[EDGE_LAST, 0] = jnp.where(col < cols - DIL_RADIUS, val, NEG_INF)
    o_ref[EDGE_BOTH, 0] = jnp.where(col < cols - DIL_RADIUS, lo, NEG_INF)


def _dil_bias_tiles(rel_bias, group, dil):
    return pl.pallas_call(
        functools.partial(_dil_bias_kernel, head0=2 * group, dil=dil),
        grid=(2,),
        in_specs=[pl.BlockSpec(memory_space=pltpu.SMEM)],
        out_specs=pl.BlockSpec((4, 1, DIL_QB, DIL_KW), lambda h: (0, h, 0, 0)),
        out_shape=jax.ShapeDtypeStruct((4, 2, DIL_QB, DIL_KW), F32),
        compiler_params=_params("arbitrary"),
        name="dil_bias_tiles",
    )(rel_bias)


def _fold_kernel(w_ref, dc_ref, ds_ref, o_ref):
    w = w_ref[...]
    o_ref[0] = jnp.dot(w, dc_ref[...], preferred_element_type=F32,
                       precision=lax.Precision.HIGHEST).astype(BF16)
    o_ref[1] = jnp.dot(w, ds_ref[...], preferred_element_type=F32,
                       precision=lax.Precision.HIGHEST).astype(BF16)


def _fold_channel_dft(w_a, dc, ds):
    gd = FOURIER_GROUP_DIM
    mat = pl.BlockSpec((gd, gd), lambda g: (0, 0))
    out = pl.pallas_call(
        _fold_kernel,
        grid=(FOURIER_GROUPS,),
        in_specs=[pl.BlockSpec((D_MODEL, gd), lambda g: (0, g)), mat, mat],
        out_specs=pl.BlockSpec((2, D_MODEL, gd), lambda g: (0, 0, g)),
        out_shape=jax.ShapeDtypeStruct((2, D_MODEL, FOURIER_WIDTH), BF16),
        compiler_params=_params("arbitrary"),
        name="fold_channel_dft",
    )(w_a, dc, ds)
    return jnp.concatenate([out[0], out[1]], axis=1)


def _ffn_kernel(x_ref, g_ref, wg_ref, wu_ref, wd_ref, *rest, final_norm):
    if final_norm:
        gf_ref, o_ref, a_ref = rest
    else:
        o_ref, a_ref = rest
    x = x_ref[...]
    h = _rms(x, g_ref[...], EPS).astype(BF16)
    for c in range(D_FF // FF_CHUNK):
        sl = slice(c * FF_CHUNK, (c + 1) * FF_CHUNK)
        gate = jnp.dot(h, wg_ref[:, sl], preferred_element_type=F32)
        up = jnp.dot(h, wu_ref[:, sl], preferred_element_type=F32)
        a_ref[:, sl] = (gate * jax.nn.sigmoid(gate) * up).astype(BF16)
    y = x + 0.5 * jnp.dot(a_ref[...], wd_ref[...], preferred_element_type=F32)
    if final_norm:
        y = _rms(y, gf_ref[...], EPS)
    o_ref[...] = y


def _ffn(x, g, wg, wu, wd, g_final=None):
    final_norm = g_final is not None
    tile = pl.BlockSpec((FFN_TM, D_MODEL), lambda i: (i, 0))
    in_specs = [tile, _resident((1, D_MODEL)), _resident((D_MODEL, D_FF)),
                _resident((D_MODEL, D_FF)), _resident((D_FF, D_MODEL))]
    args = [x, g.reshape(1, D_MODEL), wg, wu, wd]
    if final_norm:
        in_specs.append(_resident((1, D_MODEL)))
        args.append(g_final.reshape(1, D_MODEL))
    return pl.pallas_call(
        functools.partial(_ffn_kernel, final_norm=final_norm),
        grid=(TOKENS // FFN_TM,),
        in_specs=in_specs,
        out_specs=tile,
        out_shape=jax.ShapeDtypeStruct((TOKENS, D_MODEL), F32),
        scratch_shapes=[pltpu.VMEM((FFN_TM, D_FF), BF16)],
        compiler_params=_params("parallel"),
        name="ffn_final" if final_norm else "ffn",
    )(*args)


def _proj_kernel(x_ref, g_ref, wf_ref, wb0_ref, wb1_ref, wb2_ref, wc_ref, wg_ref, bg_ref,
                 pf_ref, pb0_ref, pb1_ref, pb2_ref, pc_ref, gate_ref, y_ref):
    u = _rms(x_ref[...], g_ref[...], EPS).astype(BF16)
    rows = x_ref.shape[0]

    def project(w_ref, store, bias_ref=None):
        width = w_ref.shape[1]
        for c0 in range(0, width, PROJ_CHUNK):
            sl = slice(c0, min(c0 + PROJ_CHUNK, width))
            y = jnp.dot(u, w_ref[:, sl], preferred_element_type=F32)
            if bias_ref is not None:
                y = jax.nn.sigmoid(y + bias_ref[:, sl])
            store(sl, y)

    def natural(o_ref):
        def store(sl, y):
            o_ref[:, sl] = y.astype(o_ref.dtype)
        return store

    def by_residue(o_ref, dil):
        def store(sl, y):
            for j in range((sl.stop - sl.start) // LANES):
                y_ref[j] = y[:, j * LANES:(j + 1) * LANES]
                cols = slice(sl.start + j * LANES, sl.start + (j + 1) * LANES)
                for r in range(dil):
                    o_ref[0, r, :, cols] = y_ref[j, pl.ds(r, rows // dil, stride=dil), :].astype(o_ref.dtype)
        return store

    project(wf_ref, by_residue(pf_ref, 2))
    project(wb0_ref, natural(pb0_ref))
    project(wb1_ref, by_residue(pb1_ref, DIL_PAIRS[1][1]))
    project(wb2_ref, by_residue(pb2_ref, DIL_PAIRS[2][1]))
    project(wc_ref, natural(pc_ref))
    project(wg_ref, natural(gate_ref), bg_ref)


def _proj(x, g, wf, wb, wc, wg, bg):
    tiles_per_seq = SEQ // PROJ_TM

    def tile(w):
        return pl.BlockSpec((PROJ_TM, w), lambda i: (i, 0))

    def residue_tile(dil, w):
        return pl.BlockSpec((1, dil, PROJ_TM // dil, w),
                            lambda i: (i // tiles_per_seq, 0, i % tiles_per_seq, 0))

    def residue_shape(dil, w):
        return jax.ShapeDtypeStruct((BATCH, dil, SEQ // dil, w), BF16)

    d1, d2 = DIL_PAIRS[1][1], DIL_PAIRS[2][1]
    weights = [wf, *wb, wc, wg]
    return pl.pallas_call(
        _proj_kernel,
        grid=(TOKENS // PROJ_TM,),
        in_specs=[tile(D_MODEL), _resident((1, D_MODEL))]
        + [_resident(w.shape) for w in weights] + [_resident((1, GATE_WIDTH))],
        out_specs=[residue_tile(2, 2 * FOURIER_WIDTH), tile(DIL_QKV), residue_tile(d1, DIL_QKV),
                   residue_tile(d2, DIL_QKV), tile(C_WIDTH), tile(GATE_WIDTH)],
        out_shape=[residue_shape(2, 2 * FOURIER_WIDTH),
                   jax.ShapeDtypeStruct((TOKENS, DIL_QKV), BF16),
                   residue_shape(d1, DIL_QKV), residue_shape(d2, DIL_QKV),
                   jax.ShapeDtypeStruct((TOKENS, C_WIDTH), BF16),
                   jax.ShapeDtypeStruct((TOKENS, GATE_WIDTH), BF16)],
        scratch_shapes=[pltpu.VMEM((PROJ_CHUNK // LANES, PROJ_TM, LANES), F32)],
        compiler_params=_params("parallel"),
        name="proj_in",
    )(x, g.reshape(1, D_MODEL), *weights, bg.reshape(1, GATE_WIDTH))


def _fourier_kernel(pc_ref, ps_ref, me_ref, mo_ref, o_ref):
    h = HALF_SEQ
    even = (jnp.dot(me_ref[:, :h], pc_ref[0, 0], preferred_element_type=F32)
            + jnp.dot(me_ref[:, h:], ps_ref[0, 0], preferred_element_type=F32))
    odd = (jnp.dot(mo_ref[:, :h], pc_ref[0, 1], preferred_element_type=F32)
           + jnp.dot(mo_ref[:, h:], ps_ref[0, 1], preferred_element_type=F32))
    o_ref[0, :h, :] = (even + odd).astype(o_ref.dtype)
    o_ref[0, h:, :] = (even - odd).astype(o_ref.dtype)


def _fourier(pf, m_even, m_odd):
    cn = FOURIER_CN
    nblk = FOURIER_WIDTH // cn
    return pl.pallas_call(
        _fourier_kernel,
        grid=(BATCH, nblk),
        in_specs=[pl.BlockSpec((1, 2, HALF_SEQ, cn), lambda b, j: (b, 0, 0, j)),
                  pl.BlockSpec((1, 2, HALF_SEQ, cn), lambda b, j: (b, 0, 0, nblk + j)),
                  _resident((HALF_SEQ, SEQ)), _resident((HALF_SEQ, SEQ))],
        out_specs=pl.BlockSpec((1, SEQ, cn), lambda b, j: (b, 0, j)),
        out_shape=jax.ShapeDtypeStruct((BATCH, SEQ, FOURIER_WIDTH), BF16),
        compiler_params=_params("parallel", "parallel"),
        name="fourier_mix",
    )(pf, pf, m_even, m_odd)


def _dft_matrices():
    half = HALF_SEQ
    k = lax.broadcasted_iota(jnp.int32, (half, half), 0)
    m = lax.broadcasted_iota(jnp.int32, (half, half), 1)
    ang_e = ((k * m) % half).astype(F32) * (2.0 * math.pi / half)
    ang_o = ((k * (2 * m + 1)) % SEQ).astype(F32) * (2.0 * math.pi / SEQ)
    m_even = jnp.concatenate([jnp.cos(ang_e), -jnp.sin(ang_e)], axis=1).astype(BF16)
    m_odd = jnp.concatenate([jnp.cos(ang_o), -jnp.sin(ang_o)], axis=1).astype(BF16)
    return m_even, m_odd


def _channel_dft():
    gd = FOURIER_GROUP_DIM
    idx = np.arange(gd)
    ang = 2.0 * np.pi * ((idx[:, None] * idx[None, :]) % gd) / gd
    scale = 1.0 / math.sqrt(SEQ * gd)
    return (jnp.asarray(np.cos(ang) * scale, F32), jnp.asarray(np.sin(ang) * scale, F32))


def _dil_block(q, kw, vw_aug, bias, head0_lanes):
    zero = jnp.zeros_like(q)
    pv, lse = [], []
    for hh in range(2):
        q_h = jnp.where(head0_lanes, q, zero) if hh == 0 else jnp.where(head0_lanes, zero, q)
        logits = lax.dot_general(q_h, kw, (((1,), (1,)), ((), ())),
                                 preferred_element_type=F32) + bias[hh]
        mx = jnp.max(logits, axis=-1, keepdims=True)
        e = jnp.exp2(logits - mx).astype(BF16)
        r = jnp.dot(e, vw_aug, preferred_element_type=F32)
        s = r[:, LANES:]
        pv.append(r[:, :LANES] / s)
        lse.append(mx + jnp.log2(s))
    return jnp.where(head0_lanes, pv[0], pv[1]), jnp.where(head0_lanes, lse[0], lse[1])


def _dil_kernel(g0_ref, g1_ref, g2_ref, b0_ref, b1_ref, b2_ref, out_ref,
                kpad_ref, vpad_ref, o_ref, lse_ref):
    pad = DIL_RADIUS
    head0_lanes = lax.broadcasted_iota(jnp.int32, (DIL_QB, LANES), 1) < DIL_HEAD_DIM
    zeros_k = jnp.zeros((pad, LANES), BF16)
    zeros_v = jnp.zeros((pad, 2 * LANES), BF16)
    q_sl, k_sl, v_sl = (slice(i * LANES, (i + 1) * LANES) for i in range(3))

    vpad_ref[pl.ds(pad, SEQ), LANES:] = jnp.ones((SEQ, LANES), BF16)

    def fill_padded(src, sub_len):
        kpad_ref[:pad, :] = zeros_k
        kpad_ref[pl.ds(pad, sub_len), :] = src[:, k_sl]
        kpad_ref[pl.ds(pad + sub_len, pad), :] = zeros_k
        vpad_ref[:pad, :] = zeros_v
        vpad_ref[pl.ds(pad, sub_len), :LANES] = src[:, v_sl]
        vpad_ref[pl.ds(pad + sub_len, pad), :] = zeros_v

    def store(group, rows, o, lse):
        o_ref[group, rows, :] = o
        lse_ref[group, rows, :] = lse

    def padded_block(group, q_src, bias_ref, blk, edge, dil, residue):
        r0 = pl.multiple_of(blk * DIL_QB, DIL_QB)
        o, lse = _dil_block(q_src[pl.ds(r0, DIL_QB), q_sl], kpad_ref[pl.ds(r0, DIL_KW), :],
                            vpad_ref[pl.ds(r0, DIL_KW), :], [bias_ref[edge, 0], bias_ref[edge, 1]],
                            head0_lanes)
        rows = pl.ds(residue + r0 * dil, DIL_QB, stride=dil) if dil > 1 else pl.ds(r0, DIL_QB)
        store(group, rows, o, lse)

    fill_padded(g0_ref.at[0], SEQ)
    n_blocks0 = SEQ // DIL_QB

    def g0_body(i, carry):
        for n in range(DIL_UNROLL):
            blk = i * DIL_UNROLL + n
            edge = jnp.where(blk == 0, EDGE_FIRST,
                             jnp.where(blk == n_blocks0 - 1, EDGE_LAST, EDGE_NONE))
            padded_block(0, g0_ref.at[0], b0_ref, blk, edge, 1, 0)
        return carry

    lax.fori_loop(0, n_blocks0 // DIL_UNROLL, g0_body, 0)

    dil1 = DIL_PAIRS[1][1]
    n_blocks1 = SEQ // dil1 // DIL_QB

    def g1_body(r, carry):
        src = g1_ref.at[0, r]
        fill_padded(src, SEQ // dil1)
        for blk in range(n_blocks1):
            edge = EDGE_FIRST if blk == 0 else EDGE_LAST if blk == n_blocks1 - 1 else EDGE_NONE
            padded_block(1, src, b1_ref, blk, edge, dil1, r)
        return carry

    lax.fori_loop(0, dil1, g1_body, 0)

    dil2 = DIL_PAIRS[2][1]
    assert SEQ // dil2 == DIL_QB
    for n in range(DIL_UNROLL):
        base = n * DIL_KW
        for ref, zeros in ((kpad_ref, zeros_k), (vpad_ref, zeros_v)):
            ref[base:base + pad, :] = zeros
            ref[base + pad + DIL_QB:base + DIL_KW, :] = zeros
        vpad_ref[base + pad:base + pad + DIL_QB, LANES:] = jnp.ones((DIL_QB, LANES), BF16)

    def g2_body(i, carry):
        for n in range(DIL_UNROLL):
            r = i * DIL_UNROLL + n
            src = g2_ref.at[0, r]
            base = n * DIL_KW
            kpad_ref[base + pad:base + pad + DIL_QB, :] = src[:, k_sl]
            vpad_ref[base + pad:base + pad + DIL_QB, :LANES] = src[:, v_sl]
            o, lse = _dil_block(src[:, q_sl], kpad_ref[base:base + DIL_KW, :],
                                vpad_ref[base:base + DIL_KW, :],
                                [b2_ref[EDGE_BOTH, 0], b2_ref[EDGE_BOTH, 1]], head0_lanes)
            store(2, pl.ds(r, DIL_QB, stride=dil2), o, lse)
        return carry

    lax.fori_loop(0, dil2 // DIL_UNROLL, g2_body, 0)

    chunk = 256
    for c in range(SEQ // chunk):
        rows = slice(c * chunk, (c + 1) * chunk)
        l0, l1, l2 = lse_ref[0, rows, :], lse_ref[1, rows, :], lse_ref[2, rows, :]
        mx = jnp.maximum(jnp.maximum(l0, l1), l2)
        e0, e1, e2 = jnp.exp2(l0 - mx), jnp.exp2(l1 - mx), jnp.exp2(l2 - mx)
        mixed = (e0 * o_ref[0, rows, :] + e1 * o_ref[1, rows, :] + e2 * o_ref[2, rows, :]) / (e0 + e1 + e2)
        out_ref[0, rows, :] = mixed.astype(out_ref.dtype)


def _dilated_mixture(pb0, pb1, pb2, biases):
    d1, d2 = DIL_PAIRS[1][1], DIL_PAIRS[2][1]
    bias_spec = _resident((4, 2, DIL_QB, DIL_KW))
    max_len = SEQ + 2 * DIL_RADIUS
    return pl.pallas_call(
        _dil_kernel,
        grid=(BATCH,),
        in_specs=[pl.BlockSpec((1, SEQ, DIL_QKV), lambda b: (b, 0, 0)),
                  pl.BlockSpec((1, d1, SEQ // d1, DIL_QKV), lambda b: (b, 0, 0, 0)),
                  pl.BlockSpec((1, d2, SEQ // d2, DIL_QKV), lambda b: (b, 0, 0, 0)),
                  bias_spec, bias_spec, bias_spec],
        out_specs=pl.BlockSpec((1, SEQ, LANES), lambda b: (b, 0, 0)),
        out_shape=jax.ShapeDtypeStruct((BATCH, SEQ, DIL_OUT_WIDTH), BF16),
        scratch_shapes=[pltpu.VMEM((max_len, LANES), BF16), pltpu.VMEM((max_len, 2 * LANES), BF16),
                        pltpu.VMEM((DIL_GROUPS, SEQ, LANES), F32),
                        pltpu.VMEM((DIL_GROUPS, SEQ, LANES), F32)],
        compiler_params=_params("parallel"),
        name="dilated_attn",
    )(pb0, pb1, pb2, *biases)


def _diff_kernel(q_ref, k_ref, v_ref, bias_ref, lq1_ref, lk1_ref, lq2_ref, lk2_ref, sg_ref,
                 o_ref, vaug_ref, logit_ref, rmax_ref, *, lam_init):
    s = pl.program_id(0)
    n_blocks = pl.num_programs(0) - 1
    nq = SEQ // DIFF_TQ
    cur = jnp.minimum(s, n_blocks - 1)
    prev = jnp.maximum(s - 1, 0)
    slot = s % 2

    @pl.when(s == 0)
    def _():
        logit_ref[1] = jnp.zeros(logit_ref.shape[1:], F32)
        rmax_ref[1] = jnp.zeros(rmax_ref.shape[1:], F32)

    @pl.when(prev % nq == 0)
    def _():
        vaug_ref[:, :LANES] = v_ref[0]
        vaug_ref[:, LANES:] = jnp.ones((SEQ, LANES), BF16)

    n_tiles = SEQ // DIFF_KT

    def key_tile(t):
        return slice(t * DIFF_KT, (t + 1) * DIFF_KT)

    off = pl.multiple_of((nq - 1 - cur % nq) * DIFF_TQ, DIFF_TQ)
    q = q_ref[0]
    first_half = lax.broadcasted_iota(jnp.int32, q.shape, 1) < DIFF_HEAD_DIM
    zero = jnp.zeros_like(q)
    outs = []
    for m in range(2):
        row_max = rmax_ref[1 - slot, m]
        pv = jnp.zeros((DIFF_TQ, 2 * LANES), F32)
        for t in range(n_tiles):
            old = logit_ref[1 - slot, m, :, key_tile(t)]
            e = jnp.concatenate([jnp.exp2(old[:, c * LANES:(c + 1) * LANES] - row_max)
                                 for c in range(DIFF_KT // LANES)], axis=1).astype(BF16)
            pv = pv + jnp.dot(e, vaug_ref[key_tile(t), :], preferred_element_type=F32)
        outs.append(pv[:, :LANES] / pv[:, LANES:])

    for m in range(2):
        q_m = jnp.where(first_half, q, zero) if m == 0 else jnp.where(first_half, zero, q)
        lane_max = None
        for t in range(n_tiles):
            bias = bias_ref[0, :, pl.ds(pl.multiple_of(off + t * DIFF_KT, LANES), DIFF_KT)]
            logits = lax.dot_general(q_m, k_ref[0, key_tile(t), :], (((1,), (1,)), ((), ())),
                                     preferred_element_type=F32) + bias
            logit_ref[slot, m, :, key_tile(t)] = logits
            for c in range(DIFF_KT // LANES):
                part = logits[:, c * LANES:(c + 1) * LANES]
                lane_max = part if lane_max is None else jnp.maximum(lane_max, part)
        rmax_ref[slot, m] = jnp.broadcast_to(jnp.max(lane_max, axis=-1, keepdims=True),
                                             (DIFF_TQ, LANES))

    lam = (jnp.exp(jnp.sum(lq1_ref[...] * lk1_ref[...], axis=-1, keepdims=True))
           - jnp.exp(jnp.sum(lq2_ref[...] * lk2_ref[...], axis=-1, keepdims=True))
           + lam_init)
    o = outs[0] - lam * outs[1]
    o = _rms(o, sg_ref[...], SUBLN_EPS) * (1.0 - lam_init)
    o_ref[0] = o.astype(o_ref.dtype)


def _diff_attention(pc, strip, lq1, lk1, lq2, lk2, subln_g, lam_init):
    pcv = pc.reshape(BATCH, SEQ, C_WIDTH)
    vec = _resident((1, DIFF_HEAD_DIM))
    nq = SEQ // DIFF_TQ
    n_blocks = DIFF_HEADS * BATCH * nq

    def block_of(step, lag):
        blk = jnp.clip(step - lag, 0, n_blocks - 1)
        return blk // (BATCH * nq), (blk // nq) % BATCH, blk % nq

    def q_map(s):
        h, b, j = block_of(s, 0)
        return b, j, h

    def k_map(s):
        h, b, _ = block_of(s, 0)
        return b, 0, DIFF_HEADS + h

    def v_map(s):
        h, b, _ = block_of(s, 1)
        return b, 0, 2 * DIFF_HEADS + h

    def out_map(s):
        h, b, j = block_of(s, 1)
        return b, j, h

    return pl.pallas_call(
        functools.partial(_diff_kernel, lam_init=lam_init),
        grid=(n_blocks + 1,),
        in_specs=[pl.BlockSpec((1, DIFF_TQ, LANES), q_map),
                  pl.BlockSpec((1, SEQ, LANES), k_map),
                  pl.BlockSpec((1, SEQ, LANES), v_map),
                  pl.BlockSpec((1, DIFF_TQ, DIFF_STRIP), lambda s: (block_of(s, 0)[0], 0, 0)),
                  vec, vec, vec, vec, _resident((1, DIFF_V_DIM))],
        out_specs=pl.BlockSpec((1, DIFF_TQ, LANES), out_map),
        out_shape=jax.ShapeDtypeStruct((BATCH, SEQ, DIFF_V_WIDTH), BF16),
        scratch_shapes=[pltpu.VMEM((SEQ, 2 * LANES), BF16),
                        pltpu.VMEM((2, 2, DIFF_TQ, SEQ), F32),
                        pltpu.VMEM((2, 2, DIFF_TQ, LANES), F32)],
        compiler_params=_params("arbitrary"),
        name="diff_attn",
    )(pcv, pcv, pcv, strip, lq1.reshape(1, -1), lk1.reshape(1, -1), lq2.reshape(1, -1),
      lk2.reshape(1, -1), subln_g.reshape(1, -1)).reshape(TOKENS, DIFF_V_WIDTH)


def _merge_kernel(x_ref, f_ref, b_ref, c_ref, gate_ref, wa_ref, wb_ref, wc_ref, wo_ref, out_ref):
    y_a = jnp.dot(f_ref[...], wa_ref[...], preferred_element_type=F32)
    y_b = jnp.dot(b_ref[...], wb_ref[...], preferred_element_type=F32)
    y_c = jnp.dot(c_ref[...], wc_ref[...], preferred_element_type=F32)
    merged = (gate_ref[:, :D_MODEL].astype(F32) * y_a
              + gate_ref[:, D_MODEL:2 * D_MODEL].astype(F32) * y_b
              + gate_ref[:, 2 * D_MODEL:].astype(F32) * y_c)
    out_ref[...] = x_ref[...] + jnp.dot(merged.astype(BF16), wo_ref[...],
                                        preferred_element_type=F32)


def _merge(x, f, b, c, gates, wa, wb, wc, wo):
    def tile(w):
        return pl.BlockSpec((MERGE_TM, w), lambda i: (i, 0))

    return pl.pallas_call(
        _merge_kernel,
        grid=(TOKENS // MERGE_TM,),
        in_specs=[tile(D_MODEL), tile(FOURIER_WIDTH), tile(DIL_OUT_WIDTH), tile(DIFF_V_WIDTH),
                  tile(GATE_WIDTH), _resident(wa.shape), _resident(wb.shape),
                  _resident(wc.shape), _resident(wo.shape)],
        out_specs=tile(D_MODEL),
        out_shape=jax.ShapeDtypeStruct((TOKENS, D_MODEL), F32),
        compiler_params=_params("parallel"),
        name="gated_merge",
    )(x, f, b, c, gates, wa, wb, wc, wo)


def kernel(x, g_ffn1, w_ffn1_gate, w_ffn1_up, w_ffn1_down, g_mix, w_in, w_gate, b_gate,
           w_br_a, w_br_b, w_br_c, w_out, lam_q1, lam_k1, lam_q2, lam_k2, subln_g,
           rel_bias, g_ffn2, w_ffn2_gate, w_ffn2_up, w_ffn2_down, g_final):
    assert x.shape == (BATCH, SEQ, D_MODEL) and x.dtype == F32
    bf = lambda w: w.astype(BF16)
    c0 = FOURIER_WIDTH
    c3 = c0 + 3 * DIL_WIDTH

    dil_bias = [_dil_bias_tiles(rel_bias, g, dil) for g, (_, dil) in enumerate(DIL_PAIRS)]
    diff_strip = _diff_bias_strip(rel_bias)
    m_even, m_odd = _dft_matrices()
    dc, ds = _channel_dft()

    def dil_weights(w_in_l, g):
        lo = c0 + g * LANES
        q, k, v = (w_in_l[:, lo + i * DIL_WIDTH: lo + i * DIL_WIDTH + LANES] for i in range(3))
        return bf(jnp.concatenate([q * (QK_SCALE * LOG2_E), k, v], axis=1))

    q_scale_c = jnp.concatenate([jnp.full((DIFF_QK_WIDTH,), QK_SCALE * LOG2_E, F32),
                                 jnp.ones((DIFF_QK_WIDTH + DIFF_V_WIDTH,), F32)])

    h = x.reshape(TOKENS, D_MODEL)
    for l in range(DEPTH):
        h = _ffn(h, g_ffn1[l], bf(w_ffn1_gate[l]), bf(w_ffn1_up[l]), bf(w_ffn1_down[l]))

        wf = _fold_channel_dft(w_in[l, :, :c0], dc, ds)
        pf, pb0, pb1, pb2, pc, gates = _proj(
            h, g_mix[l], wf, [dil_weights(w_in[l], g) for g in range(DIL_GROUPS)],
            bf(w_in[l, :, c3:] * q_scale_c), bf(w_gate[l]), b_gate[l])

        f = _fourier(pf, m_even, m_odd).reshape(TOKENS, FOURIER_WIDTH)
        mixed = _dilated_mixture(pb0.reshape(BATCH, SEQ, DIL_QKV), pb1, pb2, dil_bias)
        lam_init = 0.8 - 0.6 * math.exp(-0.3 * l)
        c = _diff_attention(pc, diff_strip, lam_q1[l], lam_k1[l], lam_q2[l], lam_k2[l],
                            subln_g[l], lam_init)

        h = _merge(h, f, mixed.reshape(TOKENS, DIL_OUT_WIDTH), c, gates,
                   bf(w_br_a[l]), bf(w_br_b[l]), bf(w_br_c[l]), bf(w_out[l]))

        h = _ffn(h, g_ffn2[l], bf(w_ffn2_gate[l]), bf(w_ffn2_up[l]), bf(w_ffn2_down[l]),
                 g_final if l == DEPTH - 1 else None)
    return h.reshape(BATCH, SEQ, D_MODEL)
```

```python
import functools
import math

import numpy as np
import jax
import jax.numpy as jnp
from jax import lax
from jax.experimental import pallas as pl
from jax.experimental.pallas import tpu as pltpu

F32 = jnp.float32
BF16 = jnp.bfloat16

D_MODEL = 1024
BATCH = 8
SEQ = 2048
DEPTH = 2
TOKENS = BATCH * SEQ
D_FF = 2816
EPS = 1e-6
FOURIER_GROUPS = 8
FOURIER_GROUP_DIM = 128
FOURIER_WIDTH = FOURIER_GROUPS * FOURIER_GROUP_DIM
DIL_PAIRS = ((128, 1), (512, 4), (2048, 16))
DIL_GROUPS = len(DIL_PAIRS)
DIL_HEAD_DIM = 64
DIL_HEADS = 2 * DIL_GROUPS
DIL_WIDTH = DIL_HEADS * DIL_HEAD_DIM
DIL_OUT_WIDTH = 2 * DIL_HEAD_DIM
DIL_RADIUS = 64
DIFF_HEADS = 4
DIFF_HEAD_DIM = 64
DIFF_QK_WIDTH = DIFF_HEADS * 2 * DIFF_HEAD_DIM
DIFF_V_DIM = 2 * DIFF_HEAD_DIM
DIFF_V_WIDTH = DIFF_HEADS * DIFF_V_DIM
SUBLN_EPS = 1e-5
NUM_BUCKETS = 32
MAX_DISTANCE = 1024
N_BIAS_HEADS = DIL_HEADS + DIFF_HEADS
NEG_INF = -1e30
QK_SCALE = DIL_HEAD_DIM ** -0.5
LOG2_E = math.log2(math.e)
C_WIDTH = 2 * DIFF_QK_WIDTH + DIFF_V_WIDTH
GATE_WIDTH = 3 * D_MODEL

LANES = 128
VMEM_BYTES_V7X = 64 * 1024 * 1024
VMEM_LIMIT = 56 * 1024 * 1024

FFN_TM = 1024
FF_CHUNK = 256
PROJ_TM = 512
PROJ_CHUNK = 512
MERGE_TM = 512
HALF_SEQ = SEQ // 2
FOURIER_CN = 512
DIL_QKV = 3 * LANES
DIL_QB = 128
DIL_KW = DIL_QB + 2 * DIL_RADIUS
DIL_UNROLL = 4
DIFF_TQ = 512
DIFF_STRIP = 2 * SEQ - DIFF_TQ
DIFF_KT = 512


def _params(*semantics):
    return pltpu.CompilerParams(dimension_semantics=semantics, vmem_limit_bytes=VMEM_LIMIT)


def _resident(shape):
    nd = len(shape)
    return pl.BlockSpec(shape, lambda *_: (0,) * nd, pipeline_mode=pl.Buffered(1))


def _rms(x, g, eps):
    return x * lax.rsqrt(jnp.mean(x * x, axis=-1, keepdims=True) + eps) * g


def _bucket_breakpoints():
    rel = np.arange(-(SEQ - 1), SEQ)
    half = NUM_BUCKETS // 2
    max_exact = half // 2
    n = np.abs(rel)
    nf = np.maximum(n, 1).astype(np.float64)
    large = max_exact + (np.log(nf / max_exact) / math.log(MAX_DISTANCE / max_exact)
                         * (half - max_exact)).astype(np.int32)
    large = np.minimum(large, half - 1)
    b = np.where(rel > 0, half, 0) + np.where(n < max_exact, n, large)
    pts = [(int(rel[i]), int(b[i])) for i in range(1, len(rel)) if b[i] != b[i - 1]]
    return int(b[0]), pts


def _bias_values(tbl_ref, head, rel):
    first, pts = _bucket_breakpoints()
    val = jnp.full(rel.shape, tbl_ref[first, head], F32)
    for thr, bucket in pts:
        val = jnp.where(rel >= thr, tbl_ref[bucket, head], val)
    return val * LOG2_E


def _diff_bias_kernel(tbl_ref, o_ref):
    rows, cols = o_ref.shape[1], o_ref.shape[2]
    rel = (lax.broadcasted_iota(jnp.int32, (rows, cols), 1)
           - lax.broadcasted_iota(jnp.int32, (rows, cols), 0) - (SEQ - rows))
    o_ref[0] = _bias_values(tbl_ref, DIL_HEADS + pl.program_id(0), rel)


def _diff_bias_strip(rel_bias):
    return pl.pallas_call(
        _diff_bias_kernel,
        grid=(DIFF_HEADS,),
        in_specs=[pl.BlockSpec(memory_space=pltpu.SMEM)],
        out_specs=pl.BlockSpec((1, DIFF_TQ, DIFF_STRIP), lambda h: (h, 0, 0)),
        out_shape=jax.ShapeDtypeStruct((DIFF_HEADS, DIFF_TQ, DIFF_STRIP), F32),
        compiler_params=_params("arbitrary"),
        name="diff_bias_strip",
    )(rel_bias)


EDGE_FIRST, EDGE_NONE, EDGE_LAST, EDGE_BOTH = range(4)


def _dil_bias_kernel(tbl_ref, o_ref, *, head0, dil):
    rows, cols = o_ref.shape[2], o_ref.shape[3]
    col = lax.broadcasted_iota(jnp.int32, (rows, cols), 1)
    rel_t = col - lax.broadcasted_iota(jnp.int32, (rows, cols), 0) - DIL_RADIUS
    val = _bias_values(tbl_ref, head0 + pl.program_id(0), rel_t * dil)
    val = jnp.where(jnp.abs(rel_t) <= DIL_RADIUS, val, NEG_INF)
    lo = jnp.where(col >= DIL_RADIUS, val, NEG_INF)
    o_ref[EDGE_FIRST, 0] = lo
    o_ref[EDGE_NONE, 0] = val
    o_ref[EDGE_LAST, 0] = jnp.where(col < cols - DIL_RADIUS, val, NEG_INF)
    o_ref[EDGE_BOTH, 0] = jnp.where(col < cols - DIL_RADIUS, lo, NEG_INF)


def _dil_bias_tiles(rel_bias, group, dil):
    return pl.pallas_call(
        functools.partial(_dil_bias_kernel, head0=2 * group, dil=dil),
        grid=(2,),
        in_specs=[pl.BlockSpec(memory_space=pltpu.SMEM)],
        out_specs=pl.BlockSpec((4, 1, DIL_QB, DIL_KW), lambda h: (0, h, 0, 0)),
        out_shape=jax.ShapeDtypeStruct((4, 2, DIL_QB, DIL_KW), F32),
        compiler_params=_params("arbitrary"),
        name="dil_bias_tiles",
    )(rel_bias)


def _fold_kernel(w_ref, dc_ref, ds_ref, o_ref):
    w = w_ref[...]
    o_ref[0] = jnp.dot(w, dc_ref[...], preferred_element_type=F32,
                       precision=lax.Precision.HIGHEST).astype(BF16)
    o_ref[1] = jnp.dot(w, ds_ref[...], preferred_element_type=F32,
                       precision=lax.Precision.HIGHEST).astype(BF16)


def _fold_channel_dft(w_a, dc, ds):
    gd = FOURIER_GROUP_DIM
    mat = pl.BlockSpec((gd, gd), lambda g: (0, 0))
    out = pl.pallas_call(
        _fold_kernel,
        grid=(FOURIER_GROUPS,),
        in_specs=[pl.BlockSpec((D_MODEL, gd), lambda g: (0, g)), mat, mat],
        out_specs=pl.BlockSpec((2, D_MODEL, gd), lambda g: (0, 0, g)),
        out_shape=jax.ShapeDtypeStruct((2, D_MODEL, FOURIER_WIDTH), BF16),
        compiler_params=_params("arbitrary"),
        name="fold_channel_dft",
    )(w_a, dc, ds)
    return jnp.concatenate([out[0], out[1]], axis=1)


def _ffn_kernel(x_ref, g_ref, wg_ref, wu_ref, wd_ref, *rest, final_norm):
    if final_norm:
        gf_ref, o_ref, a_ref = rest
    else:
        o_ref, a_ref = rest
    x = x_ref[...]
    h = _rms(x, g_ref[...], EPS).astype(BF16)
    for c in range(D_FF // FF_CHUNK):
        sl = slice(c * FF_CHUNK, (c + 1) * FF_CHUNK)
        gate = jnp.dot(h, wg_ref[:, sl], preferred_element_type=F32)
        up = jnp.dot(h, wu_ref[:, sl], preferred_element_type=F32)
        a_ref[:, sl] = (gate * jax.nn.sigmoid(gate) * up).astype(BF16)
    y = x + 0.5 * jnp.dot(a_ref[...], wd_ref[...], preferred_element_type=F32)
    if final_norm:
        y = _rms(y, gf_ref[...], EPS)
    o_ref[...] = y


def _ffn(x, g, wg, wu, wd, g_final=None):
    final_norm = g_final is not None
    tile = pl.BlockSpec((FFN_TM, D_MODEL), lambda i: (i, 0))
    in_specs = [tile, _resident((1, D_MODEL)), _resident((D_MODEL, D_FF)),
                _resident((D_MODEL, D_FF)), _resident((D_FF, D_MODEL))]
    args = [x, g.reshape(1, D_MODEL), wg, wu, wd]
    if final_norm:
        in_specs.append(_resident((1, D_MODEL)))
        args.append(g_final.reshape(1, D_MODEL))
    return pl.pallas_call(
        functools.partial(_ffn_kernel, final_norm=final_norm),
        grid=(TOKENS // FFN_TM,),
        in_specs=in_specs,
        out_specs=tile,
        out_shape=jax.ShapeDtypeStruct((TOKENS, D_MODEL), F32),
        scratch_shapes=[pltpu.VMEM((FFN_TM, D_FF), BF16)],
        compiler_params=_params("parallel"),
        name="ffn_final" if final_norm else "ffn",
    )(*args)


def _proj_kernel(x_ref, g_ref, wf_ref, wb0_ref, wb1_ref, wb2_ref, wc_ref, wg_ref, bg_ref,
                 pf_ref, pb0_ref, pb1_ref, pb2_ref, pc_ref, gate_ref, y_ref):
    u = _rms(x_ref[...], g_ref[...], EPS).astype(BF16)
    rows = x_ref.shape[0]

    def project(w_ref, store, bias_ref=None):
        width = w_ref.shape[1]
        for c0 in range(0, width, PROJ_CHUNK):
            sl = slice(c0, min(c0 + PROJ_CHUNK, width))
            y = jnp.dot(u, w_ref[:, sl], preferred_element_type=F32)
            if bias_ref is not None:
                y = jax.nn.sigmoid(y + bias_ref[:, sl])
            store(sl, y)

    def natural(o_ref):
        def store(sl, y):
            o_ref[:, sl] = y.astype(o_ref.dtype)
        return store

    def by_residue(o_ref, dil):
        def store(sl, y):
            for j in range((sl.stop - sl.start) // LANES):
                y_ref[j] = y[:, j * LANES:(j + 1) * LANES]
                cols = slice(sl.start + j * LANES, sl.start + (j + 1) * LANES)
                for r in range(dil):
                    o_ref[0, r, :, cols] = y_ref[j, pl.ds(r, rows // dil, stride=dil), :].astype(o_ref.dtype)
        return store

    project(wf_ref, by_residue(pf_ref, 2))
    project(wb0_ref, natural(pb0_ref))
    project(wb1_ref, by_residue(pb1_ref, DIL_PAIRS[1][1]))
    project(wb2_ref, by_residue(pb2_ref, DIL_PAIRS[2][1]))
    project(wc_ref, natural(pc_ref))
    project(wg_ref, natural(gate_ref), bg_ref)


def _proj(x, g, wf, wb, wc, wg, bg):
    tiles_per_seq = SEQ // PROJ_TM

    def tile(w):
        return pl.BlockSpec((PROJ_TM, w), lambda i: (i, 0))

    def residue_tile(dil, w):
        return pl.BlockSpec((1, dil, PROJ_TM // dil, w),
                            lambda i: (i // tiles_per_seq, 0, i % tiles_per_seq, 0))

    def residue_shape(dil, w):
        return jax.ShapeDtypeStruct((BATCH, dil, SEQ // dil, w), BF16)

    d1, d2 = DIL_PAIRS[1][1], DIL_PAIRS[2][1]
    weights = [wf, *wb, wc, wg]
    return pl.pallas_call(
        _proj_kernel,
        grid=(TOKENS // PROJ_TM,),
        in_specs=[tile(D_MODEL), _resident((1, D_MODEL))]
        + [_resident(w.shape) for w in weights] + [_resident((1, GATE_WIDTH))],
        out_specs=[residue_tile(2, 2 * FOURIER_WIDTH), tile(DIL_QKV), residue_tile(d1, DIL_QKV),
                   residue_tile(d2, DIL_QKV), tile(C_WIDTH), tile(GATE_WIDTH)],
        out_shape=[residue_shape(2, 2 * FOURIER_WIDTH),
                   jax.ShapeDtypeStruct((TOKENS, DIL_QKV), BF16),
                   residue_shape(d1, DIL_QKV), residue_shape(d2, DIL_QKV),
                   jax.ShapeDtypeStruct((TOKENS, C_WIDTH), BF16),
                   jax.ShapeDtypeStruct((TOKENS, GATE_WIDTH), BF16)],
        scratch_shapes=[pltpu.VMEM((PROJ_CHUNK // LANES, PROJ_TM, LANES), F32)],
        compiler_params=_params("parallel"),
        name="proj_in",
    )(x, g.reshape(1, D_MODEL), *weights, bg.reshape(1, GATE_WIDTH))


def _fourier_kernel(pc_ref, ps_ref, me_ref, mo_ref, o_ref):
    h = HALF_SEQ
    even = (jnp.dot(me_ref[:, :h], pc_ref[0, 0], preferred_element_type=F32)
            + jnp.dot(me_ref[:, h:], ps_ref[0, 0], preferred_element_type=F32))
    odd = (jnp.dot(mo_ref[:, :h], pc_ref[0, 1], preferred_element_type=F32)
           + jnp.dot(mo_ref[:, h:], ps_ref[0, 1], preferred_element_type=F32))
    o_ref[0, :h, :] = (even + odd).astype(o_ref.dtype)
    o_ref[0, h:, :] = (even - odd).astype(o_ref.dtype)


def _fourier(pf, m_even, m_odd):
    cn = FOURIER_CN
    nblk = FOURIER_WIDTH // cn
    return pl.pallas_call(
        _fourier_kernel,
        grid=(BATCH, nblk),
        in_specs=[pl.BlockSpec((1, 2, HALF_SEQ, cn), lambda b, j: (b, 0, 0, j)),
                  pl.BlockSpec((1, 2, HALF_SEQ, cn), lambda b, j: (b, 0, 0, nblk + j)),
                  _resident((HALF_SEQ, SEQ)), _resident((HALF_SEQ, SEQ))],
        out_specs=pl.BlockSpec((1, SEQ, cn), lambda b, j: (b, 0, j)),
        out_shape=jax.ShapeDtypeStruct((BATCH, SEQ, FOURIER_WIDTH), BF16),
        compiler_params=_params("parallel", "parallel"),
        name="fourier_mix",
    )(pf, pf, m_even, m_odd)


def _dft_matrices():
    half = HALF_SEQ
    k = lax.broadcasted_iota(jnp.int32, (half, half), 0)
    m = lax.broadcasted_iota(jnp.int32, (half, half), 1)
    ang_e = ((k * m) % half).astype(F32) * (2.0 * math.pi / half)
    ang_o = ((k * (2 * m + 1)) % SEQ).astype(F32) * (2.0 * math.pi / SEQ)
    m_even = jnp.concatenate([jnp.cos(ang_e), -jnp.sin(ang_e)], axis=1).astype(BF16)
    m_odd = jnp.concatenate([jnp.cos(ang_o), -jnp.sin(ang_o)], axis=1).astype(BF16)
    return m_even, m_odd


def _channel_dft():
    gd = FOURIER_GROUP_DIM
    idx = np.arange(gd)
    ang = 2.0 * np.pi * ((idx[:, None] * idx[None, :]) % gd) / gd
    scale = 1.0 / math.sqrt(SEQ * gd)
    return (jnp.asarray(np.cos(ang) * scale, F32), jnp.asarray(np.sin(ang) * scale, F32))


def _dil_block(q, kw, vw_aug, bias, head0_lanes):
    zero = jnp.zeros_like(q)
    pv, lse = [], []
    for hh in range(2):
        q_h = jnp.where(head0_lanes, q, zero) if hh == 0 else jnp.where(head0_lanes, zero, q)
        logits = lax.dot_general(q_h, kw, (((1,), (1,)), ((), ())),
                                 preferred_element_type=F32) + bias[hh]
        mx = jnp.max(logits, axis=-1, keepdims=True)
        e = jnp.exp2(logits - mx).astype(BF16)
        r = jnp.dot(e, vw_aug, preferred_element_type=F32)
        s = r[:, LANES:]
        pv.append(r[:, :LANES] / s)
        lse.append(mx + jnp.log2(s))
    return jnp.where(head0_lanes, pv[0], pv[1]), jnp.where(head0_lanes, lse[0], lse[1])


def _dil_kernel(g0_ref, g1_ref, g2_ref, b0_ref, b1_ref, b2_ref, out_ref,
                kpad_ref, vpad_ref, o_ref, lse_ref):
    pad = DIL_RADIUS
    head0_lanes = lax.broadcasted_iota(jnp.int32, (DIL_QB, LANES), 1) < DIL_HEAD_DIM
    zeros_k = jnp.zeros((pad, LANES), BF16)
    zeros_v = jnp.zeros((pad, 2 * LANES), BF16)
    q_sl, k_sl, v_sl = (slice(i * LANES, (i + 1) * LANES) for i in range(3))

    vpad_ref[pl.ds(pad, SEQ), LANES:] = jnp.ones((SEQ, LANES), BF16)

    def fill_padded(src, sub_len):
        kpad_ref[:pad, :] = zeros_k
        kpad_ref[pl.ds(pad, sub_len), :] = src[:, k_sl]
        kpad_ref[pl.ds(pad + sub_len, pad), :] = zeros_k
        vpad_ref[:pad, :] = zeros_v
        vpad_ref[pl.ds(pad, sub_len), :LANES] = src[:, v_sl]
        vpad_ref[pl.ds(pad + sub_len, pad), :] = zeros_v

    def store(group, rows, o, lse):
        o_ref[group, rows, :] = o
        lse_ref[group, rows, :] = lse

    def padded_block(group, q_src, bias_ref, blk, edge, dil, residue):
        r0 = pl.multiple_of(blk * DIL_QB, DIL_QB)
        o, lse = _dil_block(q_src[pl.ds(r0, DIL_QB), q_sl], kpad_ref[pl.ds(r0, DIL_KW), :],
                            vpad_ref[pl.ds(r0, DIL_KW), :], [bias_ref[edge, 0], bias_ref[edge, 1]],
                            head0_lanes)
        rows = pl.ds(residue + r0 * dil, DIL_QB, stride=dil) if dil > 1 else pl.ds(r0, DIL_QB)
        store(group, rows, o, lse)

    fill_padded(g0_ref.at[0], SEQ)
    n_blocks0 = SEQ // DIL_QB

    def g0_body(i, carry):
        for n in range(DIL_UNROLL):
            blk = i * DIL_UNROLL + n
            edge = jnp.where(blk == 0, EDGE_FIRST,
                             jnp.where(blk == n_blocks0 - 1, EDGE_LAST, EDGE_NONE))
            padded_block(0, g0_ref.at[0], b0_ref, blk, edge, 1, 0)
        return carry

    lax.fori_loop(0, n_blocks0 // DIL_UNROLL, g0_body, 0)

    dil1 = DIL_PAIRS[1][1]
    n_blocks1 = SEQ // dil1 // DIL_QB

    def g1_body(r, carry):
        src = g1_ref.at[0, r]
        fill_padded(src, SEQ // dil1)
        for blk in range(n_blocks1):
            edge = EDGE_FIRST if blk == 0 else EDGE_LAST if blk == n_blocks1 - 1 else EDGE_NONE
            padded_block(1, src, b1_ref, blk, edge, dil1, r)
        return carry

    lax.fori_loop(0, dil1, g1_body, 0)

    dil2 = DIL_PAIRS[2][1]
    assert SEQ // dil2 == DIL_QB
    for n in range(DIL_UNROLL):
        base = n * DIL_KW
        for ref, zeros in ((kpad_ref, zeros_k), (vpad_ref, zeros_v)):
            ref[base:base + pad, :] = zeros
            ref[base + pad + DIL_QB:base + DIL_KW, :] = zeros
        vpad_ref[base + pad:base + pad + DIL_QB, LANES:] = jnp.ones((DIL_QB, LANES), BF16)

    def g2_body(i, carry):
        for n in range(DIL_UNROLL):
            r = i * DIL_UNROLL + n
            src = g2_ref.at[0, r]
            base = n * DIL_KW
            kpad_ref[base + pad:base + pad + DIL_QB, :] = src[:, k_sl]
            vpad_ref[base + pad:base + pad + DIL_QB, :LANES] = src[:, v_sl]
            o, lse = _dil_block(src[:, q_sl], kpad_ref[base:base + DIL_KW, :],
                                vpad_ref[base:base + DIL_KW, :],
                                [b2_ref[EDGE_BOTH, 0], b2_ref[EDGE_BOTH, 1]], head0_lanes)
            store(2, pl.ds(r, DIL_QB, stride=dil2), o, lse)
        return carry

    lax.fori_loop(0, dil2 // DIL_UNROLL, g2_body, 0)

    chunk = 256
    for c in range(SEQ // chunk):
        rows = slice(c * chunk, (c + 1) * chunk)
        l0, l1, l2 = lse_ref[0, rows, :], lse_ref[1, rows, :], lse_ref[2, rows, :]
        mx = jnp.maximum(jnp.maximum(l0, l1), l2)
        e0, e1, e2 = jnp.exp2(l0 - mx), jnp.exp2(l1 - mx), jnp.exp2(l2 - mx)
        mixed = (e0 * o_ref[0, rows, :] + e1 * o_ref[1, rows, :] + e2 * o_ref[2, rows, :]) / (e0 + e1 + e2)
        out_ref[0, rows, :] = mixed.astype(out_ref.dtype)


def _dilated_mixture(pb0, pb1, pb2, biases):
    d1, d2 = DIL_PAIRS[1][1], DIL_PAIRS[2][1]
    bias_spec = _resident((4, 2, DIL_QB, DIL_KW))
    max_len = SEQ + 2 * DIL_RADIUS
    return pl.pallas_call(
        _dil_kernel,
        grid=(BATCH,),
        in_specs=[pl.BlockSpec((1, SEQ, DIL_QKV), lambda b: (b, 0, 0)),
                  pl.BlockSpec((1, d1, SEQ // d1, DIL_QKV), lambda b: (b, 0, 0, 0)),
                  pl.BlockSpec((1, d2, SEQ // d2, DIL_QKV), lambda b: (b, 0, 0, 0)),
                  bias_spec, bias_spec, bias_spec],
        out_specs=pl.BlockSpec((1, SEQ, LANES), lambda b: (b, 0, 0)),
        out_shape=jax.ShapeDtypeStruct((BATCH, SEQ, DIL_OUT_WIDTH), BF16),
        scratch_shapes=[pltpu.VMEM((max_len, LANES), BF16), pltpu.VMEM((max_len, 2 * LANES), BF16),
                        pltpu.VMEM((DIL_GROUPS, SEQ, LANES), F32),
                        pltpu.VMEM((DIL_GROUPS, SEQ, LANES), F32)],
        compiler_params=_params("parallel"),
        name="dilated_attn",
    )(pb0, pb1, pb2, *biases)


def _diff_kernel(q_ref, k_ref, v_ref, bias_ref, lq1_ref, lk1_ref, lq2_ref, lk2_ref, sg_ref,
                 o_ref, vaug_ref, logit_ref, rmax_ref, *, lam_init):
    s = pl.program_id(0)
    n_blocks = pl.num_programs(0) - 1
    nq = SEQ // DIFF_TQ
    cur = jnp.minimum(s, n_blocks - 1)
    prev = jnp.maximum(s - 1, 0)
    slot = s % 2

    @pl.when(s == 0)
    def _():
        logit_ref[1] = jnp.zeros(logit_ref.shape[1:], F32)
        rmax_ref[1] = jnp.zeros(rmax_ref.shape[1:], F32)

    @pl.when(prev % nq == 0)
    def _():
        vaug_ref[:, :LANES] = v_ref[0]
        vaug_ref[:, LANES:] = jnp.ones((SEQ, LANES), BF16)

    n_tiles = SEQ // DIFF_KT

    def key_tile(t):
        return slice(t * DIFF_KT, (t + 1) * DIFF_KT)

    off = pl.multiple_of((nq - 1 - cur % nq) * DIFF_TQ, DIFF_TQ)
    q = q_ref[0]
    first_half = lax.broadcasted_iota(jnp.int32, q.shape, 1) < DIFF_HEAD_DIM
    zero = jnp.zeros_like(q)
    outs = []
    for m in range(2):
        row_max = rmax_ref[1 - slot, m]
        pv = jnp.zeros((DIFF_TQ, 2 * LANES), F32)
        for t in range(n_tiles):
            old = logit_ref[1 - slot, m, :, key_tile(t)]
            e = jnp.concatenate([jnp.exp2(old[:, c * LANES:(c + 1) * LANES] - row_max)
                                 for c in range(DIFF_KT // LANES)], axis=1).astype(BF16)
            pv = pv + jnp.dot(e, vaug_ref[key_tile(t), :], preferred_element_type=F32)
        outs.append(pv[:, :LANES] / pv[:, LANES:])

    for m in range(2):
        q_m = jnp.where(first_half, q, zero) if m == 0 else jnp.where(first_half, zero, q)
        lane_max = None
        for t in range(n_tiles):
            bias = bias_ref[0, :, pl.ds(pl.multiple_of(off + t * DIFF_KT, LANES), DIFF_KT)]
            logits = lax.dot_general(q_m, k_ref[0, key_tile(t), :], (((1,), (1,)), ((), ())),
                                     preferred_element_type=F32) + bias
            logit_ref[slot, m, :, key_tile(t)] = logits
            for c in range(DIFF_KT // LANES):
                part = logits[:, c * LANES:(c + 1) * LANES]
                lane_max = part if lane_max is None else jnp.maximum(lane_max, part)
        rmax_ref[slot, m] = jnp.broadcast_to(jnp.max(lane_max, axis=-1, keepdims=True),
                                             (DIFF_TQ, LANES))

    lam = (jnp.exp(jnp.sum(lq1_ref[...] * lk1_ref[...], axis=-1, keepdims=True))
           - jnp.exp(jnp.sum(lq2_ref[...] * lk2_ref[...], axis=-1, keepdims=True))
           + lam_init)
    o = outs[0] - lam * outs[1]
    o = _rms(o, sg_ref[...], SUBLN_EPS) * (1.0 - lam_init)
    o_ref[0] = o.astype(o_ref.dtype)


def _diff_attention(pc, strip, lq1, lk1, lq2, lk2, subln_g, lam_init):
    pcv = pc.reshape(BATCH, SEQ, C_WIDTH)
    vec = _resident((1, DIFF_HEAD_DIM))
    nq = SEQ // DIFF_TQ
    n_blocks = DIFF_HEADS * BATCH * nq

    def block_of(step, lag):
        blk = jnp.clip(step - lag, 0, n_blocks - 1)
        return blk // (BATCH * nq), (blk // nq) % BATCH, blk % nq

    def q_map(s):
        h, b, j = block_of(s, 0)
        return b, j, h

    def k_map(s):
        h, b, _ = block_of(s, 0)
        return b, 0, DIFF_HEADS + h

    def v_map(s):
        h, b, _ = block_of(s, 1)
        return b, 0, 2 * DIFF_HEADS + h

    def out_map(s):
        h, b, j = block_of(s, 1)
        return b, j, h

    return pl.pallas_call(
        functools.partial(_diff_kernel, lam_init=lam_init),
        grid=(n_blocks + 1,),
        in_specs=[pl.BlockSpec((1, DIFF_TQ, LANES), q_map),
                  pl.BlockSpec((1, SEQ, LANES), k_map),
                  pl.BlockSpec((1, SEQ, LANES), v_map),
                  pl.BlockSpec((1, DIFF_TQ, DIFF_STRIP), lambda s: (block_of(s, 0)[0], 0, 0)),
                  vec, vec, vec, vec, _resident((1, DIFF_V_DIM))],
        out_specs=pl.BlockSpec((1, DIFF_TQ, LANES), out_map),
        out_shape=jax.ShapeDtypeStruct((BATCH, SEQ, DIFF_V_WIDTH), BF16),
        scratch_shapes=[pltpu.VMEM((SEQ, 2 * LANES), BF16),
                        pltpu.VMEM((2, 2, DIFF_TQ, SEQ), F32),
                        pltpu.VMEM((2, 2, DIFF_TQ, LANES), F32)],
        compiler_params=_params("arbitrary"),
        name="diff_attn",
    )(pcv, pcv, pcv, strip, lq1.reshape(1, -1), lk1.reshape(1, -1), lq2.reshape(1, -1),
      lk2.reshape(1, -1), subln_g.reshape(1, -1)).reshape(TOKENS, DIFF_V_WIDTH)


def _merge_kernel(x_ref, f_ref, b_ref, c_ref, gate_ref, wa_ref, wb_ref, wc_ref, wo_ref, out_ref):
    y_a = jnp.dot(f_ref[...], wa_ref[...], preferred_element_type=F32)
    y_b = jnp.dot(b_ref[...], wb_ref[...], preferred_element_type=F32)
    y_c = jnp.dot(c_ref[...], wc_ref[...], preferred_element_type=F32)
    merged = (gate_ref[:, :D_MODEL].astype(F32) * y_a
              + gate_ref[:, D_MODEL:2 * D_MODEL].astype(F32) * y_b
              + gate_ref[:, 2 * D_MODEL:].astype(F32) * y_c)
    out_ref[...] = x_ref[...] + jnp.dot(merged.astype(BF16), wo_ref[...],
                                        preferred_element_type=F32)


def _merge(x, f, b, c, gates, wa, wb, wc, wo):
    def tile(w):
        return pl.BlockSpec((MERGE_TM, w), lambda i: (i, 0))

    return pl.pallas_call(
        _merge_kernel,
        grid=(TOKENS // MERGE_TM,),
        in_specs=[tile(D_MODEL), tile(FOURIER_WIDTH), tile(DIL_OUT_WIDTH), tile(DIFF_V_WIDTH),
                  tile(GATE_WIDTH), _resident(wa.shape), _resident(wb.shape),
                  _resident(wc.shape), _resident(wo.shape)],
        out_specs=tile(D_MODEL),
        out_shape=jax.ShapeDtypeStruct((TOKENS, D_MODEL), F32),
        compiler_params=_params("parallel"),
        name="gated_merge",
    )(x, f, b, c, gates, wa, wb, wc, wo)


def kernel(x, g_ffn1, w_ffn1_gate, w_ffn1_up, w_ffn1_down, g_mix, w_in, w_gate, b_gate,
           w_br_a, w_br_b, w_br_c, w_out, lam_q1, lam_k1, lam_q2, lam_k2, subln_g,
           rel_bias, g_ffn2, w_ffn2_gate, w_ffn2_up, w_ffn2_down, g_final):
    assert x.shape == (BATCH, SEQ, D_MODEL) and x.dtype == F32
    bf = lambda w: w.astype(BF16)
    c0 = FOURIER_WIDTH
    c3 = c0 + 3 * DIL_WIDTH

    dil_bias = [_dil_bias_tiles(rel_bias, g, dil) for g, (_, dil) in enumerate(DIL_PAIRS)]
    diff_strip = _diff_bias_strip(rel_bias)
    m_even, m_odd = _dft_matrices()
    dc, ds = _channel_dft()

    def dil_weights(w_in_l, g):
        lo = c0 + g * LANES
        q, k, v = (w_in_l[:, lo + i * DIL_WIDTH: lo + i * DIL_WIDTH + LANES] for i in range(3))
        return bf(jnp.concatenate([q * (QK_SCALE * LOG2_E), k, v], axis=1))

    q_scale_c = jnp.concatenate([jnp.full((DIFF_QK_WIDTH,), QK_SCALE * LOG2_E, F32),
                                 jnp.ones((DIFF_QK_WIDTH + DIFF_V_WIDTH,), F32)])

    h = x.reshape(TOKENS, D_MODEL)
    for l in range(DEPTH):
        h = _ffn(h, g_ffn1[l], bf(w_ffn1_gate[l]), bf(w_ffn1_up[l]), bf(w_ffn1_down[l]))

        wf = _fold_channel_dft(w_in[l, :, :c0], dc, ds)
        pf, pb0, pb1, pb2, pc, gates = _proj(
            h, g_mix[l], wf, [dil_weights(w_in[l], g) for g in range(DIL_GROUPS)],
            bf(w_in[l, :, c3:] * q_scale_c), bf(w_gate[l]), b_gate[l])

        f = _fourier(pf, m_even, m_odd).reshape(TOKENS, FOURIER_WIDTH)
        mixed = _dilated_mixture(pb0.reshape(BATCH, SEQ, DIL_QKV), pb1, pb2, dil_bias)
        lam_init = 0.8 - 0.6 * math.exp(-0.3 * l)
        c = _diff_attention(pc, diff_strip, lam_q1[l], lam_k1[l], lam_q2[l], lam_k2[l],
                            subln_g[l], lam_init)

        h = _merge(h, f, mixed.reshape(TOKENS, DIL_OUT_WIDTH), c, gates,
                   bf(w_br_a[l]), bf(w_br_b[l]), bf(w_br_c[l]), bf(w_out[l]))

        h = _ffn(h, g_ffn2[l], bf(w_ffn2_gate[l]), bf(w_ffn2_up[l]), bf(w_ffn2_down[l]),
                 g_final if l == DEPTH - 1 else None)
    return h.reshape(BATCH, SEQ, D_MODEL)
```

```python
import functools
import math

import numpy as np
import jax
import jax.numpy as jnp
from jax import lax
from jax.experimental import pallas as pl
from jax.experimental.pallas import tpu as pltpu

F32 = jnp.float32
BF16 = jnp.bfloat16

D_MODEL = 1024
BATCH = 8
SEQ = 2048
DEPTH = 2
TOKENS = BATCH * SEQ
D_FF = 2816
EPS = 1e-6
FOURIER_GROUPS = 8
FOURIER_GROUP_DIM = 128
FOURIER_WIDTH = FOURIER_GROUPS * FOURIER_GROUP_DIM
DIL_PAIRS = ((128, 1), (512, 4), (2048, 16))
DIL_GROUPS = len(DIL_PAIRS)
DIL_HEAD_DIM = 64
DIL_HEADS = 2 * DIL_GROUPS
DIL_WIDTH = DIL_HEADS * DIL_HEAD_DIM
DIL_OUT_WIDTH = 2 * DIL_HEAD_DIM
DIL_RADIUS = 64
DIFF_HEADS = 4
DIFF_HEAD_DIM = 64
DIFF_QK_WIDTH = DIFF_HEADS * 2 * DIFF_HEAD_DIM
DIFF_V_DIM = 2 * DIFF_HEAD_DIM
DIFF_V_WIDTH = DIFF_HEADS * DIFF_V_DIM
SUBLN_EPS = 1e-5
NUM_BUCKETS = 32
MAX_DISTANCE = 1024
N_BIAS_HEADS = DIL_HEADS + DIFF_HEADS
NEG_INF = -1e30
QK_SCALE = DIL_HEAD_DIM ** -0.5
LOG2_E = math.log2(math.e)
C_WIDTH = 2 * DIFF_QK_WIDTH + DIFF_V_WIDTH
GATE_WIDTH = 3 * D_MODEL

LANES = 128
VMEM_BYTES_V7X = 64 * 1024 * 1024
VMEM_LIMIT = 56 * 1024 * 1024

FFN_TM = 1024
FF_CHUNK = 256
PROJ_TM = 512
PROJ_CHUNK = 512
MERGE_TM = 512
HALF_SEQ = SEQ // 2
FOURIER_CN = 512
DIL_QKV = 3 * LANES
DIL_QB = 128
DIL_KW = DIL_QB + 2 * DIL_RADIUS
DIL_UNROLL = 4
DIFF_TQ = 512
DIFF_STRIP = 2 * SEQ - DIFF_TQ
DIFF_KT = 512


def _params(*semantics):
    return pltpu.CompilerParams(dimension_semantics=semantics, vmem_limit_bytes=VMEM_LIMIT)


def _resident(shape):
    nd = len(shape)
    return pl.BlockSpec(shape, lambda *_: (0,) * nd, pipeline_mode=pl.Buffered(1))


def _rms(x, g, eps):
    return x * lax.rsqrt(jnp.mean(x * x, axis=-1, keepdims=True) + eps) * g


BF16_SUBLANES = 16


def _cast_specs(jobs, steps):
    in_specs, out_specs, out_shapes = [], [], []
    for src, layer in jobs:
        _, rows, cols = src.shape
        share = 1
        while rows % (steps // share * BF16_SUBLANES):
            share *= 2
        blk = rows // (steps // share)
        in_specs.append(pl.BlockSpec((None, blk, cols),
                                     lambda i, layer=layer, share=share: (layer, i // share, 0)))
        out_specs.append(pl.BlockSpec((blk, cols), lambda i, share=share: (i // share, 0)))
        out_shapes.append(jax.ShapeDtypeStruct((rows, cols), BF16))
    return in_specs, out_specs, out_shapes


def _cast_blocks(src_refs, dst_refs):
    for src, dst in zip(src_refs, dst_refs):
        dst[...] = src[...].astype(BF16)


def _bucket_breakpoints():
    rel = np.arange(-(SEQ - 1), SEQ)
    half = NUM_BUCKETS // 2
    max_exact = half // 2
    n = np.abs(rel)
    nf = np.maximum(n, 1).astype(np.float64)
    large = max_exact + (np.log(nf / max_exact) / math.log(MAX_DISTANCE / max_exact)
                         * (half - max_exact)).astype(np.int32)
    large = np.minimum(large, half - 1)
    b = np.where(rel > 0, half, 0) + np.where(n < max_exact, n, large)
    pts = [(int(rel[i]), int(b[i])) for i in range(1, len(rel)) if b[i] != b[i - 1]]
    return int(b[0]), pts


def _bias_values(tbl_ref, head, rel):
    first, pts = _bucket_breakpoints()
    val = jnp.full(rel.shape, tbl_ref[first, head], F32)
    for thr, bucket in pts:
        val = jnp.where(rel >= thr, tbl_ref[bucket, head], val)
    return val * LOG2_E


def _diff_bias_kernel(tbl_ref, o_ref):
    rows, cols = o_ref.shape[1], o_ref.shape[2]
    rel = (lax.broadcasted_iota(jnp.int32, (rows, cols), 1)
           - lax.broadcasted_iota(jnp.int32, (rows, cols), 0) - (SEQ - rows))
    o_ref[0] = _bias_values(tbl_ref, DIL_HEADS + pl.program_id(0), rel)


def _diff_bias_strip(rel_bias):
    return pl.pallas_call(
        _diff_bias_kernel,
        grid=(DIFF_HEADS,),
        in_specs=[pl.BlockSpec(memory_space=pltpu.SMEM)],
        out_specs=pl.BlockSpec((1, DIFF_TQ, DIFF_STRIP), lambda h: (h, 0, 0)),
        out_shape=jax.ShapeDtypeStruct((DIFF_HEADS, DIFF_TQ, DIFF_STRIP), F32),
        compiler_params=_params("arbitrary"),
        name="diff_bias_strip",
    )(rel_bias)


EDGE_FIRST, EDGE_NONE, EDGE_LAST, EDGE_BOTH = range(4)


def _dil_bias_kernel(tbl_ref, o_ref, *, head0, dil):
    rows, cols = o_ref.shape[2], o_ref.shape[3]
    col = lax.broadcasted_iota(jnp.int32, (rows, cols), 1)
    rel_t = col - lax.broadcasted_iota(jnp.int32, (rows, cols), 0) - DIL_RADIUS
    val = _bias_values(tbl_ref, head0 + pl.program_id(0), rel_t * dil)
    val = jnp.where(jnp.abs(rel_t) <= DIL_RADIUS, val, NEG_INF)
    lo = jnp.where(col >= DIL_RADIUS, val, NEG_INF)
    o_ref[EDGE_FIRST, 0] = lo
    o_ref[EDGE_NONE, 0] = val
    o_ref[EDGE_LAST, 0] = jnp.where(col < cols - DIL_RADIUS, val, NEG_INF)
    o_ref[EDGE_BOTH, 0] = jnp.where(col < cols - DIL_RADIUS, lo, NEG_INF)


def _dil_bias_tiles(rel_bias, group, dil):
    return pl.pallas_call(
        functools.partial(_dil_bias_kernel, head0=2 * group, dil=dil),
        grid=(2,),
        in_specs=[pl.BlockSpec(memory_space=pltpu.SMEM)],
        out_specs=pl.BlockSpec((4, 1, DIL_QB, DIL_KW), lambda h: (0, h, 0, 0)),
        out_shape=jax.ShapeDtypeStruct((4, 2, DIL_QB, DIL_KW), F32),
        compiler_params=_params("arbitrary"),
        name="dil_bias_tiles",
    )(rel_bias)


def _fold_kernel(w_ref, dc_ref, ds_ref, o_ref):
    w = w_ref[...]
    o_ref[0] = jnp.dot(w, dc_ref[...], preferred_element_type=F32,
                       precision=lax.Precision.HIGHEST).astype(BF16)
    o_ref[1] = jnp.dot(w, ds_ref[...], preferred_element_type=F32,
                       precision=lax.Precision.HIGHEST).astype(BF16)


def _fold_channel_dft(w_a, dc, ds):
    gd = FOURIER_GROUP_DIM
    mat = pl.BlockSpec((gd, gd), lambda g: (0, 0))
    out = pl.pallas_call(
        _fold_kernel,
        grid=(FOURIER_GROUPS,),
        in_specs=[pl.BlockSpec((D_MODEL, gd), lambda g: (0, g)), mat, mat],
        out_specs=pl.BlockSpec((2, D_MODEL, gd), lambda g: (0, 0, g)),
        out_shape=jax.ShapeDtypeStruct((2, D_MODEL, FOURIER_WIDTH), BF16),
        compiler_params=_params("arbitrary"),
        name="fold_channel_dft",
    )(w_a, dc, ds)
    return jnp.concatenate([out[0], out[1]], axis=1)


def _ffn_kernel(x_ref, g_ref, wg_ref, wu_ref, wd_ref, *rest, final_norm, n_cast):
    rest = list(rest)
    gf_ref = rest.pop(0) if final_norm else None
    cast_src, o_ref, cast_dst, a_ref = (rest[:n_cast], rest[n_cast], rest[n_cast + 1:-1], rest[-1])
    _cast_blocks(cast_src, cast_dst)
    x = x_ref[...]
    h = _rms(x, g_ref[...], EPS).astype(BF16)
    for c in range(D_FF // FF_CHUNK):
        sl = slice(c * FF_CHUNK, (c + 1) * FF_CHUNK)
        gate = jnp.dot(h, wg_ref[:, sl], preferred_element_type=F32)
        up = jnp.dot(h, wu_ref[:, sl], preferred_element_type=F32)
        a_ref[:, sl] = (gate * jax.nn.sigmoid(gate) * up).astype(BF16)
    y = x + 0.5 * jnp.dot(a_ref[...], wd_ref[...], preferred_element_type=F32)
    if final_norm:
        y = _rms(y, gf_ref[...], EPS)
    o_ref[...] = y


def _ffn(x, g, wg, wu, wd, g_final=None, cast_jobs=()):
    final_norm = g_final is not None
    steps = TOKENS // FFN_TM
    tile = pl.BlockSpec((FFN_TM, D_MODEL), lambda i: (i, 0))
    in_specs = [tile, _resident((1, D_MODEL)), _resident((D_MODEL, D_FF)),
                _resident((D_MODEL, D_FF)), _resident((D_FF, D_MODEL))]
    args = [x, g.reshape(1, D_MODEL), wg, wu, wd]
    if final_norm:
        in_specs.append(_resident((1, D_MODEL)))
        args.append(g_final.reshape(1, D_MODEL))
    cast_in, cast_out, cast_shapes = _cast_specs(cast_jobs, steps)
    return pl.pallas_call(
        functools.partial(_ffn_kernel, final_norm=final_norm, n_cast=len(cast_jobs)),
        grid=(steps,),
        in_specs=in_specs + cast_in,
        out_specs=[tile] + cast_out,
        out_shape=[jax.ShapeDtypeStruct((TOKENS, D_MODEL), F32)] + cast_shapes,
        scratch_shapes=[pltpu.VMEM((FFN_TM, D_FF), BF16)],
        compiler_params=_params("arbitrary"),
        name="ffn_final" if final_norm else "ffn",
    )(*args, *[src for src, _ in cast_jobs])


def _proj_kernel(x_ref, g_ref, wf_ref, wb0_ref, wb1_ref, wb2_ref, wc_ref, wg_ref, bg_ref,
                 *rest, n_cast):
    cast_src, rest = rest[:n_cast], rest[n_cast:]
    pf_ref, pb0_ref, pb1_ref, pb2_ref, pc_ref, gate_ref = rest[:6]
    cast_dst, y_ref = rest[6:-1], rest[-1]
    _cast_blocks(cast_src, cast_dst)
    u = _rms(x_ref[...], g_ref[...], EPS).astype(BF16)
    rows = x_ref.shape[0]

    def project(w_ref, store, bias_ref=None):
        width = w_ref.shape[1]
        for c0 in range(0, width, PROJ_CHUNK):
            sl = slice(c0, min(c0 + PROJ_CHUNK, width))
            y = jnp.dot(u, w_ref[:, sl], preferred_element_type=F32)
            if bias_ref is not None:
                y = jax.nn.sigmoid(y + bias_ref[:, sl])
            store(sl, y)

    def natural(o_ref):
        def store(sl, y):
            o_ref[:, sl] = y.astype(o_ref.dtype)
        return store

    def by_residue(o_ref, dil):
        def store(sl, y):
            for j in range((sl.stop - sl.start) // LANES):
                y_ref[j] = y[:, j * LANES:(j + 1) * LANES]
                cols = slice(sl.start + j * LANES, sl.start + (j + 1) * LANES)
                for r in range(dil):
                    o_ref[0, r, :, cols] = y_ref[j, pl.ds(r, rows // dil, stride=dil), :].astype(o_ref.dtype)
        return store

    project(wf_ref, by_residue(pf_ref, 2))
    project(wb0_ref, natural(pb0_ref))
    project(wb1_ref, by_residue(pb1_ref, DIL_PAIRS[1][1]))
    project(wb2_ref, by_residue(pb2_ref, DIL_PAIRS[2][1]))
    project(wc_ref, natural(pc_ref))
    project(wg_ref, natural(gate_ref), bg_ref)


def _proj(x, g, wf, wb, wc, wg, bg, cast_jobs=()):
    tiles_per_seq = SEQ // PROJ_TM
    steps = TOKENS // PROJ_TM
    cast_in, cast_out, cast_shapes = _cast_specs(cast_jobs, steps)

    def tile(w):
        return pl.BlockSpec((PROJ_TM, w), lambda i: (i, 0))

    def residue_tile(dil, w):
        return pl.BlockSpec((1, dil, PROJ_TM // dil, w),
                            lambda i: (i // tiles_per_seq, 0, i % tiles_per_seq, 0))

    def residue_shape(dil, w):
        return jax.ShapeDtypeStruct((BATCH, dil, SEQ // dil, w), BF16)

    d1, d2 = DIL_PAIRS[1][1], DIL_PAIRS[2][1]
    weights = [wf, *wb, wc, wg]
    return pl.pallas_call(
        functools.partial(_proj_kernel, n_cast=len(cast_jobs)),
        grid=(steps,),
        in_specs=[tile(D_MODEL), _resident((1, D_MODEL))]
        + [_resident(w.shape) for w in weights] + [_resident((1, GATE_WIDTH))] + cast_in,
        out_specs=[residue_tile(2, 2 * FOURIER_WIDTH), tile(DIL_QKV), residue_tile(d1, DIL_QKV),
                   residue_tile(d2, DIL_QKV), tile(C_WIDTH), tile(GATE_WIDTH)] + cast_out,
        out_shape=[residue_shape(2, 2 * FOURIER_WIDTH),
                   jax.ShapeDtypeStruct((TOKENS, DIL_QKV), BF16),
                   residue_shape(d1, DIL_QKV), residue_shape(d2, DIL_QKV),
                   jax.ShapeDtypeStruct((TOKENS, C_WIDTH), BF16),
                   jax.ShapeDtypeStruct((TOKENS, GATE_WIDTH), BF16)] + cast_shapes,
        scratch_shapes=[pltpu.VMEM((PROJ_CHUNK // LANES, PROJ_TM, LANES), F32)],
        compiler_params=_params("arbitrary"),
        name="proj_in",
    )(x, g.reshape(1, D_MODEL), *weights, bg.reshape(1, GATE_WIDTH),
      *[src for src, _ in cast_jobs])


def _fourier_kernel(pc_ref, ps_ref, me_ref, mo_ref, o_ref):
    h = HALF_SEQ
    even = (jnp.dot(me_ref[:, :h], pc_ref[0, 0], preferred_element_type=F32)
            + jnp.dot(me_ref[:, h:], ps_ref[0, 0], preferred_element_type=F32))
    odd = (jnp.dot(mo_ref[:, :h], pc_ref[0, 1], preferred_element_type=F32)
           + jnp.dot(mo_ref[:, h:], ps_ref[0, 1], preferred_element_type=F32))
    o_ref[0, :h, :] = (even + odd).astype(o_ref.dtype)
    o_ref[0, h:, :] = (even - odd).astype(o_ref.dtype)


def _fourier(pf, m_even, m_odd):
    cn = FOURIER_CN
    nblk = FOURIER_WIDTH // cn
    return pl.pallas_call(
        _fourier_kernel,
        grid=(BATCH, nblk),
        in_specs=[pl.BlockSpec((1, 2, HALF_SEQ, cn), lambda b, j: (b, 0, 0, j)),
                  pl.BlockSpec((1, 2, HALF_SEQ, cn), lambda b, j: (b, 0, 0, nblk + j)),
                  _resident((HALF_SEQ, SEQ)), _resident((HALF_SEQ, SEQ))],
        out_specs=pl.BlockSpec((1, SEQ, cn), lambda b, j: (b, 0, j)),
        out_shape=jax.ShapeDtypeStruct((BATCH, SEQ, FOURIER_WIDTH), BF16),
        compiler_params=_params("parallel", "parallel"),
        name="fourier_mix",
    )(pf, pf, m_even, m_odd)


def _dft_matrices():
    half = HALF_SEQ
    k = lax.broadcasted_iota(jnp.int32, (half, half), 0)
    m = lax.broadcasted_iota(jnp.int32, (half, half), 1)
    ang_e = ((k * m) % half).astype(F32) * (2.0 * math.pi / half)
    ang_o = ((k * (2 * m + 1)) % SEQ).astype(F32) * (2.0 * math.pi / SEQ)
    m_even = jnp.concatenate([jnp.cos(ang_e), -jnp.sin(ang_e)], axis=1).astype(BF16)
    m_odd = jnp.concatenate([jnp.cos(ang_o), -jnp.sin(ang_o)], axis=1).astype(BF16)
    return m_even, m_odd


def _channel_dft():
    gd = FOURIER_GROUP_DIM
    idx = np.arange(gd)
    ang = 2.0 * np.pi * ((idx[:, None] * idx[None, :]) % gd) / gd
    scale = 1.0 / math.sqrt(SEQ * gd)
    return (jnp.asarray(np.cos(ang) * scale, F32), jnp.asarray(np.sin(ang) * scale, F32))


def _dil_block(q, kw, vw_aug, bias, head0_lanes):
    zero = jnp.zeros_like(q)
    pv, lse = [], []
    for hh in range(2):
        q_h = jnp.where(head0_lanes, q, zero) if hh == 0 else jnp.where(head0_lanes, zero, q)
        logits = lax.dot_general(q_h, kw, (((1,), (1,)), ((), ())),
                                 preferred_element_type=F32) + bias[hh]
        mx = jnp.max(logits, axis=-1, keepdims=True)
        e = jnp.exp2(logits - mx).astype(BF16)
        r = jnp.dot(e, vw_aug, preferred_element_type=F32)
        s = r[:, LANES:]
        pv.append(r[:, :LANES] / s)
        lse.append(mx + jnp.log2(s))
    return jnp.where(head0_lanes, pv[0], pv[1]), jnp.where(head0_lanes, lse[0], lse[1])


def _dil_kernel(g0_ref, g1_ref, g2_ref, b0_ref, b1_ref, b2_ref, out_ref,
                kpad_ref, vpad_ref, o_ref, lse_ref):
    pad = DIL_RADIUS
    head0_lanes = lax.broadcasted_iota(jnp.int32, (DIL_QB, LANES), 1) < DIL_HEAD_DIM
    zeros_k = jnp.zeros((pad, LANES), BF16)
    zeros_v = jnp.zeros((pad, 2 * LANES), BF16)
    q_sl, k_sl, v_sl = (slice(i * LANES, (i + 1) * LANES) for i in range(3))

    vpad_ref[pl.ds(pad, SEQ), LANES:] = jnp.ones((SEQ, LANES), BF16)

    def fill_padded(src, sub_len):
        kpad_ref[:pad, :] = zeros_k
        kpad_ref[pl.ds(pad, sub_len), :] = src[:, k_sl]
        kpad_ref[pl.ds(pad + sub_len, pad), :] = zeros_k
        vpad_ref[:pad, :] = zeros_v
        vpad_ref[pl.ds(pad, sub_len), :LANES] = src[:, v_sl]
        vpad_ref[pl.ds(pad + sub_len, pad), :] = zeros_v

    def store(group, rows, o, lse):
        o_ref[group, rows, :] = o
        lse_ref[group, rows, :] = lse

    def padded_block(group, q_src, bias_ref, blk, edge, dil, residue):
        r0 = pl.multiple_of(blk * DIL_QB, DIL_QB)
        o, lse = _dil_block(q_src[pl.ds(r0, DIL_QB), q_sl], kpad_ref[pl.ds(r0, DIL_KW), :],
                            vpad_ref[pl.ds(r0, DIL_KW), :], [bias_ref[edge, 0], bias_ref[edge, 1]],
                            head0_lanes)
        rows = pl.ds(residue + r0 * dil, DIL_QB, stride=dil) if dil > 1 else pl.ds(r0, DIL_QB)
        store(group, rows, o, lse)

    fill_padded(g0_ref.at[0], SEQ)
    n_blocks0 = SEQ // DIL_QB

    def g0_body(i, carry):
        for n in range(DIL_UNROLL):
            blk = i * DIL_UNROLL + n
            edge = jnp.where(blk == 0, EDGE_FIRST,
                             jnp.where(blk == n_blocks0 - 1, EDGE_LAST, EDGE_NONE))
            padded_block(0, g0_ref.at[0], b0_ref, blk, edge, 1, 0)
        return carry

    lax.fori_loop(0, n_blocks0 // DIL_UNROLL, g0_body, 0)

    dil1 = DIL_PAIRS[1][1]
    n_blocks1 = SEQ // dil1 // DIL_QB

    def g1_body(r, carry):
        src = g1_ref.at[0, r]
        fill_padded(src, SEQ // dil1)
        for blk in range(n_blocks1):
            edge = EDGE_FIRST if blk == 0 else EDGE_LAST if blk == n_blocks1 - 1 else EDGE_NONE
            padded_block(1, src, b1_ref, blk, edge, dil1, r)
        return carry

    lax.fori_loop(0, dil1, g1_body, 0)

    dil2 = DIL_PAIRS[2][1]
    assert SEQ // dil2 == DIL_QB
    for n in range(DIL_UNROLL):
        base = n * DIL_KW
        for ref, zeros in ((kpad_ref, zeros_k), (vpad_ref, zeros_v)):
            ref[base:base + pad, :] = zeros
            ref[base + pad + DIL_QB:base + DIL_KW, :] = zeros
        vpad_ref[base + pad:base + pad + DIL_QB, LANES:] = jnp.ones((DIL_QB, LANES), BF16)

    def g2_body(i, carry):
        for n in range(DIL_UNROLL):
            r = i * DIL_UNROLL + n
            src = g2_ref.at[0, r]
            base = n * DIL_KW
            kpad_ref[base + pad:base + pad + DIL_QB, :] = src[:, k_sl]
            vpad_ref[base + pad:base + pad + DIL_QB, :LANES] = src[:, v_sl]
            o, lse = _dil_block(src[:, q_sl], kpad_ref[base:base + DIL_KW, :],
                                vpad_ref[base:base + DIL_KW, :],
                                [b2_ref[EDGE_BOTH, 0], b2_ref[EDGE_BOTH, 1]], head0_lanes)
            store(2, pl.ds(r, DIL_QB, stride=dil2), o, lse)
        return carry

    lax.fori_loop(0, dil2 // DIL_UNROLL, g2_body, 0)

    chunk = 256
    for c in range(SEQ // chunk):
        rows = slice(c * chunk, (c + 1) * chunk)
        l0, l1, l2 = lse_ref[0, rows, :], lse_ref[1, rows, :], lse_ref[2, rows, :]
        mx = jnp.maximum(jnp.maximum(l0, l1), l2)
        e0, e1, e2 = jnp.exp2(l0 - mx), jnp.exp2(l1 - mx), jnp.exp2(l2 - mx)
        mixed = (e0 * o_ref[0, rows, :] + e1 * o_ref[1, rows, :] + e2 * o_ref[2, rows, :]) / (e0 + e1 + e2)
        out_ref[0, rows, :] = mixed.astype(out_ref.dtype)


def _dilated_mixture(pb0, pb1, pb2, biases):
    d1, d2 = DIL_PAIRS[1][1], DIL_PAIRS[2][1]
    bias_spec = _resident((4, 2, DIL_QB, DIL_KW))
    max_len = SEQ + 2 * DIL_RADIUS
    return pl.pallas_call(
        _dil_kernel,
        grid=(BATCH,),
        in_specs=[pl.BlockSpec((1, SEQ, DIL_QKV), lambda b: (b, 0, 0)),
                  pl.BlockSpec((1, d1, SEQ // d1, DIL_QKV), lambda b: (b, 0, 0, 0)),
                  pl.BlockSpec((1, d2, SEQ // d2, DIL_QKV), lambda b: (b, 0, 0, 0)),
                  bias_spec, bias_spec, bias_spec],
        out_specs=pl.BlockSpec((1, SEQ, LANES), lambda b: (b, 0, 0)),
        out_shape=jax.ShapeDtypeStruct((BATCH, SEQ, DIL_OUT_WIDTH), BF16),
        scratch_shapes=[pltpu.VMEM((max_len, LANES), BF16), pltpu.VMEM((max_len, 2 * LANES), BF16),
                        pltpu.VMEM((DIL_GROUPS, SEQ, LANES), F32),
                        pltpu.VMEM((DIL_GROUPS, SEQ, LANES), F32)],
        compiler_params=_params("parallel"),
        name="dilated_attn",
    )(pb0, pb1, pb2, *biases)


def _diff_kernel(q_ref, k_ref, v_ref, bias_ref, lq1_ref, lk1_ref, lq2_ref, lk2_ref, sg_ref,
                 o_ref, vaug_ref, logit_ref, rmax_ref, *, lam_init):
    s = pl.program_id(0)
    n_blocks = pl.num_programs(0) - 1
    nq = SEQ // DIFF_TQ
    cur = jnp.minimum(s, n_blocks - 1)
    prev = jnp.maximum(s - 1, 0)
    slot = s % 2

    @pl.when(s == 0)
    def _():
        logit_ref[1] = jnp.zeros(logit_ref.shape[1:], F32)
        rmax_ref[1] = jnp.zeros(rmax_ref.shape[1:], F32)

    @pl.when(prev % nq == 0)
    def _():
        vaug_ref[:, :LANES] = v_ref[0]
        vaug_ref[:, LANES:] = jnp.ones((SEQ, LANES), BF16)

    n_tiles = SEQ // DIFF_KT

    def key_tile(t):
        return slice(t * DIFF_KT, (t + 1) * DIFF_KT)

    off = pl.multiple_of((nq - 1 - cur % nq) * DIFF_TQ, DIFF_TQ)
    q = q_ref[0]
    first_half = lax.broadcasted_iota(jnp.int32, q.shape, 1) < DIFF_HEAD_DIM
    zero = jnp.zeros_like(q)
    outs = []
    for m in range(2):
        row_max = rmax_ref[1 - slot, m]
        pv = jnp.zeros((DIFF_TQ, 2 * LANES), F32)
        for t in range(n_tiles):
            old = logit_ref[1 - slot, m, :, key_tile(t)]
            e = jnp.concatenate([jnp.exp2(old[:, c * LANES:(c + 1) * LANES] - row_max)
                                 for c in range(DIFF_KT // LANES)], axis=1).astype(BF16)
            pv = pv + jnp.dot(e, vaug_ref[key_tile(t), :], preferred_element_type=F32)
        outs.append(pv[:, :LANES] / pv[:, LANES:])

    for m in range(2):
        q_m = jnp.where(first_half, q, zero) if m == 0 else jnp.where(first_half, zero, q)
        lane_max = None
        for t in range(n_tiles):
            bias = bias_ref[0, :, pl.ds(pl.multiple_of(off + t * DIFF_KT, LANES), DIFF_KT)]
            logits = lax.dot_general(q_m, k_ref[0, key_tile(t), :], (((1,), (1,)), ((), ())),
                                     preferred_element_type=F32) + bias
            logit_ref[slot, m, :, key_tile(t)] = logits
            for c in range(DIFF_KT // LANES):
                part = logits[:, c * LANES:(c + 1) * LANES]
                lane_max = part if lane_max is None else jnp.maximum(lane_max, part)
        rmax_ref[slot, m] = jnp.broadcast_to(jnp.max(lane_max, axis=-1, keepdims=True),
                                             (DIFF_TQ, LANES))

    lam = (jnp.exp(jnp.sum(lq1_ref[...] * lk1_ref[...], axis=-1, keepdims=True))
           - jnp.exp(jnp.sum(lq2_ref[...] * lk2_ref[...], axis=-1, keepdims=True))
           + lam_init)
    o = outs[0] - lam * outs[1]
    o = _rms(o, sg_ref[...], SUBLN_EPS) * (1.0 - lam_init)
    o_ref[0] = o.astype(o_ref.dtype)


def _diff_attention(pc, strip, lq1, lk1, lq2, lk2, subln_g, lam_init):
    pcv = pc.reshape(BATCH, SEQ, C_WIDTH)
    vec = _resident((1, DIFF_HEAD_DIM))
    nq = SEQ // DIFF_TQ
    n_blocks = DIFF_HEADS * BATCH * nq

    def block_of(step, lag):
        blk = jnp.clip(step - lag, 0, n_blocks - 1)
        return blk // (BATCH * nq), (blk // nq) % BATCH, blk % nq

    def q_map(s):
        h, b, j = block_of(s, 0)
        return b, j, h

    def k_map(s):
        h, b, _ = block_of(s, 0)
        return b, 0, DIFF_HEADS + h

    def v_map(s):
        h, b, _ = block_of(s, 1)
        return b, 0, 2 * DIFF_HEADS + h

    def out_map(s):
        h, b, j = block_of(s, 1)
        return b, j, h

    return pl.pallas_call(
        functools.partial(_diff_kernel, lam_init=lam_init),
        grid=(n_blocks + 1,),
        in_specs=[pl.BlockSpec((1, DIFF_TQ, LANES), q_map),
                  pl.BlockSpec((1, SEQ, LANES), k_map),
                  pl.BlockSpec((1, SEQ, LANES), v_map),
                  pl.BlockSpec((1, DIFF_TQ, DIFF_STRIP), lambda s: (block_of(s, 0)[0], 0, 0)),
                  vec, vec, vec, vec, _resident((1, DIFF_V_DIM))],
        out_specs=pl.BlockSpec((1, DIFF_TQ, LANES), out_map),
        out_shape=jax.ShapeDtypeStruct((BATCH, SEQ, DIFF_V_WIDTH), BF16),
        scratch_shapes=[pltpu.VMEM((SEQ, 2 * LANES), BF16),
                        pltpu.VMEM((2, 2, DIFF_TQ, SEQ), F32),
                        pltpu.VMEM((2, 2, DIFF_TQ, LANES), F32)],
        compiler_params=_params("arbitrary"),
        name="diff_attn",
    )(pcv, pcv, pcv, strip, lq1.reshape(1, -1), lk1.reshape(1, -1), lq2.reshape(1, -1),
      lk2.reshape(1, -1), subln_g.reshape(1, -1)).reshape(TOKENS, DIFF_V_WIDTH)


def _merge_kernel(x_ref, f_ref, b_ref, c_ref, gate_ref, wa_ref, wb_ref, wc_ref, wo_ref,
                  *rest, n_cast):
    cast_src, out_ref, cast_dst = rest[:n_cast], rest[n_cast], rest[n_cast + 1:]
    _cast_blocks(cast_src, cast_dst)
    y_a = jnp.dot(f_ref[...], wa_ref[...], preferred_element_type=F32)
    y_b = jnp.dot(b_ref[...], wb_ref[...], preferred_element_type=F32)
    y_c = jnp.dot(c_ref[...], wc_ref[...], preferred_element_type=F32)
    merged = (gate_ref[:, :D_MODEL].astype(F32) * y_a
              + gate_ref[:, D_MODEL:2 * D_MODEL].astype(F32) * y_b
              + gate_ref[:, 2 * D_MODEL:].astype(F32) * y_c)
    out_ref[...] = x_ref[...] + jnp.dot(merged.astype(BF16), wo_ref[...],
                                        preferred_element_type=F32)


def _merge(x, f, b, c, gates, wa, wb, wc, wo, cast_jobs=()):
    def tile(w):
        return pl.BlockSpec((MERGE_TM, w), lambda i: (i, 0))

    steps = TOKENS // MERGE_TM
    cast_in, cast_out, cast_shapes = _cast_specs(cast_jobs, steps)
    return pl.pallas_call(
        functools.partial(_merge_kernel, n_cast=len(cast_jobs)),
        grid=(steps,),
        in_specs=[tile(D_MODEL), tile(FOURIER_WIDTH), tile(DIL_OUT_WIDTH), tile(DIFF_V_WIDTH),
                  tile(GATE_WIDTH), _resident(wa.shape), _resident(wb.shape),
                  _resident(wc.shape), _resident(wo.shape)] + cast_in,
        out_specs=[tile(D_MODEL)] + cast_out,
        out_shape=[jax.ShapeDtypeStruct((TOKENS, D_MODEL), F32)] + cast_shapes,
        compiler_params=_params("arbitrary"),
        name="gated_merge",
    )(x, f, b, c, gates, wa, wb, wc, wo, *[src for src, _ in cast_jobs])


def kernel(x, g_ffn1, w_ffn1_gate, w_ffn1_up, w_ffn1_down, g_mix, w_in, w_gate, b_gate,
           w_br_a, w_br_b, w_br_c, w_out, lam_q1, lam_k1, lam_q2, lam_k2, subln_g,
           rel_bias, g_ffn2, w_ffn2_gate, w_ffn2_up, w_ffn2_down, g_final):
    assert x.shape == (BATCH, SEQ, D_MODEL) and x.dtype == F32
    bf = lambda w: w.astype(BF16)
    c0 = FOURIER_WIDTH
    c3 = c0 + 3 * DIL_WIDTH

    dil_bias = [_dil_bias_tiles(rel_bias, g, dil) for g, (_, dil) in enumerate(DIL_PAIRS)]
    diff_strip = _diff_bias_strip(rel_bias)
    m_even, m_odd = _dft_matrices()
    dc, ds = _channel_dft()

    def dil_weights(w_in_l, g):
        lo = c0 + g * LANES
        q, k, v = (w_in_l[:, lo + i * DIL_WIDTH: lo + i * DIL_WIDTH + LANES] for i in range(3))
        return bf(jnp.concatenate([q * (QK_SCALE * LOG2_E), k, v], axis=1))

    q_scale_c = jnp.concatenate([jnp.full((DIFF_QK_WIDTH,), QK_SCALE * LOG2_E, F32),
                                 jnp.ones((DIFF_QK_WIDTH + DIFF_V_WIDTH,), F32)])

    h = x.reshape(TOKENS, D_MODEL)
    ffn1_w = [bf(w_ffn1_gate[0]), bf(w_ffn1_up[0]), bf(w_ffn1_down[0])]
    gate_w = bf(w_gate[0])
    for l in range(DEPTH):
        last = l == DEPTH - 1
        h, wa, wc_br, wo = _ffn(h, g_ffn1[l], *ffn1_w,
                                cast_jobs=[(w_br_a, l), (w_br_c, l), (w_out, l)])

        wf = _fold_channel_dft(w_in[l, :, :c0], dc, ds)
        pf, pb0, pb1, pb2, pc, gates, *ffn2_w = _proj(
            h, g_mix[l], wf, [dil_weights(w_in[l], g) for g in range(DIL_GROUPS)],
            bf(w_in[l, :, c3:] * q_scale_c), gate_w, b_gate[l],
            cast_jobs=[(w_ffn2_gate, l), (w_ffn2_up, l), (w_ffn2_down, l)])

        f = _fourier(pf, m_even, m_odd).reshape(TOKENS, FOURIER_WIDTH)
        mixed = _dilated_mixture(pb0.reshape(BATCH, SEQ, DIL_QKV), pb1, pb2, dil_bias)
        lam_init = 0.8 - 0.6 * math.exp(-0.3 * l)
        c = _diff_attention(pc, diff_strip, lam_q1[l], lam_k1[l], lam_q2[l], lam_k2[l],
                            subln_g[l], lam_init)

        h, *ffn1_w = _merge(
            h, f, mixed.reshape(TOKENS, DIL_OUT_WIDTH), c, gates, wa, bf(w_br_b[l]), wc_br, wo,
            cast_jobs=[] if last else [(w_ffn1_gate, l + 1), (w_ffn1_up, l + 1),
                                       (w_ffn1_down, l + 1)])

        h, *next_gate = _ffn(h, g_ffn2[l], *ffn2_w, g_final if last else None,
                             cast_jobs=[] if last else [(w_gate, l + 1)])
        if not last:
            gate_w, = next_gate
    return h.reshape(BATCH, SEQ, D_MODEL)
```

```python
import functools
import math

import numpy as np
import jax
import jax.numpy as jnp
from jax import lax
from jax.experimental import pallas as pl
from jax.experimental.pallas import tpu as pltpu

F32 = jnp.float32
BF16 = jnp.bfloat16

D_MODEL = 1024
BATCH = 8
SEQ = 2048
DEPTH = 2
TOKENS = BATCH * SEQ
D_FF = 2816
EPS = 1e-6
FOURIER_GROUPS = 8
FOURIER_GROUP_DIM = 128
FOURIER_WIDTH = FOURIER_GROUPS * FOURIER_GROUP_DIM
DIL_PAIRS = ((128, 1), (512, 4), (2048, 16))
DIL_GROUPS = len(DIL_PAIRS)
DIL_HEAD_DIM = 64
DIL_HEADS = 2 * DIL_GROUPS
DIL_WIDTH = DIL_HEADS * DIL_HEAD_DIM
DIL_OUT_WIDTH = 2 * DIL_HEAD_DIM
DIL_RADIUS = 64
DIFF_HEADS = 4
DIFF_HEAD_DIM = 64
DIFF_QK_WIDTH = DIFF_HEADS * 2 * DIFF_HEAD_DIM
DIFF_V_DIM = 2 * DIFF_HEAD_DIM
DIFF_V_WIDTH = DIFF_HEADS * DIFF_V_DIM
SUBLN_EPS = 1e-5
NUM_BUCKETS = 32
MAX_DISTANCE = 1024
N_BIAS_HEADS = DIL_HEADS + DIFF_HEADS
NEG_INF = -1e30
QK_SCALE = DIL_HEAD_DIM ** -0.5
LOG2_E = math.log2(math.e)
C_WIDTH = 2 * DIFF_QK_WIDTH + DIFF_V_WIDTH
GATE_WIDTH = 3 * D_MODEL

LANES = 128
VMEM_BYTES_V7X = 64 * 1024 * 1024
VMEM_LIMIT = 56 * 1024 * 1024

FFN_TM = 1024
FF_CHUNK = 256
PROJ_TM = 512
PROJ_CHUNK = 512
MERGE_TM = 512
HALF_SEQ = SEQ // 2
FOURIER_CN = 512
DIL_QKV = 3 * LANES
DIL_QB = 128
DIL_KW = DIL_QB + 2 * DIL_RADIUS
DIL_UNROLL = 4
DIFF_TQ = 512
DIFF_STRIP = 2 * SEQ - DIFF_TQ
DIFF_KT = 512


def _params(*semantics):
    return pltpu.CompilerParams(dimension_semantics=semantics, vmem_limit_bytes=VMEM_LIMIT)


def _resident(shape):
    nd = len(shape)
    return pl.BlockSpec(shape, lambda *_: (0,) * nd, pipeline_mode=pl.Buffered(1))


def _rms(x, g, eps):
    return x * lax.rsqrt(jnp.mean(x * x, axis=-1, keepdims=True) + eps) * g


BF16_SUBLANES = 16


def _cast_specs(jobs, steps):
    in_specs, out_specs, out_shapes = [], [], []
    for src, layer in jobs:
        _, rows, cols = src.shape
        share = 1
        while rows % (steps // share * BF16_SUBLANES):
            share *= 2
        blk = rows // (steps // share)
        in_specs.append(pl.BlockSpec((None, blk, cols),
                                     lambda i, layer=layer, share=share: (layer, i // share, 0)))
        out_specs.append(pl.BlockSpec((blk, cols), lambda i, share=share: (i // share, 0)))
        out_shapes.append(jax.ShapeDtypeStruct((rows, cols), BF16))
    return in_specs, out_specs, out_shapes


def _cast_blocks(src_refs, dst_refs):
    for src, dst in zip(src_refs, dst_refs):
        dst[...] = src[...].astype(BF16)


def _bucket_breakpoints():
    rel = np.arange(-(SEQ - 1), SEQ)
    half = NUM_BUCKETS // 2
    max_exact = half // 2
    n = np.abs(rel)
    nf = np.maximum(n, 1).astype(np.float64)
    large = max_exact + (np.log(nf / max_exact) / math.log(MAX_DISTANCE / max_exact)
                         * (half - max_exact)).astype(np.int32)
    large = np.minimum(large, half - 1)
    b = np.where(rel > 0, half, 0) + np.where(n < max_exact, n, large)
    pts = [(int(rel[i]), int(b[i])) for i in range(1, len(rel)) if b[i] != b[i - 1]]
    return int(b[0]), pts


def _bias_values(tbl_ref, head, rel):
    first, pts = _bucket_breakpoints()
    val = jnp.full(rel.shape, tbl_ref[first, head], F32)
    for thr, bucket in pts:
        val = jnp.where(rel >= thr, tbl_ref[bucket, head], val)
    return val * LOG2_E


def _diff_bias_kernel(tbl_ref, o_ref):
    rows, cols = o_ref.shape[1], o_ref.shape[2]
    rel = (lax.broadcasted_iota(jnp.int32, (rows, cols), 1)
           - lax.broadcasted_iota(jnp.int32, (rows, cols), 0) - (SEQ - rows))
    o_ref[0] = _bias_values(tbl_ref, DIL_HEADS + pl.program_id(0), rel)


def _diff_bias_strip(rel_bias):
    return pl.pallas_call(
        _diff_bias_kernel,
        grid=(DIFF_HEADS,),
        in_specs=[pl.BlockSpec(memory_space=pltpu.SMEM)],
        out_specs=pl.BlockSpec((1, DIFF_TQ, DIFF_STRIP), lambda h: (h, 0, 0)),
        out_shape=jax.ShapeDtypeStruct((DIFF_HEADS, DIFF_TQ, DIFF_STRIP), F32),
        compiler_params=_params("arbitrary"),
        name="diff_bias_strip",
    )(rel_bias)


EDGE_FIRST, EDGE_NONE, EDGE_LAST, EDGE_BOTH = range(4)


def _dil_bias_kernel(tbl_ref, o_ref, *, head0, dil):
    rows, cols = o_ref.shape[2], o_ref.shape[3]
    col = lax.broadcasted_iota(jnp.int32, (rows, cols), 1)
    rel_t = col - lax.broadcasted_iota(jnp.int32, (rows, cols), 0) - DIL_RADIUS
    val = _bias_values(tbl_ref, head0 + pl.program_id(0), rel_t * dil)
    val = jnp.where(jnp.abs(rel_t) <= DIL_RADIUS, val, NEG_INF)
    lo = jnp.where(col >= DIL_RADIUS, val, NEG_INF)
    o_ref[EDGE_FIRST, 0] = lo
    o_ref[EDGE_NONE, 0] = val
    o_ref[EDGE_LAST, 0] = jnp.where(col < cols - DIL_RADIUS, val, NEG_INF)
    o_ref[EDGE_BOTH, 0] = jnp.where(col < cols - DIL_RADIUS, lo, NEG_INF)


def _dil_bias_tiles(rel_bias, group, dil):
    return pl.pallas_call(
        functools.partial(_dil_bias_kernel, head0=2 * group, dil=dil),
        grid=(2,),
        in_specs=[pl.BlockSpec(memory_space=pltpu.SMEM)],
        out_specs=pl.BlockSpec((4, 1, DIL_QB, DIL_KW), lambda h: (0, h, 0, 0)),
        out_shape=jax.ShapeDtypeStruct((4, 2, DIL_QB, DIL_KW), F32),
        compiler_params=_params("arbitrary"),
        name="dil_bias_tiles",
    )(rel_bias)


def _fold_kernel(w_ref, dc_ref, ds_ref, oc_ref, os_ref):
    w = w_ref[...]
    oc_ref[...] = jnp.dot(w, dc_ref[...], preferred_element_type=F32,
                          precision=lax.Precision.HIGHEST).astype(BF16)
    os_ref[...] = jnp.dot(w, ds_ref[...], preferred_element_type=F32,
                          precision=lax.Precision.HIGHEST).astype(BF16)


def _fold_channel_dft(w_in, layer, dc, ds):
    gd = FOURIER_GROUP_DIM
    mat = pl.BlockSpec((gd, gd), lambda g: (0, 0))
    col = pl.BlockSpec((D_MODEL, gd), lambda g: (0, g))
    return pl.pallas_call(
        _fold_kernel,
        grid=(FOURIER_GROUPS,),
        in_specs=[pl.BlockSpec((None, D_MODEL, gd), lambda g: (layer, 0, g)), mat, mat],
        out_specs=[col, col],
        out_shape=[jax.ShapeDtypeStruct((D_MODEL, FOURIER_WIDTH), BF16)] * 2,
        compiler_params=_params("arbitrary"),
        name="fold_channel_dft",
    )(w_in, dc, ds)


def _ffn_kernel(x_ref, g_ref, wg_ref, wu_ref, wd_ref, *rest, final_norm, n_cast):
    rest = list(rest)
    gf_ref = rest.pop(0) if final_norm else None
    cast_src, o_ref, cast_dst, a_ref = (rest[:n_cast], rest[n_cast], rest[n_cast + 1:-1], rest[-1])
    _cast_blocks(cast_src, cast_dst)
    x = x_ref[...]
    h = _rms(x, g_ref[...], EPS).astype(BF16)
    for c in range(D_FF // FF_CHUNK):
        sl = slice(c * FF_CHUNK, (c + 1) * FF_CHUNK)
        gate = jnp.dot(h, wg_ref[:, sl], preferred_element_type=F32)
        up = jnp.dot(h, wu_ref[:, sl], preferred_element_type=F32)
        a_ref[:, sl] = (gate * jax.nn.sigmoid(gate) * up).astype(BF16)
    y = x + 0.5 * jnp.dot(a_ref[...], wd_ref[...], preferred_element_type=F32)
    if final_norm:
        y = _rms(y, gf_ref[...], EPS)
    o_ref[...] = y


def _ffn(x, g, wg, wu, wd, g_final=None, cast_jobs=()):
    final_norm = g_final is not None
    steps = TOKENS // FFN_TM
    tile = pl.BlockSpec((FFN_TM, D_MODEL), lambda i: (i, 0))
    in_specs = [tile, _resident((1, D_MODEL)), _resident((D_MODEL, D_FF)),
                _resident((D_MODEL, D_FF)), _resident((D_FF, D_MODEL))]
    args = [x, g.reshape(1, D_MODEL), wg, wu, wd]
    if final_norm:
        in_specs.append(_resident((1, D_MODEL)))
        args.append(g_final.reshape(1, D_MODEL))
    cast_in, cast_out, cast_shapes = _cast_specs(cast_jobs, steps)
    return pl.pallas_call(
        functools.partial(_ffn_kernel, final_norm=final_norm, n_cast=len(cast_jobs)),
        grid=(steps,),
        in_specs=in_specs + cast_in,
        out_specs=[tile] + cast_out,
        out_shape=[jax.ShapeDtypeStruct((TOKENS, D_MODEL), F32)] + cast_shapes,
        scratch_shapes=[pltpu.VMEM((FFN_TM, D_FF), BF16)],
        compiler_params=_params("arbitrary"),
        name="ffn_final" if final_norm else "ffn",
    )(*args, *[src for src, _ in cast_jobs])


def _proj_kernel(x_ref, g_ref, wfc_ref, wfs_ref, wb_ref, wc_ref, wg_ref, bg_ref, *rest, n_cast):
    cast_src, rest = rest[:n_cast], rest[n_cast:]
    pf_ref, pb0_ref, pb1_ref, pb2_ref, pc_ref, gate_ref = rest[:6]
    cast_dst, y_ref = rest[6:-1], rest[-1]
    _cast_blocks(cast_src, cast_dst)
    u = _rms(x_ref[...], g_ref[...], EPS).astype(BF16)
    rows = x_ref.shape[0]

    def chunks(width):
        return [slice(c0, min(c0 + PROJ_CHUNK, width)) for c0 in range(0, width, PROJ_CHUNK)]

    def project(w_ref, sl):
        return jnp.dot(u, w_ref[:, sl], preferred_element_type=F32)

    def by_residue(y, o_ref, dil, col0):
        for j in range(y.shape[1] // LANES):
            y_ref[j] = y[:, j * LANES:(j + 1) * LANES]
            cols = slice(col0 + j * LANES, col0 + (j + 1) * LANES)
            for r in range(dil):
                o_ref[0, r, :, cols] = y_ref[j, pl.ds(r, rows // dil, stride=dil), :].astype(o_ref.dtype)

    for sl in chunks(GATE_WIDTH):
        gate_ref[:, sl] = jax.nn.sigmoid(project(wg_ref, sl) + bg_ref[:, sl]).astype(gate_ref.dtype)
    for part, w_ref in enumerate((wfc_ref, wfs_ref)):
        for sl in chunks(FOURIER_WIDTH):
            by_residue(project(w_ref, sl), pf_ref, 2, part * FOURIER_WIDTH + sl.start)
    y = project(wb_ref, slice(0, DIL_GROUPS * DIL_QKV))
    pb0_ref[...] = y[:, :DIL_QKV].astype(pb0_ref.dtype)
    by_residue(y[:, DIL_QKV:2 * DIL_QKV], pb1_ref, DIL_PAIRS[1][1], 0)
    by_residue(y[:, 2 * DIL_QKV:], pb2_ref, DIL_PAIRS[2][1], 0)
    for sl in chunks(C_WIDTH):
        pc_ref[:, sl] = project(wc_ref, sl).astype(pc_ref.dtype)


def _proj(x, g, wfc, wfs, wb, wc, wg, bg, cast_jobs=()):
    tiles_per_seq = SEQ // PROJ_TM
    steps = TOKENS // PROJ_TM
    cast_in, cast_out, cast_shapes = _cast_specs(cast_jobs, steps)

    def tile(w):
        return pl.BlockSpec((PROJ_TM, w), lambda i: (i, 0))

    def residue_tile(dil, w):
        return pl.BlockSpec((1, dil, PROJ_TM // dil, w),
                            lambda i: (i // tiles_per_seq, 0, i % tiles_per_seq, 0))

    def residue_shape(dil, w):
        return jax.ShapeDtypeStruct((BATCH, dil, SEQ // dil, w), BF16)

    d1, d2 = DIL_PAIRS[1][1], DIL_PAIRS[2][1]
    weights = [wfc, wfs, wb, wc, wg]
    return pl.pallas_call(
        functools.partial(_proj_kernel, n_cast=len(cast_jobs)),
        grid=(steps,),
        in_specs=[tile(D_MODEL), _resident((1, D_MODEL))]
        + [_resident(w.shape) for w in weights] + [_resident((1, GATE_WIDTH))] + cast_in,
        out_specs=[residue_tile(2, 2 * FOURIER_WIDTH), tile(DIL_QKV), residue_tile(d1, DIL_QKV),
                   residue_tile(d2, DIL_QKV), tile(C_WIDTH), tile(GATE_WIDTH)] + cast_out,
        out_shape=[residue_shape(2, 2 * FOURIER_WIDTH),
                   jax.ShapeDtypeStruct((TOKENS, DIL_QKV), BF16),
                   residue_shape(d1, DIL_QKV), residue_shape(d2, DIL_QKV),
                   jax.ShapeDtypeStruct((TOKENS, C_WIDTH), BF16),
                   jax.ShapeDtypeStruct((TOKENS, GATE_WIDTH), BF16)] + cast_shapes,
        scratch_shapes=[pltpu.VMEM((max(PROJ_CHUNK, DIL_QKV) // LANES, PROJ_TM, LANES), F32)],
        compiler_params=_params("arbitrary"),
        name="proj_in",
    )(x, g.reshape(1, D_MODEL), *weights, bg.reshape(1, GATE_WIDTH),
      *[src for src, _ in cast_jobs])


def _fourier_kernel(pc_ref, ps_ref, me_ref, mo_ref, o_ref):
    h = HALF_SEQ
    even = (jnp.dot(me_ref[:, :h], pc_ref[0, 0], preferred_element_type=F32)
            + jnp.dot(me_ref[:, h:], ps_ref[0, 0], preferred_element_type=F32))
    odd = (jnp.dot(mo_ref[:, :h], pc_ref[0, 1], preferred_element_type=F32)
           + jnp.dot(mo_ref[:, h:], ps_ref[0, 1], preferred_element_type=F32))
    o_ref[0, :h, :] = (even + odd).astype(o_ref.dtype)
    o_ref[0, h:, :] = (even - odd).astype(o_ref.dtype)


def _fourier(pf, m_even, m_odd):
    cn = FOURIER_CN
    nblk = FOURIER_WIDTH // cn
    return pl.pallas_call(
        _fourier_kernel,
        grid=(BATCH, nblk),
        in_specs=[pl.BlockSpec((1, 2, HALF_SEQ, cn), lambda b, j: (b, 0, 0, j)),
                  pl.BlockSpec((1, 2, HALF_SEQ, cn), lambda b, j: (b, 0, 0, nblk + j)),
                  _resident((HALF_SEQ, SEQ)), _resident((HALF_SEQ, SEQ))],
        out_specs=pl.BlockSpec((1, SEQ, cn), lambda b, j: (b, 0, j)),
        out_shape=jax.ShapeDtypeStruct((BATCH, SEQ, FOURIER_WIDTH), BF16),
        compiler_params=_params("parallel", "parallel"),
        name="fourier_mix",
    )(pf, pf, m_even, m_odd)


def _dft_matrices():
    half = HALF_SEQ
    split = 32

    def table(mult):
        k = lax.broadcasted_iota(jnp.int32, (half, split), 0)
        j = lax.broadcasted_iota(jnp.int32, (half, split), 1)
        ang = ((k * j * mult) % half).astype(F32) * (2.0 * math.pi / half)
        return jnp.cos(ang), jnp.sin(ang)

    (ca, sa), (cb, sb) = table(split), table(1)
    cos_e = (ca[:, :, None] * cb[:, None, :] - sa[:, :, None] * sb[:, None, :]).reshape(half, half)
    sin_e = (sa[:, :, None] * cb[:, None, :] + ca[:, :, None] * sb[:, None, :]).reshape(half, half)
    phi = lax.broadcasted_iota(jnp.int32, (half, 1), 0).astype(F32) * (math.pi / half)
    cos_o = cos_e * jnp.cos(phi) - sin_e * jnp.sin(phi)
    sin_o = sin_e * jnp.cos(phi) + cos_e * jnp.sin(phi)
    m_even = jnp.concatenate([cos_e, -sin_e], axis=1).astype(BF16)
    m_odd = jnp.concatenate([cos_o, -sin_o], axis=1).astype(BF16)
    return m_even, m_odd


def _channel_dft():
    gd = FOURIER_GROUP_DIM
    idx = np.arange(gd)
    ang = 2.0 * np.pi * ((idx[:, None] * idx[None, :]) % gd) / gd
    scale = 1.0 / math.sqrt(SEQ * gd)
    return (jnp.asarray(np.cos(ang) * scale, F32), jnp.asarray(np.sin(ang) * scale, F32))


def _dil_block(q, kw, vw_aug, bias, head0_lanes):
    zero = jnp.zeros_like(q)
    pv, lse = [], []
    for hh in range(2):
        q_h = jnp.where(head0_lanes, q, zero) if hh == 0 else jnp.where(head0_lanes, zero, q)
        logits = lax.dot_general(q_h, kw, (((1,), (1,)), ((), ())),
                                 preferred_element_type=F32) + bias[hh]
        mx = jnp.max(logits, axis=-1, keepdims=True)
        e = jnp.exp2(logits - mx).astype(BF16)
        r = jnp.dot(e, vw_aug, preferred_element_type=F32)
        s = r[:, LANES:]
        pv.append(r[:, :LANES] / s)
        lse.append(mx + jnp.log2(s))
    return jnp.where(head0_lanes, pv[0], pv[1]), jnp.where(head0_lanes, lse[0], lse[1])


def _dil_kernel(g0_ref, g1_ref, g2_ref, b0_ref, b1_ref, b2_ref, out_ref,
                kpad_ref, vpad_ref, o_ref, lse_ref):
    pad = DIL_RADIUS
    head0_lanes = lax.broadcasted_iota(jnp.int32, (DIL_QB, LANES), 1) < DIL_HEAD_DIM
    zeros_k = jnp.zeros((pad, LANES), BF16)
    zeros_v = jnp.zeros((pad, 2 * LANES), BF16)
    q_sl, k_sl, v_sl = (slice(i * LANES, (i + 1) * LANES) for i in range(3))

    vpad_ref[pl.ds(pad, SEQ), LANES:] = jnp.ones((SEQ, LANES), BF16)

    def fill_padded(src, sub_len):
        kpad_ref[:pad, :] = zeros_k
        kpad_ref[pl.ds(pad, sub_len), :] = src[:, k_sl]
        kpad_ref[pl.ds(pad + sub_len, pad), :] = zeros_k
        vpad_ref[:pad, :] = zeros_v
        vpad_ref[pl.ds(pad, sub_len), :LANES] = src[:, v_sl]
        vpad_ref[pl.ds(pad + sub_len, pad), :] = zeros_v

    def store(group, rows, o, lse):
        o_ref[group, rows, :] = o
        lse_ref[group, rows, :] = lse

    def padded_block(group, q_src, bias_ref, blk, edge, dil, residue):
        r0 = pl.multiple_of(blk * DIL_QB, DIL_QB)
        o, lse = _dil_block(q_src[pl.ds(r0, DIL_QB), q_sl], kpad_ref[pl.ds(r0, DIL_KW), :],
                            vpad_ref[pl.ds(r0, DIL_KW), :], [bias_ref[edge, 0], bias_ref[edge, 1]],
                            head0_lanes)
        rows = pl.ds(residue + r0 * dil, DIL_QB, stride=dil) if dil > 1 else pl.ds(r0, DIL_QB)
        store(group, rows, o, lse)

    fill_padded(g0_ref.at[0], SEQ)
    n_blocks0 = SEQ // DIL_QB

    def g0_body(i, carry):
        for n in range(DIL_UNROLL):
            blk = i * DIL_UNROLL + n
            edge = jnp.where(blk == 0, EDGE_FIRST,
                             jnp.where(blk == n_blocks0 - 1, EDGE_LAST, EDGE_NONE))
            padded_block(0, g0_ref.at[0], b0_ref, blk, edge, 1, 0)
        return carry

    lax.fori_loop(0, n_blocks0 // DIL_UNROLL, g0_body, 0)

    dil1 = DIL_PAIRS[1][1]
    n_blocks1 = SEQ // dil1 // DIL_QB

    def g1_body(r, carry):
        src = g1_ref.at[0, r]
        fill_padded(src, SEQ // dil1)
        for blk in range(n_blocks1):
            edge = EDGE_FIRST if blk == 0 else EDGE_LAST if blk == n_blocks1 - 1 else EDGE_NONE
            padded_block(1, src, b1_ref, blk, edge, dil1, r)
        return carry

    lax.fori_loop(0, dil1, g1_body, 0)

    dil2 = DIL_PAIRS[2][1]
    assert SEQ // dil2 == DIL_QB
    for n in range(DIL_UNROLL):
        base = n * DIL_KW
        for ref, zeros in ((kpad_ref, zeros_k), (vpad_ref, zeros_v)):
            ref[base:base + pad, :] = zeros
            ref[base + pad + DIL_QB:base + DIL_KW, :] = zeros
        vpad_ref[base + pad:base + pad + DIL_QB, LANES:] = jnp.ones((DIL_QB, LANES), BF16)

    def g2_body(i, carry):
        for n in range(DIL_UNROLL):
            r = i * DIL_UNROLL + n
            src = g2_ref.at[0, r]
            base = n * DIL_KW
            kpad_ref[base + pad:base + pad + DIL_QB, :] = src[:, k_sl]
            vpad_ref[base + pad:base + pad + DIL_QB, :LANES] = src[:, v_sl]
            o, lse = _dil_block(src[:, q_sl], kpad_ref[base:base + DIL_KW, :],
                                vpad_ref[base:base + DIL_KW, :],
                                [b2_ref[EDGE_BOTH, 0], b2_ref[EDGE_BOTH, 1]], head0_lanes)
            store(2, pl.ds(r, DIL_QB, stride=dil2), o, lse)
        return carry

    lax.fori_loop(0, dil2 // DIL_UNROLL, g2_body, 0)

    chunk = 256
    for c in range(SEQ // chunk):
        rows = slice(c * chunk, (c + 1) * chunk)
        l0, l1, l2 = lse_ref[0, rows, :], lse_ref[1, rows, :], lse_ref[2, rows, :]
        mx = jnp.maximum(jnp.maximum(l0, l1), l2)
        e0, e1, e2 = jnp.exp2(l0 - mx), jnp.exp2(l1 - mx), jnp.exp2(l2 - mx)
        mixed = (e0 * o_ref[0, rows, :] + e1 * o_ref[1, rows, :] + e2 * o_ref[2, rows, :]) / (e0 + e1 + e2)
        out_ref[0, rows, :] = mixed.astype(out_ref.dtype)


def _dilated_mixture(pb0, pb1, pb2, biases):
    d1, d2 = DIL_PAIRS[1][1], DIL_PAIRS[2][1]
    bias_spec = _resident((4, 2, DIL_QB, DIL_KW))
    max_len = SEQ + 2 * DIL_RADIUS
    return pl.pallas_call(
        _dil_kernel,
        grid=(BATCH,),
        in_specs=[pl.BlockSpec((1, SEQ, DIL_QKV), lambda b: (b, 0, 0)),
                  pl.BlockSpec((1, d1, SEQ // d1, DIL_QKV), lambda b: (b, 0, 0, 0)),
                  pl.BlockSpec((1, d2, SEQ // d2, DIL_QKV), lambda b: (b, 0, 0, 0)),
                  bias_spec, bias_spec, bias_spec],
        out_specs=pl.BlockSpec((1, SEQ, LANES), lambda b: (b, 0, 0)),
        out_shape=jax.ShapeDtypeStruct((BATCH, SEQ, DIL_OUT_WIDTH), BF16),
        scratch_shapes=[pltpu.VMEM((max_len, LANES), BF16), pltpu.VMEM((max_len, 2 * LANES), BF16),
                        pltpu.VMEM((DIL_GROUPS, SEQ, LANES), F32),
                        pltpu.VMEM((DIL_GROUPS, SEQ, LANES), F32)],
        compiler_params=_params("parallel"),
        name="dilated_attn",
    )(pb0, pb1, pb2, *biases)


def _diff_kernel(q_ref, k_ref, v_ref, bias_ref, lq1_ref, lk1_ref, lq2_ref, lk2_ref, sg_ref,
                 o_ref, vaug_ref, logit_ref, rmax_ref, *, lam_init):
    s = pl.program_id(0)
    n_blocks = pl.num_programs(0) - 1
    nq = SEQ // DIFF_TQ
    cur = jnp.minimum(s, n_blocks - 1)
    prev = jnp.maximum(s - 1, 0)
    slot = s % 2

    @pl.when(s == 0)
    def _():
        logit_ref[1] = jnp.zeros(logit_ref.shape[1:], F32)
        rmax_ref[1] = jnp.zeros(rmax_ref.shape[1:], F32)

    @pl.when(prev % nq == 0)
    def _():
        vaug_ref[:, :LANES] = v_ref[0]
        vaug_ref[:, LANES:] = jnp.ones((SEQ, LANES), BF16)

    n_tiles = SEQ // DIFF_KT

    def key_tile(t):
        return slice(t * DIFF_KT, (t + 1) * DIFF_KT)

    off = pl.multiple_of((nq - 1 - cur % nq) * DIFF_TQ, DIFF_TQ)
    q = q_ref[0]
    first_half = lax.broadcasted_iota(jnp.int32, q.shape, 1) < DIFF_HEAD_DIM
    zero = jnp.zeros_like(q)
    outs = []
    for m in range(2):
        row_max = rmax_ref[1 - slot, m]
        pv = jnp.zeros((DIFF_TQ, 2 * LANES), F32)
        for t in range(n_tiles):
            old = logit_ref[1 - slot, m, :, key_tile(t)]
            e = jnp.concatenate([jnp.exp2(old[:, c * LANES:(c + 1) * LANES] - row_max)
                                 for c in range(DIFF_KT // LANES)], axis=1).astype(BF16)
            pv = pv + jnp.dot(e, vaug_ref[key_tile(t), :], preferred_element_type=F32)
        outs.append(pv[:, :LANES] / pv[:, LANES:])

    for m in range(2):
        q_m = jnp.where(first_half, q, zero) if m == 0 else jnp.where(first_half, zero, q)
        lane_max = None
        for t in range(n_tiles):
            bias = bias_ref[0, :, pl.ds(pl.multiple_of(off + t * DIFF_KT, LANES), DIFF_KT)]
            logits = lax.dot_general(q_m, k_ref[0, key_tile(t), :], (((1,), (1,)), ((), ())),
                                     preferred_element_type=F32) + bias
            logit_ref[slot, m, :, key_tile(t)] = logits
            for c in range(DIFF_KT // LANES):
                part = logits[:, c * LANES:(c + 1) * LANES]
                lane_max = part if lane_max is None else jnp.maximum(lane_max, part)
        rmax_ref[slot, m] = jnp.broadcast_to(jnp.max(lane_max, axis=-1, keepdims=True),
                                             (DIFF_TQ, LANES))

    lam = (jnp.exp(jnp.sum(lq1_ref[...] * lk1_ref[...], axis=-1, keepdims=True))
           - jnp.exp(jnp.sum(lq2_ref[...] * lk2_ref[...], axis=-1, keepdims=True))
           + lam_init)
    o = outs[0] - lam * outs[1]
    o = _rms(o, sg_ref[...], SUBLN_EPS) * (1.0 - lam_init)
    o_ref[0] = o.astype(o_ref.dtype)


def _diff_attention(pc, strip, lq1, lk1, lq2, lk2, subln_g, lam_init):
    pcv = pc.reshape(BATCH, SEQ, C_WIDTH)
    vec = _resident((1, DIFF_HEAD_DIM))
    nq = SEQ // DIFF_TQ
    n_blocks = DIFF_HEADS * BATCH * nq

    def block_of(step, lag):
        blk = jnp.clip(step - lag, 0, n_blocks - 1)
        return blk // (BATCH * nq), (blk // nq) % BATCH, blk % nq

    def q_map(s):
        h, b, j = block_of(s, 0)
        return b, j, h

    def k_map(s):
        h, b, _ = block_of(s, 0)
        return b, 0, DIFF_HEADS + h

    def v_map(s):
        h, b, _ = block_of(s, 1)
        return b, 0, 2 * DIFF_HEADS + h

    def out_map(s):
        h, b, j = block_of(s, 1)
        return b, j, h

    return pl.pallas_call(
        functools.partial(_diff_kernel, lam_init=lam_init),
        grid=(n_blocks + 1,),
        in_specs=[pl.BlockSpec((1, DIFF_TQ, LANES), q_map),
                  pl.BlockSpec((1, SEQ, LANES), k_map),
                  pl.BlockSpec((1, SEQ, LANES), v_map),
                  pl.BlockSpec((1, DIFF_TQ, DIFF_STRIP), lambda s: (block_of(s, 0)[0], 0, 0)),
                  vec, vec, vec, vec, _resident((1, DIFF_V_DIM))],
        out_specs=pl.BlockSpec((1, DIFF_TQ, LANES), out_map),
        out_shape=jax.ShapeDtypeStruct((BATCH, SEQ, DIFF_V_WIDTH), BF16),
        scratch_shapes=[pltpu.VMEM((SEQ, 2 * LANES), BF16),
                        pltpu.VMEM((2, 2, DIFF_TQ, SEQ), F32),
                        pltpu.VMEM((2, 2, DIFF_TQ, LANES), F32)],
        compiler_params=_params("arbitrary"),
        name="diff_attn",
    )(pcv, pcv, pcv, strip, lq1.reshape(1, -1), lk1.reshape(1, -1), lq2.reshape(1, -1),
      lk2.reshape(1, -1), subln_g.reshape(1, -1)).reshape(TOKENS, DIFF_V_WIDTH)


def _merge_kernel(x_ref, f_ref, b_ref, c_ref, gate_ref, wa_ref, wb_ref, wc_ref, wo_ref,
                  *rest, n_cast):
    cast_src, out_ref, cast_dst = rest[:n_cast], rest[n_cast], rest[n_cast + 1:]
    _cast_blocks(cast_src, cast_dst)
    y_a = jnp.dot(f_ref[...], wa_ref[...], preferred_element_type=F32)
    y_b = jnp.dot(b_ref[...], wb_ref[...], preferred_element_type=F32)
    y_c = jnp.dot(c_ref[...], wc_ref[...], preferred_element_type=F32)
    merged = (gate_ref[:, :D_MODEL].astype(F32) * y_a
              + gate_ref[:, D_MODEL:2 * D_MODEL].astype(F32) * y_b
              + gate_ref[:, 2 * D_MODEL:].astype(F32) * y_c)
    out_ref[...] = x_ref[...] + jnp.dot(merged.astype(BF16), wo_ref[...],
                                        preferred_element_type=F32)


def _merge(x, f, b, c, gates, wa, wb, wc, wo, cast_jobs=()):
    def tile(w):
        return pl.BlockSpec((MERGE_TM, w), lambda i: (i, 0))

    steps = TOKENS // MERGE_TM
    cast_in, cast_out, cast_shapes = _cast_specs(cast_jobs, steps)
    return pl.pallas_call(
        functools.partial(_merge_kernel, n_cast=len(cast_jobs)),
        grid=(steps,),
        in_specs=[tile(D_MODEL), tile(FOURIER_WIDTH), tile(DIL_OUT_WIDTH), tile(DIFF_V_WIDTH),
                  tile(GATE_WIDTH), _resident(wa.shape), _resident(wb.shape),
                  _resident(wc.shape), _resident(wo.shape)] + cast_in,
        out_specs=[tile(D_MODEL)] + cast_out,
        out_shape=[jax.ShapeDtypeStruct((TOKENS, D_MODEL), F32)] + cast_shapes,
        compiler_params=_params("arbitrary"),
        name="gated_merge",
    )(x, f, b, c, gates, wa, wb, wc, wo, *[src for src, _ in cast_jobs])


def kernel(x, g_ffn1, w_ffn1_gate, w_ffn1_up, w_ffn1_down, g_mix, w_in, w_gate, b_gate,
           w_br_a, w_br_b, w_br_c, w_out, lam_q1, lam_k1, lam_q2, lam_k2, subln_g,
           rel_bias, g_ffn2, w_ffn2_gate, w_ffn2_up, w_ffn2_down, g_final):
    assert x.shape == (BATCH, SEQ, D_MODEL) and x.dtype == F32
    bf = lambda w: w.astype(BF16)
    c0 = FOURIER_WIDTH
    c3 = c0 + 3 * DIL_WIDTH

    dil_bias = [_dil_bias_tiles(rel_bias, g, dil) for g, (_, dil) in enumerate(DIL_PAIRS)]
    diff_strip = _diff_bias_strip(rel_bias)
    m_even, m_odd = _dft_matrices()
    dc, ds = _channel_dft()

    q_scale = QK_SCALE * LOG2_E

    def dil_weights(w_in_l):
        parts = []
        for g in range(DIL_GROUPS):
            lo = c0 + g * LANES
            q, k, v = (w_in_l[:, lo + i * DIL_WIDTH: lo + i * DIL_WIDTH + LANES] for i in range(3))
            parts += [q * q_scale, k, v]
        return bf(jnp.concatenate(parts, axis=1))

    q_scale_c = jnp.concatenate([jnp.full((DIFF_QK_WIDTH,), q_scale, F32),
                                 jnp.ones((DIFF_QK_WIDTH + DIFF_V_WIDTH,), F32)])

    h = x.reshape(TOKENS, D_MODEL)
    ffn1_w = [bf(w_ffn1_gate[0]), bf(w_ffn1_up[0]), bf(w_ffn1_down[0])]
    gate_w = bf(w_gate[0])
    for l in range(DEPTH):
        last = l == DEPTH - 1
        h, wa, wc_br, wo = _ffn(h, g_ffn1[l], *ffn1_w,
                                cast_jobs=[(w_br_a, l), (w_br_c, l), (w_out, l)])

        wfc, wfs = _fold_channel_dft(w_in, l, dc, ds)
        pf, pb0, pb1, pb2, pc, gates, *ffn2_w = _proj(
            h, g_mix[l], wfc, wfs, dil_weights(w_in[l]),
            bf(w_in[l, :, c3:] * q_scale_c), gate_w, b_gate[l],
            cast_jobs=[(w_ffn2_gate, l), (w_ffn2_up, l), (w_ffn2_down, l)])

        f = _fourier(pf, m_even, m_odd).reshape(TOKENS, FOURIER_WIDTH)
        mixed = _dilated_mixture(pb0.reshape(BATCH, SEQ, DIL_QKV), pb1, pb2, dil_bias)
        lam_init = 0.8 - 0.6 * math.exp(-0.3 * l)
        c = _diff_attention(pc, diff_strip, lam_q1[l], lam_k1[l], lam_q2[l], lam_k2[l],
                            subln_g[l], lam_init)

        h, *ffn1_w = _merge(
            h, f, mixed.reshape(TOKENS, DIL_OUT_WIDTH), c, gates, wa, bf(w_br_b[l]), wc_br, wo,
            cast_jobs=[] if last else [(w_ffn1_gate, l + 1), (w_ffn1_up, l + 1),
                                       (w_ffn1_down, l + 1)])

        h, *next_gate = _ffn(h, g_ffn2[l], *ffn2_w, g_final if last else None,
                             cast_jobs=[] if last else [(w_gate, l + 1)])
        if not last:
            gate_w, = next_gate
    return h.reshape(BATCH, SEQ, D_MODEL)
```

```python
import functools
import math

import numpy as np
import jax
import jax.numpy as jnp
from jax import lax
from jax.experimental import pallas as pl
from jax.experimental.pallas import tpu as pltpu

F32 = jnp.float32
BF16 = jnp.bfloat16

D_MODEL = 1024
BATCH = 8
SEQ = 2048
DEPTH = 2
TOKENS = BATCH * SEQ
D_FF = 2816
EPS = 1e-6
FOURIER_GROUPS = 8
FOURIER_GROUP_DIM = 128
FOURIER_WIDTH = FOURIER_GROUPS * FOURIER_GROUP_DIM
DIL_PAIRS = ((128, 1), (512, 4), (2048, 16))
DIL_GROUPS = len(DIL_PAIRS)
DIL_HEAD_DIM = 64
DIL_HEADS = 2 * DIL_GROUPS
DIL_WIDTH = DIL_HEADS * DIL_HEAD_DIM
DIL_OUT_WIDTH = 2 * DIL_HEAD_DIM
DIL_RADIUS = 64
DIFF_HEADS = 4
DIFF_HEAD_DIM = 64
DIFF_QK_WIDTH = DIFF_HEADS * 2 * DIFF_HEAD_DIM
DIFF_V_DIM = 2 * DIFF_HEAD_DIM
DIFF_V_WIDTH = DIFF_HEADS * DIFF_V_DIM
SUBLN_EPS = 1e-5
NUM_BUCKETS = 32
MAX_DISTANCE = 1024
N_BIAS_HEADS = DIL_HEADS + DIFF_HEADS
NEG_INF = -1e30
QK_SCALE = DIL_HEAD_DIM ** -0.5
LOG2_E = math.log2(math.e)
C_WIDTH = 2 * DIFF_QK_WIDTH + DIFF_V_WIDTH
GATE_WIDTH = 3 * D_MODEL

LANES = 128
VMEM_BYTES_V7X = 64 * 1024 * 1024
VMEM_LIMIT = 56 * 1024 * 1024

FFN_TM = 1024
FF_CHUNK = 256
PROJ_TM = 512
PROJ_CHUNK = 512
MERGE_TM = 512
HALF_SEQ = SEQ // 2
FOURIER_CN = 512
DIL_QKV = 3 * LANES
DIL_QB = 128
DIL_KW = DIL_QB + 2 * DIL_RADIUS
DIL_UNROLL = 4
DIFF_TQ = 512
DIFF_STRIP = 2 * SEQ - DIFF_TQ
DIFF_KT = 512


def _params(*semantics):
    return pltpu.CompilerParams(dimension_semantics=semantics, vmem_limit_bytes=VMEM_LIMIT)


def _resident(shape):
    nd = len(shape)
    return pl.BlockSpec(shape, lambda *_: (0,) * nd, pipeline_mode=pl.Buffered(1))


def _rms(x, g, eps):
    return x * lax.rsqrt(jnp.mean(x * x, axis=-1, keepdims=True) + eps) * g


BF16_SUBLANES = 16


def _cast_specs(jobs, steps):
    in_specs, out_specs, out_shapes = [], [], []
    for src, layer in jobs:
        _, rows, cols = src.shape
        share = 1
        while rows % (steps // share * BF16_SUBLANES):
            share *= 2
        blk = rows // (steps // share)
        in_specs.append(pl.BlockSpec((None, blk, cols),
                                     lambda i, layer=layer, share=share: (layer, i // share, 0)))
        out_specs.append(pl.BlockSpec((blk, cols), lambda i, share=share: (i // share, 0)))
        out_shapes.append(jax.ShapeDtypeStruct((rows, cols), BF16))
    return in_specs, out_specs, out_shapes


def _cast_blocks(src_refs, dst_refs):
    for src, dst in zip(src_refs, dst_refs):
        dst[...] = src[...].astype(BF16)


def _bucket_breakpoints():
    rel = np.arange(-(SEQ - 1), SEQ)
    half = NUM_BUCKETS // 2
    max_exact = half // 2
    n = np.abs(rel)
    nf = np.maximum(n, 1).astype(np.float64)
    large = max_exact + (np.log(nf / max_exact) / math.log(MAX_DISTANCE / max_exact)
                         * (half - max_exact)).astype(np.int32)
    large = np.minimum(large, half - 1)
    b = np.where(rel > 0, half, 0) + np.where(n < max_exact, n, large)
    pts = [(int(rel[i]), int(b[i])) for i in range(1, len(rel)) if b[i] != b[i - 1]]
    return int(b[0]), pts


def _bias_values(tbl_ref, head, rel):
    first, pts = _bucket_breakpoints()
    val = jnp.full(rel.shape, tbl_ref[first, head], F32)
    for thr, bucket in pts:
        val = jnp.where(rel >= thr, tbl_ref[bucket, head], val)
    return val * LOG2_E


def _diff_bias_kernel(tbl_ref, o_ref):
    rows, cols = o_ref.shape[1], o_ref.shape[2]
    rel = (lax.broadcasted_iota(jnp.int32, (rows, cols), 1)
           - lax.broadcasted_iota(jnp.int32, (rows, cols), 0) - (SEQ - rows))
    o_ref[0] = _bias_values(tbl_ref, DIL_HEADS + pl.program_id(0), rel)


def _diff_bias_strip(rel_bias):
    return pl.pallas_call(
        _diff_bias_kernel,
        grid=(DIFF_HEADS,),
        in_specs=[pl.BlockSpec(memory_space=pltpu.SMEM)],
        out_specs=pl.BlockSpec((1, DIFF_TQ, DIFF_STRIP), lambda h: (h, 0, 0)),
        out_shape=jax.ShapeDtypeStruct((DIFF_HEADS, DIFF_TQ, DIFF_STRIP), F32),
        compiler_params=_params("arbitrary"),
        name="diff_bias_strip",
    )(rel_bias)


EDGE_FIRST, EDGE_NONE, EDGE_LAST, EDGE_BOTH = range(4)


def _dil_bias_kernel(tbl_ref, o_ref, *, head0, dil):
    rows, cols = o_ref.shape[2], o_ref.shape[3]
    col = lax.broadcasted_iota(jnp.int32, (rows, cols), 1)
    rel_t = col - lax.broadcasted_iota(jnp.int32, (rows, cols), 0) - DIL_RADIUS
    val = _bias_values(tbl_ref, head0 + pl.program_id(0), rel_t * dil)
    val = jnp.where(jnp.abs(rel_t) <= DIL_RADIUS, val, NEG_INF)
    lo = jnp.where(col >= DIL_RADIUS, val, NEG_INF)
    o_ref[EDGE_FIRST, 0] = lo
    o_ref[EDGE_NONE, 0] = val
    o_ref[EDGE_LAST, 0] = jnp.where(col < cols - DIL_RADIUS, val, NEG_INF)
    o_ref[EDGE_BOTH, 0] = jnp.where(col < cols - DIL_RADIUS, lo, NEG_INF)


def _dil_bias_tiles(rel_bias, group, dil):
    return pl.pallas_call(
        functools.partial(_dil_bias_kernel, head0=2 * group, dil=dil),
        grid=(2,),
        in_specs=[pl.BlockSpec(memory_space=pltpu.SMEM)],
        out_specs=pl.BlockSpec((4, 1, DIL_QB, DIL_KW), lambda h: (0, h, 0, 0)),
        out_shape=jax.ShapeDtypeStruct((4, 2, DIL_QB, DIL_KW), F32),
        compiler_params=_params("arbitrary"),
        name="dil_bias_tiles",
    )(rel_bias)


def _ffn_kernel(x_ref, g_ref, wg_ref, wu_ref, wd_ref, *rest, final_norm, n_cast):
    rest = list(rest)
    gf_ref = rest.pop(0) if final_norm else None
    cast_src, o_ref, cast_dst, a_ref = (rest[:n_cast], rest[n_cast], rest[n_cast + 1:-1], rest[-1])
    _cast_blocks(cast_src, cast_dst)
    x = x_ref[...]
    h = _rms(x, g_ref[...], EPS).astype(BF16)
    for c in range(D_FF // FF_CHUNK):
        sl = slice(c * FF_CHUNK, (c + 1) * FF_CHUNK)
        gate = jnp.dot(h, wg_ref[:, sl], preferred_element_type=F32)
        up = jnp.dot(h, wu_ref[:, sl], preferred_element_type=F32)
        a_ref[:, sl] = (gate * jax.nn.sigmoid(gate) * up).astype(BF16)
    y = x + 0.5 * jnp.dot(a_ref[...], wd_ref[...], preferred_element_type=F32)
    if final_norm:
        y = _rms(y, gf_ref[...], EPS)
    o_ref[...] = y


def _ffn(x, g, wg, wu, wd, g_final=None, cast_jobs=()):
    final_norm = g_final is not None
    steps = TOKENS // FFN_TM
    tile = pl.BlockSpec((FFN_TM, D_MODEL), lambda i: (i, 0))
    in_specs = [tile, _resident((1, D_MODEL)), _resident((D_MODEL, D_FF)),
                _resident((D_MODEL, D_FF)), _resident((D_FF, D_MODEL))]
    args = [x, g.reshape(1, D_MODEL), wg, wu, wd]
    if final_norm:
        in_specs.append(_resident((1, D_MODEL)))
        args.append(g_final.reshape(1, D_MODEL))
    cast_in, cast_out, cast_shapes = _cast_specs(cast_jobs, steps)
    return pl.pallas_call(
        functools.partial(_ffn_kernel, final_norm=final_norm, n_cast=len(cast_jobs)),
        grid=(steps,),
        in_specs=in_specs + cast_in,
        out_specs=[tile] + cast_out,
        out_shape=[jax.ShapeDtypeStruct((TOKENS, D_MODEL), F32)] + cast_shapes,
        scratch_shapes=[pltpu.VMEM((FFN_TM, D_FF), BF16)],
        compiler_params=_params("arbitrary"),
        name="ffn_final" if final_norm else "ffn",
    )(*args, *[src for src, _ in cast_jobs])


def _proj_kernel(x_ref, g_ref, wf_ref, wb_ref, wc_ref, wg_ref, bg_ref, *rest, n_cast):
    cast_src, rest = rest[:n_cast], rest[n_cast:]
    pf_ref, pb0_ref, pb1_ref, pb2_ref, pc_ref, gate_ref = rest[:6]
    cast_dst, y_ref = rest[6:-1], rest[-1]
    _cast_blocks(cast_src, cast_dst)
    u = _rms(x_ref[...], g_ref[...], EPS).astype(BF16)
    rows = x_ref.shape[0]

    def chunks(width):
        return [slice(c0, min(c0 + PROJ_CHUNK, width)) for c0 in range(0, width, PROJ_CHUNK)]

    def project(w_ref, sl):
        return jnp.dot(u, w_ref[:, sl], preferred_element_type=F32)

    def by_residue(y, o_ref, dil, col0):
        for j in range(y.shape[1] // LANES):
            y_ref[j] = y[:, j * LANES:(j + 1) * LANES]
            cols = slice(col0 + j * LANES, col0 + (j + 1) * LANES)
            for r in range(dil):
                o_ref[0, r, :, cols] = y_ref[j, pl.ds(r, rows // dil, stride=dil), :].astype(o_ref.dtype)

    for sl in chunks(GATE_WIDTH):
        gate_ref[:, sl] = jax.nn.sigmoid(project(wg_ref, sl) + bg_ref[:, sl]).astype(gate_ref.dtype)
    for sl in chunks(FOURIER_WIDTH):
        by_residue(project(wf_ref, sl), pf_ref, 2, sl.start)
    y = project(wb_ref, slice(0, DIL_GROUPS * DIL_QKV))
    pb0_ref[...] = y[:, :DIL_QKV].astype(pb0_ref.dtype)
    by_residue(y[:, DIL_QKV:2 * DIL_QKV], pb1_ref, DIL_PAIRS[1][1], 0)
    by_residue(y[:, 2 * DIL_QKV:], pb2_ref, DIL_PAIRS[2][1], 0)
    for sl in chunks(C_WIDTH):
        pc_ref[:, sl] = project(wc_ref, sl).astype(pc_ref.dtype)


def _proj(x, g, wf, wb, wc, wg, bg, cast_jobs=()):
    tiles_per_seq = SEQ // PROJ_TM
    steps = TOKENS // PROJ_TM
    cast_in, cast_out, cast_shapes = _cast_specs(cast_jobs, steps)

    def tile(w):
        return pl.BlockSpec((PROJ_TM, w), lambda i: (i, 0))

    def residue_tile(dil, w):
        return pl.BlockSpec((1, dil, PROJ_TM // dil, w),
                            lambda i: (i // tiles_per_seq, 0, i % tiles_per_seq, 0))

    def residue_shape(dil, w):
        return jax.ShapeDtypeStruct((BATCH, dil, SEQ // dil, w), BF16)

    d1, d2 = DIL_PAIRS[1][1], DIL_PAIRS[2][1]
    weights = [wf, wb, wc, wg]
    return pl.pallas_call(
        functools.partial(_proj_kernel, n_cast=len(cast_jobs)),
        grid=(steps,),
        in_specs=[tile(D_MODEL), _resident((1, D_MODEL))]
        + [_resident(w.shape) for w in weights] + [_resident((1, GATE_WIDTH))] + cast_in,
        out_specs=[residue_tile(2, FOURIER_WIDTH), tile(DIL_QKV), residue_tile(d1, DIL_QKV),
                   residue_tile(d2, DIL_QKV), tile(C_WIDTH), tile(GATE_WIDTH)] + cast_out,
        out_shape=[residue_shape(2, FOURIER_WIDTH),
                   jax.ShapeDtypeStruct((TOKENS, DIL_QKV), BF16),
                   residue_shape(d1, DIL_QKV), residue_shape(d2, DIL_QKV),
                   jax.ShapeDtypeStruct((TOKENS, C_WIDTH), BF16),
                   jax.ShapeDtypeStruct((TOKENS, GATE_WIDTH), BF16)] + cast_shapes,
        scratch_shapes=[pltpu.VMEM((max(PROJ_CHUNK, DIL_QKV) // LANES, PROJ_TM, LANES), F32)],
        compiler_params=_params("arbitrary"),
        name="proj_in",
    )(x, g.reshape(1, D_MODEL), *weights, bg.reshape(1, GATE_WIDTH),
      *[src for src, _ in cast_jobs])


def _fourier_kernel(a_ref, me_ref, mo_ref, d_ref, o_ref):
    h = HALF_SEQ
    r_even = jnp.dot(me_ref[...], a_ref[0, 0], preferred_element_type=F32)
    r_odd = jnp.dot(mo_ref[...], a_ref[0, 1], preferred_element_type=F32)
    halves = ((r_even + r_odd).astype(BF16), (r_even - r_odd).astype(BF16))
    for half, pq in enumerate(halves):
        for g in range(o_ref.shape[2] // LANES):
            cols = slice(g * LANES, (g + 1) * LANES)
            lhs = jnp.concatenate([pq[:h, cols], pq[h:, cols]], axis=1)
            o_ref[0, half * h:(half + 1) * h, cols] = jnp.dot(
                lhs, d_ref[...], preferred_element_type=F32).astype(o_ref.dtype)


def _fourier(pf, m_even, m_odd, d_chan):
    cn = FOURIER_CN
    return pl.pallas_call(
        _fourier_kernel,
        grid=(BATCH, FOURIER_WIDTH // cn),
        in_specs=[pl.BlockSpec((1, 2, HALF_SEQ, cn), lambda b, j: (b, 0, 0, j)),
                  _resident((SEQ, HALF_SEQ)), _resident((SEQ, HALF_SEQ)),
                  _resident((2 * FOURIER_GROUP_DIM, FOURIER_GROUP_DIM))],
        out_specs=pl.BlockSpec((1, SEQ, cn), lambda b, j: (b, 0, j)),
        out_shape=jax.ShapeDtypeStruct((BATCH, SEQ, FOURIER_WIDTH), BF16),
        compiler_params=_params("parallel", "parallel"),
        name="fourier_mix",
    )(pf, m_even, m_odd, d_chan)


def _dft_matrices():
    half = HALF_SEQ
    split = 32

    def table(mult):
        k = lax.broadcasted_iota(jnp.int32, (half, split), 0)
        j = lax.broadcasted_iota(jnp.int32, (half, split), 1)
        ang = ((k * j * mult) % half).astype(F32) * (2.0 * math.pi / half)
        return jnp.cos(ang), jnp.sin(ang)

    (ca, sa), (cb, sb) = table(split), table(1)
    cos_e = (ca[:, :, None] * cb[:, None, :] - sa[:, :, None] * sb[:, None, :]).reshape(half, half)
    sin_e = (sa[:, :, None] * cb[:, None, :] + ca[:, :, None] * sb[:, None, :]).reshape(half, half)
    phi = lax.broadcasted_iota(jnp.int32, (half, 1), 0).astype(F32) * (math.pi / half)
    cos_o = cos_e * jnp.cos(phi) - sin_e * jnp.sin(phi)
    sin_o = sin_e * jnp.cos(phi) + cos_e * jnp.sin(phi)
    m_even = jnp.concatenate([cos_e, sin_e], axis=0).astype(BF16)
    m_odd = jnp.concatenate([cos_o, sin_o], axis=0).astype(BF16)
    return m_even, m_odd


def _channel_dft():
    gd = FOURIER_GROUP_DIM
    idx = np.arange(gd)
    ang = 2.0 * np.pi * ((idx[:, None] * idx[None, :]) % gd) / gd
    scale = 1.0 / math.sqrt(SEQ * gd)
    return jnp.asarray(np.concatenate([np.cos(ang), -np.sin(ang)]) * scale, F32).astype(BF16)


def _dil_block(q, kw, vw_aug, bias, head0_lanes):
    zero = jnp.zeros_like(q)
    pv, lse = [], []
    for hh in range(2):
        q_h = jnp.where(head0_lanes, q, zero) if hh == 0 else jnp.where(head0_lanes, zero, q)
        logits = lax.dot_general(q_h, kw, (((1,), (1,)), ((), ())),
                                 preferred_element_type=F32) + bias[hh]
        mx = jnp.max(logits, axis=-1, keepdims=True)
        e = jnp.exp2(logits - mx).astype(BF16)
        r = jnp.dot(e, vw_aug, preferred_element_type=F32)
        s = r[:, LANES:]
        pv.append(r[:, :LANES] / s)
        lse.append(mx + jnp.log2(s))
    return jnp.where(head0_lanes, pv[0], pv[1]), jnp.where(head0_lanes, lse[0], lse[1])


def _dil_kernel(g0_ref, g1_ref, g2_ref, b0_ref, b1_ref, b2_ref, out_ref,
                kpad_ref, vpad_ref, o_ref, lse_ref):
    pad = DIL_RADIUS
    head0_lanes = lax.broadcasted_iota(jnp.int32, (DIL_QB, LANES), 1) < DIL_HEAD_DIM
    zeros_k = jnp.zeros((pad, LANES), BF16)
    zeros_v = jnp.zeros((pad, 2 * LANES), BF16)
    q_sl, k_sl, v_sl = (slice(i * LANES, (i + 1) * LANES) for i in range(3))

    vpad_ref[pl.ds(pad, SEQ), LANES:] = jnp.ones((SEQ, LANES), BF16)

    def fill_padded(src, sub_len):
        kpad_ref[:pad, :] = zeros_k
        kpad_ref[pl.ds(pad, sub_len), :] = src[:, k_sl]
        kpad_ref[pl.ds(pad + sub_len, pad), :] = zeros_k
        vpad_ref[:pad, :] = zeros_v
        vpad_ref[pl.ds(pad, sub_len), :LANES] = src[:, v_sl]
        vpad_ref[pl.ds(pad + sub_len, pad), :] = zeros_v

    def store(group, rows, o, lse):
        o_ref[group, rows, :] = o
        lse_ref[group, rows, :] = lse

    def padded_block(group, q_src, bias_ref, blk, edge, dil, residue):
        r0 = pl.multiple_of(blk * DIL_QB, DIL_QB)
        o, lse = _dil_block(q_src[pl.ds(r0, DIL_QB), q_sl], kpad_ref[pl.ds(r0, DIL_KW), :],
                            vpad_ref[pl.ds(r0, DIL_KW), :], [bias_ref[edge, 0], bias_ref[edge, 1]],
                            head0_lanes)
        rows = pl.ds(residue + r0 * dil, DIL_QB, stride=dil) if dil > 1 else pl.ds(r0, DIL_QB)
        store(group, rows, o, lse)

    fill_padded(g0_ref.at[0], SEQ)
    n_blocks0 = SEQ // DIL_QB

    def g0_body(i, carry):
        for n in range(DIL_UNROLL):
            blk = i * DIL_UNROLL + n
            edge = jnp.where(blk == 0, EDGE_FIRST,
                             jnp.where(blk == n_blocks0 - 1, EDGE_LAST, EDGE_NONE))
            padded_block(0, g0_ref.at[0], b0_ref, blk, edge, 1, 0)
        return carry

    lax.fori_loop(0, n_blocks0 // DIL_UNROLL, g0_body, 0)

    dil1 = DIL_PAIRS[1][1]
    n_blocks1 = SEQ // dil1 // DIL_QB

    def g1_body(r, carry):
        src = g1_ref.at[0, r]
        fill_padded(src, SEQ // dil1)
        for blk in range(n_blocks1):
            edge = EDGE_FIRST if blk == 0 else EDGE_LAST if blk == n_blocks1 - 1 else EDGE_NONE
            padded_block(1, src, b1_ref, blk, edge, dil1, r)
        return carry

    lax.fori_loop(0, dil1, g1_body, 0)

    dil2 = DIL_PAIRS[2][1]
    assert SEQ // dil2 == DIL_QB
    for n in range(DIL_UNROLL):
        base = n * DIL_KW
        for ref, zeros in ((kpad_ref, zeros_k), (vpad_ref, zeros_v)):
            ref[base:base + pad, :] = zeros
            ref[base + pad + DIL_QB:base + DIL_KW, :] = zeros
        vpad_ref[base + pad:base + pad + DIL_QB, LANES:] = jnp.ones((DIL_QB, LANES), BF16)

    def g2_body(i, carry):
        for n in range(DIL_UNROLL):
            r = i * DIL_UNROLL + n
            src = g2_ref.at[0, r]
            base = n * DIL_KW
            kpad_ref[base + pad:base + pad + DIL_QB, :] = src[:, k_sl]
            vpad_ref[base + pad:base + pad + DIL_QB, :LANES] = src[:, v_sl]
            o, lse = _dil_block(src[:, q_sl], kpad_ref[base:base + DIL_KW, :],
                                vpad_ref[base:base + DIL_KW, :],
                                [b2_ref[EDGE_BOTH, 0], b2_ref[EDGE_BOTH, 1]], head0_lanes)
            store(2, pl.ds(r, DIL_QB, stride=dil2), o, lse)
        return carry

    lax.fori_loop(0, dil2 // DIL_UNROLL, g2_body, 0)

    chunk = 256
    for c in range(SEQ // chunk):
        rows = slice(c * chunk, (c + 1) * chunk)
        l0, l1, l2 = lse_ref[0, rows, :], lse_ref[1, rows, :], lse_ref[2, rows, :]
        mx = jnp.maximum(jnp.maximum(l0, l1), l2)
        e0, e1, e2 = jnp.exp2(l0 - mx), jnp.exp2(l1 - mx), jnp.exp2(l2 - mx)
        mixed = (e0 * o_ref[0, rows, :] + e1 * o_ref[1, rows, :] + e2 * o_ref[2, rows, :]) / (e0 + e1 + e2)
        out_ref[0, rows, :] = mixed.astype(out_ref.dtype)


def _dilated_mixture(pb0, pb1, pb2, biases):
    d1, d2 = DIL_PAIRS[1][1], DIL_PAIRS[2][1]
    bias_spec = _resident((4, 2, DIL_QB, DIL_KW))
    max_len = SEQ + 2 * DIL_RADIUS
    return pl.pallas_call(
        _dil_kernel,
        grid=(BATCH,),
        in_specs=[pl.BlockSpec((1, SEQ, DIL_QKV), lambda b: (b, 0, 0)),
                  pl.BlockSpec((1, d1, SEQ // d1, DIL_QKV), lambda b: (b, 0, 0, 0)),
                  pl.BlockSpec((1, d2, SEQ // d2, DIL_QKV), lambda b: (b, 0, 0, 0)),
                  bias_spec, bias_spec, bias_spec],
        out_specs=pl.BlockSpec((1, SEQ, LANES), lambda b: (b, 0, 0)),
        out_shape=jax.ShapeDtypeStruct((BATCH, SEQ, DIL_OUT_WIDTH), BF16),
        scratch_shapes=[pltpu.VMEM((max_len, LANES), BF16), pltpu.VMEM((max_len, 2 * LANES), BF16),
                        pltpu.VMEM((DIL_GROUPS, SEQ, LANES), F32),
                        pltpu.VMEM((DIL_GROUPS, SEQ, LANES), F32)],
        compiler_params=_params("parallel"),
        name="dilated_attn",
    )(pb0, pb1, pb2, *biases)


def _diff_kernel(q_ref, k_ref, v_ref, bias_ref, lq1_ref, lk1_ref, lq2_ref, lk2_ref, sg_ref,
                 o_ref, vaug_ref, logit_ref, rmax_ref, *, lam_init):
    s = pl.program_id(0)
    n_blocks = pl.num_programs(0) - 1
    nq = SEQ // DIFF_TQ
    cur = jnp.minimum(s, n_blocks - 1)
    prev = jnp.maximum(s - 1, 0)
    slot = s % 2

    @pl.when(s == 0)
    def _():
        logit_ref[1] = jnp.zeros(logit_ref.shape[1:], F32)
        rmax_ref[1] = jnp.zeros(rmax_ref.shape[1:], F32)

    @pl.when(prev % nq == 0)
    def _():
        vaug_ref[:, :LANES] = v_ref[0]
        vaug_ref[:, LANES:] = jnp.ones((SEQ, LANES), BF16)

    n_tiles = SEQ // DIFF_KT

    def key_tile(t):
        return slice(t * DIFF_KT, (t + 1) * DIFF_KT)

    off = pl.multiple_of((nq - 1 - cur % nq) * DIFF_TQ, DIFF_TQ)
    q = q_ref[0]
    first_half = lax.broadcasted_iota(jnp.int32, q.shape, 1) < DIFF_HEAD_DIM
    zero = jnp.zeros_like(q)
    outs = []
    for m in range(2):
        row_max = rmax_ref[1 - slot, m]
        pv = jnp.zeros((DIFF_TQ, 2 * LANES), F32)
        for t in range(n_tiles):
            old = logit_ref[1 - slot, m, :, key_tile(t)]
            e = jnp.concatenate([jnp.exp2(old[:, c * LANES:(c + 1) * LANES] - row_max)
                                 for c in range(DIFF_KT // LANES)], axis=1).astype(BF16)
            pv = pv + jnp.dot(e, vaug_ref[key_tile(t), :], preferred_element_type=F32)
        outs.append(pv[:, :LANES] / pv[:, LANES:])

    for m in range(2):
        q_m = jnp.where(first_half, q, zero) if m == 0 else jnp.where(first_half, zero, q)
        lane_max = None
        for t in range(n_tiles):
            bias = bias_ref[0, :, pl.ds(pl.multiple_of(off + t * DIFF_KT, LANES), DIFF_KT)]
            logits = lax.dot_general(q_m, k_ref[0, key_tile(t), :], (((1,), (1,)), ((), ())),
                                     preferred_element_type=F32) + bias
            logit_ref[slot, m, :, key_tile(t)] = logits
            for c in range(DIFF_KT // LANES):
                part = logits[:, c * LANES:(c + 1) * LANES]
                lane_max = part if lane_max is None else jnp.maximum(lane_max, part)
        rmax_ref[slot, m] = jnp.broadcast_to(jnp.max(lane_max, axis=-1, keepdims=True),
                                             (DIFF_TQ, LANES))

    lam = (jnp.exp(jnp.sum(lq1_ref[...] * lk1_ref[...], axis=-1, keepdims=True))
           - jnp.exp(jnp.sum(lq2_ref[...] * lk2_ref[...], axis=-1, keepdims=True))
           + lam_init)
    o = outs[0] - lam * outs[1]
    o = _rms(o, sg_ref[...], SUBLN_EPS) * (1.0 - lam_init)
    o_ref[0] = o.astype(o_ref.dtype)


def _diff_attention(pc, strip, lq1, lk1, lq2, lk2, subln_g, lam_init):
    pcv = pc.reshape(BATCH, SEQ, C_WIDTH)
    vec = _resident((1, DIFF_HEAD_DIM))
    nq = SEQ // DIFF_TQ
    n_blocks = DIFF_HEADS * BATCH * nq

    def block_of(step, lag):
        blk = jnp.clip(step - lag, 0, n_blocks - 1)
        return blk // (BATCH * nq), (blk // nq) % BATCH, blk % nq

    def q_map(s):
        h, b, j = block_of(s, 0)
        return b, j, h

    def k_map(s):
        h, b, _ = block_of(s, 0)
        return b, 0, DIFF_HEADS + h

    def v_map(s):
        h, b, _ = block_of(s, 1)
        return b, 0, 2 * DIFF_HEADS + h

    def out_map(s):
        h, b, j = block_of(s, 1)
        return b, j, h

    return pl.pallas_call(
        functools.partial(_diff_kernel, lam_init=lam_init),
        grid=(n_blocks + 1,),
        in_specs=[pl.BlockSpec((1, DIFF_TQ, LANES), q_map),
                  pl.BlockSpec((1, SEQ, LANES), k_map),
                  pl.BlockSpec((1, SEQ, LANES), v_map),
                  pl.BlockSpec((1, DIFF_TQ, DIFF_STRIP), lambda s: (block_of(s, 0)[0], 0, 0)),
                  vec, vec, vec, vec, _resident((1, DIFF_V_DIM))],
        out_specs=pl.BlockSpec((1, DIFF_TQ, LANES), out_map),
        out_shape=jax.ShapeDtypeStruct((BATCH, SEQ, DIFF_V_WIDTH), BF16),
        scratch_shapes=[pltpu.VMEM((SEQ, 2 * LANES), BF16),
                        pltpu.VMEM((2, 2, DIFF_TQ, SEQ), F32),
                        pltpu.VMEM((2, 2, DIFF_TQ, LANES), F32)],
        compiler_params=_params("arbitrary"),
        name="diff_attn",
    )(pcv, pcv, pcv, strip, lq1.reshape(1, -1), lk1.reshape(1, -1), lq2.reshape(1, -1),
      lk2.reshape(1, -1), subln_g.reshape(1, -1)).reshape(TOKENS, DIFF_V_WIDTH)


def _merge_kernel(x_ref, f_ref, b_ref, c_ref, gate_ref, wa_ref, wb_ref, wc_ref, wo_ref,
                  *rest, n_cast):
    cast_src, out_ref, cast_dst = rest[:n_cast], rest[n_cast], rest[n_cast + 1:]
    _cast_blocks(cast_src, cast_dst)
    y_a = jnp.dot(f_ref[...], wa_ref[...], preferred_element_type=F32)
    y_b = jnp.dot(b_ref[...], wb_ref[...], preferred_element_type=F32)
    y_c = jnp.dot(c_ref[...], wc_ref[...], preferred_element_type=F32)
    merged = (gate_ref[:, :D_MODEL].astype(F32) * y_a
              + gate_ref[:, D_MODEL:2 * D_MODEL].astype(F32) * y_b
              + gate_ref[:, 2 * D_MODEL:].astype(F32) * y_c)
    out_ref[...] = x_ref[...] + jnp.dot(merged.astype(BF16), wo_ref[...],
                                        preferred_element_type=F32)


def _merge(x, f, b, c, gates, wa, wb, wc, wo, cast_jobs=()):
    def tile(w):
        return pl.BlockSpec((MERGE_TM, w), lambda i: (i, 0))

    steps = TOKENS // MERGE_TM
    cast_in, cast_out, cast_shapes = _cast_specs(cast_jobs, steps)
    return pl.pallas_call(
        functools.partial(_merge_kernel, n_cast=len(cast_jobs)),
        grid=(steps,),
        in_specs=[tile(D_MODEL), tile(FOURIER_WIDTH), tile(DIL_OUT_WIDTH), tile(DIFF_V_WIDTH),
                  tile(GATE_WIDTH), _resident(wa.shape), _resident(wb.shape),
                  _resident(wc.shape), _resident(wo.shape)] + cast_in,
        out_specs=[tile(D_MODEL)] + cast_out,
        out_shape=[jax.ShapeDtypeStruct((TOKENS, D_MODEL), F32)] + cast_shapes,
        compiler_params=_params("arbitrary"),
        name="gated_merge",
    )(x, f, b, c, gates, wa, wb, wc, wo, *[src for src, _ in cast_jobs])


def kernel(x, g_ffn1, w_ffn1_gate, w_ffn1_up, w_ffn1_down, g_mix, w_in, w_gate, b_gate,
           w_br_a, w_br_b, w_br_c, w_out, lam_q1, lam_k1, lam_q2, lam_k2, subln_g,
           rel_bias, g_ffn2, w_ffn2_gate, w_ffn2_up, w_ffn2_down, g_final):
    assert x.shape == (BATCH, SEQ, D_MODEL) and x.dtype == F32
    bf = lambda w: w.astype(BF16)
    c0 = FOURIER_WIDTH
    c3 = c0 + 3 * DIL_WIDTH

    dil_bias = [_dil_bias_tiles(rel_bias, g, dil) for g, (_, dil) in enumerate(DIL_PAIRS)]
    diff_strip = _diff_bias_strip(rel_bias)
    m_even, m_odd = _dft_matrices()
    d_chan = _channel_dft()

    q_scale = QK_SCALE * LOG2_E

    def dil_weights(w_in_l):
        parts = []
        for g in range(DIL_GROUPS):
            lo = c0 + g * LANES
            q, k, v = (w_in_l[:, lo + i * DIL_WIDTH: lo + i * DIL_WIDTH + LANES] for i in range(3))
            parts += [q * q_scale, k, v]
        return bf(jnp.concatenate(parts, axis=1))

    q_scale_c = jnp.concatenate([jnp.full((DIFF_QK_WIDTH,), q_scale, F32),
                                 jnp.ones((DIFF_QK_WIDTH + DIFF_V_WIDTH,), F32)])

    h = x.reshape(TOKENS, D_MODEL)
    ffn1_w = [bf(w_ffn1_gate[0]), bf(w_ffn1_up[0]), bf(w_ffn1_down[0])]
    gate_w = bf(w_gate[0])
    for l in range(DEPTH):
        last = l == DEPTH - 1
        h, wa, wc_br, wo = _ffn(h, g_ffn1[l], *ffn1_w,
                                cast_jobs=[(w_br_a, l), (w_br_c, l), (w_out, l)])

        pf, pb0, pb1, pb2, pc, gates, *ffn2_w = _proj(
            h, g_mix[l], bf(w_in[l, :, :c0]), dil_weights(w_in[l]),
            bf(w_in[l, :, c3:] * q_scale_c), gate_w, b_gate[l],
            cast_jobs=[(w_ffn2_gate, l), (w_ffn2_up, l), (w_ffn2_down, l)])

        f = _fourier(pf, m_even, m_odd, d_chan).reshape(TOKENS, FOURIER_WIDTH)
        mixed = _dilated_mixture(pb0.reshape(BATCH, SEQ, DIL_QKV), pb1, pb2, dil_bias)
        lam_init = 0.8 - 0.6 * math.exp(-0.3 * l)
        c = _diff_attention(pc, diff_strip, lam_q1[l], lam_k1[l], lam_q2[l], lam_k2[l],
                            subln_g[l], lam_init)

        h, *ffn1_w = _merge(
            h, f, mixed.reshape(TOKENS, DIL_OUT_WIDTH), c, gates, wa, bf(w_br_b[l]), wc_br, wo,
            cast_jobs=[] if last else [(w_ffn1_gate, l + 1), (w_ffn1_up, l + 1),
                                       (w_ffn1_down, l + 1)])

        h, *next_gate = _ffn(h, g_ffn2[l], *ffn2_w, g_final if last else None,
                             cast_jobs=[] if last else [(w_gate, l + 1)])
        if not last:
            gate_w, = next_gate
    return h.reshape(BATCH, SEQ, D_MODEL)
```

```python
import functools
import math

import numpy as np
import jax
import jax.numpy as jnp
from jax import lax
from jax.experimental import pallas as pl
from jax.experimental.pallas import tpu as pltpu

F32 = jnp.float32
BF16 = jnp.bfloat16

D_MODEL = 1024
BATCH = 8
SEQ = 2048
DEPTH = 2
TOKENS = BATCH * SEQ
D_FF = 2816
EPS = 1e-6
FOURIER_GROUPS = 8
FOURIER_GROUP_DIM = 128
FOURIER_WIDTH = FOURIER_GROUPS * FOURIER_GROUP_DIM
DIL_PAIRS = ((128, 1), (512, 4), (2048, 16))
DIL_GROUPS = len(DIL_PAIRS)
DIL_HEAD_DIM = 64
DIL_HEADS = 2 * DIL_GROUPS
DIL_WIDTH = DIL_HEADS * DIL_HEAD_DIM
DIL_OUT_WIDTH = 2 * DIL_HEAD_DIM
DIL_RADIUS = 64
DIFF_HEADS = 4
DIFF_HEAD_DIM = 64
DIFF_QK_WIDTH = DIFF_HEADS * 2 * DIFF_HEAD_DIM
DIFF_V_DIM = 2 * DIFF_HEAD_DIM
DIFF_V_WIDTH = DIFF_HEADS * DIFF_V_DIM
SUBLN_EPS = 1e-5
NUM_BUCKETS = 32
MAX_DISTANCE = 1024
N_BIAS_HEADS = DIL_HEADS + DIFF_HEADS
NEG_INF = -1e30
QK_SCALE = DIL_HEAD_DIM ** -0.5
LOG2_E = math.log2(math.e)
B_WIDTH = 3 * DIL_WIDTH
C_WIDTH = 2 * DIFF_QK_WIDTH + DIFF_V_WIDTH
IN_WIDTH = FOURIER_WIDTH + B_WIDTH + C_WIDTH
GATE_WIDTH = 3 * D_MODEL

LANES = 128
BF16_SUBLANES = 16
VMEM_BYTES_V7X = 64 * 1024 * 1024
VMEM_LIMIT = 56 * 1024 * 1024

FFN_TM = 1024
FF_CHUNK = 256
PROJ_TM = 512
PROJ_CHUNK = 512
MERGE_TM = 512
HALF_SEQ = SEQ // 2
FOURIER_CN = 512
DIL_QKV = 3 * LANES
DIL_QB = 128
DIL_KW = DIL_QB + 2 * DIL_RADIUS
DIL_UNROLL = 4
DIFF_TQ = 512
DIFF_STRIP = 2 * SEQ - DIFF_TQ
DIFF_KT = 512


def _params(*semantics):
    return pltpu.CompilerParams(dimension_semantics=semantics, vmem_limit_bytes=VMEM_LIMIT)


def _resident(shape):
    nd = len(shape)
    return pl.BlockSpec(shape, lambda *_: (0,) * nd, pipeline_mode=pl.Buffered(1))


def _rms(x, g, eps):
    return x * lax.rsqrt(jnp.mean(x * x, axis=-1, keepdims=True) + eps) * g


def _cast_specs(jobs, steps):
    in_specs, out_specs, out_shapes = [], [], []
    for src, layer, plan in jobs:
        _, rows, cols = src.shape
        out_cols = cols if plan is None else sum(width for _, width, _ in plan)
        share = 1
        while rows % (steps // share * BF16_SUBLANES):
            share *= 2
        blk = rows // (steps // share)
        in_specs.append(pl.BlockSpec((None, blk, cols),
                                     lambda i, layer=layer, share=share: (layer, i // share, 0)))
        out_specs.append(pl.BlockSpec((blk, out_cols), lambda i, share=share: (i // share, 0)))
        out_shapes.append(jax.ShapeDtypeStruct((rows, out_cols), BF16))
    return in_specs, out_specs, out_shapes


def _cast_blocks(src_refs, dst_refs, plans):
    for src, dst, plan in zip(src_refs, dst_refs, plans):
        if plan is None:
            dst[...] = src[...].astype(BF16)
            continue
        col = 0
        for src_col, width, scale in plan:
            part = src[:, src_col:src_col + width]
            dst[:, col:col + width] = (part if scale == 1.0 else part * scale).astype(BF16)
            col += width


def _cast_args(jobs):
    return [src for src, _, _ in jobs], tuple(plan for _, _, plan in jobs)


def _bucket_breakpoints():
    rel = np.arange(-(SEQ - 1), SEQ)
    half = NUM_BUCKETS // 2
    max_exact = half // 2
    n = np.abs(rel)
    nf = np.maximum(n, 1).astype(np.float64)
    large = max_exact + (np.log(nf / max_exact) / math.log(MAX_DISTANCE / max_exact)
                         * (half - max_exact)).astype(np.int32)
    large = np.minimum(large, half - 1)
    b = np.where(rel > 0, half, 0) + np.where(n < max_exact, n, large)
    pts = [(int(rel[i]), int(b[i])) for i in range(1, len(rel)) if b[i] != b[i - 1]]
    return int(b[0]), pts


def _bias_values(tbl_ref, head, rel):
    first, pts = _bucket_breakpoints()
    val = jnp.full(rel.shape, tbl_ref[first, head], F32)
    for thr, bucket in pts:
        val = jnp.where(rel >= thr, tbl_ref[bucket, head], val)
    return val * LOG2_E


def _diff_bias_kernel(tbl_ref, o_ref):
    rows, cols = o_ref.shape[1], o_ref.shape[2]
    rel = (lax.broadcasted_iota(jnp.int32, (rows, cols), 1)
           - lax.broadcasted_iota(jnp.int32, (rows, cols), 0) - (SEQ - rows))
    o_ref[0] = _bias_values(tbl_ref, DIL_HEADS + pl.program_id(0), rel)


def _diff_bias_strip(rel_bias):
    return pl.pallas_call(
        _diff_bias_kernel,
        grid=(DIFF_HEADS,),
        in_specs=[pl.BlockSpec(memory_space=pltpu.SMEM)],
        out_specs=pl.BlockSpec((1, DIFF_TQ, DIFF_STRIP), lambda h: (h, 0, 0)),
        out_shape=jax.ShapeDtypeStruct((DIFF_HEADS, DIFF_TQ, DIFF_STRIP), F32),
        compiler_params=_params("arbitrary"),
        name="diff_bias_strip",
    )(rel_bias)


EDGE_FIRST, EDGE_NONE, EDGE_LAST, EDGE_BOTH = range(4)


def _dil_bias_kernel(tbl_ref, o_ref, *, head0, dil):
    rows, cols = o_ref.shape[2], o_ref.shape[3]
    col = lax.broadcasted_iota(jnp.int32, (rows, cols), 1)
    rel_t = col - lax.broadcasted_iota(jnp.int32, (rows, cols), 0) - DIL_RADIUS
    val = _bias_values(tbl_ref, head0 + pl.program_id(0), rel_t * dil)
    val = jnp.where(jnp.abs(rel_t) <= DIL_RADIUS, val, NEG_INF)
    lo = jnp.where(col >= DIL_RADIUS, val, NEG_INF)
    o_ref[EDGE_FIRST, 0] = lo
    o_ref[EDGE_NONE, 0] = val
    o_ref[EDGE_LAST, 0] = jnp.where(col < cols - DIL_RADIUS, val, NEG_INF)
    o_ref[EDGE_BOTH, 0] = jnp.where(col < cols - DIL_RADIUS, lo, NEG_INF)


def _dil_bias_tiles(rel_bias, group, dil):
    return pl.pallas_call(
        functools.partial(_dil_bias_kernel, head0=2 * group, dil=dil),
        grid=(2,),
        in_specs=[pl.BlockSpec(memory_space=pltpu.SMEM)],
        out_specs=pl.BlockSpec((4, 1, DIL_QB, DIL_KW), lambda h: (0, h, 0, 0)),
        out_shape=jax.ShapeDtypeStruct((4, 2, DIL_QB, DIL_KW), F32),
        compiler_params=_params("arbitrary"),
        name="dil_bias_tiles",
    )(rel_bias)


def _ffn_kernel(x_ref, g_ref, wg_ref, wu_ref, wd_ref, *rest, final_norm, cast_plans):
    n_cast = len(cast_plans)
    rest = list(rest)
    gf_ref = rest.pop(0) if final_norm else None
    cast_src, o_ref, cast_dst, a_ref = (rest[:n_cast], rest[n_cast], rest[n_cast + 1:-1], rest[-1])
    _cast_blocks(cast_src, cast_dst, cast_plans)
    x = x_ref[...]
    h = _rms(x, g_ref[...], EPS).astype(BF16)
    for c in range(D_FF // FF_CHUNK):
        sl = slice(c * FF_CHUNK, (c + 1) * FF_CHUNK)
        gate = jnp.dot(h, wg_ref[:, sl], preferred_element_type=F32)
        up = jnp.dot(h, wu_ref[:, sl], preferred_element_type=F32)
        a_ref[:, sl] = (gate * jax.nn.sigmoid(gate) * up).astype(BF16)
    y = x + 0.5 * jnp.dot(a_ref[...], wd_ref[...], preferred_element_type=F32)
    if final_norm:
        y = _rms(y, gf_ref[...], EPS)
    o_ref[...] = y


def _ffn(x, g, wg, wu, wd, g_final=None, cast_jobs=()):
    final_norm = g_final is not None
    steps = TOKENS // FFN_TM
    tile = pl.BlockSpec((FFN_TM, D_MODEL), lambda i: (i, 0))
    in_specs = [tile, _resident((1, D_MODEL)), _resident((D_MODEL, D_FF)),
                _resident((D_MODEL, D_FF)), _resident((D_FF, D_MODEL))]
    args = [x, g.reshape(1, D_MODEL), wg, wu, wd]
    if final_norm:
        in_specs.append(_resident((1, D_MODEL)))
        args.append(g_final.reshape(1, D_MODEL))
    cast_in, cast_out, cast_shapes = _cast_specs(cast_jobs, steps)
    cast_srcs, cast_plans = _cast_args(cast_jobs)
    return pl.pallas_call(
        functools.partial(_ffn_kernel, final_norm=final_norm, cast_plans=cast_plans),
        grid=(steps,),
        in_specs=in_specs + cast_in,
        out_specs=[tile] + cast_out,
        out_shape=[jax.ShapeDtypeStruct((TOKENS, D_MODEL), F32)] + cast_shapes,
        scratch_shapes=[pltpu.VMEM((FFN_TM, D_FF), BF16)],
        compiler_params=_params("arbitrary"),
        name="ffn_final" if final_norm else "ffn",
    )(*args, *cast_srcs)


def _proj_kernel(x_ref, g_ref, w_ref, wg_ref, bg_ref, *rest, cast_plans):
    n_cast = len(cast_plans)
    cast_src, rest = rest[:n_cast], rest[n_cast:]
    pf_ref, pb0_ref, pb1_ref, pb2_ref, pc_ref, gate_ref = rest[:6]
    cast_dst, y_ref = rest[6:-1], rest[-1]
    _cast_blocks(cast_src, cast_dst, cast_plans)
    u = _rms(x_ref[...], g_ref[...], EPS).astype(BF16)
    rows = x_ref.shape[0]

    def chunks(width):
        return [slice(c0, min(c0 + PROJ_CHUNK, width)) for c0 in range(0, width, PROJ_CHUNK)]

    def project(w, col0, sl):
        return jnp.dot(u, w[:, col0 + sl.start:col0 + sl.stop], preferred_element_type=F32)

    def by_residue(y, o_ref, dil, col0):
        for j in range(y.shape[1] // LANES):
            y_ref[j] = y[:, j * LANES:(j + 1) * LANES]
            cols = slice(col0 + j * LANES, col0 + (j + 1) * LANES)
            for r in range(dil):
                o_ref[0, r, :, cols] = y_ref[j, pl.ds(r, rows // dil, stride=dil), :].astype(o_ref.dtype)

    for sl in chunks(GATE_WIDTH):
        gate_ref[:, sl] = jax.nn.sigmoid(project(wg_ref, 0, sl) + bg_ref[:, sl]).astype(gate_ref.dtype)
    for sl in chunks(FOURIER_WIDTH):
        by_residue(project(w_ref, 0, sl), pf_ref, 2, sl.start)
    y = project(w_ref, FOURIER_WIDTH, slice(0, B_WIDTH))
    pb0_ref[...] = y[:, :DIL_QKV].astype(pb0_ref.dtype)
    by_residue(y[:, DIL_QKV:2 * DIL_QKV], pb1_ref, DIL_PAIRS[1][1], 0)
    by_residue(y[:, 2 * DIL_QKV:], pb2_ref, DIL_PAIRS[2][1], 0)
    for sl in chunks(C_WIDTH):
        pc_ref[:, sl] = project(w_ref, FOURIER_WIDTH + B_WIDTH, sl).astype(pc_ref.dtype)


def _in_weight_plan():
    q_scale = QK_SCALE * LOG2_E
    plan = [(0, FOURIER_WIDTH, 1.0)]
    for g in range(DIL_GROUPS):
        lo = FOURIER_WIDTH + g * LANES
        plan += [(lo, LANES, q_scale), (lo + DIL_WIDTH, LANES, 1.0), (lo + 2 * DIL_WIDTH, LANES, 1.0)]
    c3 = FOURIER_WIDTH + B_WIDTH
    plan += [(c3, DIFF_QK_WIDTH, q_scale), (c3 + DIFF_QK_WIDTH, DIFF_QK_WIDTH + DIFF_V_WIDTH, 1.0)]
    return plan


def _proj(x, g, w, wg, bg, cast_jobs=()):
    tiles_per_seq = SEQ // PROJ_TM
    steps = TOKENS // PROJ_TM
    cast_in, cast_out, cast_shapes = _cast_specs(cast_jobs, steps)
    cast_srcs, cast_plans = _cast_args(cast_jobs)

    def tile(w):
        return pl.BlockSpec((PROJ_TM, w), lambda i: (i, 0))

    def residue_tile(dil, w):
        return pl.BlockSpec((1, dil, PROJ_TM // dil, w),
                            lambda i: (i // tiles_per_seq, 0, i % tiles_per_seq, 0))

    def residue_shape(dil, w):
        return jax.ShapeDtypeStruct((BATCH, dil, SEQ // dil, w), BF16)

    d1, d2 = DIL_PAIRS[1][1], DIL_PAIRS[2][1]
    return pl.pallas_call(
        functools.partial(_proj_kernel, cast_plans=cast_plans),
        grid=(steps,),
        in_specs=[tile(D_MODEL), _resident((1, D_MODEL)), _resident((D_MODEL, IN_WIDTH)),
                  _resident((D_MODEL, GATE_WIDTH)), _resident((1, GATE_WIDTH))] + cast_in,
        out_specs=[residue_tile(2, FOURIER_WIDTH), tile(DIL_QKV), residue_tile(d1, DIL_QKV),
                   residue_tile(d2, DIL_QKV), tile(C_WIDTH), tile(GATE_WIDTH)] + cast_out,
        out_shape=[residue_shape(2, FOURIER_WIDTH),
                   jax.ShapeDtypeStruct((TOKENS, DIL_QKV), BF16),
                   residue_shape(d1, DIL_QKV), residue_shape(d2, DIL_QKV),
                   jax.ShapeDtypeStruct((TOKENS, C_WIDTH), BF16),
                   jax.ShapeDtypeStruct((TOKENS, GATE_WIDTH), BF16)] + cast_shapes,
        scratch_shapes=[pltpu.VMEM((max(PROJ_CHUNK, DIL_QKV) // LANES, PROJ_TM, LANES), F32)],
        compiler_params=_params("arbitrary"),
        name="proj_in",
    )(x, g.reshape(1, D_MODEL), w, wg, bg.reshape(1, GATE_WIDTH), *cast_srcs)


def _fourier_kernel(a_ref, me_ref, mo_ref, d_ref, o_ref):
    h = HALF_SEQ
    r_even = jnp.dot(me_ref[...], a_ref[0, 0], preferred_element_type=F32)
    r_odd = jnp.dot(mo_ref[...], a_ref[0, 1], preferred_element_type=F32)
    halves = ((r_even + r_odd).astype(BF16), (r_even - r_odd).astype(BF16))
    for half, pq in enumerate(halves):
        for g in range(o_ref.shape[2] // LANES):
            cols = slice(g * LANES, (g + 1) * LANES)
            lhs = jnp.concatenate([pq[:h, cols], pq[h:, cols]], axis=1)
            o_ref[0, half * h:(half + 1) * h, cols] = jnp.dot(
                lhs, d_ref[...], preferred_element_type=F32).astype(o_ref.dtype)


def _fourier(pf, m_even, m_odd, d_chan):
    cn = FOURIER_CN
    return pl.pallas_call(
        _fourier_kernel,
        grid=(BATCH, FOURIER_WIDTH // cn),
        in_specs=[pl.BlockSpec((1, 2, HALF_SEQ, cn), lambda b, j: (b, 0, 0, j)),
                  _resident((SEQ, HALF_SEQ)), _resident((SEQ, HALF_SEQ)),
                  _resident((2 * FOURIER_GROUP_DIM, FOURIER_GROUP_DIM))],
        out_specs=pl.BlockSpec((1, SEQ, cn), lambda b, j: (b, 0, j)),
        out_shape=jax.ShapeDtypeStruct((BATCH, SEQ, FOURIER_WIDTH), BF16),
        compiler_params=_params("parallel", "parallel"),
        name="fourier_mix",
    )(pf, m_even, m_odd, d_chan)


def _dft_matrices():
    half = HALF_SEQ
    split = 32

    def table(mult):
        k = lax.broadcasted_iota(jnp.int32, (half, split), 0)
        j = lax.broadcasted_iota(jnp.int32, (half, split), 1)
        ang = ((k * j * mult) % half).astype(F32) * (2.0 * math.pi / half)
        return jnp.cos(ang), jnp.sin(ang)

    (ca, sa), (cb, sb) = table(split), table(1)
    cos_e = (ca[:, :, None] * cb[:, None, :] - sa[:, :, None] * sb[:, None, :]).reshape(half, half)
    sin_e = (sa[:, :, None] * cb[:, None, :] + ca[:, :, None] * sb[:, None, :]).reshape(half, half)
    phi = lax.broadcasted_iota(jnp.int32, (half, 1), 0).astype(F32) * (math.pi / half)
    cos_o = cos_e * jnp.cos(phi) - sin_e * jnp.sin(phi)
    sin_o = sin_e * jnp.cos(phi) + cos_e * jnp.sin(phi)
    m_even = jnp.concatenate([cos_e, sin_e], axis=0).astype(BF16)
    m_odd = jnp.concatenate([cos_o, sin_o], axis=0).astype(BF16)
    return m_even, m_odd


def _channel_dft():
    gd = FOURIER_GROUP_DIM
    idx = np.arange(gd)
    ang = 2.0 * np.pi * ((idx[:, None] * idx[None, :]) % gd) / gd
    scale = 1.0 / math.sqrt(SEQ * gd)
    return jnp.asarray(np.concatenate([np.cos(ang), -np.sin(ang)]) * scale, F32).astype(BF16)


def _dil_block(q, kw, vw_aug, bias, head0_lanes):
    zero = jnp.zeros_like(q)
    pv, lse = [], []
    for hh in range(2):
        q_h = jnp.where(head0_lanes, q, zero) if hh == 0 else jnp.where(head0_lanes, zero, q)
        logits = lax.dot_general(q_h, kw, (((1,), (1,)), ((), ())),
                                 preferred_element_type=F32) + bias[hh]
        mx = jnp.max(logits, axis=-1, keepdims=True)
        e = jnp.exp2(logits - mx).astype(BF16)
        r = jnp.dot(e, vw_aug, preferred_element_type=F32)
        s = r[:, LANES:]
        pv.append(r[:, :LANES] / s)
        lse.append(mx + jnp.log2(s))
    return jnp.where(head0_lanes, pv[0], pv[1]), jnp.where(head0_lanes, lse[0], lse[1])


def _dil_kernel(g0_ref, g1_ref, g2_ref, b0_ref, b1_ref, b2_ref, out_ref,
                kpad_ref, vpad_ref, o_ref, lse_ref):
    pad = DIL_RADIUS
    head0_lanes = lax.broadcasted_iota(jnp.int32, (DIL_QB, LANES), 1) < DIL_HEAD_DIM
    zeros_k = jnp.zeros((pad, LANES), BF16)
    zeros_v = jnp.zeros((pad, 2 * LANES), BF16)
    q_sl, k_sl, v_sl = (slice(i * LANES, (i + 1) * LANES) for i in range(3))

    vpad_ref[pl.ds(pad, SEQ), LANES:] = jnp.ones((SEQ, LANES), BF16)

    def fill_padded(src, sub_len):
        kpad_ref[:pad, :] = zeros_k
        kpad_ref[pl.ds(pad, sub_len), :] = src[:, k_sl]
        kpad_ref[pl.ds(pad + sub_len, pad), :] = zeros_k
        vpad_ref[:pad, :] = zeros_v
        vpad_ref[pl.ds(pad, sub_len), :LANES] = src[:, v_sl]
        vpad_ref[pl.ds(pad + sub_len, pad), :] = zeros_v

    def store(group, rows, o, lse):
        o_ref[group, rows, :] = o
        lse_ref[group, rows, :] = lse

    def padded_block(group, q_src, bias_ref, blk, edge, dil, residue):
        r0 = pl.multiple_of(blk * DIL_QB, DIL_QB)
        o, lse = _dil_block(q_src[pl.ds(r0, DIL_QB), q_sl], kpad_ref[pl.ds(r0, DIL_KW), :],
                            vpad_ref[pl.ds(r0, DIL_KW), :], [bias_ref[edge, 0], bias_ref[edge, 1]],
                            head0_lanes)
        rows = pl.ds(residue + r0 * dil, DIL_QB, stride=dil) if dil > 1 else pl.ds(r0, DIL_QB)
        store(group, rows, o, lse)

    fill_padded(g0_ref.at[0], SEQ)
    n_blocks0 = SEQ // DIL_QB

    def g0_body(i, carry):
        for n in range(DIL_UNROLL):
            blk = i * DIL_UNROLL + n
            edge = jnp.where(blk == 0, EDGE_FIRST,
                             jnp.where(blk == n_blocks0 - 1, EDGE_LAST, EDGE_NONE))
            padded_block(0, g0_ref.at[0], b0_ref, blk, edge, 1, 0)
        return carry

    lax.fori_loop(0, n_blocks0 // DIL_UNROLL, g0_body, 0)

    dil1 = DIL_PAIRS[1][1]
    n_blocks1 = SEQ // dil1 // DIL_QB

    def g1_body(r, carry):
        src = g1_ref.at[0, r]
        fill_padded(src, SEQ // dil1)
        for blk in range(n_blocks1):
            edge = EDGE_FIRST if blk == 0 else EDGE_LAST if blk == n_blocks1 - 1 else EDGE_NONE
            padded_block(1, src, b1_ref, blk, edge, dil1, r)
        return carry

    lax.fori_loop(0, dil1, g1_body, 0)

    dil2 = DIL_PAIRS[2][1]
    assert SEQ // dil2 == DIL_QB
    for n in range(DIL_UNROLL):
        base = n * DIL_KW
        for ref, zeros in ((kpad_ref, zeros_k), (vpad_ref, zeros_v)):
            ref[base:base + pad, :] = zeros
            ref[base + pad + DIL_QB:base + DIL_KW, :] = zeros
        vpad_ref[base + pad:base + pad + DIL_QB, LANES:] = jnp.ones((DIL_QB, LANES), BF16)

    def g2_body(i, carry):
        for n in range(DIL_UNROLL):
            r = i * DIL_UNROLL + n
            src = g2_ref.at[0, r]
            base = n * DIL_KW
            kpad_ref[base + pad:base + pad + DIL_QB, :] = src[:, k_sl]
            vpad_ref[base + pad:base + pad + DIL_QB, :LANES] = src[:, v_sl]
            o, lse = _dil_block(src[:, q_sl], kpad_ref[base:base + DIL_KW, :],
                                vpad_ref[base:base + DIL_KW, :],
                                [b2_ref[EDGE_BOTH, 0], b2_ref[EDGE_BOTH, 1]], head0_lanes)
            store(2, pl.ds(r, DIL_QB, stride=dil2), o, lse)
        return carry

    lax.fori_loop(0, dil2 // DIL_UNROLL, g2_body, 0)

    chunk = 256
    for c in range(SEQ // chunk):
        rows = slice(c * chunk, (c + 1) * chunk)
        l0, l1, l2 = lse_ref[0, rows, :], lse_ref[1, rows, :], lse_ref[2, rows, :]
        mx = jnp.maximum(jnp.maximum(l0, l1), l2)
        e0, e1, e2 = jnp.exp2(l0 - mx), jnp.exp2(l1 - mx), jnp.exp2(l2 - mx)
        mixed = (e0 * o_ref[0, rows, :] + e1 * o_ref[1, rows, :] + e2 * o_ref[2, rows, :]) / (e0 + e1 + e2)
        out_ref[0, rows, :] = mixed.astype(out_ref.dtype)


def _dilated_mixture(pb0, pb1, pb2, biases):
    d1, d2 = DIL_PAIRS[1][1], DIL_PAIRS[2][1]
    bias_spec = _resident((4, 2, DIL_QB, DIL_KW))
    max_len = SEQ + 2 * DIL_RADIUS
    return pl.pallas_call(
        _dil_kernel,
        grid=(BATCH,),
        in_specs=[pl.BlockSpec((1, SEQ, DIL_QKV), lambda b: (b, 0, 0)),
                  pl.BlockSpec((1, d1, SEQ // d1, DIL_QKV), lambda b: (b, 0, 0, 0)),
                  pl.BlockSpec((1, d2, SEQ // d2, DIL_QKV), lambda b: (b, 0, 0, 0)),
                  bias_spec, bias_spec, bias_spec],
        out_specs=pl.BlockSpec((1, SEQ, LANES), lambda b: (b, 0, 0)),
        out_shape=jax.ShapeDtypeStruct((BATCH, SEQ, DIL_OUT_WIDTH), BF16),
        scratch_shapes=[pltpu.VMEM((max_len, LANES), BF16), pltpu.VMEM((max_len, 2 * LANES), BF16),
                        pltpu.VMEM((DIL_GROUPS, SEQ, LANES), F32),
                        pltpu.VMEM((DIL_GROUPS, SEQ, LANES), F32)],
        compiler_params=_params("parallel"),
        name="dilated_attn",
    )(pb0, pb1, pb2, *biases)


def _diff_kernel(q_ref, k_ref, v_ref, bias_ref, lq1_ref, lk1_ref, lq2_ref, lk2_ref, sg_ref,
                 o_ref, vaug_ref, logit_ref, rmax_ref, *, lam_init):
    s = pl.program_id(0)
    n_blocks = pl.num_programs(0) - 1
    nq = SEQ // DIFF_TQ
    cur = jnp.minimum(s, n_blocks - 1)
    prev = jnp.maximum(s - 1, 0)
    slot = s % 2

    @pl.when(s == 0)
    def _():
        logit_ref[1] = jnp.zeros(logit_ref.shape[1:], F32)
        rmax_ref[1] = jnp.zeros(rmax_ref.shape[1:], F32)

    @pl.when(prev % nq == 0)
    def _():
        vaug_ref[:, :LANES] = v_ref[0]
        vaug_ref[:, LANES:] = jnp.ones((SEQ, LANES), BF16)

    n_tiles = SEQ // DIFF_KT

    def key_tile(t):
        return slice(t * DIFF_KT, (t + 1) * DIFF_KT)

    off = pl.multiple_of((nq - 1 - cur % nq) * DIFF_TQ, DIFF_TQ)
    q = q_ref[0]
    first_half = lax.broadcasted_iota(jnp.int32, q.shape, 1) < DIFF_HEAD_DIM
    zero = jnp.zeros_like(q)
    outs = []
    for m in range(2):
        row_max = rmax_ref[1 - slot, m]
        pv = jnp.zeros((DIFF_TQ, 2 * LANES), F32)
        for t in range(n_tiles):
            old = logit_ref[1 - slot, m, :, key_tile(t)]
            e = jnp.concatenate([jnp.exp2(old[:, c * LANES:(c + 1) * LANES] - row_max)
                                 for c in range(DIFF_KT // LANES)], axis=1).astype(BF16)
            pv = pv + jnp.dot(e, vaug_ref[key_tile(t), :], preferred_element_type=F32)
        outs.append(pv[:, :LANES] / pv[:, LANES:])

    for m in range(2):
        q_m = jnp.where(first_half, q, zero) if m == 0 else jnp.where(first_half, zero, q)
        lane_max = None
        for t in range(n_tiles):
            bias = bias_ref[0, :, pl.ds(pl.multiple_of(off + t * DIFF_KT, LANES), DIFF_KT)]
            logits = lax.dot_general(q_m, k_ref[0, key_tile(t), :], (((1,), (1,)), ((), ())),
                                     preferred_element_type=F32) + bias
            logit_ref[slot, m, :, key_tile(t)] = logits
            for c in range(DIFF_KT // LANES):
                part = logits[:, c * LANES:(c + 1) * LANES]
                lane_max = part if lane_max is None else jnp.maximum(lane_max, part)
        rmax_ref[slot, m] = jnp.broadcast_to(jnp.max(lane_max, axis=-1, keepdims=True),
                                             (DIFF_TQ, LANES))

    lam = (jnp.exp(jnp.sum(lq1_ref[...] * lk1_ref[...], axis=-1, keepdims=True))
           - jnp.exp(jnp.sum(lq2_ref[...] * lk2_ref[...], axis=-1, keepdims=True))
           + lam_init)
    o = outs[0] - lam * outs[1]
    o = _rms(o, sg_ref[...], SUBLN_EPS) * (1.0 - lam_init)
    o_ref[0] = o.astype(o_ref.dtype)


def _diff_attention(pc, strip, lq1, lk1, lq2, lk2, subln_g, lam_init):
    pcv = pc.reshape(BATCH, SEQ, C_WIDTH)
    vec = _resident((1, DIFF_HEAD_DIM))
    nq = SEQ // DIFF_TQ
    n_blocks = DIFF_HEADS * BATCH * nq

    def block_of(step, lag):
        blk = jnp.clip(step - lag, 0, n_blocks - 1)
        return blk // (BATCH * nq), (blk // nq) % BATCH, blk % nq

    def q_map(s):
        h, b, j = block_of(s, 0)
        return b, j, h

    def k_map(s):
        h, b, _ = block_of(s, 0)
        return b, 0, DIFF_HEADS + h

    def v_map(s):
        h, b, _ = block_of(s, 1)
        return b, 0, 2 * DIFF_HEADS + h

    def out_map(s):
        h, b, j = block_of(s, 1)
        return b, j, h

    return pl.pallas_call(
        functools.partial(_diff_kernel, lam_init=lam_init),
        grid=(n_blocks + 1,),
        in_specs=[pl.BlockSpec((1, DIFF_TQ, LANES), q_map),
                  pl.BlockSpec((1, SEQ, LANES), k_map),
                  pl.BlockSpec((1, SEQ, LANES), v_map),
                  pl.BlockSpec((1, DIFF_TQ, DIFF_STRIP), lambda s: (block_of(s, 0)[0], 0, 0)),
                  vec, vec, vec, vec, _resident((1, DIFF_V_DIM))],
        out_specs=pl.BlockSpec((1, DIFF_TQ, LANES), out_map),
        out_shape=jax.ShapeDtypeStruct((BATCH, SEQ, DIFF_V_WIDTH), BF16),
        scratch_shapes=[pltpu.VMEM((SEQ, 2 * LANES), BF16),
                        pltpu.VMEM((2, 2, DIFF_TQ, SEQ), F32),
                        pltpu.VMEM((2, 2, DIFF_TQ, LANES), F32)],
        compiler_params=_params("arbitrary"),
        name="diff_attn",
    )(pcv, pcv, pcv, strip, lq1.reshape(1, -1), lk1.reshape(1, -1), lq2.reshape(1, -1),
      lk2.reshape(1, -1), subln_g.reshape(1, -1)).reshape(TOKENS, DIFF_V_WIDTH)


def _merge_kernel(x_ref, f_ref, b_ref, c_ref, gate_ref, wa_ref, wb_ref, wc_ref, wo_ref,
                  *rest, cast_plans):
    n_cast = len(cast_plans)
    cast_src, out_ref, cast_dst = rest[:n_cast], rest[n_cast], rest[n_cast + 1:]
    _cast_blocks(cast_src, cast_dst, cast_plans)
    y_a = jnp.dot(f_ref[...], wa_ref[...], preferred_element_type=F32)
    y_b = jnp.dot(b_ref[...], wb_ref[...], preferred_element_type=F32)
    y_c = jnp.dot(c_ref[...], wc_ref[...], preferred_element_type=F32)
    merged = (gate_ref[:, :D_MODEL].astype(F32) * y_a
              + gate_ref[:, D_MODEL:2 * D_MODEL].astype(F32) * y_b
              + gate_ref[:, 2 * D_MODEL:].astype(F32) * y_c)
    out_ref[...] = x_ref[...] + jnp.dot(merged.astype(BF16), wo_ref[...],
                                        preferred_element_type=F32)


def _merge(x, f, b, c, gates, wa, wb, wc, wo, cast_jobs=()):
    def tile(w):
        return pl.BlockSpec((MERGE_TM, w), lambda i: (i, 0))

    steps = TOKENS // MERGE_TM
    cast_in, cast_out, cast_shapes = _cast_specs(cast_jobs, steps)
    cast_srcs, cast_plans = _cast_args(cast_jobs)
    return pl.pallas_call(
        functools.partial(_merge_kernel, cast_plans=cast_plans),
        grid=(steps,),
        in_specs=[tile(D_MODEL), tile(FOURIER_WIDTH), tile(DIL_OUT_WIDTH), tile(DIFF_V_WIDTH),
                  tile(GATE_WIDTH), _resident(wa.shape), _resident(wb.shape),
                  _resident(wc.shape), _resident(wo.shape)] + cast_in,
        out_specs=[tile(D_MODEL)] + cast_out,
        out_shape=[jax.ShapeDtypeStruct((TOKENS, D_MODEL), F32)] + cast_shapes,
        compiler_params=_params("arbitrary"),
        name="gated_merge",
    )(x, f, b, c, gates, wa, wb, wc, wo, *cast_srcs)


def kernel(x, g_ffn1, w_ffn1_gate, w_ffn1_up, w_ffn1_down, g_mix, w_in, w_gate, b_gate,
           w_br_a, w_br_b, w_br_c, w_out, lam_q1, lam_k1, lam_q2, lam_k2, subln_g,
           rel_bias, g_ffn2, w_ffn2_gate, w_ffn2_up, w_ffn2_down, g_final):
    assert x.shape == (BATCH, SEQ, D_MODEL) and x.dtype == F32
    assert w_in.shape == (DEPTH, D_MODEL, IN_WIDTH)
    bf = lambda w: w.astype(BF16)

    dil_bias = [_dil_bias_tiles(rel_bias, g, dil) for g, (_, dil) in enumerate(DIL_PAIRS)]
    diff_strip = _diff_bias_strip(rel_bias)
    m_even, m_odd = _dft_matrices()
    d_chan = _channel_dft()
    in_plan = _in_weight_plan()

    h = x.reshape(TOKENS, D_MODEL)
    ffn1_w = [bf(w_ffn1_gate[0]), bf(w_ffn1_up[0]), bf(w_ffn1_down[0])]
    for l in range(DEPTH):
        last = l == DEPTH - 1
        h, wa, wc_br, wo, gate_w, in_w = _ffn(
            h, g_ffn1[l], *ffn1_w,
            cast_jobs=[(w_br_a, l, None), (w_br_c, l, None), (w_out, l, None),
                       (w_gate, l, None), (w_in, l, in_plan)])

        pf, pb0, pb1, pb2, pc, gates, *ffn2_w = _proj(
            h, g_mix[l], in_w, gate_w, b_gate[l],
            cast_jobs=[(w_ffn2_gate, l, None), (w_ffn2_up, l, None), (w_ffn2_down, l, None)])

        f = _fourier(pf, m_even, m_odd, d_chan).reshape(TOKENS, FOURIER_WIDTH)
        mixed = _dilated_mixture(pb0.reshape(BATCH, SEQ, DIL_QKV), pb1, pb2, dil_bias)
        lam_init = 0.8 - 0.6 * math.exp(-0.3 * l)
        c = _diff_attention(pc, diff_strip, lam_q1[l], lam_k1[l], lam_q2[l], lam_k2[l],
                            subln_g[l], lam_init)

        h, *ffn1_w = _merge(
            h, f, mixed.reshape(TOKENS, DIL_OUT_WIDTH), c, gates, wa, bf(w_br_b[l]), wc_br, wo,
            cast_jobs=[] if last else [(w_ffn1_gate, l + 1, None), (w_ffn1_up, l + 1, None),
                                       (w_ffn1_down, l + 1, None)])

        h, = _ffn(h, g_ffn2[l], *ffn2_w, g_final if last else None)
    return h.reshape(BATCH, SEQ, D_MODEL)
```

```python
import functools
import math

import numpy as np
import jax
import jax.numpy as jnp
from jax import lax
from jax.experimental import pallas as pl
from jax.experimental.pallas import tpu as pltpu

F32 = jnp.float32
BF16 = jnp.bfloat16

D_MODEL = 1024
BATCH = 8
SEQ = 2048
DEPTH = 2
TOKENS = BATCH * SEQ
D_FF = 2816
EPS = 1e-6
FOURIER_GROUPS = 8
FOURIER_GROUP_DIM = 128
FOURIER_WIDTH = FOURIER_GROUPS * FOURIER_GROUP_DIM
DIL_PAIRS = ((128, 1), (512, 4), (2048, 16))
DIL_GROUPS = len(DIL_PAIRS)
DIL_HEAD_DIM = 64
DIL_HEADS = 2 * DIL_GROUPS
DIL_WIDTH = DIL_HEADS * DIL_HEAD_DIM
DIL_OUT_WIDTH = 2 * DIL_HEAD_DIM
DIL_RADIUS = 64
DIFF_HEADS = 4
DIFF_HEAD_DIM = 64
DIFF_QK_WIDTH = DIFF_HEADS * 2 * DIFF_HEAD_DIM
DIFF_V_DIM = 2 * DIFF_HEAD_DIM
DIFF_V_WIDTH = DIFF_HEADS * DIFF_V_DIM
SUBLN_EPS = 1e-5
NUM_BUCKETS = 32
MAX_DISTANCE = 1024
N_BIAS_HEADS = DIL_HEADS + DIFF_HEADS
NEG_INF = -1e30
QK_SCALE = DIL_HEAD_DIM ** -0.5
LOG2_E = math.log2(math.e)
B_WIDTH = 3 * DIL_WIDTH
C_WIDTH = 2 * DIFF_QK_WIDTH + DIFF_V_WIDTH
IN_WIDTH = FOURIER_WIDTH + B_WIDTH + C_WIDTH
GATE_WIDTH = 3 * D_MODEL

LANES = 128
BF16_SUBLANES = 16
VMEM_BYTES_V7X = 64 * 1024 * 1024
VMEM_LIMIT = 56 * 1024 * 1024

FFN_TM = 1024
FF_CHUNK = 256
PROJ_TM = 512
PROJ_CHUNK = 512
MERGE_TM = 512
HALF_SEQ = SEQ // 2
FOURIER_CN = 512
DIL_QKV = 3 * LANES
DIL_QB = 128
DIL_KW = DIL_QB + 2 * DIL_RADIUS
DIL_UNROLL = 8
DIFF_TQ = 512
DIFF_STRIP = 2 * SEQ - DIFF_TQ
DIFF_KT = 512


def _params(*semantics):
    return pltpu.CompilerParams(dimension_semantics=semantics, vmem_limit_bytes=VMEM_LIMIT)


def _resident(shape):
    nd = len(shape)
    return pl.BlockSpec(shape, lambda *_: (0,) * nd, pipeline_mode=pl.Buffered(1))


def _rms(x, g, eps):
    return x * lax.rsqrt(jnp.mean(x * x, axis=-1, keepdims=True) + eps) * g


def _cast_specs(jobs, steps):
    in_specs, out_specs, out_shapes = [], [], []
    for src, layer, plan in jobs:
        _, rows, cols = src.shape
        out_cols = cols if plan is None else sum(width for _, width, _ in plan)
        share = 1
        while rows % (steps // share * BF16_SUBLANES):
            share *= 2
        blk = rows // (steps // share)
        in_specs.append(pl.BlockSpec((None, blk, cols),
                                     lambda i, layer=layer, share=share: (layer, i // share, 0)))
        out_specs.append(pl.BlockSpec((blk, out_cols), lambda i, share=share: (i // share, 0)))
        out_shapes.append(jax.ShapeDtypeStruct((rows, out_cols), BF16))
    return in_specs, out_specs, out_shapes


def _cast_blocks(src_refs, dst_refs, plans):
    for src, dst, plan in zip(src_refs, dst_refs, plans):
        if plan is None:
            dst[...] = src[...].astype(BF16)
            continue
        col = 0
        for src_col, width, scale in plan:
            part = src[:, src_col:src_col + width]
            dst[:, col:col + width] = (part if scale == 1.0 else part * scale).astype(BF16)
            col += width


def _cast_args(jobs):
    return [src for src, _, _ in jobs], tuple(plan for _, _, plan in jobs)


def _bucket_breakpoints():
    rel = np.arange(-(SEQ - 1), SEQ)
    half = NUM_BUCKETS // 2
    max_exact = half // 2
    n = np.abs(rel)
    nf = np.maximum(n, 1).astype(np.float64)
    large = max_exact + (np.log(nf / max_exact) / math.log(MAX_DISTANCE / max_exact)
                         * (half - max_exact)).astype(np.int32)
    large = np.minimum(large, half - 1)
    b = np.where(rel > 0, half, 0) + np.where(n < max_exact, n, large)
    pts = [(int(rel[i]), int(b[i])) for i in range(1, len(rel)) if b[i] != b[i - 1]]
    return int(b[0]), pts


def _bias_values(tbl_ref, head, rel):
    first, pts = _bucket_breakpoints()
    val = jnp.full(rel.shape, tbl_ref[first, head], F32)
    for thr, bucket in pts:
        val = jnp.where(rel >= thr, tbl_ref[bucket, head], val)
    return val * LOG2_E


def _diff_bias_kernel(tbl_ref, o_ref):
    rows, cols = o_ref.shape[1], o_ref.shape[2]
    rel = (lax.broadcasted_iota(jnp.int32, (rows, cols), 1)
           - lax.broadcasted_iota(jnp.int32, (rows, cols), 0) - (SEQ - rows))
    o_ref[0] = _bias_values(tbl_ref, DIL_HEADS + pl.program_id(0), rel)


def _diff_bias_strip(rel_bias):
    return pl.pallas_call(
        _diff_bias_kernel,
        grid=(DIFF_HEADS,),
        in_specs=[pl.BlockSpec(memory_space=pltpu.SMEM)],
        out_specs=pl.BlockSpec((1, DIFF_TQ, DIFF_STRIP), lambda h: (h, 0, 0)),
        out_shape=jax.ShapeDtypeStruct((DIFF_HEADS, DIFF_TQ, DIFF_STRIP), F32),
        compiler_params=_params("arbitrary"),
        name="diff_bias_strip",
    )(rel_bias)


EDGE_FIRST, EDGE_NONE, EDGE_LAST, EDGE_BOTH = range(4)


def _dil_bias_kernel(tbl_ref, o_ref, *, head0, dil):
    rows, cols = o_ref.shape[2], o_ref.shape[3]
    col = lax.broadcasted_iota(jnp.int32, (rows, cols), 1)
    rel_t = col - lax.broadcasted_iota(jnp.int32, (rows, cols), 0) - DIL_RADIUS
    val = _bias_values(tbl_ref, head0 + pl.program_id(0), rel_t * dil)
    val = jnp.where(jnp.abs(rel_t) <= DIL_RADIUS, val, NEG_INF)
    lo = jnp.where(col >= DIL_RADIUS, val, NEG_INF)
    o_ref[EDGE_FIRST, 0] = lo
    o_ref[EDGE_NONE, 0] = val
    o_ref[EDGE_LAST, 0] = jnp.where(col < cols - DIL_RADIUS, val, NEG_INF)
    o_ref[EDGE_BOTH, 0] = jnp.where(col < cols - DIL_RADIUS, lo, NEG_INF)


def _dil_bias_tiles(rel_bias, group, dil):
    return pl.pallas_call(
        functools.partial(_dil_bias_kernel, head0=2 * group, dil=dil),
        grid=(2,),
        in_specs=[pl.BlockSpec(memory_space=pltpu.SMEM)],
        out_specs=pl.BlockSpec((4, 1, DIL_QB, DIL_KW), lambda h: (0, h, 0, 0)),
        out_shape=jax.ShapeDtypeStruct((4, 2, DIL_QB, DIL_KW), F32),
        compiler_params=_params("arbitrary"),
        name="dil_bias_tiles",
    )(rel_bias)


def _ffn_kernel(x_ref, g_ref, wg_ref, wu_ref, wd_ref, *rest, final_norm, cast_plans):
    n_cast = len(cast_plans)
    rest = list(rest)
    gf_ref = rest.pop(0) if final_norm else None
    cast_src, o_ref, cast_dst, a_ref = (rest[:n_cast], rest[n_cast], rest[n_cast + 1:-1], rest[-1])
    _cast_blocks(cast_src, cast_dst, cast_plans)
    x = x_ref[...]
    h = _rms(x, g_ref[...], EPS).astype(BF16)
    for c in range(D_FF // FF_CHUNK):
        sl = slice(c * FF_CHUNK, (c + 1) * FF_CHUNK)
        gate = jnp.dot(h, wg_ref[:, sl], preferred_element_type=F32)
        up = jnp.dot(h, wu_ref[:, sl], preferred_element_type=F32)
        a_ref[:, sl] = (gate * jax.nn.sigmoid(gate) * up).astype(BF16)
    y = x + 0.5 * jnp.dot(a_ref[...], wd_ref[...], preferred_element_type=F32)
    if final_norm:
        y = _rms(y, gf_ref[...], EPS)
    o_ref[...] = y


def _ffn(x, g, wg, wu, wd, g_final=None, cast_jobs=()):
    final_norm = g_final is not None
    steps = TOKENS // FFN_TM
    tile = pl.BlockSpec((FFN_TM, D_MODEL), lambda i: (i, 0))
    in_specs = [tile, _resident((1, D_MODEL)), _resident((D_MODEL, D_FF)),
                _resident((D_MODEL, D_FF)), _resident((D_FF, D_MODEL))]
    args = [x, g.reshape(1, D_MODEL), wg, wu, wd]
    if final_norm:
        in_specs.append(_resident((1, D_MODEL)))
        args.append(g_final.reshape(1, D_MODEL))
    cast_in, cast_out, cast_shapes = _cast_specs(cast_jobs, steps)
    cast_srcs, cast_plans = _cast_args(cast_jobs)
    return pl.pallas_call(
        functools.partial(_ffn_kernel, final_norm=final_norm, cast_plans=cast_plans),
        grid=(steps,),
        in_specs=in_specs + cast_in,
        out_specs=[tile] + cast_out,
        out_shape=[jax.ShapeDtypeStruct((TOKENS, D_MODEL), F32)] + cast_shapes,
        scratch_shapes=[pltpu.VMEM((FFN_TM, D_FF), BF16)],
        compiler_params=_params("arbitrary"),
        name="ffn_final" if final_norm else "ffn",
    )(*args, *cast_srcs)


def _proj_kernel(x_ref, g_ref, w_ref, wg_ref, bg_ref, *rest, cast_plans):
    n_cast = len(cast_plans)
    cast_src, rest = rest[:n_cast], rest[n_cast:]
    pf_ref, pb0_ref, pb1_ref, pb2_ref, pc_ref, gate_ref = rest[:6]
    cast_dst, y_ref = rest[6:-1], rest[-1]
    _cast_blocks(cast_src, cast_dst, cast_plans)
    u = _rms(x_ref[...], g_ref[...], EPS).astype(BF16)
    rows = x_ref.shape[0]

    def chunks(width):
        return [slice(c0, min(c0 + PROJ_CHUNK, width)) for c0 in range(0, width, PROJ_CHUNK)]

    def project(w, col0, sl):
        return jnp.dot(u, w[:, col0 + sl.start:col0 + sl.stop], preferred_element_type=F32)

    def by_residue(y, o_ref, dil, col0):
        for j in range(y.shape[1] // LANES):
            y_ref[j] = y[:, j * LANES:(j + 1) * LANES]
            cols = slice(col0 + j * LANES, col0 + (j + 1) * LANES)
            for r in range(dil):
                o_ref[0, r, :, cols] = y_ref[j, pl.ds(r, rows // dil, stride=dil), :].astype(o_ref.dtype)

    for sl in chunks(GATE_WIDTH):
        gate_ref[:, sl] = jax.nn.sigmoid(project(wg_ref, 0, sl) + bg_ref[:, sl]).astype(gate_ref.dtype)
    for sl in chunks(FOURIER_WIDTH):
        by_residue(project(w_ref, 0, sl), pf_ref, 2, sl.start)
    y = project(w_ref, FOURIER_WIDTH, slice(0, B_WIDTH))
    pb0_ref[...] = y[:, :DIL_QKV].astype(pb0_ref.dtype)
    by_residue(y[:, DIL_QKV:2 * DIL_QKV], pb1_ref, DIL_PAIRS[1][1], 0)
    by_residue(y[:, 2 * DIL_QKV:], pb2_ref, DIL_PAIRS[2][1], 0)
    for sl in chunks(C_WIDTH):
        pc_ref[:, sl] = project(w_ref, FOURIER_WIDTH + B_WIDTH, sl).astype(pc_ref.dtype)


def _in_weight_plan():
    q_scale = QK_SCALE * LOG2_E
    plan = [(0, FOURIER_WIDTH, 1.0)]
    for g in range(DIL_GROUPS):
        lo = FOURIER_WIDTH + g * LANES
        plan += [(lo, LANES, q_scale), (lo + DIL_WIDTH, LANES, 1.0), (lo + 2 * DIL_WIDTH, LANES, 1.0)]
    c3 = FOURIER_WIDTH + B_WIDTH
    plan += [(c3, DIFF_QK_WIDTH, q_scale), (c3 + DIFF_QK_WIDTH, DIFF_QK_WIDTH + DIFF_V_WIDTH, 1.0)]
    return plan


def _proj(x, g, w, wg, bg, cast_jobs=()):
    tiles_per_seq = SEQ // PROJ_TM
    steps = TOKENS // PROJ_TM
    cast_in, cast_out, cast_shapes = _cast_specs(cast_jobs, steps)
    cast_srcs, cast_plans = _cast_args(cast_jobs)

    def tile(w):
        return pl.BlockSpec((PROJ_TM, w), lambda i: (i, 0))

    def residue_tile(dil, w):
        return pl.BlockSpec((1, dil, PROJ_TM // dil, w),
                            lambda i: (i // tiles_per_seq, 0, i % tiles_per_seq, 0))

    def residue_shape(dil, w):
        return jax.ShapeDtypeStruct((BATCH, dil, SEQ // dil, w), BF16)

    d1, d2 = DIL_PAIRS[1][1], DIL_PAIRS[2][1]
    return pl.pallas_call(
        functools.partial(_proj_kernel, cast_plans=cast_plans),
        grid=(steps,),
        in_specs=[tile(D_MODEL), _resident((1, D_MODEL)), _resident((D_MODEL, IN_WIDTH)),
                  _resident((D_MODEL, GATE_WIDTH)), _resident((1, GATE_WIDTH))] + cast_in,
        out_specs=[residue_tile(2, FOURIER_WIDTH), tile(DIL_QKV), residue_tile(d1, DIL_QKV),
                   residue_tile(d2, DIL_QKV), tile(C_WIDTH), tile(GATE_WIDTH)] + cast_out,
        out_shape=[residue_shape(2, FOURIER_WIDTH),
                   jax.ShapeDtypeStruct((TOKENS, DIL_QKV), BF16),
                   residue_shape(d1, DIL_QKV), residue_shape(d2, DIL_QKV),
                   jax.ShapeDtypeStruct((TOKENS, C_WIDTH), BF16),
                   jax.ShapeDtypeStruct((TOKENS, GATE_WIDTH), BF16)] + cast_shapes,
        scratch_shapes=[pltpu.VMEM((max(PROJ_CHUNK, DIL_QKV) // LANES, PROJ_TM, LANES), F32)],
        compiler_params=_params("arbitrary"),
        name="proj_in",
    )(x, g.reshape(1, D_MODEL), w, wg, bg.reshape(1, GATE_WIDTH), *cast_srcs)


def _fourier_kernel(a_ref, me_ref, mo_ref, d_ref, o_ref):
    h = HALF_SEQ
    r_even = jnp.dot(me_ref[...], a_ref[0, 0], preferred_element_type=F32)
    r_odd = jnp.dot(mo_ref[...], a_ref[0, 1], preferred_element_type=F32)
    halves = ((r_even + r_odd).astype(BF16), (r_even - r_odd).astype(BF16))
    for half, pq in enumerate(halves):
        for g in range(o_ref.shape[2] // LANES):
            cols = slice(g * LANES, (g + 1) * LANES)
            lhs = jnp.concatenate([pq[:h, cols], pq[h:, cols]], axis=1)
            o_ref[0, half * h:(half + 1) * h, cols] = jnp.dot(
                lhs, d_ref[...], preferred_element_type=F32).astype(o_ref.dtype)


def _fourier(pf, m_even, m_odd, d_chan):
    cn = FOURIER_CN
    return pl.pallas_call(
        _fourier_kernel,
        grid=(BATCH, FOURIER_WIDTH // cn),
        in_specs=[pl.BlockSpec((1, 2, HALF_SEQ, cn), lambda b, j: (b, 0, 0, j)),
                  _resident((SEQ, HALF_SEQ)), _resident((SEQ, HALF_SEQ)),
                  _resident((2 * FOURIER_GROUP_DIM, FOURIER_GROUP_DIM))],
        out_specs=pl.BlockSpec((1, SEQ, cn), lambda b, j: (b, 0, j)),
        out_shape=jax.ShapeDtypeStruct((BATCH, SEQ, FOURIER_WIDTH), BF16),
        compiler_params=_params("parallel", "parallel"),
        name="fourier_mix",
    )(pf, m_even, m_odd, d_chan)


def _dft_matrices():
    half = HALF_SEQ
    split = 32

    def table(mult):
        k = lax.broadcasted_iota(jnp.int32, (half, split), 0)
        j = lax.broadcasted_iota(jnp.int32, (half, split), 1)
        ang = ((k * j * mult) % half).astype(F32) * (2.0 * math.pi / half)
        return jnp.cos(ang), jnp.sin(ang)

    (ca, sa), (cb, sb) = table(split), table(1)
    cos_e = (ca[:, :, None] * cb[:, None, :] - sa[:, :, None] * sb[:, None, :]).reshape(half, half)
    sin_e = (sa[:, :, None] * cb[:, None, :] + ca[:, :, None] * sb[:, None, :]).reshape(half, half)
    phi = lax.broadcasted_iota(jnp.int32, (half, 1), 0).astype(F32) * (math.pi / half)
    cos_o = cos_e * jnp.cos(phi) - sin_e * jnp.sin(phi)
    sin_o = sin_e * jnp.cos(phi) + cos_e * jnp.sin(phi)
    m_even = jnp.concatenate([cos_e, sin_e], axis=0).astype(BF16)
    m_odd = jnp.concatenate([cos_o, sin_o], axis=0).astype(BF16)
    return m_even, m_odd


def _channel_dft():
    gd = FOURIER_GROUP_DIM
    idx = np.arange(gd)
    ang = 2.0 * np.pi * ((idx[:, None] * idx[None, :]) % gd) / gd
    scale = 1.0 / math.sqrt(SEQ * gd)
    return jnp.asarray(np.concatenate([np.cos(ang), -np.sin(ang)]) * scale, F32).astype(BF16)


def _dil_block(q, kw, vw_aug, bias, head0_lanes):
    zero = jnp.zeros_like(q)
    pv, lse = [], []
    for hh in range(2):
        q_h = jnp.where(head0_lanes, q, zero) if hh == 0 else jnp.where(head0_lanes, zero, q)
        logits = lax.dot_general(q_h, kw, (((1,), (1,)), ((), ())),
                                 preferred_element_type=F32) + bias[hh]
        mx = jnp.max(logits, axis=-1, keepdims=True)
        e = jnp.exp2(logits - mx).astype(BF16)
        r = jnp.dot(e, vw_aug, preferred_element_type=F32)
        s = r[:, LANES:]
        pv.append(r[:, :LANES] / s)
        lse.append(mx + jnp.log2(s))
    return jnp.where(head0_lanes, pv[0], pv[1]), jnp.where(head0_lanes, lse[0], lse[1])


def _dil_kernel(g0_ref, g1_ref, g2_ref, b0_ref, b1_ref, b2_ref, out_ref,
                kpad_ref, vpad_ref, o_ref, lse_ref):
    pad = DIL_RADIUS
    head0_lanes = lax.broadcasted_iota(jnp.int32, (DIL_QB, LANES), 1) < DIL_HEAD_DIM
    zeros_k = jnp.zeros((pad, LANES), BF16)
    zeros_v = jnp.zeros((pad, 2 * LANES), BF16)
    q_sl, k_sl, v_sl = (slice(i * LANES, (i + 1) * LANES) for i in range(3))

    vpad_ref[pl.ds(pad, SEQ), LANES:] = jnp.ones((SEQ, LANES), BF16)

    def fill_padded(src, sub_len):
        kpad_ref[:pad, :] = zeros_k
        kpad_ref[pl.ds(pad, sub_len), :] = src[:, k_sl]
        kpad_ref[pl.ds(pad + sub_len, pad), :] = zeros_k
        vpad_ref[:pad, :] = zeros_v
        vpad_ref[pl.ds(pad, sub_len), :LANES] = src[:, v_sl]
        vpad_ref[pl.ds(pad + sub_len, pad), :] = zeros_v

    def store(group, rows, o, lse):
        o_ref[group, rows, :] = o
        lse_ref[group, rows, :] = lse

    def padded_block(group, q_src, bias_ref, blk, edge, dil, residue):
        r0 = pl.multiple_of(blk * DIL_QB, DIL_QB)
        o, lse = _dil_block(q_src[pl.ds(r0, DIL_QB), q_sl], kpad_ref[pl.ds(r0, DIL_KW), :],
                            vpad_ref[pl.ds(r0, DIL_KW), :], [bias_ref[edge, 0], bias_ref[edge, 1]],
                            head0_lanes)
        rows = pl.ds(residue + r0 * dil, DIL_QB, stride=dil) if dil > 1 else pl.ds(r0, DIL_QB)
        store(group, rows, o, lse)

    fill_padded(g0_ref.at[0], SEQ)
    n_blocks0 = SEQ // DIL_QB

    def g0_body(i, carry):
        for n in range(DIL_UNROLL):
            blk = i * DIL_UNROLL + n
            edge = jnp.where(blk == 0, EDGE_FIRST,
                             jnp.where(blk == n_blocks0 - 1, EDGE_LAST, EDGE_NONE))
            padded_block(0, g0_ref.at[0], b0_ref, blk, edge, 1, 0)
        return carry

    lax.fori_loop(0, n_blocks0 // DIL_UNROLL, g0_body, 0)

    dil1 = DIL_PAIRS[1][1]
    len1 = SEQ // dil1
    n_blocks1 = len1 // DIL_QB
    per_iter1 = DIL_UNROLL // n_blocks1
    slot1 = len1 + 2 * pad
    for n in range(per_iter1):
        for ref, zeros in ((kpad_ref, zeros_k), (vpad_ref, zeros_v)):
            ref[n * slot1:n * slot1 + pad, :] = zeros
            ref[n * slot1 + pad + len1:(n + 1) * slot1, :] = zeros
        vpad_ref[n * slot1 + pad:n * slot1 + pad + len1, LANES:] = jnp.ones((len1, LANES), BF16)

    def g1_body(i, carry):
        for n in range(per_iter1):
            r = i * per_iter1 + n
            src = g1_ref.at[0, r]
            base = n * slot1
            kpad_ref[base + pad:base + pad + len1, :] = src[:, k_sl]
            vpad_ref[base + pad:base + pad + len1, :LANES] = src[:, v_sl]
            for blk in range(n_blocks1):
                edge = EDGE_FIRST if blk == 0 else EDGE_LAST if blk == n_blocks1 - 1 else EDGE_NONE
                r0 = blk * DIL_QB
                o, lse = _dil_block(src[r0:r0 + DIL_QB, q_sl],
                                    kpad_ref[base + r0:base + r0 + DIL_KW, :],
                                    vpad_ref[base + r0:base + r0 + DIL_KW, :],
                                    [b1_ref[edge, 0], b1_ref[edge, 1]], head0_lanes)
                store(1, pl.ds(r + r0 * dil1, DIL_QB, stride=dil1), o, lse)
        return carry

    lax.fori_loop(0, dil1 // per_iter1, g1_body, 0)

    dil2 = DIL_PAIRS[2][1]
    assert SEQ // dil2 == DIL_QB
    for n in range(DIL_UNROLL):
        base = n * DIL_KW
        for ref, zeros in ((kpad_ref, zeros_k), (vpad_ref, zeros_v)):
            ref[base:base + pad, :] = zeros
            ref[base + pad + DIL_QB:base + DIL_KW, :] = zeros
        vpad_ref[base + pad:base + pad + DIL_QB, LANES:] = jnp.ones((DIL_QB, LANES), BF16)

    def g2_body(i, carry):
        for n in range(DIL_UNROLL):
            r = i * DIL_UNROLL + n
            src = g2_ref.at[0, r]
            base = n * DIL_KW
            kpad_ref[base + pad:base + pad + DIL_QB, :] = src[:, k_sl]
            vpad_ref[base + pad:base + pad + DIL_QB, :LANES] = src[:, v_sl]
            o, lse = _dil_block(src[:, q_sl], kpad_ref[base:base + DIL_KW, :],
                                vpad_ref[base:base + DIL_KW, :],
                                [b2_ref[EDGE_BOTH, 0], b2_ref[EDGE_BOTH, 1]], head0_lanes)
            store(2, pl.ds(r, DIL_QB, stride=dil2), o, lse)
        return carry

    lax.fori_loop(0, dil2 // DIL_UNROLL, g2_body, 0)

    chunk = 256
    for c in range(SEQ // chunk):
        rows = slice(c * chunk, (c + 1) * chunk)
        l0, l1, l2 = lse_ref[0, rows, :], lse_ref[1, rows, :], lse_ref[2, rows, :]
        mx = jnp.maximum(jnp.maximum(l0, l1), l2)
        e0, e1, e2 = jnp.exp2(l0 - mx), jnp.exp2(l1 - mx), jnp.exp2(l2 - mx)
        mixed = (e0 * o_ref[0, rows, :] + e1 * o_ref[1, rows, :] + e2 * o_ref[2, rows, :]) / (e0 + e1 + e2)
        out_ref[0, rows, :] = mixed.astype(out_ref.dtype)


def _dilated_mixture(pb0, pb1, pb2, biases):
    d1, d2 = DIL_PAIRS[1][1], DIL_PAIRS[2][1]
    bias_spec = _resident((4, 2, DIL_QB, DIL_KW))
    max_len = SEQ + 2 * DIL_RADIUS
    return pl.pallas_call(
        _dil_kernel,
        grid=(BATCH,),
        in_specs=[pl.BlockSpec((1, SEQ, DIL_QKV), lambda b: (b, 0, 0)),
                  pl.BlockSpec((1, d1, SEQ // d1, DIL_QKV), lambda b: (b, 0, 0, 0)),
                  pl.BlockSpec((1, d2, SEQ // d2, DIL_QKV), lambda b: (b, 0, 0, 0)),
                  bias_spec, bias_spec, bias_spec],
        out_specs=pl.BlockSpec((1, SEQ, LANES), lambda b: (b, 0, 0)),
        out_shape=jax.ShapeDtypeStruct((BATCH, SEQ, DIL_OUT_WIDTH), BF16),
        scratch_shapes=[pltpu.VMEM((max_len, LANES), BF16), pltpu.VMEM((max_len, 2 * LANES), BF16),
                        pltpu.VMEM((DIL_GROUPS, SEQ, LANES), F32),
                        pltpu.VMEM((DIL_GROUPS, SEQ, LANES), F32)],
        compiler_params=_params("parallel"),
        name="dilated_attn",
    )(pb0, pb1, pb2, *biases)


def _diff_kernel(q_ref, k_ref, v_ref, bias_ref, lq1_ref, lk1_ref, lq2_ref, lk2_ref, sg_ref,
                 o_ref, vaug_ref, logit_ref, rmax_ref, *, lam_init):
    s = pl.program_id(0)
    n_blocks = pl.num_programs(0) - 1
    nq = SEQ // DIFF_TQ
    cur = jnp.minimum(s, n_blocks - 1)
    prev = jnp.maximum(s - 1, 0)
    slot = s % 2

    @pl.when(s == 0)
    def _():
        logit_ref[1] = jnp.zeros(logit_ref.shape[1:], F32)
        rmax_ref[1] = jnp.zeros(rmax_ref.shape[1:], F32)

    @pl.when(prev % nq == 0)
    def _():
        vaug_ref[:, :LANES] = v_ref[0]
        vaug_ref[:, LANES:] = jnp.ones((SEQ, LANES), BF16)

    n_tiles = SEQ // DIFF_KT

    def key_tile(t):
        return slice(t * DIFF_KT, (t + 1) * DIFF_KT)

    off = pl.multiple_of((nq - 1 - cur % nq) * DIFF_TQ, DIFF_TQ)
    q = q_ref[0]
    first_half = lax.broadcasted_iota(jnp.int32, q.shape, 1) < DIFF_HEAD_DIM
    zero = jnp.zeros_like(q)
    outs = []
    for m in range(2):
        row_max = rmax_ref[1 - slot, m]
        pv = jnp.zeros((DIFF_TQ, 2 * LANES), F32)
        for t in range(n_tiles):
            old = logit_ref[1 - slot, m, :, key_tile(t)]
            e = jnp.concatenate([jnp.exp2(old[:, c * LANES:(c + 1) * LANES] - row_max)
                                 for c in range(DIFF_KT // LANES)], axis=1).astype(BF16)
            pv = pv + jnp.dot(e, vaug_ref[key_tile(t), :], preferred_element_type=F32)
        outs.append(pv[:, :LANES] / pv[:, LANES:])

    for m in range(2):
        q_m = jnp.where(first_half, q, zero) if m == 0 else jnp.where(first_half, zero, q)
        lane_max = None
        for t in range(n_tiles):
            bias = bias_ref[0, :, pl.ds(pl.multiple_of(off + t * DIFF_KT, LANES), DIFF_KT)]
            logits = lax.dot_general(q_m, k_ref[0, key_tile(t), :], (((1,), (1,)), ((), ())),
                                     preferred_element_type=F32) + bias
            logit_ref[slot, m, :, key_tile(t)] = logits
            for c in range(DIFF_KT // LANES):
                part = logits[:, c * LANES:(c + 1) * LANES]
                lane_max = part if lane_max is None else jnp.maximum(lane_max, part)
        rmax_ref[slot, m] = jnp.broadcast_to(jnp.max(lane_max, axis=-1, keepdims=True),
                                             (DIFF_TQ, LANES))

    lam = (jnp.exp(jnp.sum(lq1_ref[...] * lk1_ref[...], axis=-1, keepdims=True))
           - jnp.exp(jnp.sum(lq2_ref[...] * lk2_ref[...], axis=-1, keepdims=True))
           + lam_init)
    o = outs[0] - lam * outs[1]
    o = _rms(o, sg_ref[...], SUBLN_EPS) * (1.0 - lam_init)
    o_ref[0] = o.astype(o_ref.dtype)


def _diff_attention(pc, strip, lq1, lk1, lq2, lk2, subln_g, lam_init):
    pcv = pc.reshape(BATCH, SEQ, C_WIDTH)
    vec = _resident((1, DIFF_HEAD_DIM))
    nq = SEQ // DIFF_TQ
    n_blocks = DIFF_HEADS * BATCH * nq

    def block_of(step, lag):
        blk = jnp.clip(step - lag, 0, n_blocks - 1)
        return blk // (BATCH * nq), (blk // nq) % BATCH, blk % nq

    def q_map(s):
        h, b, j = block_of(s, 0)
        return b, j, h

    def k_map(s):
        h, b, _ = block_of(s, 0)
        return b, 0, DIFF_HEADS + h

    def v_map(s):
        h, b, _ = block_of(s, 1)
        return b, 0, 2 * DIFF_HEADS + h

    def out_map(s):
        h, b, j = block_of(s, 1)
        return b, j, h

    return pl.pallas_call(
        functools.partial(_diff_kernel, lam_init=lam_init),
        grid=(n_blocks + 1,),
        in_specs=[pl.BlockSpec((1, DIFF_TQ, LANES), q_map),
                  pl.BlockSpec((1, SEQ, LANES), k_map),
                  pl.BlockSpec((1, SEQ, LANES), v_map),
                  pl.BlockSpec((1, DIFF_TQ, DIFF_STRIP), lambda s: (block_of(s, 0)[0], 0, 0)),
                  vec, vec, vec, vec, _resident((1, DIFF_V_DIM))],
        out_specs=pl.BlockSpec((1, DIFF_TQ, LANES), out_map),
        out_shape=jax.ShapeDtypeStruct((BATCH, SEQ, DIFF_V_WIDTH), BF16),
        scratch_shapes=[pltpu.VMEM((SEQ, 2 * LANES), BF16),
                        pltpu.VMEM((2, 2, DIFF_TQ, SEQ), F32),
                        pltpu.VMEM((2, 2, DIFF_TQ, LANES), F32)],
        compiler_params=_params("arbitrary"),
        name="diff_attn",
    )(pcv, pcv, pcv, strip, lq1.reshape(1, -1), lk1.reshape(1, -1), lq2.reshape(1, -1),
      lk2.reshape(1, -1), subln_g.reshape(1, -1)).reshape(TOKENS, DIFF_V_WIDTH)


def _merge_kernel(x_ref, f_ref, b_ref, c_ref, gate_ref, wa_ref, wb_ref, wc_ref, wo_ref,
                  *rest, cast_plans):
    n_cast = len(cast_plans)
    cast_src, out_ref, cast_dst = rest[:n_cast], rest[n_cast], rest[n_cast + 1:]
    _cast_blocks(cast_src, cast_dst, cast_plans)
    y_a = jnp.dot(f_ref[...], wa_ref[...], preferred_element_type=F32)
    y_b = jnp.dot(b_ref[...], wb_ref[...], preferred_element_type=F32)
    y_c = jnp.dot(c_ref[...], wc_ref[...], preferred_element_type=F32)
    merged = (gate_ref[:, :D_MODEL].astype(F32) * y_a
              + gate_ref[:, D_MODEL:2 * D_MODEL].astype(F32) * y_b
              + gate_ref[:, 2 * D_MODEL:].astype(F32) * y_c)
    out_ref[...] = x_ref[...] + jnp.dot(merged.astype(BF16), wo_ref[...],
                                        preferred_element_type=F32)


def _merge(x, f, b, c, gates, wa, wb, wc, wo, cast_jobs=()):
    def tile(w):
        return pl.BlockSpec((MERGE_TM, w), lambda i: (i, 0))

    steps = TOKENS // MERGE_TM
    cast_in, cast_out, cast_shapes = _cast_specs(cast_jobs, steps)
    cast_srcs, cast_plans = _cast_args(cast_jobs)
    return pl.pallas_call(
        functools.partial(_merge_kernel, cast_plans=cast_plans),
        grid=(steps,),
        in_specs=[tile(D_MODEL), tile(FOURIER_WIDTH), tile(DIL_OUT_WIDTH), tile(DIFF_V_WIDTH),
                  tile(GATE_WIDTH), _resident(wa.shape), _resident(wb.shape),
                  _resident(wc.shape), _resident(wo.shape)] + cast_in,
        out_specs=[tile(D_MODEL)] + cast_out,
        out_shape=[jax.ShapeDtypeStruct((TOKENS, D_MODEL), F32)] + cast_shapes,
        compiler_params=_params("arbitrary"),
        name="gated_merge",
    )(x, f, b, c, gates, wa, wb, wc, wo, *cast_srcs)


def kernel(x, g_ffn1, w_ffn1_gate, w_ffn1_up, w_ffn1_down, g_mix, w_in, w_gate, b_gate,
           w_br_a, w_br_b, w_br_c, w_out, lam_q1, lam_k1, lam_q2, lam_k2, subln_g,
           rel_bias, g_ffn2, w_ffn2_gate, w_ffn2_up, w_ffn2_down, g_final):
    assert x.shape == (BATCH, SEQ, D_MODEL) and x.dtype == F32
    assert w_in.shape == (DEPTH, D_MODEL, IN_WIDTH)
    bf = lambda w: w.astype(BF16)

    dil_bias = [_dil_bias_tiles(rel_bias, g, dil) for g, (_, dil) in enumerate(DIL_PAIRS)]
    diff_strip = _diff_bias_strip(rel_bias)
    m_even, m_odd = _dft_matrices()
    d_chan = _channel_dft()
    in_plan = _in_weight_plan()

    h = x.reshape(TOKENS, D_MODEL)
    ffn1_w = [bf(w_ffn1_gate[0]), bf(w_ffn1_up[0]), bf(w_ffn1_down[0])]
    for l in range(DEPTH):
        last = l == DEPTH - 1
        h, wa, wc_br, wo, gate_w, in_w = _ffn(
            h, g_ffn1[l], *ffn1_w,
            cast_jobs=[(w_br_a, l, None), (w_br_c, l, None), (w_out, l, None),
                       (w_gate, l, None), (w_in, l, in_plan)])

        next_ffn1 = [] if last else [(w_ffn1_gate, l + 1, None), (w_ffn1_up, l + 1, None),
                                     (w_ffn1_down, l + 1, None)]
        pf, pb0, pb1, pb2, pc, gates, *ffn_w = _proj(
            h, g_mix[l], in_w, gate_w, b_gate[l],
            cast_jobs=[(w_ffn2_gate, l, None), (w_ffn2_up, l, None), (w_ffn2_down, l, None)]
            + next_ffn1)
        ffn2_w, ffn1_w = ffn_w[:3], ffn_w[3:]

        f = _fourier(pf, m_even, m_odd, d_chan).reshape(TOKENS, FOURIER_WIDTH)
        mixed = _dilated_mixture(pb0.reshape(BATCH, SEQ, DIL_QKV), pb1, pb2, dil_bias)
        lam_init = 0.8 - 0.6 * math.exp(-0.3 * l)
        c = _diff_attention(pc, diff_strip, lam_q1[l], lam_k1[l], lam_q2[l], lam_k2[l],
                            subln_g[l], lam_init)

        h, = _merge(h, f, mixed.reshape(TOKENS, DIL_OUT_WIDTH), c, gates, wa, bf(w_br_b[l]),
                    wc_br, wo)

        h, = _ffn(h, g_ffn2[l], *ffn2_w, g_final if last else None)
    return h.reshape(BATCH, SEQ, D_MODEL)
```

```python
import functools
import math

import numpy as np
import jax
import jax.numpy as jnp
from jax import lax
from jax.experimental import pallas as pl
from jax.experimental.pallas import tpu as pltpu

F32 = jnp.float32
BF16 = jnp.bfloat16

D_MODEL = 1024
BATCH = 8
SEQ = 2048
DEPTH = 2
TOKENS = BATCH * SEQ
D_FF = 2816
EPS = 1e-6
FOURIER_GROUPS = 8
FOURIER_GROUP_DIM = 128
FOURIER_WIDTH = FOURIER_GROUPS * FOURIER_GROUP_DIM
DIL_PAIRS = ((128, 1), (512, 4), (2048, 16))
DIL_GROUPS = len(DIL_PAIRS)
DIL_HEAD_DIM = 64
DIL_HEADS = 2 * DIL_GROUPS
DIL_WIDTH = DIL_HEADS * DIL_HEAD_DIM
DIL_OUT_WIDTH = 2 * DIL_HEAD_DIM
DIL_RADIUS = 64
DIFF_HEADS = 4
DIFF_HEAD_DIM = 64
DIFF_QK_WIDTH = DIFF_HEADS * 2 * DIFF_HEAD_DIM
DIFF_V_DIM = 2 * DIFF_HEAD_DIM
DIFF_V_WIDTH = DIFF_HEADS * DIFF_V_DIM
SUBLN_EPS = 1e-5
NUM_BUCKETS = 32
MAX_DISTANCE = 1024
N_BIAS_HEADS = DIL_HEADS + DIFF_HEADS
NEG_INF = -1e30
QK_SCALE = DIL_HEAD_DIM ** -0.5
LOG2_E = math.log2(math.e)
B_WIDTH = 3 * DIL_WIDTH
C_WIDTH = 2 * DIFF_QK_WIDTH + DIFF_V_WIDTH
IN_WIDTH = FOURIER_WIDTH + B_WIDTH + C_WIDTH
GATE_WIDTH = 3 * D_MODEL

LANES = 128
BF16_SUBLANES = 16
VMEM_BYTES_V7X = 64 * 1024 * 1024
VMEM_LIMIT = 56 * 1024 * 1024

FFN_TM = 1024
FF_CHUNK = 256
PROJ_TM = 512
PROJ_CHUNK = 512
MERGE_TM = 512
FOURIER_RADIX = 4
QUARTER_SEQ = SEQ // FOURIER_RADIX
FOURIER_CN = 512
DIL_QKV = 3 * LANES
DIL_QB = 128
DIL_KW = DIL_QB + 2 * DIL_RADIUS
DIL_UNROLL = 8
DIFF_TQ = 512
DIFF_STRIP = 2 * SEQ - DIFF_TQ
DIFF_RING = 2 * SEQ
DIFF_KT = 512


def _params(*semantics):
    return pltpu.CompilerParams(dimension_semantics=semantics, vmem_limit_bytes=VMEM_LIMIT)


def _resident(shape):
    nd = len(shape)
    return pl.BlockSpec(shape, lambda *_: (0,) * nd, pipeline_mode=pl.Buffered(1))


def _rms(x, g, eps):
    return x * lax.rsqrt(jnp.mean(x * x, axis=-1, keepdims=True) + eps) * g


def _cast_specs(jobs, steps):
    in_specs, out_specs, out_shapes = [], [], []
    for src, layer, plan in jobs:
        _, rows, cols = src.shape
        out_cols = cols if plan is None else sum(width for _, width, _ in plan)
        share = 1
        while rows % (steps // share * BF16_SUBLANES):
            share *= 2
        blk = rows // (steps // share)
        in_specs.append(pl.BlockSpec((None, blk, cols),
                                     lambda i, layer=layer, share=share: (layer, i // share, 0)))
        out_specs.append(pl.BlockSpec((blk, out_cols), lambda i, share=share: (i // share, 0)))
        out_shapes.append(jax.ShapeDtypeStruct((rows, out_cols), BF16))
    return in_specs, out_specs, out_shapes


def _cast_blocks(src_refs, dst_refs, plans):
    for src, dst, plan in zip(src_refs, dst_refs, plans):
        if plan is None:
            dst[...] = src[...].astype(BF16)
            continue
        col = 0
        for src_col, width, scale in plan:
            part = src[:, src_col:src_col + width]
            dst[:, col:col + width] = (part if scale == 1.0 else part * scale).astype(BF16)
            col += width


def _cast_args(jobs):
    return [src for src, _, _ in jobs], tuple(plan for _, _, plan in jobs)


def _bucket_breakpoints():
    rel = np.arange(-(SEQ - 1), SEQ)
    half = NUM_BUCKETS // 2
    max_exact = half // 2
    n = np.abs(rel)
    nf = np.maximum(n, 1).astype(np.float64)
    large = max_exact + (np.log(nf / max_exact) / math.log(MAX_DISTANCE / max_exact)
                         * (half - max_exact)).astype(np.int32)
    large = np.minimum(large, half - 1)
    b = np.where(rel > 0, half, 0) + np.where(n < max_exact, n, large)
    pts = [(int(rel[i]), int(b[i])) for i in range(1, len(rel)) if b[i] != b[i - 1]]
    return int(b[0]), pts


def _bias_values(tbl_ref, head, rel):
    first, pts = _bucket_breakpoints()
    val = jnp.full(rel.shape, tbl_ref[first, head], F32)
    for thr, bucket in pts:
        val = jnp.where(rel >= thr, tbl_ref[bucket, head], val)
    return val * LOG2_E


def _diff_bias_kernel(tbl_ref, o_ref):
    rows, cols = o_ref.shape[1], o_ref.shape[2]
    ring = DIFF_RING
    u = lax.broadcasted_iota(jnp.int32, (8, ring), 1)
    offset = jnp.where(u < cols, u, u - ring)
    line = _bias_values(tbl_ref, DIL_HEADS + pl.program_id(0), offset - (SEQ - rows))
    tiled = jnp.broadcast_to(line[:1], (rows, ring))
    o_ref[0] = pltpu.roll(tiled, 0, 1, stride=1, stride_axis=0)[:, :cols]


def _diff_bias_strip(rel_bias):
    return pl.pallas_call(
        _diff_bias_kernel,
        grid=(DIFF_HEADS,),
        in_specs=[pl.BlockSpec(memory_space=pltpu.SMEM)],
        out_specs=pl.BlockSpec((1, DIFF_TQ, DIFF_STRIP), lambda h: (h, 0, 0)),
        out_shape=jax.ShapeDtypeStruct((DIFF_HEADS, DIFF_TQ, DIFF_STRIP), F32),
        compiler_params=_params("arbitrary"),
        name="diff_bias_strip",
    )(rel_bias)


EDGE_FIRST, EDGE_NONE, EDGE_LAST, EDGE_BOTH = range(4)


def _dil_bias_kernel(tbl_ref, o_ref):
    head = pl.program_id(0)
    group = head // 2
    dil = DIL_PAIRS[0][1]
    for g in range(1, DIL_GROUPS):
        dil = jnp.where(group == g, DIL_PAIRS[g][1], dil)
    rows, cols = o_ref.shape[3], o_ref.shape[4]
    col = lax.broadcasted_iota(jnp.int32, (rows, cols), 1)
    rel_t = col - lax.broadcasted_iota(jnp.int32, (rows, cols), 0) - DIL_RADIUS
    val = _bias_values(tbl_ref, head, rel_t * dil)
    val = jnp.where(jnp.abs(rel_t) <= DIL_RADIUS, val, NEG_INF)
    lo = jnp.where(col >= DIL_RADIUS, val, NEG_INF)
    o_ref[0, EDGE_FIRST, 0] = lo
    o_ref[0, EDGE_NONE, 0] = val
    o_ref[0, EDGE_LAST, 0] = jnp.where(col < cols - DIL_RADIUS, val, NEG_INF)
    o_ref[0, EDGE_BOTH, 0] = jnp.where(col < cols - DIL_RADIUS, lo, NEG_INF)


def _dil_bias_tiles(rel_bias):
    return pl.pallas_call(
        _dil_bias_kernel,
        grid=(DIL_HEADS,),
        in_specs=[pl.BlockSpec(memory_space=pltpu.SMEM)],
        out_specs=pl.BlockSpec((1, 4, 1, DIL_QB, DIL_KW), lambda h: (h // 2, 0, h % 2, 0, 0)),
        out_shape=jax.ShapeDtypeStruct((DIL_GROUPS, 4, 2, DIL_QB, DIL_KW), F32),
        compiler_params=_params("arbitrary"),
        name="dil_bias_tiles",
    )(rel_bias)


def _swiglu_residual(x, g_ref, wg_ref, wu_ref, wd_ref, a_ref):
    h = _rms(x, g_ref[...], EPS).astype(BF16)
    for c in range(D_FF // FF_CHUNK):
        sl = slice(c * FF_CHUNK, (c + 1) * FF_CHUNK)
        gate = jnp.dot(h, wg_ref[:, sl], preferred_element_type=F32)
        up = jnp.dot(h, wu_ref[:, sl], preferred_element_type=F32)
        a_ref[:, sl] = (gate * jax.nn.sigmoid(gate) * up).astype(BF16)
    return x + 0.5 * jnp.dot(a_ref[...], wd_ref[...], preferred_element_type=F32)


def _ffn_kernel(x_ref, g_ref, wg_ref, wu_ref, wd_ref, *rest, cast_plans):
    n_cast = len(cast_plans)
    cast_src, o_ref, cast_dst, a_ref = (rest[:n_cast], rest[n_cast], rest[n_cast + 1:-1], rest[-1])
    _cast_blocks(cast_src, cast_dst, cast_plans)
    o_ref[...] = _swiglu_residual(x_ref[...], g_ref, wg_ref, wu_ref, wd_ref, a_ref)


def _ffn(x, g, wg, wu, wd, cast_jobs=()):
    steps = TOKENS // FFN_TM
    tile = pl.BlockSpec((FFN_TM, D_MODEL), lambda i: (i, 0))
    cast_in, cast_out, cast_shapes = _cast_specs(cast_jobs, steps)
    cast_srcs, cast_plans = _cast_args(cast_jobs)
    return pl.pallas_call(
        functools.partial(_ffn_kernel, cast_plans=cast_plans),
        grid=(steps,),
        in_specs=[tile, _resident((1, D_MODEL)), _resident((D_MODEL, D_FF)),
                  _resident((D_MODEL, D_FF)), _resident((D_FF, D_MODEL))] + cast_in,
        out_specs=[tile] + cast_out,
        out_shape=[jax.ShapeDtypeStruct((TOKENS, D_MODEL), F32)] + cast_shapes,
        scratch_shapes=[pltpu.VMEM((FFN_TM, D_FF), BF16)],
        compiler_params=_params("arbitrary"),
        name="ffn",
    )(x, g.reshape(1, D_MODEL), wg, wu, wd, *cast_srcs)


def _proj_kernel(x_ref, g_ref, w_ref, wg_ref, bg_ref, *rest, cast_plans):
    n_cast = len(cast_plans)
    cast_src, rest = rest[:n_cast], rest[n_cast:]
    pf_ref, pb0_ref, pb1_ref, pb2_ref, pc_ref, gate_ref = rest[:6]
    cast_dst, y_ref = rest[6:-1], rest[-1]
    _cast_blocks(cast_src, cast_dst, cast_plans)
    u = _rms(x_ref[...], g_ref[...], EPS).astype(BF16)
    rows = x_ref.shape[0]

    def chunks(width):
        return [slice(c0, min(c0 + PROJ_CHUNK, width)) for c0 in range(0, width, PROJ_CHUNK)]

    def project(w, col0, sl):
        return jnp.dot(u, w[:, col0 + sl.start:col0 + sl.stop], preferred_element_type=F32)

    def by_residue(y, o_ref, dil, col0):
        for j in range(y.shape[1] // LANES):
            y_ref[j] = y[:, j * LANES:(j + 1) * LANES]
            cols = slice(col0 + j * LANES, col0 + (j + 1) * LANES)
            for r in range(dil):
                o_ref[0, r, :, cols] = y_ref[j, pl.ds(r, rows // dil, stride=dil), :].astype(o_ref.dtype)

    for sl in chunks(GATE_WIDTH):
        gate_ref[:, sl] = jax.nn.sigmoid(project(wg_ref, 0, sl) + bg_ref[:, sl]).astype(gate_ref.dtype)
    for sl in chunks(FOURIER_WIDTH):
        by_residue(project(w_ref, 0, sl), pf_ref, FOURIER_RADIX, sl.start)
    y = project(w_ref, FOURIER_WIDTH, slice(0, B_WIDTH))
    pb0_ref[...] = y[:, :DIL_QKV].astype(pb0_ref.dtype)
    by_residue(y[:, DIL_QKV:2 * DIL_QKV], pb1_ref, DIL_PAIRS[1][1], 0)
    by_residue(y[:, 2 * DIL_QKV:], pb2_ref, DIL_PAIRS[2][1], 0)
    for sl in chunks(C_WIDTH):
        pc_ref[:, sl] = project(w_ref, FOURIER_WIDTH + B_WIDTH, sl).astype(pc_ref.dtype)


def _in_weight_plan():
    q_scale = QK_SCALE * LOG2_E
    plan = [(0, FOURIER_WIDTH, 1.0)]
    for g in range(DIL_GROUPS):
        lo = FOURIER_WIDTH + g * LANES
        plan += [(lo, LANES, q_scale), (lo + DIL_WIDTH, LANES, 1.0), (lo + 2 * DIL_WIDTH, LANES, 1.0)]
    c3 = FOURIER_WIDTH + B_WIDTH
    plan += [(c3, DIFF_QK_WIDTH, q_scale), (c3 + DIFF_QK_WIDTH, DIFF_QK_WIDTH + DIFF_V_WIDTH, 1.0)]
    return plan


def _proj(x, g, w, wg, bg, cast_jobs=()):
    tiles_per_seq = SEQ // PROJ_TM
    steps = TOKENS // PROJ_TM
    cast_in, cast_out, cast_shapes = _cast_specs(cast_jobs, steps)
    cast_srcs, cast_plans = _cast_args(cast_jobs)

    def tile(w):
        return pl.BlockSpec((PROJ_TM, w), lambda i: (i, 0))

    def residue_tile(dil, w):
        return pl.BlockSpec((1, dil, PROJ_TM // dil, w),
                            lambda i: (i // tiles_per_seq, 0, i % tiles_per_seq, 0))

    def residue_shape(dil, w):
        return jax.ShapeDtypeStruct((BATCH, dil, SEQ // dil, w), BF16)

    d1, d2 = DIL_PAIRS[1][1], DIL_PAIRS[2][1]
    return pl.pallas_call(
        functools.partial(_proj_kernel, cast_plans=cast_plans),
        grid=(steps,),
        in_specs=[tile(D_MODEL), _resident((1, D_MODEL)), _resident((D_MODEL, IN_WIDTH)),
                  _resident((D_MODEL, GATE_WIDTH)), _resident((1, GATE_WIDTH))] + cast_in,
        out_specs=[residue_tile(FOURIER_RADIX, FOURIER_WIDTH), tile(DIL_QKV),
                   residue_tile(d1, DIL_QKV),
                   residue_tile(d2, DIL_QKV), tile(C_WIDTH), tile(GATE_WIDTH)] + cast_out,
        out_shape=[residue_shape(FOURIER_RADIX, FOURIER_WIDTH),
                   jax.ShapeDtypeStruct((TOKENS, DIL_QKV), BF16),
                   residue_shape(d1, DIL_QKV), residue_shape(d2, DIL_QKV),
                   jax.ShapeDtypeStruct((TOKENS, C_WIDTH), BF16),
                   jax.ShapeDtypeStruct((TOKENS, GATE_WIDTH), BF16)] + cast_shapes,
        scratch_shapes=[pltpu.VMEM((max(PROJ_CHUNK, DIL_QKV) // LANES, PROJ_TM, LANES), F32)],
        compiler_params=_params("arbitrary"),
        name="proj_in",
    )(x, g.reshape(1, D_MODEL), w, wg, bg.reshape(1, GATE_WIDTH), *cast_srcs)


def _fourier_kernel(a_ref, m_ref, d_ref, o_ref):
    q = QUARTER_SEQ
    tc, ts = [], []
    for r in range(FOURIER_RADIX):
        t = jnp.dot(m_ref[r], a_ref[0, r], preferred_element_type=F32)
        tc.append(t[:q])
        ts.append(t[q:])
    cos_sums = (tc[0] + tc[1] + tc[2] + tc[3],
                tc[0] - ts[1] - tc[2] + ts[3],
                tc[0] - tc[1] + tc[2] - tc[3],
                tc[0] + ts[1] - tc[2] - ts[3])
    sin_sums = (ts[0] + ts[1] + ts[2] + ts[3],
                ts[0] + tc[1] - ts[2] - tc[3],
                ts[0] - ts[1] + ts[2] - ts[3],
                ts[0] - tc[1] - ts[2] + tc[3])
    for j in range(FOURIER_RADIX):
        p, s = cos_sums[j].astype(BF16), sin_sums[j].astype(BF16)
        for g in range(o_ref.shape[2] // LANES):
            cols = slice(g * LANES, (g + 1) * LANES)
            lhs = jnp.concatenate([p[:, cols], s[:, cols]], axis=1)
            o_ref[0, j * q:(j + 1) * q, cols] = jnp.dot(
                lhs, d_ref[...], preferred_element_type=F32).astype(o_ref.dtype)


def _fourier(pf, m_seq, d_chan):
    cn = FOURIER_CN
    return pl.pallas_call(
        _fourier_kernel,
        grid=(BATCH, FOURIER_WIDTH // cn),
        in_specs=[pl.BlockSpec((1, FOURIER_RADIX, QUARTER_SEQ, cn), lambda b, j: (b, 0, 0, j)),
                  _resident((FOURIER_RADIX, 2 * QUARTER_SEQ, QUARTER_SEQ)),
                  _resident((2 * FOURIER_GROUP_DIM, FOURIER_GROUP_DIM))],
        out_specs=pl.BlockSpec((1, SEQ, cn), lambda b, j: (b, 0, j)),
        out_shape=jax.ShapeDtypeStruct((BATCH, SEQ, FOURIER_WIDTH), BF16),
        compiler_params=_params("parallel", "parallel"),
        name="fourier_mix",
    )(pf, m_seq, d_chan)


def _dft_matrices():
    q = QUARTER_SEQ
    split = 32

    def table(mult, width):
        k = lax.broadcasted_iota(jnp.int32, (q, width), 0)
        j = lax.broadcasted_iota(jnp.int32, (q, width), 1)
        ang = ((k * j * mult) % q).astype(F32) * (2.0 * math.pi / q)
        return jnp.cos(ang), jnp.sin(ang)

    (ca, sa), (cb, sb) = table(split, q // split), table(1, split)
    cos_0 = (ca[:, :, None] * cb[:, None, :] - sa[:, :, None] * sb[:, None, :]).reshape(q, q)
    sin_0 = (sa[:, :, None] * cb[:, None, :] + ca[:, :, None] * sb[:, None, :]).reshape(q, q)
    mats = []
    for r in range(FOURIER_RADIX):
        phi = lax.broadcasted_iota(jnp.int32, (q, 1), 0).astype(F32) * (2.0 * math.pi * r / SEQ)
        cos_r = cos_0 * jnp.cos(phi) - sin_0 * jnp.sin(phi)
        sin_r = sin_0 * jnp.cos(phi) + cos_0 * jnp.sin(phi)
        mats.append(jnp.concatenate([cos_r, sin_r], axis=0))
    return jnp.stack(mats).astype(BF16)


def _channel_dft():
    gd = FOURIER_GROUP_DIM
    idx = np.arange(gd)
    ang = 2.0 * np.pi * ((idx[:, None] * idx[None, :]) % gd) / gd
    scale = 1.0 / math.sqrt(SEQ * gd)
    return jnp.asarray(np.concatenate([np.cos(ang), -np.sin(ang)]) * scale, F32).astype(BF16)


def _dil_block(q, kw, vw_aug, bias, head0_lanes):
    zero = jnp.zeros_like(q)
    pv, lse = [], []
    for hh in range(2):
        q_h = jnp.where(head0_lanes, q, zero) if hh == 0 else jnp.where(head0_lanes, zero, q)
        logits = lax.dot_general(q_h, kw, (((1,), (1,)), ((), ())),
                                 preferred_element_type=F32) + bias[hh]
        mx = jnp.max(logits, axis=-1, keepdims=True)
        e = jnp.exp2(logits - mx).astype(BF16)
        r = jnp.dot(e, vw_aug, preferred_element_type=F32)
        s = r[:, LANES:]
        pv.append(r[:, :LANES] / s)
        lse.append(mx + jnp.log2(s))
    return jnp.where(head0_lanes, pv[0], pv[1]), jnp.where(head0_lanes, lse[0], lse[1])


def _dil_kernel(g0_ref, g1_ref, g2_ref, bias_ref, out_ref,
                kpad_ref, vpad_ref, o_ref, lse_ref):
    pad = DIL_RADIUS
    head0_lanes = lax.broadcasted_iota(jnp.int32, (DIL_QB, LANES), 1) < DIL_HEAD_DIM
    zeros_k = jnp.zeros((pad, LANES), BF16)
    zeros_v = jnp.zeros((pad, 2 * LANES), BF16)
    q_sl, k_sl, v_sl = (slice(i * LANES, (i + 1) * LANES) for i in range(3))

    vpad_ref[pl.ds(pad, SEQ), LANES:] = jnp.ones((SEQ, LANES), BF16)

    def fill_padded(src, sub_len):
        kpad_ref[:pad, :] = zeros_k
        kpad_ref[pl.ds(pad, sub_len), :] = src[:, k_sl]
        kpad_ref[pl.ds(pad + sub_len, pad), :] = zeros_k
        vpad_ref[:pad, :] = zeros_v
        vpad_ref[pl.ds(pad, sub_len), :LANES] = src[:, v_sl]
        vpad_ref[pl.ds(pad + sub_len, pad), :] = zeros_v

    def store(group, rows, o, lse):
        o_ref[group, rows, :] = o
        lse_ref[group, rows, :] = lse

    def padded_block(group, q_src, blk, edge, dil, residue):
        r0 = pl.multiple_of(blk * DIL_QB, DIL_QB)
        o, lse = _dil_block(q_src[pl.ds(r0, DIL_QB), q_sl], kpad_ref[pl.ds(r0, DIL_KW), :],
                            vpad_ref[pl.ds(r0, DIL_KW), :],
                            [bias_ref[group, edge, 0], bias_ref[group, edge, 1]],
                            head0_lanes)
        rows = pl.ds(residue + r0 * dil, DIL_QB, stride=dil) if dil > 1 else pl.ds(r0, DIL_QB)
        store(group, rows, o, lse)

    fill_padded(g0_ref.at[0], SEQ)
    n_blocks0 = SEQ // DIL_QB

    def g0_body(i, carry):
        for n in range(DIL_UNROLL):
            blk = i * DIL_UNROLL + n
            edge = jnp.where(blk == 0, EDGE_FIRST,
                             jnp.where(blk == n_blocks0 - 1, EDGE_LAST, EDGE_NONE))
            padded_block(0, g0_ref.at[0], blk, edge, 1, 0)
        return carry

    lax.fori_loop(0, n_blocks0 // DIL_UNROLL, g0_body, 0)

    dil1 = DIL_PAIRS[1][1]
    len1 = SEQ // dil1
    n_blocks1 = len1 // DIL_QB
    per_iter1 = DIL_UNROLL // n_blocks1
    slot1 = len1 + 2 * pad
    for n in range(per_iter1):
        for ref, zeros in ((kpad_ref, zeros_k), (vpad_ref, zeros_v)):
            ref[n * slot1:n * slot1 + pad, :] = zeros
            ref[n * slot1 + pad + len1:(n + 1) * slot1, :] = zeros
        vpad_ref[n * slot1 + pad:n * slot1 + pad + len1, LANES:] = jnp.ones((len1, LANES), BF16)

    def g1_body(i, carry):
        for n in range(per_iter1):
            r = i * per_iter1 + n
            src = g1_ref.at[0, r]
            base = n * slot1
            kpad_ref[base + pad:base + pad + len1, :] = src[:, k_sl]
            vpad_ref[base + pad:base + pad + len1, :LANES] = src[:, v_sl]
            for blk in range(n_blocks1):
                edge = EDGE_FIRST if blk == 0 else EDGE_LAST if blk == n_blocks1 - 1 else EDGE_NONE
                r0 = blk * DIL_QB
                o, lse = _dil_block(src[r0:r0 + DIL_QB, q_sl],
                                    kpad_ref[base + r0:base + r0 + DIL_KW, :],
                                    vpad_ref[base + r0:base + r0 + DIL_KW, :],
                                    [bias_ref[1, edge, 0], bias_ref[1, edge, 1]], head0_lanes)
                store(1, pl.ds(r + r0 * dil1, DIL_QB, stride=dil1), o, lse)
        return carry

    lax.fori_loop(0, dil1 // per_iter1, g1_body, 0)

    dil2 = DIL_PAIRS[2][1]
    assert SEQ // dil2 == DIL_QB
    for n in range(DIL_UNROLL):
        base = n * DIL_KW
        for ref, zeros in ((kpad_ref, zeros_k), (vpad_ref, zeros_v)):
            ref[base:base + pad, :] = zeros
            ref[base + pad + DIL_QB:base + DIL_KW, :] = zeros
        vpad_ref[base + pad:base + pad + DIL_QB, LANES:] = jnp.ones((DIL_QB, LANES), BF16)

    def g2_body(i, carry):
        for n in range(DIL_UNROLL):
            r = i * DIL_UNROLL + n
            src = g2_ref.at[0, r]
            base = n * DIL_KW
            kpad_ref[base + pad:base + pad + DIL_QB, :] = src[:, k_sl]
            vpad_ref[base + pad:base + pad + DIL_QB, :LANES] = src[:, v_sl]
            o, lse = _dil_block(src[:, q_sl], kpad_ref[base:base + DIL_KW, :],
                                vpad_ref[base:base + DIL_KW, :],
                                [bias_ref[2, EDGE_BOTH, 0], bias_ref[2, EDGE_BOTH, 1]],
                                head0_lanes)
            store(2, pl.ds(r, DIL_QB, stride=dil2), o, lse)
        return carry

    lax.fori_loop(0, dil2 // DIL_UNROLL, g2_body, 0)

    chunk = 256
    for c in range(SEQ // chunk):
        rows = slice(c * chunk, (c + 1) * chunk)
        l0, l1, l2 = lse_ref[0, rows, :], lse_ref[1, rows, :], lse_ref[2, rows, :]
        mx = jnp.maximum(jnp.maximum(l0, l1), l2)
        e0, e1, e2 = jnp.exp2(l0 - mx), jnp.exp2(l1 - mx), jnp.exp2(l2 - mx)
        mixed = (e0 * o_ref[0, rows, :] + e1 * o_ref[1, rows, :] + e2 * o_ref[2, rows, :]) / (e0 + e1 + e2)
        out_ref[0, rows, :] = mixed.astype(out_ref.dtype)


def _dilated_mixture(pb0, pb1, pb2, biases):
    d1, d2 = DIL_PAIRS[1][1], DIL_PAIRS[2][1]
    bias_spec = _resident((DIL_GROUPS, 4, 2, DIL_QB, DIL_KW))
    max_len = SEQ + 2 * DIL_RADIUS
    return pl.pallas_call(
        _dil_kernel,
        grid=(BATCH,),
        in_specs=[pl.BlockSpec((1, SEQ, DIL_QKV), lambda b: (b, 0, 0)),
                  pl.BlockSpec((1, d1, SEQ // d1, DIL_QKV), lambda b: (b, 0, 0, 0)),
                  pl.BlockSpec((1, d2, SEQ // d2, DIL_QKV), lambda b: (b, 0, 0, 0)),
                  bias_spec],
        out_specs=pl.BlockSpec((1, SEQ, LANES), lambda b: (b, 0, 0)),
        out_shape=jax.ShapeDtypeStruct((BATCH, SEQ, DIL_OUT_WIDTH), BF16),
        scratch_shapes=[pltpu.VMEM((max_len, LANES), BF16), pltpu.VMEM((max_len, 2 * LANES), BF16),
                        pltpu.VMEM((DIL_GROUPS, SEQ, LANES), F32),
                        pltpu.VMEM((DIL_GROUPS, SEQ, LANES), F32)],
        compiler_params=_params("parallel"),
        name="dilated_attn",
    )(pb0, pb1, pb2, biases)


def _diff_kernel(q_ref, k_ref, v_ref, bias_ref, lq1_ref, lk1_ref, lq2_ref, lk2_ref, sg_ref,
                 o_ref, vaug_ref, logit_ref, rmax_ref, *, lam_init):
    s = pl.program_id(0)
    n_blocks = pl.num_programs(0) - 1
    nq = SEQ // DIFF_TQ
    cur = jnp.minimum(s, n_blocks - 1)
    prev = jnp.maximum(s - 1, 0)
    slot = s % 2

    @pl.when(s == 0)
    def _():
        logit_ref[1] = jnp.zeros(logit_ref.shape[1:], F32)
        rmax_ref[1] = jnp.zeros(rmax_ref.shape[1:], F32)

    @pl.when(prev % nq == 0)
    def _():
        vaug_ref[:, :LANES] = v_ref[0]
        vaug_ref[:, LANES:] = jnp.ones((SEQ, LANES), BF16)

    n_tiles = SEQ // DIFF_KT

    def key_tile(t):
        return slice(t * DIFF_KT, (t + 1) * DIFF_KT)

    off = pl.multiple_of((nq - 1 - cur % nq) * DIFF_TQ, DIFF_TQ)
    q = q_ref[0]
    first_half = lax.broadcasted_iota(jnp.int32, q.shape, 1) < DIFF_HEAD_DIM
    zero = jnp.zeros_like(q)
    outs = []
    for m in range(2):
        row_max = rmax_ref[1 - slot, m]
        pv = jnp.zeros((DIFF_TQ, 2 * LANES), F32)
        for t in range(n_tiles):
            old = logit_ref[1 - slot, m, :, key_tile(t)]
            e = jnp.concatenate([jnp.exp2(old[:, c * LANES:(c + 1) * LANES] - row_max)
                                 for c in range(DIFF_KT // LANES)], axis=1).astype(BF16)
            pv = pv + jnp.dot(e, vaug_ref[key_tile(t), :], preferred_element_type=F32)
        outs.append(pv[:, :LANES] / pv[:, LANES:])

    for m in range(2):
        q_m = jnp.where(first_half, q, zero) if m == 0 else jnp.where(first_half, zero, q)
        lane_max = None
        for t in range(n_tiles):
            bias = bias_ref[0, :, pl.ds(pl.multiple_of(off + t * DIFF_KT, LANES), DIFF_KT)]
            logits = lax.dot_general(q_m, k_ref[0, key_tile(t), :], (((1,), (1,)), ((), ())),
                                     preferred_element_type=F32) + bias
            logit_ref[slot, m, :, key_tile(t)] = logits
            for c in range(DIFF_KT // LANES):
                part = logits[:, c * LANES:(c + 1) * LANES]
                lane_max = part if lane_max is None else jnp.maximum(lane_max, part)
        rmax_ref[slot, m] = jnp.broadcast_to(jnp.max(lane_max, axis=-1, keepdims=True),
                                             (DIFF_TQ, LANES))

    lam = (jnp.exp(jnp.sum(lq1_ref[...] * lk1_ref[...], axis=-1, keepdims=True))
           - jnp.exp(jnp.sum(lq2_ref[...] * lk2_ref[...], axis=-1, keepdims=True))
           + lam_init)
    o = outs[0] - lam * outs[1]
    o = _rms(o, sg_ref[...], SUBLN_EPS) * (1.0 - lam_init)
    o_ref[0] = o.astype(o_ref.dtype)


def _diff_attention(pc, strip, lq1, lk1, lq2, lk2, subln_g, lam_init):
    pcv = pc.reshape(BATCH, SEQ, C_WIDTH)
    vec = _resident((1, DIFF_HEAD_DIM))
    nq = SEQ // DIFF_TQ
    n_blocks = DIFF_HEADS * BATCH * nq

    def block_of(step, lag):
        blk = jnp.clip(step - lag, 0, n_blocks - 1)
        return blk // (BATCH * nq), (blk // nq) % BATCH, blk % nq

    def q_map(s):
        h, b, j = block_of(s, 0)
        return b, j, h

    def k_map(s):
        h, b, _ = block_of(s, 0)
        return b, 0, DIFF_HEADS + h

    def v_map(s):
        h, b, _ = block_of(s, 1)
        return b, 0, 2 * DIFF_HEADS + h

    def out_map(s):
        h, b, j = block_of(s, 1)
        return b, j, h

    return pl.pallas_call(
        functools.partial(_diff_kernel, lam_init=lam_init),
        grid=(n_blocks + 1,),
        in_specs=[pl.BlockSpec((1, DIFF_TQ, LANES), q_map),
                  pl.BlockSpec((1, SEQ, LANES), k_map),
                  pl.BlockSpec((1, SEQ, LANES), v_map),
                  pl.BlockSpec((1, DIFF_TQ, DIFF_STRIP), lambda s: (block_of(s, 0)[0], 0, 0)),
                  vec, vec, vec, vec, _resident((1, DIFF_V_DIM))],
        out_specs=pl.BlockSpec((1, DIFF_TQ, LANES), out_map),
        out_shape=jax.ShapeDtypeStruct((BATCH, SEQ, DIFF_V_WIDTH), BF16),
        scratch_shapes=[pltpu.VMEM((SEQ, 2 * LANES), BF16),
                        pltpu.VMEM((2, 2, DIFF_TQ, SEQ), F32),
                        pltpu.VMEM((2, 2, DIFF_TQ, LANES), F32)],
        compiler_params=_params("arbitrary"),
        name="diff_attn",
    )(pcv, pcv, pcv, strip, lq1.reshape(1, -1), lk1.reshape(1, -1), lq2.reshape(1, -1),
      lk2.reshape(1, -1), subln_g.reshape(1, -1)).reshape(TOKENS, DIFF_V_WIDTH)


def _merge_ffn_kernel(x_ref, f_ref, b_ref, c_ref, gate_ref, wa_ref, wb_ref, wc_ref, wo_ref,
                      g_ref, wg_ref, wu_ref, wd_ref, *rest, final_norm):
    rest = list(rest)
    gf_ref = rest.pop(0) if final_norm else None
    out_ref, a_ref = rest
    y_a = jnp.dot(f_ref[...], wa_ref[...], preferred_element_type=F32)
    y_b = jnp.dot(b_ref[...], wb_ref[...], preferred_element_type=F32)
    y_c = jnp.dot(c_ref[...], wc_ref[...], preferred_element_type=F32)
    merged = (gate_ref[:, :D_MODEL].astype(F32) * y_a
              + gate_ref[:, D_MODEL:2 * D_MODEL].astype(F32) * y_b
              + gate_ref[:, 2 * D_MODEL:].astype(F32) * y_c)
    x = x_ref[...] + jnp.dot(merged.astype(BF16), wo_ref[...], preferred_element_type=F32)
    y = _swiglu_residual(x, g_ref, wg_ref, wu_ref, wd_ref, a_ref)
    if final_norm:
        y = _rms(y, gf_ref[...], EPS)
    out_ref[...] = y


def _merge_ffn(x, f, b, c, gates, wa, wb, wc, wo, g, wg, wu, wd, g_final=None):
    def tile(w):
        return pl.BlockSpec((MERGE_TM, w), lambda i: (i, 0))

    final_norm = g_final is not None
    weights = [wa, wb, wc, wo]
    gain = _resident((1, D_MODEL))
    in_specs = ([tile(D_MODEL), tile(FOURIER_WIDTH), tile(DIL_OUT_WIDTH), tile(DIFF_V_WIDTH),
                 tile(GATE_WIDTH)] + [_resident(w.shape) for w in weights]
                + [gain, _resident(wg.shape), _resident(wu.shape), _resident(wd.shape)])
    args = [x, f, b, c, gates, *weights, g.reshape(1, D_MODEL), wg, wu, wd]
    if final_norm:
        in_specs.append(gain)
        args.append(g_final.reshape(1, D_MODEL))
    return pl.pallas_call(
        functools.partial(_merge_ffn_kernel, final_norm=final_norm),
        grid=(TOKENS // MERGE_TM,),
        in_specs=in_specs,
        out_specs=tile(D_MODEL),
        out_shape=jax.ShapeDtypeStruct((TOKENS, D_MODEL), F32),
        scratch_shapes=[pltpu.VMEM((MERGE_TM, D_FF), BF16)],
        compiler_params=_params("parallel"),
        name="merge_ffn_final" if final_norm else "merge_ffn",
    )(*args)


def kernel(x, g_ffn1, w_ffn1_gate, w_ffn1_up, w_ffn1_down, g_mix, w_in, w_gate, b_gate,
           w_br_a, w_br_b, w_br_c, w_out, lam_q1, lam_k1, lam_q2, lam_k2, subln_g,
           rel_bias, g_ffn2, w_ffn2_gate, w_ffn2_up, w_ffn2_down, g_final):
    assert x.shape == (BATCH, SEQ, D_MODEL) and x.dtype == F32
    assert w_in.shape == (DEPTH, D_MODEL, IN_WIDTH)
    bf = lambda w: w.astype(BF16)

    dil_bias = _dil_bias_tiles(rel_bias)
    diff_strip = _diff_bias_strip(rel_bias)
    m_seq = _dft_matrices()
    d_chan = _channel_dft()
    in_plan = _in_weight_plan()

    h = x.reshape(TOKENS, D_MODEL)
    ffn1_w = [bf(w_ffn1_gate[0]), bf(w_ffn1_up[0]), bf(w_ffn1_down[0])]
    for l in range(DEPTH):
        last = l == DEPTH - 1
        h, wa, wc_br, wo, gate_w, in_w = _ffn(
            h, g_ffn1[l], *ffn1_w,
            cast_jobs=[(w_br_a, l, None), (w_br_c, l, None), (w_out, l, None),
                       (w_gate, l, None), (w_in, l, in_plan)])

        next_ffn1 = [] if last else [(w_ffn1_gate, l + 1, None), (w_ffn1_up, l + 1, None),
                                     (w_ffn1_down, l + 1, None)]
        pf, pb0, pb1, pb2, pc, gates, *ffn_w = _proj(
            h, g_mix[l], in_w, gate_w, b_gate[l],
            cast_jobs=[(w_ffn2_gate, l, None), (w_ffn2_up, l, None), (w_ffn2_down, l, None)]
            + next_ffn1)
        ffn2_w, ffn1_w = ffn_w[:3], ffn_w[3:]

        f = _fourier(pf, m_seq, d_chan).reshape(TOKENS, FOURIER_WIDTH)
        mixed = _dilated_mixture(pb0.reshape(BATCH, SEQ, DIL_QKV), pb1, pb2, dil_bias)
        lam_init = 0.8 - 0.6 * math.exp(-0.3 * l)
        c = _diff_attention(pc, diff_strip, lam_q1[l], lam_k1[l], lam_q2[l], lam_k2[l],
                            subln_g[l], lam_init)

        h = _merge_ffn(h, f, mixed.reshape(TOKENS, DIL_OUT_WIDTH), c, gates, wa, bf(w_br_b[l]),
                       wc_br, wo, g_ffn2[l], *ffn2_w, g_final if last else None)
    return h.reshape(BATCH, SEQ, D_MODEL)
```

```python
import functools
import math

import numpy as np
import jax
import jax.numpy as jnp
from jax import lax
from jax.experimental import pallas as pl
from jax.experimental.pallas import tpu as pltpu

F32 = jnp.float32
BF16 = jnp.bfloat16

D_MODEL = 1024
BATCH = 8
SEQ = 2048
DEPTH = 2
TOKENS = BATCH * SEQ
D_FF = 2816
EPS = 1e-6
FOURIER_GROUPS = 8
FOURIER_GROUP_DIM = 128
FOURIER_WIDTH = FOURIER_GROUPS * FOURIER_GROUP_DIM
DIL_PAIRS = ((128, 1), (512, 4), (2048, 16))
DIL_GROUPS = len(DIL_PAIRS)
DIL_HEAD_DIM = 64
DIL_HEADS = 2 * DIL_GROUPS
DIL_WIDTH = DIL_HEADS * DIL_HEAD_DIM
DIL_OUT_WIDTH = 2 * DIL_HEAD_DIM
DIL_RADIUS = 64
DIFF_HEADS = 4
DIFF_HEAD_DIM = 64
DIFF_QK_WIDTH = DIFF_HEADS * 2 * DIFF_HEAD_DIM
DIFF_V_DIM = 2 * DIFF_HEAD_DIM
DIFF_V_WIDTH = DIFF_HEADS * DIFF_V_DIM
SUBLN_EPS = 1e-5
NUM_BUCKETS = 32
MAX_DISTANCE = 1024
N_BIAS_HEADS = DIL_HEADS + DIFF_HEADS
NEG_INF = -1e30
QK_SCALE = DIL_HEAD_DIM ** -0.5
LOG2_E = math.log2(math.e)
B_WIDTH = 3 * DIL_WIDTH
C_WIDTH = 2 * DIFF_QK_WIDTH + DIFF_V_WIDTH
IN_WIDTH = FOURIER_WIDTH + B_WIDTH + C_WIDTH
GATE_WIDTH = 3 * D_MODEL

LANES = 128
BF16_SUBLANES = 16
VMEM_BYTES_V7X = 64 * 1024 * 1024
VMEM_LIMIT = 56 * 1024 * 1024

FFN_TM = 1024
FF_CHUNK = 256
PROJ_TM = 512
PROJ_CHUNK = 512
MERGE_TM = 512
FOURIER_RADIX = 4
QUARTER_SEQ = SEQ // FOURIER_RADIX
FOURIER_CN = 512
DIL_QKV = 3 * LANES
DIL_QB = 128
DIL_KW = DIL_QB + 2 * DIL_RADIUS
DIL_UNROLL = 8
DIFF_TQ = 512
DIFF_STRIP = 2 * SEQ - DIFF_TQ
DIFF_RING = 2 * SEQ
DIFF_KT = 512


def _params(*semantics):
    return pltpu.CompilerParams(dimension_semantics=semantics, vmem_limit_bytes=VMEM_LIMIT)


def _resident(shape):
    nd = len(shape)
    return pl.BlockSpec(shape, lambda *_: (0,) * nd, pipeline_mode=pl.Buffered(1))


def _rms(x, g, eps):
    return x * lax.rsqrt(jnp.mean(x * x, axis=-1, keepdims=True) + eps) * g


def _cast_specs(jobs, steps):
    in_specs, out_specs, out_shapes = [], [], []
    for src, layer, plan in jobs:
        _, rows, cols = src.shape
        out_cols = cols if plan is None else sum(width for _, width, _ in plan)
        share = 1
        while rows % (steps // share * BF16_SUBLANES):
            share *= 2
        blk = rows // (steps // share)
        in_specs.append(pl.BlockSpec((None, blk, cols),
                                     lambda i, layer=layer, share=share: (layer, i // share, 0)))
        out_specs.append(pl.BlockSpec((blk, out_cols), lambda i, share=share: (i // share, 0)))
        out_shapes.append(jax.ShapeDtypeStruct((rows, out_cols), BF16))
    return in_specs, out_specs, out_shapes


def _cast_blocks(src_refs, dst_refs, plans):
    for src, dst, plan in zip(src_refs, dst_refs, plans):
        if plan is None:
            dst[...] = src[...].astype(BF16)
            continue
        col = 0
        for src_col, width, scale in plan:
            part = src[:, src_col:src_col + width]
            dst[:, col:col + width] = (part if scale == 1.0 else part * scale).astype(BF16)
            col += width


def _cast_args(jobs):
    return [src for src, _, _ in jobs], tuple(plan for _, _, plan in jobs)


def _bucket_breakpoints():
    rel = np.arange(-(SEQ - 1), SEQ)
    half = NUM_BUCKETS // 2
    max_exact = half // 2
    n = np.abs(rel)
    nf = np.maximum(n, 1).astype(np.float64)
    large = max_exact + (np.log(nf / max_exact) / math.log(MAX_DISTANCE / max_exact)
                         * (half - max_exact)).astype(np.int32)
    large = np.minimum(large, half - 1)
    b = np.where(rel > 0, half, 0) + np.where(n < max_exact, n, large)
    pts = [(int(rel[i]), int(b[i])) for i in range(1, len(rel)) if b[i] != b[i - 1]]
    return int(b[0]), pts


def _bias_values(tbl_ref, head, rel):
    first, pts = _bucket_breakpoints()
    val = jnp.full(rel.shape, tbl_ref[first, head], F32)
    for thr, bucket in pts:
        val = jnp.where(rel >= thr, tbl_ref[bucket, head], val)
    return val * LOG2_E


def _diff_bias_kernel(tbl_ref, o_ref):
    rows, cols = o_ref.shape[1], o_ref.shape[2]
    ring = DIFF_RING
    u = lax.broadcasted_iota(jnp.int32, (8, ring), 1)
    offset = jnp.where(u < cols, u, u - ring)
    line = _bias_values(tbl_ref, DIL_HEADS + pl.program_id(0), offset - (SEQ - rows))
    tiled = jnp.broadcast_to(line[:1], (rows, ring))
    o_ref[0] = pltpu.roll(tiled, 0, 1, stride=1, stride_axis=0)[:, :cols]


def _diff_bias_strip(rel_bias):
    return pl.pallas_call(
        _diff_bias_kernel,
        grid=(DIFF_HEADS,),
        in_specs=[pl.BlockSpec(memory_space=pltpu.SMEM)],
        out_specs=pl.BlockSpec((1, DIFF_TQ, DIFF_STRIP), lambda h: (h, 0, 0)),
        out_shape=jax.ShapeDtypeStruct((DIFF_HEADS, DIFF_TQ, DIFF_STRIP), F32),
        compiler_params=_params("arbitrary"),
        name="diff_bias_strip",
    )(rel_bias)


EDGE_FIRST, EDGE_NONE, EDGE_LAST, EDGE_BOTH = range(4)


def _dil_bias_kernel(tbl_ref, o_ref):
    head = pl.program_id(0)
    group = head // 2
    dil = DIL_PAIRS[0][1]
    for g in range(1, DIL_GROUPS):
        dil = jnp.where(group == g, DIL_PAIRS[g][1], dil)
    rows, cols = o_ref.shape[3], o_ref.shape[4]
    col = lax.broadcasted_iota(jnp.int32, (rows, cols), 1)
    rel_t = col - lax.broadcasted_iota(jnp.int32, (rows, cols), 0) - DIL_RADIUS
    val = _bias_values(tbl_ref, head, rel_t * dil)
    val = jnp.where(jnp.abs(rel_t) <= DIL_RADIUS, val, NEG_INF)
    lo = jnp.where(col >= DIL_RADIUS, val, NEG_INF)
    o_ref[0, EDGE_FIRST, 0] = lo
    o_ref[0, EDGE_NONE, 0] = val
    o_ref[0, EDGE_LAST, 0] = jnp.where(col < cols - DIL_RADIUS, val, NEG_INF)
    o_ref[0, EDGE_BOTH, 0] = jnp.where(col < cols - DIL_RADIUS, lo, NEG_INF)


def _dil_bias_tiles(rel_bias):
    return pl.pallas_call(
        _dil_bias_kernel,
        grid=(DIL_HEADS,),
        in_specs=[pl.BlockSpec(memory_space=pltpu.SMEM)],
        out_specs=pl.BlockSpec((1, 4, 1, DIL_QB, DIL_KW), lambda h: (h // 2, 0, h % 2, 0, 0)),
        out_shape=jax.ShapeDtypeStruct((DIL_GROUPS, 4, 2, DIL_QB, DIL_KW), F32),
        compiler_params=_params("arbitrary"),
        name="dil_bias_tiles",
    )(rel_bias)


def _swiglu_residual(x, g_ref, wg_ref, wu_ref, wd_ref, a_ref):
    h = _rms(x, g_ref[...], EPS).astype(BF16)
    for c in range(D_FF // FF_CHUNK):
        sl = slice(c * FF_CHUNK, (c + 1) * FF_CHUNK)
        gate = jnp.dot(h, wg_ref[:, sl], preferred_element_type=F32)
        up = jnp.dot(h, wu_ref[:, sl], preferred_element_type=F32)
        a_ref[:, sl] = (gate * jax.nn.sigmoid(gate) * up).astype(BF16)
    return x + 0.5 * jnp.dot(a_ref[...], wd_ref[...], preferred_element_type=F32)


def _ffn_kernel(x_ref, g_ref, wg_ref, wu_ref, wd_ref, *rest, cast_plans):
    n_cast = len(cast_plans)
    cast_src, o_ref, cast_dst, a_ref = (rest[:n_cast], rest[n_cast], rest[n_cast + 1:-1], rest[-1])
    _cast_blocks(cast_src, cast_dst, cast_plans)
    o_ref[...] = _swiglu_residual(x_ref[...], g_ref, wg_ref, wu_ref, wd_ref, a_ref)


def _ffn(x, g, wg, wu, wd, cast_jobs=()):
    steps = TOKENS // FFN_TM
    tile = pl.BlockSpec((FFN_TM, D_MODEL), lambda i: (i, 0))
    cast_in, cast_out, cast_shapes = _cast_specs(cast_jobs, steps)
    cast_srcs, cast_plans = _cast_args(cast_jobs)
    return pl.pallas_call(
        functools.partial(_ffn_kernel, cast_plans=cast_plans),
        grid=(steps,),
        in_specs=[tile, _resident((1, D_MODEL)), _resident((D_MODEL, D_FF)),
                  _resident((D_MODEL, D_FF)), _resident((D_FF, D_MODEL))] + cast_in,
        out_specs=[tile] + cast_out,
        out_shape=[jax.ShapeDtypeStruct((TOKENS, D_MODEL), F32)] + cast_shapes,
        scratch_shapes=[pltpu.VMEM((FFN_TM, D_FF), BF16)],
        compiler_params=_params("arbitrary"),
        name="ffn",
    )(x, g.reshape(1, D_MODEL), wg, wu, wd, *cast_srcs)


def _proj_kernel(x_ref, g_ref, w_ref, wg_ref, bg_ref, *rest, cast_plans):
    n_cast = len(cast_plans)
    cast_src, rest = rest[:n_cast], rest[n_cast:]
    pf_ref, pb0_ref, pb1_ref, pb2_ref, pc_ref, gate_ref = rest[:6]
    cast_dst, y_ref = rest[6:-1], rest[-1]
    _cast_blocks(cast_src, cast_dst, cast_plans)
    u = _rms(x_ref[...], g_ref[...], EPS).astype(BF16)
    rows = x_ref.shape[0]

    def chunks(width):
        return [slice(c0, min(c0 + PROJ_CHUNK, width)) for c0 in range(0, width, PROJ_CHUNK)]

    def project(w, col0, sl):
        return jnp.dot(u, w[:, col0 + sl.start:col0 + sl.stop], preferred_element_type=F32)

    def by_residue(y, o_ref, dil, col0):
        for j in range(y.shape[1] // LANES):
            y_ref[j] = y[:, j * LANES:(j + 1) * LANES]
            cols = slice(col0 + j * LANES, col0 + (j + 1) * LANES)
            for r in range(dil):
                o_ref[0, r, :, cols] = y_ref[j, pl.ds(r, rows // dil, stride=dil), :].astype(o_ref.dtype)

    for sl in chunks(GATE_WIDTH):
        gate_ref[:, sl] = jax.nn.sigmoid(project(wg_ref, 0, sl) + bg_ref[:, sl]).astype(gate_ref.dtype)
    for sl in chunks(FOURIER_WIDTH):
        by_residue(project(w_ref, 0, sl), pf_ref, FOURIER_RADIX, sl.start)
    y = project(w_ref, FOURIER_WIDTH, slice(0, B_WIDTH))
    pb0_ref[...] = y[:, :DIL_QKV].astype(pb0_ref.dtype)
    by_residue(y[:, DIL_QKV:2 * DIL_QKV], pb1_ref, DIL_PAIRS[1][1], 0)
    by_residue(y[:, 2 * DIL_QKV:], pb2_ref, DIL_PAIRS[2][1], 0)
    for sl in chunks(C_WIDTH):
        pc_ref[:, sl] = project(w_ref, FOURIER_WIDTH + B_WIDTH, sl).astype(pc_ref.dtype)


def _in_weight_plan():
    q_scale = QK_SCALE * LOG2_E
    plan = [(0, FOURIER_WIDTH, 1.0)]
    for g in range(DIL_GROUPS):
        lo = FOURIER_WIDTH + g * LANES
        plan += [(lo, LANES, q_scale), (lo + DIL_WIDTH, LANES, 1.0), (lo + 2 * DIL_WIDTH, LANES, 1.0)]
    c3 = FOURIER_WIDTH + B_WIDTH
    plan += [(c3, DIFF_QK_WIDTH, q_scale), (c3 + DIFF_QK_WIDTH, DIFF_QK_WIDTH + DIFF_V_WIDTH, 1.0)]
    return plan


def _proj(x, g, w, wg, bg, cast_jobs=()):
    tiles_per_seq = SEQ // PROJ_TM
    steps = TOKENS // PROJ_TM
    cast_in, cast_out, cast_shapes = _cast_specs(cast_jobs, steps)
    cast_srcs, cast_plans = _cast_args(cast_jobs)

    def tile(w):
        return pl.BlockSpec((PROJ_TM, w), lambda i: (i, 0))

    def residue_tile(dil, w):
        return pl.BlockSpec((1, dil, PROJ_TM // dil, w),
                            lambda i: (i // tiles_per_seq, 0, i % tiles_per_seq, 0))

    def residue_shape(dil, w):
        return jax.ShapeDtypeStruct((BATCH, dil, SEQ // dil, w), BF16)

    d1, d2 = DIL_PAIRS[1][1], DIL_PAIRS[2][1]
    return pl.pallas_call(
        functools.partial(_proj_kernel, cast_plans=cast_plans),
        grid=(steps,),
        in_specs=[tile(D_MODEL), _resident((1, D_MODEL)), _resident((D_MODEL, IN_WIDTH)),
                  _resident((D_MODEL, GATE_WIDTH)), _resident((1, GATE_WIDTH))] + cast_in,
        out_specs=[residue_tile(FOURIER_RADIX, FOURIER_WIDTH), tile(DIL_QKV),
                   residue_tile(d1, DIL_QKV),
                   residue_tile(d2, DIL_QKV), tile(C_WIDTH), tile(GATE_WIDTH)] + cast_out,
        out_shape=[residue_shape(FOURIER_RADIX, FOURIER_WIDTH),
                   jax.ShapeDtypeStruct((TOKENS, DIL_QKV), BF16),
                   residue_shape(d1, DIL_QKV), residue_shape(d2, DIL_QKV),
                   jax.ShapeDtypeStruct((TOKENS, C_WIDTH), BF16),
                   jax.ShapeDtypeStruct((TOKENS, GATE_WIDTH), BF16)] + cast_shapes,
        scratch_shapes=[pltpu.VMEM((max(PROJ_CHUNK, DIL_QKV) // LANES, PROJ_TM, LANES), F32)],
        compiler_params=_params("arbitrary"),
        name="proj_in",
    )(x, g.reshape(1, D_MODEL), w, wg, bg.reshape(1, GATE_WIDTH), *cast_srcs)


def _fourier_kernel(a_ref, m_ref, d_ref, o_ref):
    q = QUARTER_SEQ
    tc, ts = [], []
    for r in range(FOURIER_RADIX):
        t = jnp.dot(m_ref[r], a_ref[0, r], preferred_element_type=F32)
        tc.append(t[:q])
        ts.append(t[q:])
    c02p, c02m, c13p, c13m = tc[0] + tc[2], tc[0] - tc[2], tc[1] + tc[3], tc[1] - tc[3]
    s02p, s02m, s13p, s13m = ts[0] + ts[2], ts[0] - ts[2], ts[1] + ts[3], ts[1] - ts[3]
    cos_sums = (c02p + c13p, c02m - s13m, c02p - c13p, c02m + s13m)
    sin_sums = (s02p + s13p, s02m + c13m, s02p - s13p, s02m - c13m)
    for j in range(FOURIER_RADIX):
        p, s = cos_sums[j].astype(BF16), sin_sums[j].astype(BF16)
        for g in range(o_ref.shape[2] // LANES):
            cols = slice(g * LANES, (g + 1) * LANES)
            lhs = jnp.concatenate([p[:, cols], s[:, cols]], axis=1)
            o_ref[0, j * q:(j + 1) * q, cols] = jnp.dot(
                lhs, d_ref[...], preferred_element_type=F32).astype(o_ref.dtype)


def _fourier(pf, m_seq, d_chan):
    cn = FOURIER_CN
    return pl.pallas_call(
        _fourier_kernel,
        grid=(BATCH, FOURIER_WIDTH // cn),
        in_specs=[pl.BlockSpec((1, FOURIER_RADIX, QUARTER_SEQ, cn), lambda b, j: (b, 0, 0, j)),
                  _resident((FOURIER_RADIX, 2 * QUARTER_SEQ, QUARTER_SEQ)),
                  _resident((2 * FOURIER_GROUP_DIM, FOURIER_GROUP_DIM))],
        out_specs=pl.BlockSpec((1, SEQ, cn), lambda b, j: (b, 0, j)),
        out_shape=jax.ShapeDtypeStruct((BATCH, SEQ, FOURIER_WIDTH), BF16),
        compiler_params=_params("parallel", "parallel"),
        name="fourier_mix",
    )(pf, m_seq, d_chan)


def _dft_matrices():
    q = QUARTER_SEQ
    split = 32

    def table(mult, width):
        k = lax.broadcasted_iota(jnp.int32, (q, width), 0)
        j = lax.broadcasted_iota(jnp.int32, (q, width), 1)
        ang = ((k * j * mult) % q).astype(F32) * (2.0 * math.pi / q)
        return jnp.cos(ang), jnp.sin(ang)

    (ca, sa), (cb, sb) = table(split, q // split), table(1, split)
    cos_0 = (ca[:, :, None] * cb[:, None, :] - sa[:, :, None] * sb[:, None, :]).reshape(q, q)
    sin_0 = (sa[:, :, None] * cb[:, None, :] + ca[:, :, None] * sb[:, None, :]).reshape(q, q)
    mats = []
    for r in range(FOURIER_RADIX):
        phi = lax.broadcasted_iota(jnp.int32, (q, 1), 0).astype(F32) * (2.0 * math.pi * r / SEQ)
        cos_r = cos_0 * jnp.cos(phi) - sin_0 * jnp.sin(phi)
        sin_r = sin_0 * jnp.cos(phi) + cos_0 * jnp.sin(phi)
        mats.append(jnp.concatenate([cos_r, sin_r], axis=0))
    return jnp.stack(mats).astype(BF16)


def _channel_dft():
    gd = FOURIER_GROUP_DIM
    idx = np.arange(gd)
    ang = 2.0 * np.pi * ((idx[:, None] * idx[None, :]) % gd) / gd
    scale = 1.0 / math.sqrt(SEQ * gd)
    return jnp.asarray(np.concatenate([np.cos(ang), -np.sin(ang)]) * scale, F32).astype(BF16)


def _dil_block(q, kw, vw_aug, bias, head0_lanes):
    zero = jnp.zeros_like(q)
    pv, lse = [], []
    for hh in range(2):
        q_h = jnp.where(head0_lanes, q, zero) if hh == 0 else jnp.where(head0_lanes, zero, q)
        logits = lax.dot_general(q_h, kw, (((1,), (1,)), ((), ())),
                                 preferred_element_type=F32) + bias[hh]
        mx = jnp.max(logits, axis=-1, keepdims=True)
        e = jnp.exp2(logits - mx).astype(BF16)
        r = jnp.dot(e, vw_aug, preferred_element_type=F32)
        s = r[:, LANES:]
        pv.append(r[:, :LANES] / s)
        lse.append(mx + jnp.log2(s))
    return jnp.where(head0_lanes, pv[0], pv[1]), jnp.where(head0_lanes, lse[0], lse[1])


def _dil_kernel(g0_ref, g1_ref, g2_ref, bias_ref, out_ref,
                kpad_ref, vpad_ref, o_ref, lse_ref):
    pad = DIL_RADIUS
    head0_lanes = lax.broadcasted_iota(jnp.int32, (DIL_QB, LANES), 1) < DIL_HEAD_DIM
    zeros_k = jnp.zeros((pad, LANES), BF16)
    zeros_v = jnp.zeros((pad, 2 * LANES), BF16)
    q_sl, k_sl, v_sl = (slice(i * LANES, (i + 1) * LANES) for i in range(3))

    vpad_ref[pl.ds(pad, SEQ), LANES:] = jnp.ones((SEQ, LANES), BF16)

    def fill_padded(src, sub_len):
        kpad_ref[:pad, :] = zeros_k
        kpad_ref[pl.ds(pad, sub_len), :] = src[:, k_sl]
        kpad_ref[pl.ds(pad + sub_len, pad), :] = zeros_k
        vpad_ref[:pad, :] = zeros_v
        vpad_ref[pl.ds(pad, sub_len), :LANES] = src[:, v_sl]
        vpad_ref[pl.ds(pad + sub_len, pad), :] = zeros_v

    def store(group, rows, o, lse):
        o_ref[group, rows, :] = o
        lse_ref[group, rows, :] = lse

    def padded_block(group, q_src, blk, edge, dil, residue):
        r0 = pl.multiple_of(blk * DIL_QB, DIL_QB)
        o, lse = _dil_block(q_src[pl.ds(r0, DIL_QB), q_sl], kpad_ref[pl.ds(r0, DIL_KW), :],
                            vpad_ref[pl.ds(r0, DIL_KW), :],
                            [bias_ref[group, edge, 0], bias_ref[group, edge, 1]],
                            head0_lanes)
        rows = pl.ds(residue + r0 * dil, DIL_QB, stride=dil) if dil > 1 else pl.ds(r0, DIL_QB)
        store(group, rows, o, lse)

    fill_padded(g0_ref.at[0], SEQ)
    n_blocks0 = SEQ // DIL_QB

    def g0_body(i, carry):
        for n in range(DIL_UNROLL):
            blk = i * DIL_UNROLL + n
            edge = jnp.where(blk == 0, EDGE_FIRST,
                             jnp.where(blk == n_blocks0 - 1, EDGE_LAST, EDGE_NONE))
            padded_block(0, g0_ref.at[0], blk, edge, 1, 0)
        return carry

    lax.fori_loop(0, n_blocks0 // DIL_UNROLL, g0_body, 0)

    dil1 = DIL_PAIRS[1][1]
    len1 = SEQ // dil1
    n_blocks1 = len1 // DIL_QB
    per_iter1 = DIL_UNROLL // n_blocks1
    slot1 = len1 + 2 * pad
    for n in range(per_iter1):
        for ref, zeros in ((kpad_ref, zeros_k), (vpad_ref, zeros_v)):
            ref[n * slot1:n * slot1 + pad, :] = zeros
            ref[n * slot1 + pad + len1:(n + 1) * slot1, :] = zeros
        vpad_ref[n * slot1 + pad:n * slot1 + pad + len1, LANES:] = jnp.ones((len1, LANES), BF16)

    def g1_body(i, carry):
        for n in range(per_iter1):
            r = i * per_iter1 + n
            src = g1_ref.at[0, r]
            base = n * slot1
            kpad_ref[base + pad:base + pad + len1, :] = src[:, k_sl]
            vpad_ref[base + pad:base + pad + len1, :LANES] = src[:, v_sl]
            for blk in range(n_blocks1):
                edge = EDGE_FIRST if blk == 0 else EDGE_LAST if blk == n_blocks1 - 1 else EDGE_NONE
                r0 = blk * DIL_QB
                o, lse = _dil_block(src[r0:r0 + DIL_QB, q_sl],
                                    kpad_ref[base + r0:base + r0 + DIL_KW, :],
                                    vpad_ref[base + r0:base + r0 + DIL_KW, :],
                                    [bias_ref[1, edge, 0], bias_ref[1, edge, 1]], head0_lanes)
                store(1, pl.ds(r + r0 * dil1, DIL_QB, stride=dil1), o, lse)
        return carry

    lax.fori_loop(0, dil1 // per_iter1, g1_body, 0)

    dil2 = DIL_PAIRS[2][1]
    assert SEQ // dil2 == DIL_QB
    for n in range(DIL_UNROLL):
        base = n * DIL_KW
        for ref, zeros in ((kpad_ref, zeros_k), (vpad_ref, zeros_v)):
            ref[base:base + pad, :] = zeros
            ref[base + pad + DIL_QB:base + DIL_KW, :] = zeros
        vpad_ref[base + pad:base + pad + DIL_QB, LANES:] = jnp.ones((DIL_QB, LANES), BF16)

    def g2_body(i, carry):
        for n in range(DIL_UNROLL):
            r = i * DIL_UNROLL + n
            src = g2_ref.at[0, r]
            base = n * DIL_KW
            kpad_ref[base + pad:base + pad + DIL_QB, :] = src[:, k_sl]
            vpad_ref[base + pad:base + pad + DIL_QB, :LANES] = src[:, v_sl]
            o, lse = _dil_block(src[:, q_sl], kpad_ref[base:base + DIL_KW, :],
                                vpad_ref[base:base + DIL_KW, :],
                                [bias_ref[2, EDGE_BOTH, 0], bias_ref[2, EDGE_BOTH, 1]],
                                head0_lanes)
            store(2, pl.ds(r, DIL_QB, stride=dil2), o, lse)
        return carry

    lax.fori_loop(0, dil2 // DIL_UNROLL, g2_body, 0)

    chunk = 256
    for c in range(SEQ // chunk):
        rows = slice(c * chunk, (c + 1) * chunk)
        l0, l1, l2 = lse_ref[0, rows, :], lse_ref[1, rows, :], lse_ref[2, rows, :]
        mx = jnp.maximum(jnp.maximum(l0, l1), l2)
        e0, e1, e2 = jnp.exp2(l0 - mx), jnp.exp2(l1 - mx), jnp.exp2(l2 - mx)
        mixed = (e0 * o_ref[0, rows, :] + e1 * o_ref[1, rows, :] + e2 * o_ref[2, rows, :]) / (e0 + e1 + e2)
        out_ref[0, rows, :] = mixed.astype(out_ref.dtype)


def _dilated_mixture(pb0, pb1, pb2, biases):
    d1, d2 = DIL_PAIRS[1][1], DIL_PAIRS[2][1]
    bias_spec = _resident((DIL_GROUPS, 4, 2, DIL_QB, DIL_KW))
    max_len = SEQ + 2 * DIL_RADIUS
    return pl.pallas_call(
        _dil_kernel,
        grid=(BATCH,),
        in_specs=[pl.BlockSpec((1, SEQ, DIL_QKV), lambda b: (b, 0, 0)),
                  pl.BlockSpec((1, d1, SEQ // d1, DIL_QKV), lambda b: (b, 0, 0, 0)),
                  pl.BlockSpec((1, d2, SEQ // d2, DIL_QKV), lambda b: (b, 0, 0, 0)),
                  bias_spec],
        out_specs=pl.BlockSpec((1, SEQ, LANES), lambda b: (b, 0, 0)),
        out_shape=jax.ShapeDtypeStruct((BATCH, SEQ, DIL_OUT_WIDTH), BF16),
        scratch_shapes=[pltpu.VMEM((max_len, LANES), BF16), pltpu.VMEM((max_len, 2 * LANES), BF16),
                        pltpu.VMEM((DIL_GROUPS, SEQ, LANES), F32),
                        pltpu.VMEM((DIL_GROUPS, SEQ, LANES), F32)],
        compiler_params=_params("parallel"),
        name="dilated_attn",
    )(pb0, pb1, pb2, biases)


def _diff_kernel(q_ref, k_ref, v_ref, bias_ref, lq1_ref, lk1_ref, lq2_ref, lk2_ref, sg_ref,
                 o_ref, vaug_ref, logit_ref, rmax_ref, *, lam_init):
    s = pl.program_id(0)
    n_blocks = pl.num_programs(0) - 1
    nq = SEQ // DIFF_TQ
    cur = jnp.minimum(s, n_blocks - 1)
    prev = jnp.maximum(s - 1, 0)
    slot = s % 2

    @pl.when(s == 0)
    def _():
        logit_ref[1] = jnp.zeros(logit_ref.shape[1:], F32)
        rmax_ref[1] = jnp.zeros(rmax_ref.shape[1:], F32)

    @pl.when(prev % nq == 0)
    def _():
        vaug_ref[:, :LANES] = v_ref[0]
        vaug_ref[:, LANES:] = jnp.ones((SEQ, LANES), BF16)

    n_tiles = SEQ // DIFF_KT

    def key_tile(t):
        return slice(t * DIFF_KT, (t + 1) * DIFF_KT)

    off = pl.multiple_of((nq - 1 - cur % nq) * DIFF_TQ, DIFF_TQ)
    q = q_ref[0]
    first_half = lax.broadcasted_iota(jnp.int32, q.shape, 1) < DIFF_HEAD_DIM
    zero = jnp.zeros_like(q)
    outs = []
    for m in range(2):
        row_max = rmax_ref[1 - slot, m]
        pv = jnp.zeros((DIFF_TQ, 2 * LANES), F32)
        for t in range(n_tiles):
            old = logit_ref[1 - slot, m, :, key_tile(t)]
            e = jnp.concatenate([jnp.exp2(old[:, c * LANES:(c + 1) * LANES] - row_max)
                                 for c in range(DIFF_KT // LANES)], axis=1).astype(BF16)
            pv = pv + jnp.dot(e, vaug_ref[key_tile(t), :], preferred_element_type=F32)
        outs.append(pv[:, :LANES] / pv[:, LANES:])

    for m in range(2):
        q_m = jnp.where(first_half, q, zero) if m == 0 else jnp.where(first_half, zero, q)
        lane_max = None
        for t in range(n_tiles):
            bias = bias_ref[0, :, pl.ds(pl.multiple_of(off + t * DIFF_KT, LANES), DIFF_KT)]
            logits = lax.dot_general(q_m, k_ref[0, key_tile(t), :], (((1,), (1,)), ((), ())),
                                     preferred_element_type=F32) + bias
            logit_ref[slot, m, :, key_tile(t)] = logits
            for c in range(DIFF_KT // LANES):
                part = logits[:, c * LANES:(c + 1) * LANES]
                lane_max = part if lane_max is None else jnp.maximum(lane_max, part)
        rmax_ref[slot, m] = jnp.broadcast_to(jnp.max(lane_max, axis=-1, keepdims=True),
                                             (DIFF_TQ, LANES))

    lam = (jnp.exp(jnp.sum(lq1_ref[...] * lk1_ref[...], axis=-1, keepdims=True))
           - jnp.exp(jnp.sum(lq2_ref[...] * lk2_ref[...], axis=-1, keepdims=True))
           + lam_init)
    o = outs[0] - lam * outs[1]
    o = _rms(o, sg_ref[...], SUBLN_EPS) * (1.0 - lam_init)
    o_ref[0] = o.astype(o_ref.dtype)


def _diff_attention(pc, strip, lq1, lk1, lq2, lk2, subln_g, lam_init):
    pcv = pc.reshape(BATCH, SEQ, C_WIDTH)
    vec = _resident((1, DIFF_HEAD_DIM))
    nq = SEQ // DIFF_TQ
    n_blocks = DIFF_HEADS * BATCH * nq

    def block_of(step, lag):
        blk = jnp.clip(step - lag, 0, n_blocks - 1)
        return blk // (BATCH * nq), (blk // nq) % BATCH, blk % nq

    def q_map(s):
        h, b, j = block_of(s, 0)
        return b, j, h

    def k_map(s):
        h, b, _ = block_of(s, 0)
        return b, 0, DIFF_HEADS + h

    def v_map(s):
        h, b, _ = block_of(s, 1)
        return b, 0, 2 * DIFF_HEADS + h

    def out_map(s):
        h, b, j = block_of(s, 1)
        return b, j, h

    return pl.pallas_call(
        functools.partial(_diff_kernel, lam_init=lam_init),
        grid=(n_blocks + 1,),
        in_specs=[pl.BlockSpec((1, DIFF_TQ, LANES), q_map),
                  pl.BlockSpec((1, SEQ, LANES), k_map),
                  pl.BlockSpec((1, SEQ, LANES), v_map),
                  pl.BlockSpec((1, DIFF_TQ, DIFF_STRIP), lambda s: (block_of(s, 0)[0], 0, 0)),
                  vec, vec, vec, vec, _resident((1, DIFF_V_DIM))],
        out_specs=pl.BlockSpec((1, DIFF_TQ, LANES), out_map),
        out_shape=jax.ShapeDtypeStruct((BATCH, SEQ, DIFF_V_WIDTH), BF16),
        scratch_shapes=[pltpu.VMEM((SEQ, 2 * LANES), BF16),
                        pltpu.VMEM((2, 2, DIFF_TQ, SEQ), F32),
                        pltpu.VMEM((2, 2, DIFF_TQ, LANES), F32)],
        compiler_params=_params("arbitrary"),
        name="diff_attn",
    )(pcv, pcv, pcv, strip, lq1.reshape(1, -1), lk1.reshape(1, -1), lq2.reshape(1, -1),
      lk2.reshape(1, -1), subln_g.reshape(1, -1)).reshape(TOKENS, DIFF_V_WIDTH)


def _merge_ffn_kernel(x_ref, f_ref, b_ref, c_ref, gate_ref, wa_ref, wb_ref, wc_ref, wo_ref,
                      g_ref, wg_ref, wu_ref, wd_ref, *rest, final_norm):
    rest = list(rest)
    gf_ref = rest.pop(0) if final_norm else None
    out_ref, a_ref = rest
    y_a = jnp.dot(f_ref[...], wa_ref[...], preferred_element_type=F32)
    y_b = jnp.dot(b_ref[...], wb_ref[...], preferred_element_type=F32)
    y_c = jnp.dot(c_ref[...], wc_ref[...], preferred_element_type=F32)
    merged = (gate_ref[:, :D_MODEL].astype(F32) * y_a
              + gate_ref[:, D_MODEL:2 * D_MODEL].astype(F32) * y_b
              + gate_ref[:, 2 * D_MODEL:].astype(F32) * y_c)
    x = x_ref[...] + jnp.dot(merged.astype(BF16), wo_ref[...], preferred_element_type=F32)
    y = _swiglu_residual(x, g_ref, wg_ref, wu_ref, wd_ref, a_ref)
    if final_norm:
        y = _rms(y, gf_ref[...], EPS)
    out_ref[...] = y


def _merge_ffn(x, f, b, c, gates, wa, wb, wc, wo, g, wg, wu, wd, g_final=None):
    def tile(w):
        return pl.BlockSpec((MERGE_TM, w), lambda i: (i, 0))

    final_norm = g_final is not None
    weights = [wa, wb, wc, wo]
    gain = _resident((1, D_MODEL))
    in_specs = ([tile(D_MODEL), tile(FOURIER_WIDTH), tile(DIL_OUT_WIDTH), tile(DIFF_V_WIDTH),
                 tile(GATE_WIDTH)] + [_resident(w.shape) for w in weights]
                + [gain, _resident(wg.shape), _resident(wu.shape), _resident(wd.shape)])
    args = [x, f, b, c, gates, *weights, g.reshape(1, D_MODEL), wg, wu, wd]
    if final_norm:
        in_specs.append(gain)
        args.append(g_final.reshape(1, D_MODEL))
    return pl.pallas_call(
        functools.partial(_merge_ffn_kernel, final_norm=final_norm),
        grid=(TOKENS // MERGE_TM,),
        in_specs=in_specs,
        out_specs=tile(D_MODEL),
        out_shape=jax.ShapeDtypeStruct((TOKENS, D_MODEL), F32),
        scratch_shapes=[pltpu.VMEM((MERGE_TM, D_FF), BF16)],
        compiler_params=_params("parallel"),
        name="merge_ffn_final" if final_norm else "merge_ffn",
    )(*args)


def kernel(x, g_ffn1, w_ffn1_gate, w_ffn1_up, w_ffn1_down, g_mix, w_in, w_gate, b_gate,
           w_br_a, w_br_b, w_br_c, w_out, lam_q1, lam_k1, lam_q2, lam_k2, subln_g,
           rel_bias, g_ffn2, w_ffn2_gate, w_ffn2_up, w_ffn2_down, g_final):
    assert x.shape == (BATCH, SEQ, D_MODEL) and x.dtype == F32
    assert w_in.shape == (DEPTH, D_MODEL, IN_WIDTH)
    bf = lambda w: w.astype(BF16)

    dil_bias = _dil_bias_tiles(rel_bias)
    diff_strip = _diff_bias_strip(rel_bias)
    m_seq = _dft_matrices()
    d_chan = _channel_dft()
    in_plan = _in_weight_plan()

    h = x.reshape(TOKENS, D_MODEL)
    ffn1_w = [bf(w_ffn1_gate[0]), bf(w_ffn1_up[0]), bf(w_ffn1_down[0])]
    for l in range(DEPTH):
        last = l == DEPTH - 1
        h, wa, wc_br, wo, gate_w, in_w = _ffn(
            h, g_ffn1[l], *ffn1_w,
            cast_jobs=[(w_br_a, l, None), (w_br_c, l, None), (w_out, l, None),
                       (w_gate, l, None), (w_in, l, in_plan)])

        next_ffn1 = [] if last else [(w_ffn1_gate, l + 1, None), (w_ffn1_up, l + 1, None),
                                     (w_ffn1_down, l + 1, None)]
        pf, pb0, pb1, pb2, pc, gates, *ffn_w = _proj(
            h, g_mix[l], in_w, gate_w, b_gate[l],
            cast_jobs=[(w_ffn2_gate, l, None), (w_ffn2_up, l, None), (w_ffn2_down, l, None)]
            + next_ffn1)
        ffn2_w, ffn1_w = ffn_w[:3], ffn_w[3:]

        f = _fourier(pf, m_seq, d_chan).reshape(TOKENS, FOURIER_WIDTH)
        mixed = _dilated_mixture(pb0.reshape(BATCH, SEQ, DIL_QKV), pb1, pb2, dil_bias)
        lam_init = 0.8 - 0.6 * math.exp(-0.3 * l)
        c = _diff_attention(pc, diff_strip, lam_q1[l], lam_k1[l], lam_q2[l], lam_k2[l],
                            subln_g[l], lam_init)

        h = _merge_ffn(h, f, mixed.reshape(TOKENS, DIL_OUT_WIDTH), c, gates, wa, bf(w_br_b[l]),
                       wc_br, wo, g_ffn2[l], *ffn2_w, g_final if last else None)
    return h.reshape(BATCH, SEQ, D_MODEL)
```

```python
import functools
import math

import numpy as np
import jax
import jax.numpy as jnp
from jax import lax
from jax.experimental import pallas as pl
from jax.experimental.pallas import tpu as pltpu

F32 = jnp.float32
BF16 = jnp.bfloat16

D_MODEL = 1024
BATCH = 8
SEQ = 2048
DEPTH = 2
TOKENS = BATCH * SEQ
D_FF = 2816
EPS = 1e-6
FOURIER_GROUPS = 8
FOURIER_GROUP_DIM = 128
FOURIER_WIDTH = FOURIER_GROUPS * FOURIER_GROUP_DIM
DIL_PAIRS = ((128, 1), (512, 4), (2048, 16))
DIL_GROUPS = len(DIL_PAIRS)
DIL_HEAD_DIM = 64
DIL_HEADS = 2 * DIL_GROUPS
DIL_WIDTH = DIL_HEADS * DIL_HEAD_DIM
DIL_OUT_WIDTH = 2 * DIL_HEAD_DIM
DIL_RADIUS = 64
DIFF_HEADS = 4
DIFF_HEAD_DIM = 64
DIFF_QK_WIDTH = DIFF_HEADS * 2 * DIFF_HEAD_DIM
DIFF_V_DIM = 2 * DIFF_HEAD_DIM
DIFF_V_WIDTH = DIFF_HEADS * DIFF_V_DIM
SUBLN_EPS = 1e-5
NUM_BUCKETS = 32
MAX_DISTANCE = 1024
N_BIAS_HEADS = DIL_HEADS + DIFF_HEADS
NEG_INF = -1e30
QK_SCALE = DIL_HEAD_DIM ** -0.5
LOG2_E = math.log2(math.e)
B_WIDTH = 3 * DIL_WIDTH
C_WIDTH = 2 * DIFF_QK_WIDTH + DIFF_V_WIDTH
IN_WIDTH = FOURIER_WIDTH + B_WIDTH + C_WIDTH
GATE_WIDTH = 3 * D_MODEL

LANES = 128
BF16_SUBLANES = 16
VMEM_BYTES_V7X = 64 * 1024 * 1024
VMEM_LIMIT = 56 * 1024 * 1024

FFN_TM = 1024
FF_CHUNK = 256
PROJ_TM = 512
PROJ_CHUNK = 512
MERGE_TM = 512
FOURIER_RADIX = 4
QUARTER_SEQ = SEQ // FOURIER_RADIX
FOURIER_CN = 512
DIL_QKV = 3 * LANES
DIL_QB = 128
DIL_KW = DIL_QB + 2 * DIL_RADIUS
DIL_UNROLL = 8
DIFF_TQ = 512
DIFF_STRIP = 2 * SEQ - DIFF_TQ
DIFF_RING = 2 * SEQ
DIFF_KT = 512


def _params(*semantics):
    return pltpu.CompilerParams(dimension_semantics=semantics, vmem_limit_bytes=VMEM_LIMIT)


def _resident(shape):
    nd = len(shape)
    return pl.BlockSpec(shape, lambda *_: (0,) * nd, pipeline_mode=pl.Buffered(1))


def _rms(x, g, eps):
    return x * lax.rsqrt(jnp.mean(x * x, axis=-1, keepdims=True) + eps) * g


def _cast_specs(jobs, steps):
    in_specs, out_specs, out_shapes = [], [], []
    for src, layer, plan in jobs:
        _, rows, cols = src.shape
        out_cols = cols if plan is None else sum(width for _, width, _ in plan)
        share = 1
        while rows % (steps // share * BF16_SUBLANES):
            share *= 2
        blk = rows // (steps // share)
        in_specs.append(pl.BlockSpec((None, blk, cols),
                                     lambda i, layer=layer, share=share: (layer, i // share, 0)))
        out_specs.append(pl.BlockSpec((blk, out_cols), lambda i, share=share: (i // share, 0)))
        out_shapes.append(jax.ShapeDtypeStruct((rows, out_cols), BF16))
    return in_specs, out_specs, out_shapes


def _cast_blocks(src_refs, dst_refs, plans):
    for src, dst, plan in zip(src_refs, dst_refs, plans):
        if plan is None:
            dst[...] = src[...].astype(BF16)
            continue
        col = 0
        for src_col, width, scale in plan:
            part = src[:, src_col:src_col + width]
            dst[:, col:col + width] = (part if scale == 1.0 else part * scale).astype(BF16)
            col += width


def _cast_args(jobs):
    return [src for src, _, _ in jobs], tuple(plan for _, _, plan in jobs)


def _bucket_breakpoints():
    rel = np.arange(-(SEQ - 1), SEQ)
    half = NUM_BUCKETS // 2
    max_exact = half // 2
    n = np.abs(rel)
    nf = np.maximum(n, 1).astype(np.float64)
    large = max_exact + (np.log(nf / max_exact) / math.log(MAX_DISTANCE / max_exact)
                         * (half - max_exact)).astype(np.int32)
    large = np.minimum(large, half - 1)
    b = np.where(rel > 0, half, 0) + np.where(n < max_exact, n, large)
    pts = [(int(rel[i]), int(b[i])) for i in range(1, len(rel)) if b[i] != b[i - 1]]
    return int(b[0]), pts


def _bias_values(tbl_ref, head, rel):
    first, pts = _bucket_breakpoints()
    val = jnp.full(rel.shape, tbl_ref[first, head], F32)
    for thr, bucket in pts:
        val = jnp.where(rel >= thr, tbl_ref[bucket, head], val)
    return val * LOG2_E


def _diff_bias_kernel(tbl_ref, o_ref):
    rows, cols = o_ref.shape[1], o_ref.shape[2]
    ring = DIFF_RING
    u = lax.broadcasted_iota(jnp.int32, (8, ring), 1)
    offset = jnp.where(u < cols, u, u - ring)
    line = _bias_values(tbl_ref, DIL_HEADS + pl.program_id(0), offset - (SEQ - rows))
    tiled = jnp.broadcast_to(line[:1], (rows, ring))
    o_ref[0] = pltpu.roll(tiled, 0, 1, stride=1, stride_axis=0)[:, :cols]


def _diff_bias_strip(rel_bias):
    return pl.pallas_call(
        _diff_bias_kernel,
        grid=(DIFF_HEADS,),
        in_specs=[pl.BlockSpec(memory_space=pltpu.SMEM)],
        out_specs=pl.BlockSpec((1, DIFF_TQ, DIFF_STRIP), lambda h: (h, 0, 0)),
        out_shape=jax.ShapeDtypeStruct((DIFF_HEADS, DIFF_TQ, DIFF_STRIP), F32),
        compiler_params=_params("arbitrary"),
        name="diff_bias_strip",
    )(rel_bias)


EDGE_FIRST, EDGE_NONE, EDGE_LAST, EDGE_BOTH = range(4)


def _dil_bias_kernel(tbl_ref, o_ref):
    head = pl.program_id(0)
    group = head // 2
    dil = DIL_PAIRS[0][1]
    for g in range(1, DIL_GROUPS):
        dil = jnp.where(group == g, DIL_PAIRS[g][1], dil)
    rows, cols = o_ref.shape[3], o_ref.shape[4]
    col = lax.broadcasted_iota(jnp.int32, (rows, cols), 1)
    rel_t = col - lax.broadcasted_iota(jnp.int32, (rows, cols), 0) - DIL_RADIUS
    val = _bias_values(tbl_ref, head, rel_t * dil)
    val = jnp.where(jnp.abs(rel_t) <= DIL_RADIUS, val, NEG_INF)
    lo = jnp.where(col >= DIL_RADIUS, val, NEG_INF)
    o_ref[0, EDGE_FIRST, 0] = lo
    o_ref[0, EDGE_NONE, 0] = val
    o_ref[0, EDGE_LAST, 0] = jnp.where(col < cols - DIL_RADIUS, val, NEG_INF)
    o_ref[0, EDGE_BOTH, 0] = jnp.where(col < cols - DIL_RADIUS, lo, NEG_INF)


def _dil_bias_tiles(rel_bias):
    return pl.pallas_call(
        _dil_bias_kernel,
        grid=(DIL_HEADS,),
        in_specs=[pl.BlockSpec(memory_space=pltpu.SMEM)],
        out_specs=pl.BlockSpec((1, 4, 1, DIL_QB, DIL_KW), lambda h: (h // 2, 0, h % 2, 0, 0)),
        out_shape=jax.ShapeDtypeStruct((DIL_GROUPS, 4, 2, DIL_QB, DIL_KW), F32),
        compiler_params=_params("arbitrary"),
        name="dil_bias_tiles",
    )(rel_bias)


def _swiglu_residual(x, g_ref, wg_ref, wu_ref, wd_ref, a_ref):
    h = _rms(x, g_ref[...], EPS).astype(BF16)
    for c in range(D_FF // FF_CHUNK):
        sl = slice(c * FF_CHUNK, (c + 1) * FF_CHUNK)
        gate = jnp.dot(h, wg_ref[:, sl], preferred_element_type=F32)
        up = jnp.dot(h, wu_ref[:, sl], preferred_element_type=F32)
        a_ref[:, sl] = (gate * jax.nn.sigmoid(gate) * up).astype(BF16)
    return x + 0.5 * jnp.dot(a_ref[...], wd_ref[...], preferred_element_type=F32)


def _ffn_kernel(x_ref, g_ref, wg_ref, wu_ref, wd_ref, *rest, cast_plans):
    n_cast = len(cast_plans)
    cast_src, o_ref, cast_dst, a_ref = (rest[:n_cast], rest[n_cast], rest[n_cast + 1:-1], rest[-1])
    _cast_blocks(cast_src, cast_dst, cast_plans)
    o_ref[...] = _swiglu_residual(x_ref[...], g_ref, wg_ref, wu_ref, wd_ref, a_ref)


def _ffn(x, g, wg, wu, wd, cast_jobs=()):
    steps = TOKENS // FFN_TM
    tile = pl.BlockSpec((FFN_TM, D_MODEL), lambda i: (i, 0))
    cast_in, cast_out, cast_shapes = _cast_specs(cast_jobs, steps)
    cast_srcs, cast_plans = _cast_args(cast_jobs)
    return pl.pallas_call(
        functools.partial(_ffn_kernel, cast_plans=cast_plans),
        grid=(steps,),
        in_specs=[tile, _resident((1, D_MODEL)), _resident((D_MODEL, D_FF)),
                  _resident((D_MODEL, D_FF)), _resident((D_FF, D_MODEL))] + cast_in,
        out_specs=[tile] + cast_out,
        out_shape=[jax.ShapeDtypeStruct((TOKENS, D_MODEL), F32)] + cast_shapes,
        scratch_shapes=[pltpu.VMEM((FFN_TM, D_FF), BF16)],
        compiler_params=_params("arbitrary"),
        name="ffn",
    )(x, g.reshape(1, D_MODEL), wg, wu, wd, *cast_srcs)


def _proj_kernel(x_ref, g_ref, w_ref, wg_ref, bg_ref, *rest, cast_plans):
    n_cast = len(cast_plans)
    cast_src, rest = rest[:n_cast], rest[n_cast:]
    pf_ref, pb0_ref, pb1_ref, pb2_ref, pc_ref, gate_ref = rest[:6]
    cast_dst, y_ref = rest[6:-1], rest[-1]
    _cast_blocks(cast_src, cast_dst, cast_plans)
    u = _rms(x_ref[...], g_ref[...], EPS).astype(BF16)
    rows = x_ref.shape[0]

    def chunks(width):
        return [slice(c0, min(c0 + PROJ_CHUNK, width)) for c0 in range(0, width, PROJ_CHUNK)]

    def project(w, col0, sl):
        return jnp.dot(u, w[:, col0 + sl.start:col0 + sl.stop], preferred_element_type=F32)

    def by_residue(y, o_ref, dil, col0):
        for j in range(y.shape[1] // LANES):
            y_ref[j] = y[:, j * LANES:(j + 1) * LANES]
            cols = slice(col0 + j * LANES, col0 + (j + 1) * LANES)
            for r in range(dil):
                o_ref[0, r, :, cols] = y_ref[j, pl.ds(r, rows // dil, stride=dil), :].astype(o_ref.dtype)

    for sl in chunks(GATE_WIDTH):
        gate_ref[:, sl] = jax.nn.sigmoid(project(wg_ref, 0, sl) + bg_ref[:, sl]).astype(gate_ref.dtype)
    for sl in chunks(FOURIER_WIDTH):
        by_residue(project(w_ref, 0, sl), pf_ref, FOURIER_RADIX, sl.start)
    y = project(w_ref, FOURIER_WIDTH, slice(0, B_WIDTH))
    pb0_ref[...] = y[:, :DIL_QKV].astype(pb0_ref.dtype)
    by_residue(y[:, DIL_QKV:2 * DIL_QKV], pb1_ref, DIL_PAIRS[1][1], 0)
    by_residue(y[:, 2 * DIL_QKV:], pb2_ref, DIL_PAIRS[2][1], 0)
    for sl in chunks(C_WIDTH):
        pc_ref[:, sl] = project(w_ref, FOURIER_WIDTH + B_WIDTH, sl).astype(pc_ref.dtype)


def _in_weight_plan():
    q_scale = QK_SCALE * LOG2_E
    plan = [(0, FOURIER_WIDTH, 1.0)]
    for g in range(DIL_GROUPS):
        lo = FOURIER_WIDTH + g * LANES
        plan += [(lo, LANES, q_scale), (lo + DIL_WIDTH, LANES, 1.0), (lo + 2 * DIL_WIDTH, LANES, 1.0)]
    c3 = FOURIER_WIDTH + B_WIDTH
    plan += [(c3, DIFF_QK_WIDTH, q_scale), (c3 + DIFF_QK_WIDTH, DIFF_QK_WIDTH + DIFF_V_WIDTH, 1.0)]
    return plan


def _proj(x, g, w, wg, bg, cast_jobs=()):
    tiles_per_seq = SEQ // PROJ_TM
    steps = TOKENS // PROJ_TM
    cast_in, cast_out, cast_shapes = _cast_specs(cast_jobs, steps)
    cast_srcs, cast_plans = _cast_args(cast_jobs)

    def tile(w):
        return pl.BlockSpec((PROJ_TM, w), lambda i: (i, 0))

    def residue_tile(dil, w):
        return pl.BlockSpec((1, dil, PROJ_TM // dil, w),
                            lambda i: (i // tiles_per_seq, 0, i % tiles_per_seq, 0))

    def residue_shape(dil, w):
        return jax.ShapeDtypeStruct((BATCH, dil, SEQ // dil, w), BF16)

    d1, d2 = DIL_PAIRS[1][1], DIL_PAIRS[2][1]
    return pl.pallas_call(
        functools.partial(_proj_kernel, cast_plans=cast_plans),
        grid=(steps,),
        in_specs=[tile(D_MODEL), _resident((1, D_MODEL)), _resident((D_MODEL, IN_WIDTH)),
                  _resident((D_MODEL, GATE_WIDTH)), _resident((1, GATE_WIDTH))] + cast_in,
        out_specs=[residue_tile(FOURIER_RADIX, FOURIER_WIDTH), tile(DIL_QKV),
                   residue_tile(d1, DIL_QKV),
                   residue_tile(d2, DIL_QKV), tile(C_WIDTH), tile(GATE_WIDTH)] + cast_out,
        out_shape=[residue_shape(FOURIER_RADIX, FOURIER_WIDTH),
                   jax.ShapeDtypeStruct((TOKENS, DIL_QKV), BF16),
                   residue_shape(d1, DIL_QKV), residue_shape(d2, DIL_QKV),
                   jax.ShapeDtypeStruct((TOKENS, C_WIDTH), BF16),
                   jax.ShapeDtypeStruct((TOKENS, GATE_WIDTH), BF16)] + cast_shapes,
        scratch_shapes=[pltpu.VMEM((max(PROJ_CHUNK, DIL_QKV) // LANES, PROJ_TM, LANES), F32)],
        compiler_params=_params("arbitrary"),
        name="proj_in",
    )(x, g.reshape(1, D_MODEL), w, wg, bg.reshape(1, GATE_WIDTH), *cast_srcs)


def _fourier_kernel(a_ref, m_ref, d_ref, o_ref):
    q = QUARTER_SEQ
    tc, ts = [], []
    for r in range(FOURIER_RADIX):
        t = jnp.dot(m_ref[r], a_ref[0, r], preferred_element_type=F32)
        tc.append(t[:q])
        ts.append(t[q:])
    c02p, c02m, c13p, c13m = tc[0] + tc[2], tc[0] - tc[2], tc[1] + tc[3], tc[1] - tc[3]
    s02p, s02m, s13p, s13m = ts[0] + ts[2], ts[0] - ts[2], ts[1] + ts[3], ts[1] - ts[3]
    cos_sums = (c02p + c13p, c02m - s13m, c02p - c13p, c02m + s13m)
    sin_sums = (s02p + s13p, s02m + c13m, s02p - s13p, s02m - c13m)
    for j in range(FOURIER_RADIX):
        p, s = cos_sums[j].astype(BF16), sin_sums[j].astype(BF16)
        for g in range(o_ref.shape[2] // LANES):
            cols = slice(g * LANES, (g + 1) * LANES)
            lhs = jnp.concatenate([p[:, cols], s[:, cols]], axis=1)
            o_ref[0, j * q:(j + 1) * q, cols] = jnp.dot(
                lhs, d_ref[...], preferred_element_type=F32).astype(o_ref.dtype)


def _fourier(pf, m_seq, d_chan):
    cn = FOURIER_CN
    return pl.pallas_call(
        _fourier_kernel,
        grid=(BATCH, FOURIER_WIDTH // cn),
        in_specs=[pl.BlockSpec((1, FOURIER_RADIX, QUARTER_SEQ, cn), lambda b, j: (b, 0, 0, j)),
                  _resident((FOURIER_RADIX, 2 * QUARTER_SEQ, QUARTER_SEQ)),
                  _resident((2 * FOURIER_GROUP_DIM, FOURIER_GROUP_DIM))],
        out_specs=pl.BlockSpec((1, SEQ, cn), lambda b, j: (b, 0, j)),
        out_shape=jax.ShapeDtypeStruct((BATCH, SEQ, FOURIER_WIDTH), BF16),
        compiler_params=_params("parallel", "parallel"),
        name="fourier_mix",
    )(pf, m_seq, d_chan)


def _dft_matrices():
    q = QUARTER_SEQ
    split = 32

    def table(mult, width):
        k = lax.broadcasted_iota(jnp.int32, (q, width), 0)
        j = lax.broadcasted_iota(jnp.int32, (q, width), 1)
        ang = ((k * j * mult) % q).astype(F32) * (2.0 * math.pi / q)
        return jnp.cos(ang), jnp.sin(ang)

    (ca, sa), (cb, sb) = table(split, q // split), table(1, split)
    cos_0 = (ca[:, :, None] * cb[:, None, :] - sa[:, :, None] * sb[:, None, :]).reshape(q, q)
    sin_0 = (sa[:, :, None] * cb[:, None, :] + ca[:, :, None] * sb[:, None, :]).reshape(q, q)
    mats = []
    for r in range(FOURIER_RADIX):
        phi = lax.broadcasted_iota(jnp.int32, (q, 1), 0).astype(F32) * (2.0 * math.pi * r / SEQ)
        cos_r = cos_0 * jnp.cos(phi) - sin_0 * jnp.sin(phi)
        sin_r = sin_0 * jnp.cos(phi) + cos_0 * jnp.sin(phi)
        mats.append(jnp.concatenate([cos_r, sin_r], axis=0))
    return jnp.stack(mats).astype(BF16)


def _channel_dft():
    gd = FOURIER_GROUP_DIM
    idx = np.arange(gd)
    ang = 2.0 * np.pi * ((idx[:, None] * idx[None, :]) % gd) / gd
    scale = 1.0 / math.sqrt(SEQ * gd)
    return jnp.asarray(np.concatenate([np.cos(ang), -np.sin(ang)]) * scale, F32).astype(BF16)


def _dil_block(q, kw, vw_aug, bias, head0_lanes):
    zero = jnp.zeros_like(q)
    pv, lse = [], []
    for hh in range(2):
        q_h = jnp.where(head0_lanes, q, zero) if hh == 0 else jnp.where(head0_lanes, zero, q)
        logits = lax.dot_general(q_h, kw, (((1,), (1,)), ((), ())),
                                 preferred_element_type=F32) + bias[hh]
        mx = jnp.max(logits, axis=-1, keepdims=True)
        e = jnp.exp2(logits - mx).astype(BF16)
        r = jnp.dot(e, vw_aug, preferred_element_type=F32)
        s = r[:, LANES:]
        pv.append(r[:, :LANES] / s)
        lse.append(mx + jnp.log2(s))
    return jnp.where(head0_lanes, pv[0], pv[1]), jnp.where(head0_lanes, lse[0], lse[1])


def _dil_kernel(g0_ref, g1_ref, g2_ref, bias_ref, out_ref,
                kpad_ref, vpad_ref, o_ref, lse_ref):
    pad = DIL_RADIUS
    head0_lanes = lax.broadcasted_iota(jnp.int32, (DIL_QB, LANES), 1) < DIL_HEAD_DIM
    zeros_k = jnp.zeros((pad, LANES), BF16)
    zeros_v = jnp.zeros((pad, 2 * LANES), BF16)
    q_sl, k_sl, v_sl = (slice(i * LANES, (i + 1) * LANES) for i in range(3))

    vpad_ref[pl.ds(pad, SEQ), LANES:] = jnp.ones((SEQ, LANES), BF16)

    def fill_padded(src, sub_len):
        kpad_ref[:pad, :] = zeros_k
        kpad_ref[pl.ds(pad, sub_len), :] = src[:, k_sl]
        kpad_ref[pl.ds(pad + sub_len, pad), :] = zeros_k
        vpad_ref[:pad, :] = zeros_v
        vpad_ref[pl.ds(pad, sub_len), :LANES] = src[:, v_sl]
        vpad_ref[pl.ds(pad + sub_len, pad), :] = zeros_v

    def store(group, rows, o, lse):
        o_ref[group, rows, :] = o
        lse_ref[group, rows, :] = lse

    def padded_block(group, q_src, blk, edge, dil, residue):
        r0 = pl.multiple_of(blk * DIL_QB, DIL_QB)
        o, lse = _dil_block(q_src[pl.ds(r0, DIL_QB), q_sl], kpad_ref[pl.ds(r0, DIL_KW), :],
                            vpad_ref[pl.ds(r0, DIL_KW), :],
                            [bias_ref[group, edge, 0], bias_ref[group, edge, 1]],
                            head0_lanes)
        rows = pl.ds(residue + r0 * dil, DIL_QB, stride=dil) if dil > 1 else pl.ds(r0, DIL_QB)
        store(group, rows, o, lse)

    fill_padded(g0_ref.at[0], SEQ)
    n_blocks0 = SEQ // DIL_QB

    def g0_body(i, carry):
        for n in range(DIL_UNROLL):
            blk = i * DIL_UNROLL + n
            edge = jnp.where(blk == 0, EDGE_FIRST,
                             jnp.where(blk == n_blocks0 - 1, EDGE_LAST, EDGE_NONE))
            padded_block(0, g0_ref.at[0], blk, edge, 1, 0)
        return carry

    lax.fori_loop(0, n_blocks0 // DIL_UNROLL, g0_body, 0)

    dil1 = DIL_PAIRS[1][1]
    len1 = SEQ // dil1
    n_blocks1 = len1 // DIL_QB
    per_iter1 = DIL_UNROLL // n_blocks1
    slot1 = len1 + 2 * pad
    for n in range(per_iter1):
        for ref, zeros in ((kpad_ref, zeros_k), (vpad_ref, zeros_v)):
            ref[n * slot1:n * slot1 + pad, :] = zeros
            ref[n * slot1 + pad + len1:(n + 1) * slot1, :] = zeros
        vpad_ref[n * slot1 + pad:n * slot1 + pad + len1, LANES:] = jnp.ones((len1, LANES), BF16)

    def g1_body(i, carry):
        for n in range(per_iter1):
            r = i * per_iter1 + n
            src = g1_ref.at[0, r]
            base = n * slot1
            kpad_ref[base + pad:base + pad + len1, :] = src[:, k_sl]
            vpad_ref[base + pad:base + pad + len1, :LANES] = src[:, v_sl]
            for blk in range(n_blocks1):
                edge = EDGE_FIRST if blk == 0 else EDGE_LAST if blk == n_blocks1 - 1 else EDGE_NONE
                r0 = blk * DIL_QB
                o, lse = _dil_block(src[r0:r0 + DIL_QB, q_sl],
                                    kpad_ref[base + r0:base + r0 + DIL_KW, :],
                                    vpad_ref[base + r0:base + r0 + DIL_KW, :],
                                    [bias_ref[1, edge, 0], bias_ref[1, edge, 1]], head0_lanes)
                store(1, pl.ds(r + r0 * dil1, DIL_QB, stride=dil1), o, lse)
        return carry

    lax.fori_loop(0, dil1 // per_iter1, g1_body, 0)

    dil2 = DIL_PAIRS[2][1]
    assert SEQ // dil2 == DIL_QB
    for n in range(DIL_UNROLL):
        base = n * DIL_KW
        for ref, zeros in ((kpad_ref, zeros_k), (vpad_ref, zeros_v)):
            ref[base:base + pad, :] = zeros
            ref[base + pad + DIL_QB:base + DIL_KW, :] = zeros
        vpad_ref[base + pad:base + pad + DIL_QB, LANES:] = jnp.ones((DIL_QB, LANES), BF16)

    def g2_body(i, carry):
        for n in range(DIL_UNROLL):
            r = i * DIL_UNROLL + n
            src = g2_ref.at[0, r]
            base = n * DIL_KW
            kpad_ref[base + pad:base + pad + DIL_QB, :] = src[:, k_sl]
            vpad_ref[base + pad:base + pad + DIL_QB, :LANES] = src[:, v_sl]
            o, lse = _dil_block(src[:, q_sl], kpad_ref[base:base + DIL_KW, :],
                                vpad_ref[base:base + DIL_KW, :],
                                [bias_ref[2, EDGE_BOTH, 0], bias_ref[2, EDGE_BOTH, 1]],
                                head0_lanes)
            store(2, pl.ds(r, DIL_QB, stride=dil2), o, lse)
        return carry

    lax.fori_loop(0, dil2 // DIL_UNROLL, g2_body, 0)

    chunk = 256
    for c in range(SEQ // chunk):
        rows = slice(c * chunk, (c + 1) * chunk)
        l0, l1, l2 = lse_ref[0, rows, :], lse_ref[1, rows, :], lse_ref[2, rows, :]
        mx = jnp.maximum(jnp.maximum(l0, l1), l2)
        e0, e1, e2 = jnp.exp2(l0 - mx), jnp.exp2(l1 - mx), jnp.exp2(l2 - mx)
        mixed = (e0 * o_ref[0, rows, :] + e1 * o_ref[1, rows, :] + e2 * o_ref[2, rows, :]) / (e0 + e1 + e2)
        out_ref[0, rows, :] = mixed.astype(out_ref.dtype)


def _dilated_mixture(pb0, pb1, pb2, biases):
    d1, d2 = DIL_PAIRS[1][1], DIL_PAIRS[2][1]
    bias_spec = _resident((DIL_GROUPS, 4, 2, DIL_QB, DIL_KW))
    max_len = SEQ + 2 * DIL_RADIUS
    return pl.pallas_call(
        _dil_kernel,
        grid=(BATCH,),
        in_specs=[pl.BlockSpec((1, SEQ, DIL_QKV), lambda b: (b, 0, 0)),
                  pl.BlockSpec((1, d1, SEQ // d1, DIL_QKV), lambda b: (b, 0, 0, 0)),
                  pl.BlockSpec((1, d2, SEQ // d2, DIL_QKV), lambda b: (b, 0, 0, 0)),
                  bias_spec],
        out_specs=pl.BlockSpec((1, SEQ, LANES), lambda b: (b, 0, 0)),
        out_shape=jax.ShapeDtypeStruct((BATCH, SEQ, DIL_OUT_WIDTH), BF16),
        scratch_shapes=[pltpu.VMEM((max_len, LANES), BF16), pltpu.VMEM((max_len, 2 * LANES), BF16),
                        pltpu.VMEM((DIL_GROUPS, SEQ, LANES), F32),
                        pltpu.VMEM((DIL_GROUPS, SEQ, LANES), F32)],
        compiler_params=_params("parallel"),
        name="dilated_attn",
    )(pb0, pb1, pb2, biases)


def _diff_kernel(q_ref, k_ref, v_ref, bias_ref, lq1_ref, lk1_ref, lq2_ref, lk2_ref, sg_ref,
                 o_ref, vaug_ref, logit_ref, rmax_ref, *, lam_init):
    nq = SEQ // DIFF_TQ
    n_tiles = SEQ // DIFF_KT
    vaug_ref[:, :LANES] = v_ref[0]
    vaug_ref[:, LANES:] = jnp.ones((SEQ, LANES), BF16)
    lam = (jnp.exp(jnp.sum(lq1_ref[...] * lk1_ref[...], axis=-1, keepdims=True))
           - jnp.exp(jnp.sum(lq2_ref[...] * lk2_ref[...], axis=-1, keepdims=True))
           + lam_init)

    def key_tile(t):
        return slice(t * DIFF_KT, (t + 1) * DIFF_KT)

    def rows(j):
        return pl.ds(pl.multiple_of(j * DIFF_TQ, DIFF_TQ), DIFF_TQ)

    def pass1(j):
        slot = j % 2
        off = (nq - 1 - j) * DIFF_TQ
        q = q_ref[0, rows(j), :]
        first_half = lax.broadcasted_iota(jnp.int32, q.shape, 1) < DIFF_HEAD_DIM
        zero = jnp.zeros_like(q)
        for m in range(2):
            q_m = jnp.where(first_half, q, zero) if m == 0 else jnp.where(first_half, zero, q)
            lane_max = None
            for t in range(n_tiles):
                bias = bias_ref[0, :, pl.ds(pl.multiple_of(off + t * DIFF_KT, LANES), DIFF_KT)]
                logits = lax.dot_general(q_m, k_ref[0, key_tile(t), :], (((1,), (1,)), ((), ())),
                                         preferred_element_type=F32) + bias
                logit_ref[slot, m, :, key_tile(t)] = logits
                for c in range(DIFF_KT // LANES):
                    part = logits[:, c * LANES:(c + 1) * LANES]
                    lane_max = part if lane_max is None else jnp.maximum(lane_max, part)
            rmax_ref[slot, m] = jnp.broadcast_to(jnp.max(lane_max, axis=-1, keepdims=True),
                                                 (DIFF_TQ, LANES))

    def pass2(j):
        slot = j % 2
        outs = []
        for m in range(2):
            row_max = rmax_ref[slot, m]
            pv = jnp.zeros((DIFF_TQ, 2 * LANES), F32)
            for t in range(n_tiles):
                old = logit_ref[slot, m, :, key_tile(t)]
                e = jnp.concatenate([jnp.exp2(old[:, c * LANES:(c + 1) * LANES] - row_max)
                                     for c in range(DIFF_KT // LANES)], axis=1).astype(BF16)
                pv = pv + jnp.dot(e, vaug_ref[key_tile(t), :], preferred_element_type=F32)
            outs.append(pv[:, :LANES] / pv[:, LANES:])
        o = outs[0] - lam * outs[1]
        o = _rms(o, sg_ref[...], SUBLN_EPS) * (1.0 - lam_init)
        o_ref[0, rows(j), :] = o.astype(o_ref.dtype)

    def steady(j, carry):
        pass2(j - 1)
        pass1(j)
        return carry

    pass1(0)
    lax.fori_loop(1, nq, steady, 0)
    pass2(nq - 1)


def _diff_attention(pc, strip, lq1, lk1, lq2, lk2, subln_g, lam_init):
    pcv = pc.reshape(BATCH, SEQ, C_WIDTH)
    vec = _resident((1, DIFF_HEAD_DIM))

    def lane_block(part):
        return pl.BlockSpec((1, SEQ, LANES), lambda h, b: (b, 0, part * DIFF_HEADS + h))

    return pl.pallas_call(
        functools.partial(_diff_kernel, lam_init=lam_init),
        grid=(DIFF_HEADS, BATCH),
        in_specs=[lane_block(0), lane_block(1), lane_block(2),
                  pl.BlockSpec((1, DIFF_TQ, DIFF_STRIP), lambda h, b: (h, 0, 0)),
                  vec, vec, vec, vec, _resident((1, DIFF_V_DIM))],
        out_specs=lane_block(0),
        out_shape=jax.ShapeDtypeStruct((BATCH, SEQ, DIFF_V_WIDTH), BF16),
        scratch_shapes=[pltpu.VMEM((SEQ, 2 * LANES), BF16),
                        pltpu.VMEM((2, 2, DIFF_TQ, SEQ), F32),
                        pltpu.VMEM((2, 2, DIFF_TQ, LANES), F32)],
        compiler_params=_params("parallel", "parallel"),
        name="diff_attn",
    )(pcv, pcv, pcv, strip, lq1.reshape(1, -1), lk1.reshape(1, -1), lq2.reshape(1, -1),
      lk2.reshape(1, -1), subln_g.reshape(1, -1)).reshape(TOKENS, DIFF_V_WIDTH)


def _merge_ffn_kernel(x_ref, f_ref, b_ref, c_ref, gate_ref, wa_ref, wb_ref, wc_ref, wo_ref,
                      g_ref, wg_ref, wu_ref, wd_ref, *rest, final_norm):
    rest = list(rest)
    gf_ref = rest.pop(0) if final_norm else None
    out_ref, a_ref = rest
    y_a = jnp.dot(f_ref[...], wa_ref[...], preferred_element_type=F32)
    y_b = jnp.dot(b_ref[...], wb_ref[...], preferred_element_type=F32)
    y_c = jnp.dot(c_ref[...], wc_ref[...], preferred_element_type=F32)
    merged = (gate_ref[:, :D_MODEL].astype(F32) * y_a
              + gate_ref[:, D_MODEL:2 * D_MODEL].astype(F32) * y_b
              + gate_ref[:, 2 * D_MODEL:].astype(F32) * y_c)
    x = x_ref[...] + jnp.dot(merged.astype(BF16), wo_ref[...], preferred_element_type=F32)
    y = _swiglu_residual(x, g_ref, wg_ref, wu_ref, wd_ref, a_ref)
    if final_norm:
        y = _rms(y, gf_ref[...], EPS)
    out_ref[...] = y


def _merge_ffn(x, f, b, c, gates, wa, wb, wc, wo, g, wg, wu, wd, g_final=None):
    def tile(w):
        return pl.BlockSpec((MERGE_TM, w), lambda i: (i, 0))

    final_norm = g_final is not None
    weights = [wa, wb, wc, wo]
    gain = _resident((1, D_MODEL))
    in_specs = ([tile(D_MODEL), tile(FOURIER_WIDTH), tile(DIL_OUT_WIDTH), tile(DIFF_V_WIDTH),
                 tile(GATE_WIDTH)] + [_resident(w.shape) for w in weights]
                + [gain, _resident(wg.shape), _resident(wu.shape), _resident(wd.shape)])
    args = [x, f, b, c, gates, *weights, g.reshape(1, D_MODEL), wg, wu, wd]
    if final_norm:
        in_specs.append(gain)
        args.append(g_final.reshape(1, D_MODEL))
    return pl.pallas_call(
        functools.partial(_merge_ffn_kernel, final_norm=final_norm),
        grid=(TOKENS // MERGE_TM,),
        in_specs=in_specs,
        out_specs=tile(D_MODEL),
        out_shape=jax.ShapeDtypeStruct((TOKENS, D_MODEL), F32),
        scratch_shapes=[pltpu.VMEM((MERGE_TM, D_FF), BF16)],
        compiler_params=_params("parallel"),
        name="merge_ffn_final" if final_norm else "merge_ffn",
    )(*args)


def kernel(x, g_ffn1, w_ffn1_gate, w_ffn1_up, w_ffn1_down, g_mix, w_in, w_gate, b_gate,
           w_br_a, w_br_b, w_br_c, w_out, lam_q1, lam_k1, lam_q2, lam_k2, subln_g,
           rel_bias, g_ffn2, w_ffn2_gate, w_ffn2_up, w_ffn2_down, g_final):
    assert x.shape == (BATCH, SEQ, D_MODEL) and x.dtype == F32
    assert w_in.shape == (DEPTH, D_MODEL, IN_WIDTH)
    bf = lambda w: w.astype(BF16)

    dil_bias = _dil_bias_tiles(rel_bias)
    diff_strip = _diff_bias_strip(rel_bias)
    m_seq = _dft_matrices()
    d_chan = _channel_dft()
    in_plan = _in_weight_plan()

    h = x.reshape(TOKENS, D_MODEL)
    ffn1_w = [bf(w_ffn1_gate[0]), bf(w_ffn1_up[0]), bf(w_ffn1_down[0])]
    for l in range(DEPTH):
        last = l == DEPTH - 1
        h, wa, wc_br, wo, gate_w, in_w = _ffn(
            h, g_ffn1[l], *ffn1_w,
            cast_jobs=[(w_br_a, l, None), (w_br_c, l, None), (w_out, l, None),
                       (w_gate, l, None), (w_in, l, in_plan)])

        next_ffn1 = [] if last else [(w_ffn1_gate, l + 1, None), (w_ffn1_up, l + 1, None),
                                     (w_ffn1_down, l + 1, None)]
        pf, pb0, pb1, pb2, pc, gates, *ffn_w = _proj(
            h, g_mix[l], in_w, gate_w, b_gate[l],
            cast_jobs=[(w_ffn2_gate, l, None), (w_ffn2_up, l, None), (w_ffn2_down, l, None)]
            + next_ffn1)
        ffn2_w, ffn1_w = ffn_w[:3], ffn_w[3:]

        f = _fourier(pf, m_seq, d_chan).reshape(TOKENS, FOURIER_WIDTH)
        mixed = _dilated_mixture(pb0.reshape(BATCH, SEQ, DIL_QKV), pb1, pb2, dil_bias)
        lam_init = 0.8 - 0.6 * math.exp(-0.3 * l)
        c = _diff_attention(pc, diff_strip, lam_q1[l], lam_k1[l], lam_q2[l], lam_k2[l],
                            subln_g[l], lam_init)

        h = _merge_ffn(h, f, mixed.reshape(TOKENS, DIL_OUT_WIDTH), c, gates, wa, bf(w_br_b[l]),
                       wc_br, wo, g_ffn2[l], *ffn2_w, g_final if last else None)
    return h.reshape(BATCH, SEQ, D_MODEL)
```

```python
import functools
import math

import numpy as np
import jax
import jax.numpy as jnp
from jax import lax
from jax.experimental import pallas as pl
from jax.experimental.pallas import tpu as pltpu

F32 = jnp.float32
BF16 = jnp.bfloat16

D_MODEL = 1024
BATCH = 8
SEQ = 2048
DEPTH = 2
TOKENS = BATCH * SEQ
D_FF = 2816
EPS = 1e-6
FOURIER_GROUPS = 8
FOURIER_GROUP_DIM = 128
FOURIER_WIDTH = FOURIER_GROUPS * FOURIER_GROUP_DIM
DIL_PAIRS = ((128, 1), (512, 4), (2048, 16))
DIL_GROUPS = len(DIL_PAIRS)
DIL_HEAD_DIM = 64
DIL_HEADS = 2 * DIL_GROUPS
DIL_WIDTH = DIL_HEADS * DIL_HEAD_DIM
DIL_OUT_WIDTH = 2 * DIL_HEAD_DIM
DIL_RADIUS = 64
DIFF_HEADS = 4
DIFF_HEAD_DIM = 64
DIFF_QK_WIDTH = DIFF_HEADS * 2 * DIFF_HEAD_DIM
DIFF_V_DIM = 2 * DIFF_HEAD_DIM
DIFF_V_WIDTH = DIFF_HEADS * DIFF_V_DIM
SUBLN_EPS = 1e-5
NUM_BUCKETS = 32
MAX_DISTANCE = 1024
N_BIAS_HEADS = DIL_HEADS + DIFF_HEADS
NEG_INF = -1e30
QK_SCALE = DIL_HEAD_DIM ** -0.5
LOG2_E = math.log2(math.e)
B_WIDTH = 3 * DIL_WIDTH
C_WIDTH = 2 * DIFF_QK_WIDTH + DIFF_V_WIDTH
IN_WIDTH = FOURIER_WIDTH + B_WIDTH + C_WIDTH
GATE_WIDTH = 3 * D_MODEL

LANES = 128
BF16_SUBLANES = 16
VMEM_BYTES_V7X = 64 * 1024 * 1024
VMEM_LIMIT = 56 * 1024 * 1024

FFN_TM = 1024
FF_CHUNK = 256
PROJ_TM = 512
PROJ_CHUNK = 512
MERGE_TM = 512
FOURIER_RADIX = 4
QUARTER_SEQ = SEQ // FOURIER_RADIX
FOURIER_CN = 1024
DIL_QKV = 3 * LANES
DIL_QB = 128
DIL_KW = DIL_QB + 2 * DIL_RADIUS
DIL_UNROLL = 8
DIFF_TQ = 512
DIFF_STRIP = 2 * SEQ - DIFF_TQ
DIFF_RING = 2 * SEQ
DIFF_KT = 512


def _params(*semantics):
    return pltpu.CompilerParams(dimension_semantics=semantics, vmem_limit_bytes=VMEM_LIMIT)


def _resident(shape):
    nd = len(shape)
    return pl.BlockSpec(shape, lambda *_: (0,) * nd, pipeline_mode=pl.Buffered(1))


def _rms(x, g, eps):
    return x * lax.rsqrt(jnp.mean(x * x, axis=-1, keepdims=True) + eps) * g


def _cast_specs(jobs, steps):
    in_specs, out_specs, out_shapes = [], [], []
    for src, layer, plan in jobs:
        _, rows, cols = src.shape
        out_cols = cols if plan is None else sum(width for _, width, _ in plan)
        share = 1
        while rows % (steps // share * BF16_SUBLANES):
            share *= 2
        blk = rows // (steps // share)
        in_specs.append(pl.BlockSpec((None, blk, cols),
                                     lambda i, layer=layer, share=share: (layer, i // share, 0)))
        out_specs.append(pl.BlockSpec((blk, out_cols), lambda i, share=share: (i // share, 0)))
        out_shapes.append(jax.ShapeDtypeStruct((rows, out_cols), BF16))
    return in_specs, out_specs, out_shapes


def _cast_blocks(src_refs, dst_refs, plans):
    for src, dst, plan in zip(src_refs, dst_refs, plans):
        if plan is None:
            dst[...] = src[...].astype(BF16)
            continue
        col = 0
        for src_col, width, scale in plan:
            part = src[:, src_col:src_col + width]
            dst[:, col:col + width] = (part if scale == 1.0 else part * scale).astype(BF16)
            col += width


def _cast_args(jobs):
    return [src for src, _, _ in jobs], tuple(plan for _, _, plan in jobs)


def _bucket_breakpoints():
    rel = np.arange(-(SEQ - 1), SEQ)
    half = NUM_BUCKETS // 2
    max_exact = half // 2
    n = np.abs(rel)
    nf = np.maximum(n, 1).astype(np.float64)
    large = max_exact + (np.log(nf / max_exact) / math.log(MAX_DISTANCE / max_exact)
                         * (half - max_exact)).astype(np.int32)
    large = np.minimum(large, half - 1)
    b = np.where(rel > 0, half, 0) + np.where(n < max_exact, n, large)
    pts = [(int(rel[i]), int(b[i])) for i in range(1, len(rel)) if b[i] != b[i - 1]]
    return int(b[0]), pts


def _bias_values(tbl_ref, head, rel):
    first, pts = _bucket_breakpoints()
    val = jnp.full(rel.shape, tbl_ref[first, head], F32)
    for thr, bucket in pts:
        val = jnp.where(rel >= thr, tbl_ref[bucket, head], val)
    return val * LOG2_E


def _diff_bias_kernel(tbl_ref, o_ref):
    rows, cols = o_ref.shape[1], o_ref.shape[2]
    ring = DIFF_RING
    u = lax.broadcasted_iota(jnp.int32, (8, ring), 1)
    offset = jnp.where(u < cols, u, u - ring)
    line = _bias_values(tbl_ref, DIL_HEADS + pl.program_id(0), offset - (SEQ - rows))
    tiled = jnp.broadcast_to(line[:1], (rows, ring))
    o_ref[0] = pltpu.roll(tiled, 0, 1, stride=1, stride_axis=0)[:, :cols]


def _diff_bias_strip(rel_bias):
    return pl.pallas_call(
        _diff_bias_kernel,
        grid=(DIFF_HEADS,),
        in_specs=[pl.BlockSpec(memory_space=pltpu.SMEM)],
        out_specs=pl.BlockSpec((1, DIFF_TQ, DIFF_STRIP), lambda h: (h, 0, 0)),
        out_shape=jax.ShapeDtypeStruct((DIFF_HEADS, DIFF_TQ, DIFF_STRIP), F32),
        compiler_params=_params("arbitrary"),
        name="diff_bias_strip",
    )(rel_bias)


EDGE_FIRST, EDGE_NONE, EDGE_LAST, EDGE_BOTH = range(4)


def _dil_bias_kernel(tbl_ref, o_ref):
    head = pl.program_id(0)
    group = head // 2
    dil = DIL_PAIRS[0][1]
    for g in range(1, DIL_GROUPS):
        dil = jnp.where(group == g, DIL_PAIRS[g][1], dil)
    rows, cols = o_ref.shape[3], o_ref.shape[4]
    col = lax.broadcasted_iota(jnp.int32, (rows, cols), 1)
    rel_t = col - lax.broadcasted_iota(jnp.int32, (rows, cols), 0) - DIL_RADIUS
    val = _bias_values(tbl_ref, head, rel_t * dil)
    val = jnp.where(jnp.abs(rel_t) <= DIL_RADIUS, val, NEG_INF)
    lo = jnp.where(col >= DIL_RADIUS, val, NEG_INF)
    o_ref[0, EDGE_FIRST, 0] = lo
    o_ref[0, EDGE_NONE, 0] = val
    o_ref[0, EDGE_LAST, 0] = jnp.where(col < cols - DIL_RADIUS, val, NEG_INF)
    o_ref[0, EDGE_BOTH, 0] = jnp.where(col < cols - DIL_RADIUS, lo, NEG_INF)


def _dil_bias_tiles(rel_bias):
    return pl.pallas_call(
        _dil_bias_kernel,
        grid=(DIL_HEADS,),
        in_specs=[pl.BlockSpec(memory_space=pltpu.SMEM)],
        out_specs=pl.BlockSpec((1, 4, 1, DIL_QB, DIL_KW), lambda h: (h // 2, 0, h % 2, 0, 0)),
        out_shape=jax.ShapeDtypeStruct((DIL_GROUPS, 4, 2, DIL_QB, DIL_KW), F32),
        compiler_params=_params("arbitrary"),
        name="dil_bias_tiles",
    )(rel_bias)


def _swiglu_residual(x, g_ref, wg_ref, wu_ref, wd_ref, a_ref):
    h = _rms(x, g_ref[...], EPS).astype(BF16)
    for c in range(D_FF // FF_CHUNK):
        sl = slice(c * FF_CHUNK, (c + 1) * FF_CHUNK)
        gate = jnp.dot(h, wg_ref[:, sl], preferred_element_type=F32)
        up = jnp.dot(h, wu_ref[:, sl], preferred_element_type=F32)
        a_ref[:, sl] = (gate * jax.nn.sigmoid(gate) * up).astype(BF16)
    return x + 0.5 * jnp.dot(a_ref[...], wd_ref[...], preferred_element_type=F32)


def _ffn_kernel(x_ref, g_ref, wg_ref, wu_ref, wd_ref, *rest, cast_plans):
    n_cast = len(cast_plans)
    cast_src, o_ref, cast_dst, a_ref = (rest[:n_cast], rest[n_cast], rest[n_cast + 1:-1], rest[-1])
    _cast_blocks(cast_src, cast_dst, cast_plans)
    o_ref[...] = _swiglu_residual(x_ref[...], g_ref, wg_ref, wu_ref, wd_ref, a_ref)


def _ffn(x, g, wg, wu, wd, cast_jobs=()):
    steps = TOKENS // FFN_TM
    tile = pl.BlockSpec((FFN_TM, D_MODEL), lambda i: (i, 0))
    cast_in, cast_out, cast_shapes = _cast_specs(cast_jobs, steps)
    cast_srcs, cast_plans = _cast_args(cast_jobs)
    return pl.pallas_call(
        functools.partial(_ffn_kernel, cast_plans=cast_plans),
        grid=(steps,),
        in_specs=[tile, _resident((1, D_MODEL)), _resident((D_MODEL, D_FF)),
                  _resident((D_MODEL, D_FF)), _resident((D_FF, D_MODEL))] + cast_in,
        out_specs=[tile] + cast_out,
        out_shape=[jax.ShapeDtypeStruct((TOKENS, D_MODEL), F32)] + cast_shapes,
        scratch_shapes=[pltpu.VMEM((FFN_TM, D_FF), BF16)],
        compiler_params=_params("arbitrary"),
        name="ffn",
    )(x, g.reshape(1, D_MODEL), wg, wu, wd, *cast_srcs)


def _proj_kernel(x_ref, g_ref, w_ref, wg_ref, bg_ref, *rest, cast_plans):
    n_cast = len(cast_plans)
    cast_src, rest = rest[:n_cast], rest[n_cast:]
    pf_ref, pb0_ref, pb1_ref, pb2_ref, pc_ref, gate_ref = rest[:6]
    cast_dst, y_ref = rest[6:-1], rest[-1]
    _cast_blocks(cast_src, cast_dst, cast_plans)
    u = _rms(x_ref[...], g_ref[...], EPS).astype(BF16)
    rows = x_ref.shape[0]

    def chunks(width):
        return [slice(c0, min(c0 + PROJ_CHUNK, width)) for c0 in range(0, width, PROJ_CHUNK)]

    def project(w, col0, sl):
        return jnp.dot(u, w[:, col0 + sl.start:col0 + sl.stop], preferred_element_type=F32)

    def by_residue(y, o_ref, dil, col0):
        for j in range(y.shape[1] // LANES):
            y_ref[j] = y[:, j * LANES:(j + 1) * LANES]
            cols = slice(col0 + j * LANES, col0 + (j + 1) * LANES)
            for r in range(dil):
                o_ref[0, r, :, cols] = y_ref[j, pl.ds(r, rows // dil, stride=dil), :].astype(o_ref.dtype)

    for sl in chunks(GATE_WIDTH):
        gate_ref[:, sl] = jax.nn.sigmoid(project(wg_ref, 0, sl) + bg_ref[:, sl]).astype(gate_ref.dtype)
    for sl in chunks(FOURIER_WIDTH):
        by_residue(project(w_ref, 0, sl), pf_ref, FOURIER_RADIX, sl.start)
    y = project(w_ref, FOURIER_WIDTH, slice(0, B_WIDTH))
    pb0_ref[...] = y[:, :DIL_QKV].astype(pb0_ref.dtype)
    by_residue(y[:, DIL_QKV:2 * DIL_QKV], pb1_ref, DIL_PAIRS[1][1], 0)
    by_residue(y[:, 2 * DIL_QKV:], pb2_ref, DIL_PAIRS[2][1], 0)
    for sl in chunks(C_WIDTH):
        pc_ref[:, sl] = project(w_ref, FOURIER_WIDTH + B_WIDTH, sl).astype(pc_ref.dtype)


def _in_weight_plan():
    q_scale = QK_SCALE * LOG2_E
    plan = [(0, FOURIER_WIDTH, 1.0)]
    for g in range(DIL_GROUPS):
        lo = FOURIER_WIDTH + g * LANES
        plan += [(lo, LANES, q_scale), (lo + DIL_WIDTH, LANES, 1.0), (lo + 2 * DIL_WIDTH, LANES, 1.0)]
    c3 = FOURIER_WIDTH + B_WIDTH
    plan += [(c3, DIFF_QK_WIDTH, q_scale), (c3 + DIFF_QK_WIDTH, DIFF_QK_WIDTH + DIFF_V_WIDTH, 1.0)]
    return plan


def _proj(x, g, w, wg, bg, cast_jobs=()):
    tiles_per_seq = SEQ // PROJ_TM
    steps = TOKENS // PROJ_TM
    cast_in, cast_out, cast_shapes = _cast_specs(cast_jobs, steps)
    cast_srcs, cast_plans = _cast_args(cast_jobs)

    def tile(w):
        return pl.BlockSpec((PROJ_TM, w), lambda i: (i, 0))

    def residue_tile(dil, w):
        return pl.BlockSpec((1, dil, PROJ_TM // dil, w),
                            lambda i: (i // tiles_per_seq, 0, i % tiles_per_seq, 0))

    def residue_shape(dil, w):
        return jax.ShapeDtypeStruct((BATCH, dil, SEQ // dil, w), BF16)

    d1, d2 = DIL_PAIRS[1][1], DIL_PAIRS[2][1]
    return pl.pallas_call(
        functools.partial(_proj_kernel, cast_plans=cast_plans),
        grid=(steps,),
        in_specs=[tile(D_MODEL), _resident((1, D_MODEL)), _resident((D_MODEL, IN_WIDTH)),
                  _resident((D_MODEL, GATE_WIDTH)), _resident((1, GATE_WIDTH))] + cast_in,
        out_specs=[residue_tile(FOURIER_RADIX, FOURIER_WIDTH), tile(DIL_QKV),
                   residue_tile(d1, DIL_QKV),
                   residue_tile(d2, DIL_QKV), tile(C_WIDTH), tile(GATE_WIDTH)] + cast_out,
        out_shape=[residue_shape(FOURIER_RADIX, FOURIER_WIDTH),
                   jax.ShapeDtypeStruct((TOKENS, DIL_QKV), BF16),
                   residue_shape(d1, DIL_QKV), residue_shape(d2, DIL_QKV),
                   jax.ShapeDtypeStruct((TOKENS, C_WIDTH), BF16),
                   jax.ShapeDtypeStruct((TOKENS, GATE_WIDTH), BF16)] + cast_shapes,
        scratch_shapes=[pltpu.VMEM((max(PROJ_CHUNK, DIL_QKV) // LANES, PROJ_TM, LANES), F32)],
        compiler_params=_params("arbitrary"),
        name="proj_in",
    )(x, g.reshape(1, D_MODEL), w, wg, bg.reshape(1, GATE_WIDTH), *cast_srcs)


def _fourier_kernel(a_ref, m_ref, d_ref, o_ref):
    q = QUARTER_SEQ
    tc, ts = [], []
    for r in range(FOURIER_RADIX):
        t = jnp.dot(m_ref[r], a_ref[0, r], preferred_element_type=F32)
        tc.append(t[:q])
        ts.append(t[q:])
    c02p, c02m, c13p, c13m = tc[0] + tc[2], tc[0] - tc[2], tc[1] + tc[3], tc[1] - tc[3]
    s02p, s02m, s13p, s13m = ts[0] + ts[2], ts[0] - ts[2], ts[1] + ts[3], ts[1] - ts[3]
    cos_sums = (c02p + c13p, c02m - s13m, c02p - c13p, c02m + s13m)
    sin_sums = (s02p + s13p, s02m + c13m, s02p - s13p, s02m - c13m)
    for j in range(FOURIER_RADIX):
        p, s = cos_sums[j].astype(BF16), sin_sums[j].astype(BF16)
        for g in range(o_ref.shape[2] // LANES):
            cols = slice(g * LANES, (g + 1) * LANES)
            lhs = jnp.concatenate([p[:, cols], s[:, cols]], axis=1)
            o_ref[0, j * q:(j + 1) * q, cols] = jnp.dot(
                lhs, d_ref[...], preferred_element_type=F32).astype(o_ref.dtype)


def _fourier(pf, m_seq, d_chan):
    cn = FOURIER_CN
    return pl.pallas_call(
        _fourier_kernel,
        grid=(BATCH, FOURIER_WIDTH // cn),
        in_specs=[pl.BlockSpec((1, FOURIER_RADIX, QUARTER_SEQ, cn), lambda b, j: (b, 0, 0, j)),
                  _resident((FOURIER_RADIX, 2 * QUARTER_SEQ, QUARTER_SEQ)),
                  _resident((2 * FOURIER_GROUP_DIM, FOURIER_GROUP_DIM))],
        out_specs=pl.BlockSpec((1, SEQ, cn), lambda b, j: (b, 0, j)),
        out_shape=jax.ShapeDtypeStruct((BATCH, SEQ, FOURIER_WIDTH), BF16),
        compiler_params=_params("parallel", "parallel"),
        name="fourier_mix",
    )(pf, m_seq, d_chan)


def _dft_matrices():
    q = QUARTER_SEQ
    split = 32

    def table(mult, width):
        k = lax.broadcasted_iota(jnp.int32, (q, width), 0)
        j = lax.broadcasted_iota(jnp.int32, (q, width), 1)
        ang = ((k * j * mult) % q).astype(F32) * (2.0 * math.pi / q)
        return jnp.cos(ang), jnp.sin(ang)

    (ca, sa), (cb, sb) = table(split, q // split), table(1, split)
    cos_0 = (ca[:, :, None] * cb[:, None, :] - sa[:, :, None] * sb[:, None, :]).reshape(q, q)
    sin_0 = (sa[:, :, None] * cb[:, None, :] + ca[:, :, None] * sb[:, None, :]).reshape(q, q)
    mats = []
    for r in range(FOURIER_RADIX):
        phi = lax.broadcasted_iota(jnp.int32, (q, 1), 0).astype(F32) * (2.0 * math.pi * r / SEQ)
        cos_r = cos_0 * jnp.cos(phi) - sin_0 * jnp.sin(phi)
        sin_r = sin_0 * jnp.cos(phi) + cos_0 * jnp.sin(phi)
        mats.append(jnp.concatenate([cos_r, sin_r], axis=0))
    return jnp.stack(mats).astype(BF16)


def _channel_dft():
    gd = FOURIER_GROUP_DIM
    idx = np.arange(gd)
    ang = 2.0 * np.pi * ((idx[:, None] * idx[None, :]) % gd) / gd
    scale = 1.0 / math.sqrt(SEQ * gd)
    return jnp.asarray(np.concatenate([np.cos(ang), -np.sin(ang)]) * scale, F32).astype(BF16)


def _dil_block(q, kw, vw_aug, bias, head0_lanes):
    zero = jnp.zeros_like(q)
    pv, lse = [], []
    for hh in range(2):
        q_h = jnp.where(head0_lanes, q, zero) if hh == 0 else jnp.where(head0_lanes, zero, q)
        logits = lax.dot_general(q_h, kw, (((1,), (1,)), ((), ())),
                                 preferred_element_type=F32) + bias[hh]
        mx = jnp.max(logits, axis=-1, keepdims=True)
        e = jnp.exp2(logits - mx).astype(BF16)
        r = jnp.dot(e, vw_aug, preferred_element_type=F32)
        s = r[:, LANES:]
        pv.append(r[:, :LANES] / s)
        lse.append(mx + jnp.log2(s))
    return jnp.where(head0_lanes, pv[0], pv[1]), jnp.where(head0_lanes, lse[0], lse[1])


def _dil_kernel(g0_ref, g1_ref, g2_ref, bias_ref, out_ref,
                kpad_ref, vpad_ref, o_ref, lse_ref):
    pad = DIL_RADIUS
    head0_lanes = lax.broadcasted_iota(jnp.int32, (DIL_QB, LANES), 1) < DIL_HEAD_DIM
    zeros_k = jnp.zeros((pad, LANES), BF16)
    zeros_v = jnp.zeros((pad, 2 * LANES), BF16)
    q_sl, k_sl, v_sl = (slice(i * LANES, (i + 1) * LANES) for i in range(3))

    vpad_ref[pl.ds(pad, SEQ), LANES:] = jnp.ones((SEQ, LANES), BF16)

    def fill_padded(src, sub_len):
        kpad_ref[:pad, :] = zeros_k
        kpad_ref[pl.ds(pad, sub_len), :] = src[:, k_sl]
        kpad_ref[pl.ds(pad + sub_len, pad), :] = zeros_k
        vpad_ref[:pad, :] = zeros_v
        vpad_ref[pl.ds(pad, sub_len), :LANES] = src[:, v_sl]
        vpad_ref[pl.ds(pad + sub_len, pad), :] = zeros_v

    def store(group, rows, o, lse):
        o_ref[group, rows, :] = o
        lse_ref[group, rows, :] = lse

    def padded_block(group, q_src, blk, edge, dil, residue):
        r0 = pl.multiple_of(blk * DIL_QB, DIL_QB)
        o, lse = _dil_block(q_src[pl.ds(r0, DIL_QB), q_sl], kpad_ref[pl.ds(r0, DIL_KW), :],
                            vpad_ref[pl.ds(r0, DIL_KW), :],
                            [bias_ref[group, edge, 0], bias_ref[group, edge, 1]],
                            head0_lanes)
        rows = pl.ds(residue + r0 * dil, DIL_QB, stride=dil) if dil > 1 else pl.ds(r0, DIL_QB)
        store(group, rows, o, lse)

    fill_padded(g0_ref.at[0], SEQ)
    n_blocks0 = SEQ // DIL_QB

    def g0_body(i, carry):
        for n in range(DIL_UNROLL):
            blk = i * DIL_UNROLL + n
            edge = jnp.where(blk == 0, EDGE_FIRST,
                             jnp.where(blk == n_blocks0 - 1, EDGE_LAST, EDGE_NONE))
            padded_block(0, g0_ref.at[0], blk, edge, 1, 0)
        return carry

    lax.fori_loop(0, n_blocks0 // DIL_UNROLL, g0_body, 0)

    dil1 = DIL_PAIRS[1][1]
    len1 = SEQ // dil1
    n_blocks1 = len1 // DIL_QB
    per_iter1 = DIL_UNROLL // n_blocks1
    slot1 = len1 + 2 * pad
    for n in range(per_iter1):
        for ref, zeros in ((kpad_ref, zeros_k), (vpad_ref, zeros_v)):
            ref[n * slot1:n * slot1 + pad, :] = zeros
            ref[n * slot1 + pad + len1:(n + 1) * slot1, :] = zeros
        vpad_ref[n * slot1 + pad:n * slot1 + pad + len1, LANES:] = jnp.ones((len1, LANES), BF16)

    def g1_body(i, carry):
        for n in range(per_iter1):
            r = i * per_iter1 + n
            src = g1_ref.at[0, r]
            base = n * slot1
            kpad_ref[base + pad:base + pad + len1, :] = src[:, k_sl]
            vpad_ref[base + pad:base + pad + len1, :LANES] = src[:, v_sl]
            for blk in range(n_blocks1):
                edge = EDGE_FIRST if blk == 0 else EDGE_LAST if blk == n_blocks1 - 1 else EDGE_NONE
                r0 = blk * DIL_QB
                o, lse = _dil_block(src[r0:r0 + DIL_QB, q_sl],
                                    kpad_ref[base + r0:base + r0 + DIL_KW, :],
                                    vpad_ref[base + r0:base + r0 + DIL_KW, :],
                                    [bias_ref[1, edge, 0], bias_ref[1, edge, 1]], head0_lanes)
                store(1, pl.ds(r + r0 * dil1, DIL_QB, stride=dil1), o, lse)
        return carry

    lax.fori_loop(0, dil1 // per_iter1, g1_body, 0)

    dil2 = DIL_PAIRS[2][1]
    assert SEQ // dil2 == DIL_QB
    for n in range(DIL_UNROLL):
        base = n * DIL_KW
        for ref, zeros in ((kpad_ref, zeros_k), (vpad_ref, zeros_v)):
            ref[base:base + pad, :] = zeros
            ref[base + pad + DIL_QB:base + DIL_KW, :] = zeros
        vpad_ref[base + pad:base + pad + DIL_QB, LANES:] = jnp.ones((DIL_QB, LANES), BF16)

    def g2_body(i, carry):
        for n in range(DIL_UNROLL):
            r = i * DIL_UNROLL + n
            src = g2_ref.at[0, r]
            base = n * DIL_KW
            kpad_ref[base + pad:base + pad + DIL_QB, :] = src[:, k_sl]
            vpad_ref[base + pad:base + pad + DIL_QB, :LANES] = src[:, v_sl]
            o, lse = _dil_block(src[:, q_sl], kpad_ref[base:base + DIL_KW, :],
                                vpad_ref[base:base + DIL_KW, :],
                                [bias_ref[2, EDGE_BOTH, 0], bias_ref[2, EDGE_BOTH, 1]],
                                head0_lanes)
            store(2, pl.ds(r, DIL_QB, stride=dil2), o, lse)
        return carry

    lax.fori_loop(0, dil2 // DIL_UNROLL, g2_body, 0)

    chunk = 256
    for c in range(SEQ // chunk):
        rows = slice(c * chunk, (c + 1) * chunk)
        l0, l1, l2 = lse_ref[0, rows, :], lse_ref[1, rows, :], lse_ref[2, rows, :]
        mx = jnp.maximum(jnp.maximum(l0, l1), l2)
        e0, e1, e2 = jnp.exp2(l0 - mx), jnp.exp2(l1 - mx), jnp.exp2(l2 - mx)
        mixed = (e0 * o_ref[0, rows, :] + e1 * o_ref[1, rows, :] + e2 * o_ref[2, rows, :]) / (e0 + e1 + e2)
        out_ref[0, rows, :] = mixed.astype(out_ref.dtype)


def _dilated_mixture(pb0, pb1, pb2, biases):
    d1, d2 = DIL_PAIRS[1][1], DIL_PAIRS[2][1]
    bias_spec = _resident((DIL_GROUPS, 4, 2, DIL_QB, DIL_KW))
    max_len = SEQ + 2 * DIL_RADIUS
    return pl.pallas_call(
        _dil_kernel,
        grid=(BATCH,),
        in_specs=[pl.BlockSpec((1, SEQ, DIL_QKV), lambda b: (b, 0, 0)),
                  pl.BlockSpec((1, d1, SEQ // d1, DIL_QKV), lambda b: (b, 0, 0, 0)),
                  pl.BlockSpec((1, d2, SEQ // d2, DIL_QKV), lambda b: (b, 0, 0, 0)),
                  bias_spec],
        out_specs=pl.BlockSpec((1, SEQ, LANES), lambda b: (b, 0, 0)),
        out_shape=jax.ShapeDtypeStruct((BATCH, SEQ, DIL_OUT_WIDTH), BF16),
        scratch_shapes=[pltpu.VMEM((max_len, LANES), BF16), pltpu.VMEM((max_len, 2 * LANES), BF16),
                        pltpu.VMEM((DIL_GROUPS, SEQ, LANES), F32),
                        pltpu.VMEM((DIL_GROUPS, SEQ, LANES), F32)],
        compiler_params=_params("parallel"),
        name="dilated_attn",
    )(pb0, pb1, pb2, biases)


def _diff_kernel(q_ref, k_ref, v_ref, bias_ref, lq1_ref, lk1_ref, lq2_ref, lk2_ref, sg_ref,
                 o_ref, vaug_ref, logit_ref, rmax_ref, *, lam_init):
    nq = SEQ // DIFF_TQ
    n_tiles = SEQ // DIFF_KT
    vaug_ref[:, :LANES] = v_ref[0]
    vaug_ref[:, LANES:] = jnp.ones((SEQ, LANES), BF16)
    lam = (jnp.exp(jnp.sum(lq1_ref[...] * lk1_ref[...], axis=-1, keepdims=True))
           - jnp.exp(jnp.sum(lq2_ref[...] * lk2_ref[...], axis=-1, keepdims=True))
           + lam_init)

    def key_tile(t):
        return slice(t * DIFF_KT, (t + 1) * DIFF_KT)

    def rows(j):
        return pl.ds(pl.multiple_of(j * DIFF_TQ, DIFF_TQ), DIFF_TQ)

    def pass1(j):
        slot = j % 2
        off = (nq - 1 - j) * DIFF_TQ
        q = q_ref[0, rows(j), :]
        first_half = lax.broadcasted_iota(jnp.int32, q.shape, 1) < DIFF_HEAD_DIM
        zero = jnp.zeros_like(q)
        for m in range(2):
            q_m = jnp.where(first_half, q, zero) if m == 0 else jnp.where(first_half, zero, q)
            lane_max = None
            for t in range(n_tiles):
                bias = bias_ref[0, :, pl.ds(pl.multiple_of(off + t * DIFF_KT, LANES), DIFF_KT)]
                logits = lax.dot_general(q_m, k_ref[0, key_tile(t), :], (((1,), (1,)), ((), ())),
                                         preferred_element_type=F32) + bias
                logit_ref[slot, m, :, key_tile(t)] = logits
                for c in range(DIFF_KT // LANES):
                    part = logits[:, c * LANES:(c + 1) * LANES]
                    lane_max = part if lane_max is None else jnp.maximum(lane_max, part)
            rmax_ref[slot, m] = jnp.broadcast_to(jnp.max(lane_max, axis=-1, keepdims=True),
                                                 (DIFF_TQ, LANES))

    def pass2(j):
        slot = j % 2
        outs = []
        for m in range(2):
            row_max = rmax_ref[slot, m]
            pv = jnp.zeros((DIFF_TQ, 2 * LANES), F32)
            for t in range(n_tiles):
                old = logit_ref[slot, m, :, key_tile(t)]
                e = jnp.concatenate([jnp.exp2(old[:, c * LANES:(c + 1) * LANES] - row_max)
                                     for c in range(DIFF_KT // LANES)], axis=1).astype(BF16)
                pv = pv + jnp.dot(e, vaug_ref[key_tile(t), :], preferred_element_type=F32)
            outs.append(pv[:, :LANES] / pv[:, LANES:])
        o = outs[0] - lam * outs[1]
        o = _rms(o, sg_ref[...], SUBLN_EPS) * (1.0 - lam_init)
        o_ref[0, rows(j), :] = o.astype(o_ref.dtype)

    def steady(j, carry):
        pass2(j - 1)
        pass1(j)
        return carry

    pass1(0)
    lax.fori_loop(1, nq, steady, 0)
    pass2(nq - 1)


def _diff_attention(pc, strip, lq1, lk1, lq2, lk2, subln_g, lam_init):
    pcv = pc.reshape(BATCH, SEQ, C_WIDTH)
    vec = _resident((1, DIFF_HEAD_DIM))

    def lane_block(part):
        return pl.BlockSpec((1, SEQ, LANES), lambda h, b: (b, 0, part * DIFF_HEADS + h))

    return pl.pallas_call(
        functools.partial(_diff_kernel, lam_init=lam_init),
        grid=(DIFF_HEADS, BATCH),
        in_specs=[lane_block(0), lane_block(1), lane_block(2),
                  pl.BlockSpec((1, DIFF_TQ, DIFF_STRIP), lambda h, b: (h, 0, 0)),
                  vec, vec, vec, vec, _resident((1, DIFF_V_DIM))],
        out_specs=lane_block(0),
        out_shape=jax.ShapeDtypeStruct((BATCH, SEQ, DIFF_V_WIDTH), BF16),
        scratch_shapes=[pltpu.VMEM((SEQ, 2 * LANES), BF16),
                        pltpu.VMEM((2, 2, DIFF_TQ, SEQ), F32),
                        pltpu.VMEM((2, 2, DIFF_TQ, LANES), F32)],
        compiler_params=_params("parallel", "parallel"),
        name="diff_attn",
    )(pcv, pcv, pcv, strip, lq1.reshape(1, -1), lk1.reshape(1, -1), lq2.reshape(1, -1),
      lk2.reshape(1, -1), subln_g.reshape(1, -1)).reshape(TOKENS, DIFF_V_WIDTH)


def _merge_ffn_kernel(x_ref, f_ref, b_ref, c_ref, gate_ref, wa_ref, wb_ref, wc_ref, wo_ref,
                      g_ref, wg_ref, wu_ref, wd_ref, *rest, final_norm):
    rest = list(rest)
    gf_ref = rest.pop(0) if final_norm else None
    out_ref, a_ref = rest
    y_a = jnp.dot(f_ref[...], wa_ref[...], preferred_element_type=F32)
    y_b = jnp.dot(b_ref[...], wb_ref[...], preferred_element_type=F32)
    y_c = jnp.dot(c_ref[...], wc_ref[...], preferred_element_type=F32)
    merged = (gate_ref[:, :D_MODEL].astype(F32) * y_a
              + gate_ref[:, D_MODEL:2 * D_MODEL].astype(F32) * y_b
              + gate_ref[:, 2 * D_MODEL:].astype(F32) * y_c)
    x = x_ref[...] + jnp.dot(merged.astype(BF16), wo_ref[...], preferred_element_type=F32)
    y = _swiglu_residual(x, g_ref, wg_ref, wu_ref, wd_ref, a_ref)
    if final_norm:
        y = _rms(y, gf_ref[...], EPS)
    out_ref[...] = y


def _merge_ffn(x, f, b, c, gates, wa, wb, wc, wo, g, wg, wu, wd, g_final=None):
    def tile(w):
        return pl.BlockSpec((MERGE_TM, w), lambda i: (i, 0))

    final_norm = g_final is not None
    weights = [wa, wb, wc, wo]
    gain = _resident((1, D_MODEL))
    in_specs = ([tile(D_MODEL), tile(FOURIER_WIDTH), tile(DIL_OUT_WIDTH), tile(DIFF_V_WIDTH),
                 tile(GATE_WIDTH)] + [_resident(w.shape) for w in weights]
                + [gain, _resident(wg.shape), _resident(wu.shape), _resident(wd.shape)])
    args = [x, f, b, c, gates, *weights, g.reshape(1, D_MODEL), wg, wu, wd]
    if final_norm:
        in_specs.append(gain)
        args.append(g_final.reshape(1, D_MODEL))
    return pl.pallas_call(
        functools.partial(_merge_ffn_kernel, final_norm=final_norm),
        grid=(TOKENS // MERGE_TM,),
        in_specs=in_specs,
        out_specs=tile(D_MODEL),
        out_shape=jax.ShapeDtypeStruct((TOKENS, D_MODEL), F32),
        scratch_shapes=[pltpu.VMEM((MERGE_TM, D_FF), BF16)],
        compiler_params=_params("parallel"),
        name="merge_ffn_final" if final_norm else "merge_ffn",
    )(*args)


def kernel(x, g_ffn1, w_ffn1_gate, w_ffn1_up, w_ffn1_down, g_mix, w_in, w_gate, b_gate,
           w_br_a, w_br_b, w_br_c, w_out, lam_q1, lam_k1, lam_q2, lam_k2, subln_g,
           rel_bias, g_ffn2, w_ffn2_gate, w_ffn2_up, w_ffn2_down, g_final):
    assert x.shape == (BATCH, SEQ, D_MODEL) and x.dtype == F32
    assert w_in.shape == (DEPTH, D_MODEL, IN_WIDTH)
    bf = lambda w: w.astype(BF16)

    dil_bias = _dil_bias_tiles(rel_bias)
    diff_strip = _diff_bias_strip(rel_bias)
    m_seq = _dft_matrices()
    d_chan = _channel_dft()
    in_plan = _in_weight_plan()

    h = x.reshape(TOKENS, D_MODEL)
    ffn1_w = [bf(w_ffn1_gate[0]), bf(w_ffn1_up[0]), bf(w_ffn1_down[0])]
    for l in range(DEPTH):
        last = l == DEPTH - 1
        h, wa, wc_br, wo, gate_w, in_w = _ffn(
            h, g_ffn1[l], *ffn1_w,
            cast_jobs=[(w_br_a, l, None), (w_br_c, l, None), (w_out, l, None),
                       (w_gate, l, None), (w_in, l, in_plan)])

        next_ffn1 = [] if last else [(w_ffn1_gate, l + 1, None), (w_ffn1_up, l + 1, None),
                                     (w_ffn1_down, l + 1, None)]
        pf, pb0, pb1, pb2, pc, gates, *ffn_w = _proj(
            h, g_mix[l], in_w, gate_w, b_gate[l],
            cast_jobs=[(w_ffn2_gate, l, None), (w_ffn2_up, l, None), (w_ffn2_down, l, None)]
            + next_ffn1)
        ffn2_w, ffn1_w = ffn_w[:3], ffn_w[3:]

        f = _fourier(pf, m_seq, d_chan).reshape(TOKENS, FOURIER_WIDTH)
        mixed = _dilated_mixture(pb0.reshape(BATCH, SEQ, DIL_QKV), pb1, pb2, dil_bias)
        lam_init = 0.8 - 0.6 * math.exp(-0.3 * l)
        c = _diff_attention(pc, diff_strip, lam_q1[l], lam_k1[l], lam_q2[l], lam_k2[l],
                            subln_g[l], lam_init)

        h = _merge_ffn(h, f, mixed.reshape(TOKENS, DIL_OUT_WIDTH), c, gates, wa, bf(w_br_b[l]),
                       wc_br, wo, g_ffn2[l], *ffn2_w, g_final if last else None)
    return h.reshape(BATCH, SEQ, D_MODEL)
```

```python
import functools
import math

import numpy as np
import jax
import jax.numpy as jnp
from jax import lax
from jax.experimental import pallas as pl
from jax.experimental.pallas import tpu as pltpu

F32 = jnp.float32
BF16 = jnp.bfloat16

D_MODEL = 1024
BATCH = 8
SEQ = 2048
DEPTH = 2
TOKENS = BATCH * SEQ
D_FF = 2816
EPS = 1e-6
FOURIER_GROUPS = 8
FOURIER_GROUP_DIM = 128
FOURIER_WIDTH = FOURIER_GROUPS * FOURIER_GROUP_DIM
DIL_PAIRS = ((128, 1), (512, 4), (2048, 16))
DIL_GROUPS = len(DIL_PAIRS)
DIL_HEAD_DIM = 64
DIL_HEADS = 2 * DIL_GROUPS
DIL_WIDTH = DIL_HEADS * DIL_HEAD_DIM
DIL_OUT_WIDTH = 2 * DIL_HEAD_DIM
DIL_RADIUS = 64
DIFF_HEADS = 4
DIFF_HEAD_DIM = 64
DIFF_QK_WIDTH = DIFF_HEADS * 2 * DIFF_HEAD_DIM
DIFF_V_DIM = 2 * DIFF_HEAD_DIM
DIFF_V_WIDTH = DIFF_HEADS * DIFF_V_DIM
SUBLN_EPS = 1e-5
NUM_BUCKETS = 32
MAX_DISTANCE = 1024
N_BIAS_HEADS = DIL_HEADS + DIFF_HEADS
NEG_INF = -1e30
QK_SCALE = DIL_HEAD_DIM ** -0.5
LOG2_E = math.log2(math.e)
B_WIDTH = 3 * DIL_WIDTH
C_WIDTH = 2 * DIFF_QK_WIDTH + DIFF_V_WIDTH
IN_WIDTH = FOURIER_WIDTH + B_WIDTH + C_WIDTH
GATE_WIDTH = 3 * D_MODEL

LANES = 128
BF16_SUBLANES = 16
VMEM_BYTES_V7X = 64 * 1024 * 1024
VMEM_LIMIT = VMEM_BYTES_V7X - 8 * 1024 * 1024

FFN_TM = 1024
FF_CHUNK = 256
PROJ_TM = 512
PROJ_CHUNK = 512
MERGE_TM = 512
FOURIER_RADIX = 4
QUARTER_SEQ = SEQ // FOURIER_RADIX
FOURIER_CN = 1024
DIL_QKV = 3 * LANES
DIL_QB = 128
DIL_KW = DIL_QB + 2 * DIL_RADIUS
DIL_UNROLL = 8
DIFF_TQ = 512
DIFF_STRIP = 2 * SEQ - DIFF_TQ
DIFF_RING = 2 * SEQ
DIFF_KT = 512


def _params(*semantics):
    return pltpu.CompilerParams(dimension_semantics=semantics, vmem_limit_bytes=VMEM_LIMIT)


def _resident(shape):
    nd = len(shape)
    return pl.BlockSpec(shape, lambda *_: (0,) * nd, pipeline_mode=pl.Buffered(1))


def _rms(x, g, eps):
    return x * lax.rsqrt(jnp.mean(x * x, axis=-1, keepdims=True) + eps) * g


def _cast_specs(jobs, steps):
    in_specs, out_specs, out_shapes = [], [], []
    for src, layer, plan in jobs:
        _, rows, cols = src.shape
        out_cols = cols if plan is None else sum(width for _, width, _ in plan)
        share = 1
        while rows % (steps // share * BF16_SUBLANES):
            share *= 2
        blk = rows // (steps // share)
        in_specs.append(pl.BlockSpec((None, blk, cols),
                                     lambda i, layer=layer, share=share: (layer, i // share, 0)))
        out_specs.append(pl.BlockSpec((blk, out_cols), lambda i, share=share: (i // share, 0)))
        out_shapes.append(jax.ShapeDtypeStruct((rows, out_cols), BF16))
    return in_specs, out_specs, out_shapes


def _cast_blocks(src_refs, dst_refs, plans):
    for src, dst, plan in zip(src_refs, dst_refs, plans):
        if plan is None:
            dst[...] = src[...].astype(BF16)
            continue
        col = 0
        for src_col, width, scale in plan:
            part = src[:, src_col:src_col + width]
            dst[:, col:col + width] = (part if scale == 1.0 else part * scale).astype(BF16)
            col += width


def _cast_args(jobs):
    return [src for src, _, _ in jobs], tuple(plan for _, _, plan in jobs)


def _bucket_breakpoints():
    rel = np.arange(-(SEQ - 1), SEQ)
    half = NUM_BUCKETS // 2
    max_exact = half // 2
    n = np.abs(rel)
    nf = np.maximum(n, 1).astype(np.float64)
    large = max_exact + (np.log(nf / max_exact) / math.log(MAX_DISTANCE / max_exact)
                         * (half - max_exact)).astype(np.int32)
    large = np.minimum(large, half - 1)
    b = np.where(rel > 0, half, 0) + np.where(n < max_exact, n, large)
    pts = [(int(rel[i]), int(b[i])) for i in range(1, len(rel)) if b[i] != b[i - 1]]
    return int(b[0]), pts


def _bias_values(tbl_ref, head, rel):
    first, pts = _bucket_breakpoints()
    val = jnp.full(rel.shape, tbl_ref[first, head], F32)
    for thr, bucket in pts:
        val = jnp.where(rel >= thr, tbl_ref[bucket, head], val)
    return val * LOG2_E


def _diff_bias_kernel(tbl_ref, o_ref):
    rows, cols = o_ref.shape[1], o_ref.shape[2]
    ring = DIFF_RING
    u = lax.broadcasted_iota(jnp.int32, (8, ring), 1)
    offset = jnp.where(u < cols, u, u - ring)
    line = _bias_values(tbl_ref, DIL_HEADS + pl.program_id(0), offset - (SEQ - rows))
    tiled = jnp.broadcast_to(line[:1], (rows, ring))
    o_ref[0] = pltpu.roll(tiled, 0, 1, stride=1, stride_axis=0)[:, :cols]


def _diff_bias_strip(rel_bias):
    return pl.pallas_call(
        _diff_bias_kernel,
        grid=(DIFF_HEADS,),
        in_specs=[pl.BlockSpec(memory_space=pltpu.SMEM)],
        out_specs=pl.BlockSpec((1, DIFF_TQ, DIFF_STRIP), lambda h: (h, 0, 0)),
        out_shape=jax.ShapeDtypeStruct((DIFF_HEADS, DIFF_TQ, DIFF_STRIP), F32),
        compiler_params=_params("arbitrary"),
        name="diff_bias_strip",
    )(rel_bias)


EDGE_FIRST, EDGE_NONE, EDGE_LAST, EDGE_BOTH = range(4)


def _dil_bias_kernel(tbl_ref, o_ref):
    head = pl.program_id(0)
    group = head // 2
    dil = DIL_PAIRS[0][1]
    for g in range(1, DIL_GROUPS):
        dil = jnp.where(group == g, DIL_PAIRS[g][1], dil)
    rows, cols = o_ref.shape[3], o_ref.shape[4]
    col = lax.broadcasted_iota(jnp.int32, (rows, cols), 1)
    rel_t = col - lax.broadcasted_iota(jnp.int32, (rows, cols), 0) - DIL_RADIUS
    val = _bias_values(tbl_ref, head, rel_t * dil)
    val = jnp.where(jnp.abs(rel_t) <= DIL_RADIUS, val, NEG_INF)
    lo = jnp.where(col >= DIL_RADIUS, val, NEG_INF)
    o_ref[0, EDGE_FIRST, 0] = lo
    o_ref[0, EDGE_NONE, 0] = val
    o_ref[0, EDGE_LAST, 0] = jnp.where(col < cols - DIL_RADIUS, val, NEG_INF)
    o_ref[0, EDGE_BOTH, 0] = jnp.where(col < cols - DIL_RADIUS, lo, NEG_INF)


def _dil_bias_tiles(rel_bias):
    return pl.pallas_call(
        _dil_bias_kernel,
        grid=(DIL_HEADS,),
        in_specs=[pl.BlockSpec(memory_space=pltpu.SMEM)],
        out_specs=pl.BlockSpec((1, 4, 1, DIL_QB, DIL_KW), lambda h: (h // 2, 0, h % 2, 0, 0)),
        out_shape=jax.ShapeDtypeStruct((DIL_GROUPS, 4, 2, DIL_QB, DIL_KW), F32),
        compiler_params=_params("arbitrary"),
        name="dil_bias_tiles",
    )(rel_bias)


def _swiglu_residual(x, g_ref, wg_ref, wu_ref, wd_ref, a_ref):
    h = _rms(x, g_ref[...], EPS).astype(BF16)
    for c in range(D_FF // FF_CHUNK):
        sl = slice(c * FF_CHUNK, (c + 1) * FF_CHUNK)
        gate = jnp.dot(h, wg_ref[:, sl], preferred_element_type=F32)
        up = jnp.dot(h, wu_ref[:, sl], preferred_element_type=F32)
        a_ref[:, sl] = (gate * jax.nn.sigmoid(gate) * up).astype(BF16)
    return x + 0.5 * jnp.dot(a_ref[...], wd_ref[...], preferred_element_type=F32)


def _ffn_kernel(x_ref, g_ref, wg_ref, wu_ref, wd_ref, *rest, cast_plans):
    n_cast = len(cast_plans)
    cast_src, o_ref, cast_dst, a_ref = (rest[:n_cast], rest[n_cast], rest[n_cast + 1:-1], rest[-1])
    _cast_blocks(cast_src, cast_dst, cast_plans)
    o_ref[...] = _swiglu_residual(x_ref[...], g_ref, wg_ref, wu_ref, wd_ref, a_ref)


def _ffn(x, g, wg, wu, wd, cast_jobs=()):
    steps = TOKENS // FFN_TM
    tile = pl.BlockSpec((FFN_TM, D_MODEL), lambda i: (i, 0))
    cast_in, cast_out, cast_shapes = _cast_specs(cast_jobs, steps)
    cast_srcs, cast_plans = _cast_args(cast_jobs)
    return pl.pallas_call(
        functools.partial(_ffn_kernel, cast_plans=cast_plans),
        grid=(steps,),
        in_specs=[tile, _resident((1, D_MODEL)), _resident((D_MODEL, D_FF)),
                  _resident((D_MODEL, D_FF)), _resident((D_FF, D_MODEL))] + cast_in,
        out_specs=[tile] + cast_out,
        out_shape=[jax.ShapeDtypeStruct((TOKENS, D_MODEL), F32)] + cast_shapes,
        scratch_shapes=[pltpu.VMEM((FFN_TM, D_FF), BF16)],
        compiler_params=_params("arbitrary"),
        name="ffn",
    )(x, g.reshape(1, D_MODEL), wg, wu, wd, *cast_srcs)


def _proj_kernel(x_ref, g_ref, w_ref, wg_ref, bg_ref, *rest, cast_plans):
    n_cast = len(cast_plans)
    cast_src, rest = rest[:n_cast], rest[n_cast:]
    pf_ref, pb0_ref, pb1_ref, pb2_ref, pc_ref, gate_ref = rest[:6]
    cast_dst, y_ref = rest[6:-1], rest[-1]
    _cast_blocks(cast_src, cast_dst, cast_plans)
    u = _rms(x_ref[...], g_ref[...], EPS).astype(BF16)
    rows = x_ref.shape[0]

    def chunks(width):
        return [slice(c0, min(c0 + PROJ_CHUNK, width)) for c0 in range(0, width, PROJ_CHUNK)]

    def project(w, col0, sl):
        return jnp.dot(u, w[:, col0 + sl.start:col0 + sl.stop], preferred_element_type=F32)

    def by_residue(y, o_ref, dil, col0):
        for j in range(y.shape[1] // LANES):
            y_ref[j] = y[:, j * LANES:(j + 1) * LANES]
            cols = slice(col0 + j * LANES, col0 + (j + 1) * LANES)
            for r in range(dil):
                o_ref[0, r, :, cols] = y_ref[j, pl.ds(r, rows // dil, stride=dil), :].astype(o_ref.dtype)

    for sl in chunks(GATE_WIDTH):
        gate_ref[:, sl] = jax.nn.sigmoid(project(wg_ref, 0, sl) + bg_ref[:, sl]).astype(gate_ref.dtype)
    for sl in chunks(FOURIER_WIDTH):
        by_residue(project(w_ref, 0, sl), pf_ref, FOURIER_RADIX, sl.start)
    y = project(w_ref, FOURIER_WIDTH, slice(0, B_WIDTH))
    pb0_ref[...] = y[:, :DIL_QKV].astype(pb0_ref.dtype)
    by_residue(y[:, DIL_QKV:2 * DIL_QKV], pb1_ref, DIL_PAIRS[1][1], 0)
    by_residue(y[:, 2 * DIL_QKV:], pb2_ref, DIL_PAIRS[2][1], 0)
    for sl in chunks(C_WIDTH):
        pc_ref[:, sl] = project(w_ref, FOURIER_WIDTH + B_WIDTH, sl).astype(pc_ref.dtype)


def _in_weight_plan():
    q_scale = QK_SCALE * LOG2_E
    plan = [(0, FOURIER_WIDTH, 1.0)]
    for g in range(DIL_GROUPS):
        lo = FOURIER_WIDTH + g * LANES
        plan += [(lo, LANES, q_scale), (lo + DIL_WIDTH, LANES, 1.0), (lo + 2 * DIL_WIDTH, LANES, 1.0)]
    c3 = FOURIER_WIDTH + B_WIDTH
    plan += [(c3, DIFF_QK_WIDTH, q_scale), (c3 + DIFF_QK_WIDTH, DIFF_QK_WIDTH + DIFF_V_WIDTH, 1.0)]
    return plan


def _proj(x, g, w, wg, bg, cast_jobs=()):
    tiles_per_seq = SEQ // PROJ_TM
    steps = TOKENS // PROJ_TM
    cast_in, cast_out, cast_shapes = _cast_specs(cast_jobs, steps)
    cast_srcs, cast_plans = _cast_args(cast_jobs)

    def tile(w):
        return pl.BlockSpec((PROJ_TM, w), lambda i: (i, 0))

    def residue_tile(dil, w):
        return pl.BlockSpec((1, dil, PROJ_TM // dil, w),
                            lambda i: (i // tiles_per_seq, 0, i % tiles_per_seq, 0))

    def residue_shape(dil, w):
        return jax.ShapeDtypeStruct((BATCH, dil, SEQ // dil, w), BF16)

    d1, d2 = DIL_PAIRS[1][1], DIL_PAIRS[2][1]
    return pl.pallas_call(
        functools.partial(_proj_kernel, cast_plans=cast_plans),
        grid=(steps,),
        in_specs=[tile(D_MODEL), _resident((1, D_MODEL)), _resident((D_MODEL, IN_WIDTH)),
                  _resident((D_MODEL, GATE_WIDTH)), _resident((1, GATE_WIDTH))] + cast_in,
        out_specs=[residue_tile(FOURIER_RADIX, FOURIER_WIDTH), tile(DIL_QKV),
                   residue_tile(d1, DIL_QKV),
                   residue_tile(d2, DIL_QKV), tile(C_WIDTH), tile(GATE_WIDTH)] + cast_out,
        out_shape=[residue_shape(FOURIER_RADIX, FOURIER_WIDTH),
                   jax.ShapeDtypeStruct((TOKENS, DIL_QKV), BF16),
                   residue_shape(d1, DIL_QKV), residue_shape(d2, DIL_QKV),
                   jax.ShapeDtypeStruct((TOKENS, C_WIDTH), BF16),
                   jax.ShapeDtypeStruct((TOKENS, GATE_WIDTH), BF16)] + cast_shapes,
        scratch_shapes=[pltpu.VMEM((max(PROJ_CHUNK, DIL_QKV) // LANES, PROJ_TM, LANES), F32)],
        compiler_params=_params("arbitrary"),
        name="proj_in",
    )(x, g.reshape(1, D_MODEL), w, wg, bg.reshape(1, GATE_WIDTH), *cast_srcs)


def _fourier_kernel(a_ref, m_ref, d_ref, o_ref):
    q = QUARTER_SEQ
    tc, ts = [], []
    for r in range(FOURIER_RADIX):
        t = jnp.dot(m_ref[r], a_ref[0, r], preferred_element_type=F32)
        tc.append(t[:q])
        ts.append(t[q:])
    c02p, c02m, c13p, c13m = tc[0] + tc[2], tc[0] - tc[2], tc[1] + tc[3], tc[1] - tc[3]
    s02p, s02m, s13p, s13m = ts[0] + ts[2], ts[0] - ts[2], ts[1] + ts[3], ts[1] - ts[3]
    cos_sums = (c02p + c13p, c02m - s13m, c02p - c13p, c02m + s13m)
    sin_sums = (s02p + s13p, s02m + c13m, s02p - s13p, s02m - c13m)
    for j in range(FOURIER_RADIX):
        p, s = cos_sums[j].astype(BF16), sin_sums[j].astype(BF16)
        for g in range(o_ref.shape[2] // LANES):
            cols = slice(g * LANES, (g + 1) * LANES)
            lhs = jnp.concatenate([p[:, cols], s[:, cols]], axis=1)
            o_ref[0, j * q:(j + 1) * q, cols] = jnp.dot(
                lhs, d_ref[...], preferred_element_type=F32).astype(o_ref.dtype)


def _fourier(pf, m_seq, d_chan):
    cn = FOURIER_CN
    return pl.pallas_call(
        _fourier_kernel,
        grid=(BATCH, FOURIER_WIDTH // cn),
        in_specs=[pl.BlockSpec((1, FOURIER_RADIX, QUARTER_SEQ, cn), lambda b, j: (b, 0, 0, j)),
                  _resident((FOURIER_RADIX, 2 * QUARTER_SEQ, QUARTER_SEQ)),
                  _resident((2 * FOURIER_GROUP_DIM, FOURIER_GROUP_DIM))],
        out_specs=pl.BlockSpec((1, SEQ, cn), lambda b, j: (b, 0, j)),
        out_shape=jax.ShapeDtypeStruct((BATCH, SEQ, FOURIER_WIDTH), BF16),
        compiler_params=_params("parallel", "parallel"),
        name="fourier_mix",
    )(pf, m_seq, d_chan)


def _dft_matrices():
    q = QUARTER_SEQ
    split = 32

    def table(mult, width):
        k = lax.broadcasted_iota(jnp.int32, (q, width), 0)
        j = lax.broadcasted_iota(jnp.int32, (q, width), 1)
        ang = ((k * j * mult) % q).astype(F32) * (2.0 * math.pi / q)
        return jnp.cos(ang), jnp.sin(ang)

    (ca, sa), (cb, sb) = table(split, q // split), table(1, split)
    cos_0 = (ca[:, :, None] * cb[:, None, :] - sa[:, :, None] * sb[:, None, :]).reshape(q, q)
    sin_0 = (sa[:, :, None] * cb[:, None, :] + ca[:, :, None] * sb[:, None, :]).reshape(q, q)
    mats = []
    for r in range(FOURIER_RADIX):
        phi = lax.broadcasted_iota(jnp.int32, (q, 1), 0).astype(F32) * (2.0 * math.pi * r / SEQ)
        cos_r = cos_0 * jnp.cos(phi) - sin_0 * jnp.sin(phi)
        sin_r = sin_0 * jnp.cos(phi) + cos_0 * jnp.sin(phi)
        mats.append(jnp.concatenate([cos_r, sin_r], axis=0))
    return jnp.stack(mats).astype(BF16)


def _channel_dft():
    gd = FOURIER_GROUP_DIM
    idx = np.arange(gd)
    ang = 2.0 * np.pi * ((idx[:, None] * idx[None, :]) % gd) / gd
    scale = 1.0 / math.sqrt(SEQ * gd)
    return jnp.asarray(np.concatenate([np.cos(ang), -np.sin(ang)]) * scale, F32).astype(BF16)


def _dil_block(q, kw, vw_aug, bias, head0_lanes):
    zero = jnp.zeros_like(q)
    pv, lse = [], []
    for hh in range(2):
        q_h = jnp.where(head0_lanes, q, zero) if hh == 0 else jnp.where(head0_lanes, zero, q)
        logits = lax.dot_general(q_h, kw, (((1,), (1,)), ((), ())),
                                 preferred_element_type=F32) + bias[hh]
        mx = jnp.max(logits, axis=-1, keepdims=True)
        e = jnp.exp2(logits - mx).astype(BF16)
        r = jnp.dot(e, vw_aug, preferred_element_type=F32)
        s = r[:, LANES:]
        pv.append(r[:, :LANES] / s)
        lse.append(mx + jnp.log2(s))
    return jnp.where(head0_lanes, pv[0], pv[1]), jnp.where(head0_lanes, lse[0], lse[1])


def _dil_kernel(g0_ref, g1_ref, g2_ref, bias_ref, out_ref,
                kpad_ref, vpad_ref, o_ref, lse_ref):
    pad = DIL_RADIUS
    head0_lanes = lax.broadcasted_iota(jnp.int32, (DIL_QB, LANES), 1) < DIL_HEAD_DIM
    zeros_k = jnp.zeros((pad, LANES), BF16)
    zeros_v = jnp.zeros((pad, 2 * LANES), BF16)
    q_sl, k_sl, v_sl = (slice(i * LANES, (i + 1) * LANES) for i in range(3))

    vpad_ref[pl.ds(pad, SEQ), LANES:] = jnp.ones((SEQ, LANES), BF16)

    def fill_padded(src, sub_len):
        kpad_ref[:pad, :] = zeros_k
        kpad_ref[pl.ds(pad, sub_len), :] = src[:, k_sl]
        kpad_ref[pl.ds(pad + sub_len, pad), :] = zeros_k
        vpad_ref[:pad, :] = zeros_v
        vpad_ref[pl.ds(pad, sub_len), :LANES] = src[:, v_sl]
        vpad_ref[pl.ds(pad + sub_len, pad), :] = zeros_v

    def store(group, rows, o, lse):
        o_ref[group, rows, :] = o
        lse_ref[group, rows, :] = lse

    def padded_block(group, q_src, blk, edge, dil, residue):
        r0 = pl.multiple_of(blk * DIL_QB, DIL_QB)
        o, lse = _dil_block(q_src[pl.ds(r0, DIL_QB), q_sl], kpad_ref[pl.ds(r0, DIL_KW), :],
                            vpad_ref[pl.ds(r0, DIL_KW), :],
                            [bias_ref[group, edge, 0], bias_ref[group, edge, 1]],
                            head0_lanes)
        rows = pl.ds(residue + r0 * dil, DIL_QB, stride=dil) if dil > 1 else pl.ds(r0, DIL_QB)
        store(group, rows, o, lse)

    fill_padded(g0_ref.at[0], SEQ)
    n_blocks0 = SEQ // DIL_QB

    def g0_body(i, carry):
        for n in range(DIL_UNROLL):
            blk = i * DIL_UNROLL + n
            edge = jnp.where(blk == 0, EDGE_FIRST,
                             jnp.where(blk == n_blocks0 - 1, EDGE_LAST, EDGE_NONE))
            padded_block(0, g0_ref.at[0], blk, edge, 1, 0)
        return carry

    lax.fori_loop(0, n_blocks0 // DIL_UNROLL, g0_body, 0)

    dil1 = DIL_PAIRS[1][1]
    len1 = SEQ // dil1
    n_blocks1 = len1 // DIL_QB
    per_iter1 = DIL_UNROLL // n_blocks1
    slot1 = len1 + 2 * pad
    for n in range(per_iter1):
        for ref, zeros in ((kpad_ref, zeros_k), (vpad_ref, zeros_v)):
            ref[n * slot1:n * slot1 + pad, :] = zeros
            ref[n * slot1 + pad + len1:(n + 1) * slot1, :] = zeros
        vpad_ref[n * slot1 + pad:n * slot1 + pad + len1, LANES:] = jnp.ones((len1, LANES), BF16)

    def g1_body(i, carry):
        for n in range(per_iter1):
            r = i * per_iter1 + n
            src = g1_ref.at[0, r]
            base = n * slot1
            kpad_ref[base + pad:base + pad + len1, :] = src[:, k_sl]
            vpad_ref[base + pad:base + pad + len1, :LANES] = src[:, v_sl]
            for blk in range(n_blocks1):
                edge = EDGE_FIRST if blk == 0 else EDGE_LAST if blk == n_blocks1 - 1 else EDGE_NONE
                r0 = blk * DIL_QB
                o, lse = _dil_block(src[r0:r0 + DIL_QB, q_sl],
                                    kpad_ref[base + r0:base + r0 + DIL_KW, :],
                                    vpad_ref[base + r0:base + r0 + DIL_KW, :],
                                    [bias_ref[1, edge, 0], bias_ref[1, edge, 1]], head0_lanes)
                store(1, pl.ds(r + r0 * dil1, DIL_QB, stride=dil1), o, lse)
        return carry

    lax.fori_loop(0, dil1 // per_iter1, g1_body, 0)

    dil2 = DIL_PAIRS[2][1]
    assert SEQ // dil2 == DIL_QB
    for n in range(DIL_UNROLL):
        base = n * DIL_KW
        for ref, zeros in ((kpad_ref, zeros_k), (vpad_ref, zeros_v)):
            ref[base:base + pad, :] = zeros
            ref[base + pad + DIL_QB:base + DIL_KW, :] = zeros
        vpad_ref[base + pad:base + pad + DIL_QB, LANES:] = jnp.ones((DIL_QB, LANES), BF16)

    def g2_body(i, carry):
        for n in range(DIL_UNROLL):
            r = i * DIL_UNROLL + n
            src = g2_ref.at[0, r]
            base = n * DIL_KW
            kpad_ref[base + pad:base + pad + DIL_QB, :] = src[:, k_sl]
            vpad_ref[base + pad:base + pad + DIL_QB, :LANES] = src[:, v_sl]
            o, lse = _dil_block(src[:, q_sl], kpad_ref[base:base + DIL_KW, :],
                                vpad_ref[base:base + DIL_KW, :],
                                [bias_ref[2, EDGE_BOTH, 0], bias_ref[2, EDGE_BOTH, 1]],
                                head0_lanes)
            store(2, pl.ds(r, DIL_QB, stride=dil2), o, lse)
        return carry

    lax.fori_loop(0, dil2 // DIL_UNROLL, g2_body, 0)

    chunk = 256
    for c in range(SEQ // chunk):
        rows = slice(c * chunk, (c + 1) * chunk)
        l0, l1, l2 = lse_ref[0, rows, :], lse_ref[1, rows, :], lse_ref[2, rows, :]
        mx = jnp.maximum(jnp.maximum(l0, l1), l2)
        e0, e1, e2 = jnp.exp2(l0 - mx), jnp.exp2(l1 - mx), jnp.exp2(l2 - mx)
        mixed = (e0 * o_ref[0, rows, :] + e1 * o_ref[1, rows, :] + e2 * o_ref[2, rows, :]) / (e0 + e1 + e2)
        out_ref[0, rows, :] = mixed.astype(out_ref.dtype)


def _dilated_mixture(pb0, pb1, pb2, biases):
    d1, d2 = DIL_PAIRS[1][1], DIL_PAIRS[2][1]
    bias_spec = _resident((DIL_GROUPS, 4, 2, DIL_QB, DIL_KW))
    max_len = SEQ + 2 * DIL_RADIUS
    return pl.pallas_call(
        _dil_kernel,
        grid=(BATCH,),
        in_specs=[pl.BlockSpec((1, SEQ, DIL_QKV), lambda b: (b, 0, 0)),
                  pl.BlockSpec((1, d1, SEQ // d1, DIL_QKV), lambda b: (b, 0, 0, 0)),
                  pl.BlockSpec((1, d2, SEQ // d2, DIL_QKV), lambda b: (b, 0, 0, 0)),
                  bias_spec],
        out_specs=pl.BlockSpec((1, SEQ, LANES), lambda b: (b, 0, 0)),
        out_shape=jax.ShapeDtypeStruct((BATCH, SEQ, DIL_OUT_WIDTH), BF16),
        scratch_shapes=[pltpu.VMEM((max_len, LANES), BF16), pltpu.VMEM((max_len, 2 * LANES), BF16),
                        pltpu.VMEM((DIL_GROUPS, SEQ, LANES), F32),
                        pltpu.VMEM((DIL_GROUPS, SEQ, LANES), F32)],
        compiler_params=_params("parallel"),
        name="dilated_attn",
    )(pb0, pb1, pb2, biases)


def _diff_kernel(q_ref, k_ref, v_ref, bias_ref, lq1_ref, lk1_ref, lq2_ref, lk2_ref, sg_ref,
                 o_ref, vaug_ref, logit_ref, rmax_ref, *, lam_init):
    nq = SEQ // DIFF_TQ
    n_tiles = SEQ // DIFF_KT
    vaug_ref[:, :LANES] = v_ref[0]
    vaug_ref[:, LANES:] = jnp.ones((SEQ, LANES), BF16)
    lam = (jnp.exp(jnp.sum(lq1_ref[...] * lk1_ref[...], axis=-1, keepdims=True))
           - jnp.exp(jnp.sum(lq2_ref[...] * lk2_ref[...], axis=-1, keepdims=True))
           + lam_init)

    def key_tile(t):
        return slice(t * DIFF_KT, (t + 1) * DIFF_KT)

    def rows(j):
        return pl.ds(pl.multiple_of(j * DIFF_TQ, DIFF_TQ), DIFF_TQ)

    def pass1(j):
        slot = j % 2
        off = (nq - 1 - j) * DIFF_TQ
        q = q_ref[0, rows(j), :]
        first_half = lax.broadcasted_iota(jnp.int32, q.shape, 1) < DIFF_HEAD_DIM
        zero = jnp.zeros_like(q)
        for m in range(2):
            q_m = jnp.where(first_half, q, zero) if m == 0 else jnp.where(first_half, zero, q)
            lane_max = None
            for t in range(n_tiles):
                bias = bias_ref[0, :, pl.ds(pl.multiple_of(off + t * DIFF_KT, LANES), DIFF_KT)]
                logits = lax.dot_general(q_m, k_ref[0, key_tile(t), :], (((1,), (1,)), ((), ())),
                                         preferred_element_type=F32) + bias
                logit_ref[slot, m, :, key_tile(t)] = logits
                for c in range(DIFF_KT // LANES):
                    part = logits[:, c * LANES:(c + 1) * LANES]
                    lane_max = part if lane_max is None else jnp.maximum(lane_max, part)
            rmax_ref[slot, m] = jnp.broadcast_to(jnp.max(lane_max, axis=-1, keepdims=True),
                                                 (DIFF_TQ, LANES))

    def pass2(j):
        slot = j % 2
        outs = []
        for m in range(2):
            row_max = rmax_ref[slot, m]
            pv = jnp.zeros((DIFF_TQ, 2 * LANES), F32)
            for t in range(n_tiles):
                old = logit_ref[slot, m, :, key_tile(t)]
                e = jnp.concatenate([jnp.exp2(old[:, c * LANES:(c + 1) * LANES] - row_max)
                                     for c in range(DIFF_KT // LANES)], axis=1).astype(BF16)
                pv = pv + jnp.dot(e, vaug_ref[key_tile(t), :], preferred_element_type=F32)
            outs.append(pv[:, :LANES] / pv[:, LANES:])
        o = outs[0] - lam * outs[1]
        o = _rms(o, sg_ref[...], SUBLN_EPS) * (1.0 - lam_init)
        o_ref[0, rows(j), :] = o.astype(o_ref.dtype)

    def steady(j, carry):
        pass2(j - 1)
        pass1(j)
        return carry

    pass1(0)
    lax.fori_loop(1, nq, steady, 0)
    pass2(nq - 1)


def _diff_attention(pc, strip, lq1, lk1, lq2, lk2, subln_g, lam_init):
    pcv = pc.reshape(BATCH, SEQ, C_WIDTH)
    vec = _resident((1, DIFF_HEAD_DIM))

    def lane_block(part):
        return pl.BlockSpec((1, SEQ, LANES), lambda h, b: (b, 0, part * DIFF_HEADS + h))

    return pl.pallas_call(
        functools.partial(_diff_kernel, lam_init=lam_init),
        grid=(DIFF_HEADS, BATCH),
        in_specs=[lane_block(0), lane_block(1), lane_block(2),
                  pl.BlockSpec((1, DIFF_TQ, DIFF_STRIP), lambda h, b: (h, 0, 0)),
                  vec, vec, vec, vec, _resident((1, DIFF_V_DIM))],
        out_specs=lane_block(0),
        out_shape=jax.ShapeDtypeStruct((BATCH, SEQ, DIFF_V_WIDTH), BF16),
        scratch_shapes=[pltpu.VMEM((SEQ, 2 * LANES), BF16),
                        pltpu.VMEM((2, 2, DIFF_TQ, SEQ), F32),
                        pltpu.VMEM((2, 2, DIFF_TQ, LANES), F32)],
        compiler_params=_params("parallel", "parallel"),
        name="diff_attn",
    )(pcv, pcv, pcv, strip, lq1.reshape(1, -1), lk1.reshape(1, -1), lq2.reshape(1, -1),
      lk2.reshape(1, -1), subln_g.reshape(1, -1)).reshape(TOKENS, DIFF_V_WIDTH)


def _merge_ffn_kernel(x_ref, f_ref, b_ref, c_ref, gate_ref, wa_ref, wb_ref, wc_ref, wo_ref,
                      g_ref, wg_ref, wu_ref, wd_ref, *rest, final_norm):
    rest = list(rest)
    gf_ref = rest.pop(0) if final_norm else None
    out_ref, a_ref = rest
    y_a = jnp.dot(f_ref[...], wa_ref[...], preferred_element_type=F32)
    y_b = jnp.dot(b_ref[...], wb_ref[...], preferred_element_type=F32)
    y_c = jnp.dot(c_ref[...], wc_ref[...], preferred_element_type=F32)
    merged = (gate_ref[:, :D_MODEL].astype(F32) * y_a
              + gate_ref[:, D_MODEL:2 * D_MODEL].astype(F32) * y_b
              + gate_ref[:, 2 * D_MODEL:].astype(F32) * y_c)
    x = x_ref[...] + jnp.dot(merged.astype(BF16), wo_ref[...], preferred_element_type=F32)
    y = _swiglu_residual(x, g_ref, wg_ref, wu_ref, wd_ref, a_ref)
    if final_norm:
        y = _rms(y, gf_ref[...], EPS)
    out_ref[...] = y


def _merge_ffn(x, f, b, c, gates, wa, wb, wc, wo, g, wg, wu, wd, g_final=None):
    def tile(w):
        return pl.BlockSpec((MERGE_TM, w), lambda i: (i, 0))

    final_norm = g_final is not None
    weights = [wa, wb, wc, wo]
    gain = _resident((1, D_MODEL))
    in_specs = ([tile(D_MODEL), tile(FOURIER_WIDTH), tile(DIL_OUT_WIDTH), tile(DIFF_V_WIDTH),
                 tile(GATE_WIDTH)] + [_resident(w.shape) for w in weights]
                + [gain, _resident(wg.shape), _resident(wu.shape), _resident(wd.shape)])
    args = [x, f, b, c, gates, *weights, g.reshape(1, D_MODEL), wg, wu, wd]
    if final_norm:
        in_specs.append(gain)
        args.append(g_final.reshape(1, D_MODEL))
    return pl.pallas_call(
        functools.partial(_merge_ffn_kernel, final_norm=final_norm),
        grid=(TOKENS // MERGE_TM,),
        in_specs=in_specs,
        out_specs=tile(D_MODEL),
        out_shape=jax.ShapeDtypeStruct((TOKENS, D_MODEL), F32),
        scratch_shapes=[pltpu.VMEM((MERGE_TM, D_FF), BF16)],
        compiler_params=_params("parallel"),
        name="merge_ffn_final" if final_norm else "merge_ffn",
    )(*args)


def kernel(x, g_ffn1, w_ffn1_gate, w_ffn1_up, w_ffn1_down, g_mix, w_in, w_gate, b_gate,
           w_br_a, w_br_b, w_br_c, w_out, lam_q1, lam_k1, lam_q2, lam_k2, subln_g,
           rel_bias, g_ffn2, w_ffn2_gate, w_ffn2_up, w_ffn2_down, g_final):
    assert x.shape == (BATCH, SEQ, D_MODEL) and x.dtype == F32
    assert w_in.shape == (DEPTH, D_MODEL, IN_WIDTH)
    assert rel_bias.shape == (NUM_BUCKETS, N_BIAS_HEADS)
    bf = lambda w: w.astype(BF16)

    dil_bias = _dil_bias_tiles(rel_bias)
    diff_strip = _diff_bias_strip(rel_bias)
    m_seq = _dft_matrices()
    d_chan = _channel_dft()
    in_plan = _in_weight_plan()

    h = x.reshape(TOKENS, D_MODEL)
    ffn1_w = [bf(w_ffn1_gate[0]), bf(w_ffn1_up[0]), bf(w_ffn1_down[0])]
    for l in range(DEPTH):
        last = l == DEPTH - 1
        h, wa, wc_br, wo, gate_w, in_w = _ffn(
            h, g_ffn1[l], *ffn1_w,
            cast_jobs=[(w_br_a, l, None), (w_br_c, l, None), (w_out, l, None),
                       (w_gate, l, None), (w_in, l, in_plan)])

        next_ffn1 = [] if last else [(w_ffn1_gate, l + 1, None), (w_ffn1_up, l + 1, None),
                                     (w_ffn1_down, l + 1, None)]
        pf, pb0, pb1, pb2, pc, gates, *ffn_w = _proj(
            h, g_mix[l], in_w, gate_w, b_gate[l],
            cast_jobs=[(w_ffn2_gate, l, None), (w_ffn2_up, l, None), (w_ffn2_down, l, None)]
            + next_ffn1)
        ffn2_w, ffn1_w = ffn_w[:3], ffn_w[3:]

        f = _fourier(pf, m_seq, d_chan).reshape(TOKENS, FOURIER_WIDTH)
        mixed = _dilated_mixture(pb0.reshape(BATCH, SEQ, DIL_QKV), pb1, pb2, dil_bias)
        lam_init = 0.8 - 0.6 * math.exp(-0.3 * l)
        c = _diff_attention(pc, diff_strip, lam_q1[l], lam_k1[l], lam_q2[l], lam_k2[l],
                            subln_g[l], lam_init)

        h = _merge_ffn(h, f, mixed.reshape(TOKENS, DIL_OUT_WIDTH), c, gates, wa, bf(w_br_b[l]),
                       wc_br, wo, g_ffn2[l], *ffn2_w, g_final if last else None)
    return h.reshape(BATCH, SEQ, D_MODEL)
```

```python
import functools
import math

import numpy as np
import jax
import jax.numpy as jnp
from jax import lax
from jax.experimental import pallas as pl
from jax.experimental.pallas import tpu as pltpu

F32 = jnp.float32
BF16 = jnp.bfloat16

D_MODEL = 1024
BATCH = 8
SEQ = 2048
DEPTH = 2
TOKENS = BATCH * SEQ
D_FF = 2816
EPS = 1e-6
FOURIER_GROUPS = 8
FOURIER_GROUP_DIM = 128
FOURIER_WIDTH = FOURIER_GROUPS * FOURIER_GROUP_DIM
DIL_PAIRS = ((128, 1), (512, 4), (2048, 16))
DIL_GROUPS = len(DIL_PAIRS)
DIL_HEAD_DIM = 64
DIL_HEADS = 2 * DIL_GROUPS
DIL_WIDTH = DIL_HEADS * DIL_HEAD_DIM
DIL_OUT_WIDTH = 2 * DIL_HEAD_DIM
DIL_RADIUS = 64
DIFF_HEADS = 4
DIFF_HEAD_DIM = 64
DIFF_QK_WIDTH = DIFF_HEADS * 2 * DIFF_HEAD_DIM
DIFF_V_DIM = 2 * DIFF_HEAD_DIM
DIFF_V_WIDTH = DIFF_HEADS * DIFF_V_DIM
SUBLN_EPS = 1e-5
NUM_BUCKETS = 32
MAX_DISTANCE = 1024
N_BIAS_HEADS = DIL_HEADS + DIFF_HEADS
NEG_INF = -1e30
QK_SCALE = DIL_HEAD_DIM ** -0.5
LOG2_E = math.log2(math.e)
B_WIDTH = 3 * DIL_WIDTH
C_WIDTH = 2 * DIFF_QK_WIDTH + DIFF_V_WIDTH
IN_WIDTH = FOURIER_WIDTH + B_WIDTH + C_WIDTH
GATE_WIDTH = 3 * D_MODEL

LANES = 128
BF16_SUBLANES = 16
VMEM_BYTES_V7X = 64 * 1024 * 1024
VMEM_LIMIT = VMEM_BYTES_V7X - 8 * 1024 * 1024

FFN_TM = 1024
FF_CHUNK = 256
PROJ_TM = 512
PROJ_CHUNK = 512
NORM_SPLIT = 2
MERGE_TM = 512
FOURIER_RADIX = 4
QUARTER_SEQ = SEQ // FOURIER_RADIX
FOURIER_CN = 1024
DIL_QKV = 3 * LANES
DIL_QB = 128
DIL_KW = DIL_QB + 2 * DIL_RADIUS
DIL_UNROLL = 8
DIFF_TQ = 512
DIFF_STRIP = 2 * SEQ - DIFF_TQ
DIFF_RING = 2 * SEQ
DIFF_KT = 512


def _params(*semantics):
    return pltpu.CompilerParams(dimension_semantics=semantics, vmem_limit_bytes=VMEM_LIMIT)


def _resident(shape):
    nd = len(shape)
    return pl.BlockSpec(shape, lambda *_: (0,) * nd, pipeline_mode=pl.Buffered(1))


def _rms(x, g, eps):
    return x * lax.rsqrt(jnp.mean(x * x, axis=-1, keepdims=True) + eps) * g


def _cast_specs(jobs, steps):
    in_specs, out_specs, out_shapes = [], [], []
    for src, layer, plan in jobs:
        _, rows, cols = src.shape
        out_cols = cols if plan is None else sum(width for _, width, _ in plan)
        share = 1
        while rows % (steps // share * BF16_SUBLANES):
            share *= 2
        blk = rows // (steps // share)
        in_specs.append(pl.BlockSpec((None, blk, cols),
                                     lambda i, layer=layer, share=share: (layer, i // share, 0)))
        out_specs.append(pl.BlockSpec((blk, out_cols), lambda i, share=share: (i // share, 0)))
        out_shapes.append(jax.ShapeDtypeStruct((rows, out_cols), BF16))
    return in_specs, out_specs, out_shapes


def _cast_blocks(src_refs, dst_refs, plans):
    for src, dst, plan in zip(src_refs, dst_refs, plans):
        if plan is None:
            dst[...] = src[...].astype(BF16)
            continue
        col = 0
        for src_col, width, scale in plan:
            part = src[:, src_col:src_col + width]
            dst[:, col:col + width] = (part if scale == 1.0 else part * scale).astype(BF16)
            col += width


def _cast_args(jobs):
    return [src for src, _, _ in jobs], tuple(plan for _, _, plan in jobs)


def _bucket_breakpoints():
    rel = np.arange(-(SEQ - 1), SEQ)
    half = NUM_BUCKETS // 2
    max_exact = half // 2
    n = np.abs(rel)
    nf = np.maximum(n, 1).astype(np.float64)
    large = max_exact + (np.log(nf / max_exact) / math.log(MAX_DISTANCE / max_exact)
                         * (half - max_exact)).astype(np.int32)
    large = np.minimum(large, half - 1)
    b = np.where(rel > 0, half, 0) + np.where(n < max_exact, n, large)
    pts = [(int(rel[i]), int(b[i])) for i in range(1, len(rel)) if b[i] != b[i - 1]]
    return int(b[0]), pts


def _bias_values(tbl_ref, head, rel):
    first, pts = _bucket_breakpoints()
    val = jnp.full(rel.shape, tbl_ref[first, head], F32)
    for thr, bucket in pts:
        val = jnp.where(rel >= thr, tbl_ref[bucket, head], val)
    return val * LOG2_E


def _diff_bias_kernel(tbl_ref, o_ref):
    rows, cols = o_ref.shape[1], o_ref.shape[2]
    ring = DIFF_RING
    u = lax.broadcasted_iota(jnp.int32, (8, ring), 1)
    offset = jnp.where(u < cols, u, u - ring)
    line = _bias_values(tbl_ref, DIL_HEADS + pl.program_id(0), offset - (SEQ - rows))
    tiled = jnp.broadcast_to(line[:1], (rows, ring))
    o_ref[0] = pltpu.roll(tiled, 0, 1, stride=1, stride_axis=0)[:, :cols]


def _diff_bias_strip(rel_bias):
    return pl.pallas_call(
        _diff_bias_kernel,
        grid=(DIFF_HEADS,),
        in_specs=[pl.BlockSpec(memory_space=pltpu.SMEM)],
        out_specs=pl.BlockSpec((1, DIFF_TQ, DIFF_STRIP), lambda h: (h, 0, 0)),
        out_shape=jax.ShapeDtypeStruct((DIFF_HEADS, DIFF_TQ, DIFF_STRIP), F32),
        compiler_params=_params("arbitrary"),
        name="diff_bias_strip",
    )(rel_bias)


EDGE_FIRST, EDGE_NONE, EDGE_LAST, EDGE_BOTH = range(4)


def _dil_bias_kernel(tbl_ref, o_ref):
    head = pl.program_id(0)
    group = head // 2
    dil = DIL_PAIRS[0][1]
    for g in range(1, DIL_GROUPS):
        dil = jnp.where(group == g, DIL_PAIRS[g][1], dil)
    rows, cols = o_ref.shape[3], o_ref.shape[4]
    col = lax.broadcasted_iota(jnp.int32, (rows, cols), 1)
    rel_t = col - lax.broadcasted_iota(jnp.int32, (rows, cols), 0) - DIL_RADIUS
    val = _bias_values(tbl_ref, head, rel_t * dil)
    val = jnp.where(jnp.abs(rel_t) <= DIL_RADIUS, val, NEG_INF)
    lo = jnp.where(col >= DIL_RADIUS, val, NEG_INF)
    o_ref[0, EDGE_FIRST, 0] = lo
    o_ref[0, EDGE_NONE, 0] = val
    o_ref[0, EDGE_LAST, 0] = jnp.where(col < cols - DIL_RADIUS, val, NEG_INF)
    o_ref[0, EDGE_BOTH, 0] = jnp.where(col < cols - DIL_RADIUS, lo, NEG_INF)


def _dil_bias_tiles(rel_bias):
    return pl.pallas_call(
        _dil_bias_kernel,
        grid=(DIL_HEADS,),
        in_specs=[pl.BlockSpec(memory_space=pltpu.SMEM)],
        out_specs=pl.BlockSpec((1, 4, 1, DIL_QB, DIL_KW), lambda h: (h // 2, 0, h % 2, 0, 0)),
        out_shape=jax.ShapeDtypeStruct((DIL_GROUPS, 4, 2, DIL_QB, DIL_KW), F32),
        compiler_params=_params("arbitrary"),
        name="dil_bias_tiles",
    )(rel_bias)


def _swiglu_residual(x, g_ref, wg_ref, wu_ref, wd_ref, a_ref):
    def hidden(h, sl):
        gate = jnp.dot(h, wg_ref[:, sl], preferred_element_type=F32)
        up = jnp.dot(h, wu_ref[:, sl], preferred_element_type=F32)
        return (gate * jax.nn.sigmoid(gate) * up).astype(BF16)

    part = x.shape[0] // NORM_SPLIT
    h_parts = [_rms(x[i * part:(i + 1) * part], g_ref[...], EPS).astype(BF16)
               for i in range(NORM_SPLIT)]
    h = jnp.concatenate(h_parts, axis=0)
    for i, h_part in enumerate(h_parts):
        a_ref[i * part:(i + 1) * part, :FF_CHUNK] = hidden(h_part, slice(0, FF_CHUNK))
    for c in range(1, D_FF // FF_CHUNK):
        sl = slice(c * FF_CHUNK, (c + 1) * FF_CHUNK)
        a_ref[:, sl] = hidden(h, sl)
    return x + 0.5 * jnp.dot(a_ref[...], wd_ref[...], preferred_element_type=F32)


def _ffn_kernel(x_ref, g_ref, wg_ref, wu_ref, wd_ref, *rest, cast_plans):
    n_cast = len(cast_plans)
    cast_src, o_ref, cast_dst, a_ref = (rest[:n_cast], rest[n_cast], rest[n_cast + 1:-1], rest[-1])
    _cast_blocks(cast_src, cast_dst, cast_plans)
    o_ref[...] = _swiglu_residual(x_ref[...], g_ref, wg_ref, wu_ref, wd_ref, a_ref)


def _ffn(x, g, wg, wu, wd, cast_jobs=()):
    steps = TOKENS // FFN_TM
    tile = pl.BlockSpec((FFN_TM, D_MODEL), lambda i: (i, 0))
    cast_in, cast_out, cast_shapes = _cast_specs(cast_jobs, steps)
    cast_srcs, cast_plans = _cast_args(cast_jobs)
    return pl.pallas_call(
        functools.partial(_ffn_kernel, cast_plans=cast_plans),
        grid=(steps,),
        in_specs=[tile, _resident((1, D_MODEL)), _resident((D_MODEL, D_FF)),
                  _resident((D_MODEL, D_FF)), _resident((D_FF, D_MODEL))] + cast_in,
        out_specs=[tile] + cast_out,
        out_shape=[jax.ShapeDtypeStruct((TOKENS, D_MODEL), F32)] + cast_shapes,
        scratch_shapes=[pltpu.VMEM((FFN_TM, D_FF), BF16)],
        compiler_params=_params("arbitrary"),
        name="ffn",
    )(x, g.reshape(1, D_MODEL), wg, wu, wd, *cast_srcs)


def _proj_kernel(x_ref, g_ref, w_ref, wg_ref, bg_ref, *rest, cast_plans):
    n_cast = len(cast_plans)
    cast_src, rest = rest[:n_cast], rest[n_cast:]
    pf_ref, pb0_ref, pb1_ref, pb2_ref, pc_ref, gate_ref = rest[:6]
    cast_dst, y_ref = rest[6:-1], rest[-1]
    _cast_blocks(cast_src, cast_dst, cast_plans)
    rows = x_ref.shape[0]
    half = rows // NORM_SPLIT
    u_halves = [_rms(x_ref[i * half:(i + 1) * half, :], g_ref[...], EPS).astype(BF16)
                for i in range(NORM_SPLIT)]
    u = jnp.concatenate(u_halves, axis=0)

    def chunks(width):
        return [slice(c0, min(c0 + PROJ_CHUNK, width)) for c0 in range(0, width, PROJ_CHUNK)]

    def project(w, col0, sl, lhs=None):
        return jnp.dot(u if lhs is None else lhs, w[:, col0 + sl.start:col0 + sl.stop],
                       preferred_element_type=F32)

    def by_residue(y, o_ref, dil, col0):
        for j in range(y.shape[1] // LANES):
            y_ref[j] = y[:, j * LANES:(j + 1) * LANES]
            cols = slice(col0 + j * LANES, col0 + (j + 1) * LANES)
            for r in range(dil):
                o_ref[0, r, :, cols] = y_ref[j, pl.ds(r, rows // dil, stride=dil), :].astype(o_ref.dtype)

    gate_chunks = chunks(GATE_WIDTH)
    for i, u_half in enumerate(u_halves):
        sl = gate_chunks[0]
        gate_ref[i * half:(i + 1) * half, sl] = jax.nn.sigmoid(
            project(wg_ref, 0, sl, u_half) + bg_ref[:, sl]).astype(gate_ref.dtype)
    for sl in gate_chunks[1:]:
        gate_ref[:, sl] = jax.nn.sigmoid(project(wg_ref, 0, sl) + bg_ref[:, sl]).astype(gate_ref.dtype)
    for sl in chunks(FOURIER_WIDTH):
        by_residue(project(w_ref, 0, sl), pf_ref, FOURIER_RADIX, sl.start)
    y = project(w_ref, FOURIER_WIDTH, slice(0, B_WIDTH))
    pb0_ref[...] = y[:, :DIL_QKV].astype(pb0_ref.dtype)
    by_residue(y[:, DIL_QKV:2 * DIL_QKV], pb1_ref, DIL_PAIRS[1][1], 0)
    by_residue(y[:, 2 * DIL_QKV:], pb2_ref, DIL_PAIRS[2][1], 0)
    for sl in chunks(C_WIDTH):
        pc_ref[:, sl] = project(w_ref, FOURIER_WIDTH + B_WIDTH, sl).astype(pc_ref.dtype)


def _in_weight_plan():
    q_scale = QK_SCALE * LOG2_E
    plan = [(0, FOURIER_WIDTH, 1.0)]
    for g in range(DIL_GROUPS):
        lo = FOURIER_WIDTH + g * LANES
        plan += [(lo, LANES, q_scale), (lo + DIL_WIDTH, LANES, 1.0), (lo + 2 * DIL_WIDTH, LANES, 1.0)]
    c3 = FOURIER_WIDTH + B_WIDTH
    plan += [(c3, DIFF_QK_WIDTH, q_scale), (c3 + DIFF_QK_WIDTH, DIFF_QK_WIDTH + DIFF_V_WIDTH, 1.0)]
    return plan


def _proj(x, g, w, wg, bg, cast_jobs=()):
    tiles_per_seq = SEQ // PROJ_TM
    steps = TOKENS // PROJ_TM
    cast_in, cast_out, cast_shapes = _cast_specs(cast_jobs, steps)
    cast_srcs, cast_plans = _cast_args(cast_jobs)

    def tile(w):
        return pl.BlockSpec((PROJ_TM, w), lambda i: (i, 0))

    def residue_tile(dil, w):
        return pl.BlockSpec((1, dil, PROJ_TM // dil, w),
                            lambda i: (i // tiles_per_seq, 0, i % tiles_per_seq, 0))

    def residue_shape(dil, w):
        return jax.ShapeDtypeStruct((BATCH, dil, SEQ // dil, w), BF16)

    d1, d2 = DIL_PAIRS[1][1], DIL_PAIRS[2][1]
    return pl.pallas_call(
        functools.partial(_proj_kernel, cast_plans=cast_plans),
        grid=(steps,),
        in_specs=[tile(D_MODEL), _resident((1, D_MODEL)), _resident((D_MODEL, IN_WIDTH)),
                  _resident((D_MODEL, GATE_WIDTH)), _resident((1, GATE_WIDTH))] + cast_in,
        out_specs=[residue_tile(FOURIER_RADIX, FOURIER_WIDTH), tile(DIL_QKV),
                   residue_tile(d1, DIL_QKV),
                   residue_tile(d2, DIL_QKV), tile(C_WIDTH), tile(GATE_WIDTH)] + cast_out,
        out_shape=[residue_shape(FOURIER_RADIX, FOURIER_WIDTH),
                   jax.ShapeDtypeStruct((TOKENS, DIL_QKV), BF16),
                   residue_shape(d1, DIL_QKV), residue_shape(d2, DIL_QKV),
                   jax.ShapeDtypeStruct((TOKENS, C_WIDTH), BF16),
                   jax.ShapeDtypeStruct((TOKENS, GATE_WIDTH), BF16)] + cast_shapes,
        scratch_shapes=[pltpu.VMEM((max(PROJ_CHUNK, DIL_QKV) // LANES, PROJ_TM, LANES), F32)],
        compiler_params=_params("arbitrary"),
        name="proj_in",
    )(x, g.reshape(1, D_MODEL), w, wg, bg.reshape(1, GATE_WIDTH), *cast_srcs)


def _fourier_kernel(a_ref, m_ref, d_ref, o_ref):
    q = QUARTER_SEQ
    tc, ts = [], []
    for r in range(FOURIER_RADIX):
        t = jnp.dot(m_ref[r], a_ref[0, r], preferred_element_type=F32)
        tc.append(t[:q])
        ts.append(t[q:])
    c02p, c02m, c13p, c13m = tc[0] + tc[2], tc[0] - tc[2], tc[1] + tc[3], tc[1] - tc[3]
    s02p, s02m, s13p, s13m = ts[0] + ts[2], ts[0] - ts[2], ts[1] + ts[3], ts[1] - ts[3]
    cos_sums = (c02p + c13p, c02m - s13m, c02p - c13p, c02m + s13m)
    sin_sums = (s02p + s13p, s02m + c13m, s02p - s13p, s02m - c13m)
    for j in range(FOURIER_RADIX):
        p, s = cos_sums[j].astype(BF16), sin_sums[j].astype(BF16)
        for g in range(o_ref.shape[2] // LANES):
            cols = slice(g * LANES, (g + 1) * LANES)
            lhs = jnp.concatenate([p[:, cols], s[:, cols]], axis=1)
            o_ref[0, j * q:(j + 1) * q, cols] = jnp.dot(
                lhs, d_ref[...], preferred_element_type=F32).astype(o_ref.dtype)


def _fourier(pf, m_seq, d_chan):
    cn = FOURIER_CN
    return pl.pallas_call(
        _fourier_kernel,
        grid=(BATCH, FOURIER_WIDTH // cn),
        in_specs=[pl.BlockSpec((1, FOURIER_RADIX, QUARTER_SEQ, cn), lambda b, j: (b, 0, 0, j)),
                  _resident((FOURIER_RADIX, 2 * QUARTER_SEQ, QUARTER_SEQ)),
                  _resident((2 * FOURIER_GROUP_DIM, FOURIER_GROUP_DIM))],
        out_specs=pl.BlockSpec((1, SEQ, cn), lambda b, j: (b, 0, j)),
        out_shape=jax.ShapeDtypeStruct((BATCH, SEQ, FOURIER_WIDTH), BF16),
        compiler_params=_params("parallel", "parallel"),
        name="fourier_mix",
    )(pf, m_seq, d_chan)


def _dft_matrices():
    q = QUARTER_SEQ
    split = 32

    def table(mult, width):
        k = lax.broadcasted_iota(jnp.int32, (q, width), 0)
        j = lax.broadcasted_iota(jnp.int32, (q, width), 1)
        ang = ((k * j * mult) % q).astype(F32) * (2.0 * math.pi / q)
        return jnp.cos(ang), jnp.sin(ang)

    (ca, sa), (cb, sb) = table(split, q // split), table(1, split)
    cos_0 = (ca[:, :, None] * cb[:, None, :] - sa[:, :, None] * sb[:, None, :]).reshape(q, q)
    sin_0 = (sa[:, :, None] * cb[:, None, :] + ca[:, :, None] * sb[:, None, :]).reshape(q, q)
    mats = []
    for r in range(FOURIER_RADIX):
        phi = lax.broadcasted_iota(jnp.int32, (q, 1), 0).astype(F32) * (2.0 * math.pi * r / SEQ)
        cos_r = cos_0 * jnp.cos(phi) - sin_0 * jnp.sin(phi)
        sin_r = sin_0 * jnp.cos(phi) + cos_0 * jnp.sin(phi)
        mats.append(jnp.concatenate([cos_r, sin_r], axis=0))
    return jnp.stack(mats).astype(BF16)


def _channel_dft():
    gd = FOURIER_GROUP_DIM
    idx = np.arange(gd)
    ang = 2.0 * np.pi * ((idx[:, None] * idx[None, :]) % gd) / gd
    scale = 1.0 / math.sqrt(SEQ * gd)
    return jnp.asarray(np.concatenate([np.cos(ang), -np.sin(ang)]) * scale, F32).astype(BF16)


def _dil_block(q, kw, vw_aug, bias, head0_lanes):
    zero = jnp.zeros_like(q)
    pv, lse = [], []
    for hh in range(2):
        q_h = jnp.where(head0_lanes, q, zero) if hh == 0 else jnp.where(head0_lanes, zero, q)
        logits = lax.dot_general(q_h, kw, (((1,), (1,)), ((), ())),
                                 preferred_element_type=F32) + bias[hh]
        mx = jnp.max(logits, axis=-1, keepdims=True)
        e = jnp.exp2(logits - mx).astype(BF16)
        r = jnp.dot(e, vw_aug, preferred_element_type=F32)
        s = r[:, LANES:]
        pv.append(r[:, :LANES] / s)
        lse.append(mx + jnp.log2(s))
    return jnp.where(head0_lanes, pv[0], pv[1]), jnp.where(head0_lanes, lse[0], lse[1])


def _dil_kernel(g0_ref, g1_ref, g2_ref, bias_ref, out_ref,
                kpad_ref, vpad_ref, o_ref, lse_ref):
    pad = DIL_RADIUS
    head0_lanes = lax.broadcasted_iota(jnp.int32, (DIL_QB, LANES), 1) < DIL_HEAD_DIM
    zeros_k = jnp.zeros((pad, LANES), BF16)
    zeros_v = jnp.zeros((pad, 2 * LANES), BF16)
    q_sl, k_sl, v_sl = (slice(i * LANES, (i + 1) * LANES) for i in range(3))

    vpad_ref[pl.ds(pad, SEQ), LANES:] = jnp.ones((SEQ, LANES), BF16)

    def fill_padded(src, sub_len):
        kpad_ref[:pad, :] = zeros_k
        kpad_ref[pl.ds(pad, sub_len), :] = src[:, k_sl]
        kpad_ref[pl.ds(pad + sub_len, pad), :] = zeros_k
        vpad_ref[:pad, :] = zeros_v
        vpad_ref[pl.ds(pad, sub_len), :LANES] = src[:, v_sl]
        vpad_ref[pl.ds(pad + sub_len, pad), :] = zeros_v

    def store(group, rows, o, lse):
        o_ref[group, rows, :] = o
        lse_ref[group, rows, :] = lse

    def padded_block(group, q_src, blk, edge, dil, residue):
        r0 = pl.multiple_of(blk * DIL_QB, DIL_QB)
        o, lse = _dil_block(q_src[pl.ds(r0, DIL_QB), q_sl], kpad_ref[pl.ds(r0, DIL_KW), :],
                            vpad_ref[pl.ds(r0, DIL_KW), :],
                            [bias_ref[group, edge, 0], bias_ref[group, edge, 1]],
                            head0_lanes)
        rows = pl.ds(residue + r0 * dil, DIL_QB, stride=dil) if dil > 1 else pl.ds(r0, DIL_QB)
        store(group, rows, o, lse)

    fill_padded(g0_ref.at[0], SEQ)
    n_blocks0 = SEQ // DIL_QB

    def g0_body(i, carry):
        for n in range(DIL_UNROLL):
            blk = i * DIL_UNROLL + n
            edge = jnp.where(blk == 0, EDGE_FIRST,
                             jnp.where(blk == n_blocks0 - 1, EDGE_LAST, EDGE_NONE))
            padded_block(0, g0_ref.at[0], blk, edge, 1, 0)
        return carry

    lax.fori_loop(0, n_blocks0 // DIL_UNROLL, g0_body, 0)

    dil1 = DIL_PAIRS[1][1]
    len1 = SEQ // dil1
    n_blocks1 = len1 // DIL_QB
    per_iter1 = DIL_UNROLL // n_blocks1
    slot1 = len1 + 2 * pad
    for n in range(per_iter1):
        for ref, zeros in ((kpad_ref, zeros_k), (vpad_ref, zeros_v)):
            ref[n * slot1:n * slot1 + pad, :] = zeros
            ref[n * slot1 + pad + len1:(n + 1) * slot1, :] = zeros
        vpad_ref[n * slot1 + pad:n * slot1 + pad + len1, LANES:] = jnp.ones((len1, LANES), BF16)

    def g1_body(i, carry):
        for n in range(per_iter1):
            r = i * per_iter1 + n
            src = g1_ref.at[0, r]
            base = n * slot1
            kpad_ref[base + pad:base + pad + len1, :] = src[:, k_sl]
            vpad_ref[base + pad:base + pad + len1, :LANES] = src[:, v_sl]
            for blk in range(n_blocks1):
                edge = EDGE_FIRST if blk == 0 else EDGE_LAST if blk == n_blocks1 - 1 else EDGE_NONE
                r0 = blk * DIL_QB
                o, lse = _dil_block(src[r0:r0 + DIL_QB, q_sl],
                                    kpad_ref[base + r0:base + r0 + DIL_KW, :],
                                    vpad_ref[base + r0:base + r0 + DIL_KW, :],
                                    [bias_ref[1, edge, 0], bias_ref[1, edge, 1]], head0_lanes)
                store(1, pl.ds(r + r0 * dil1, DIL_QB, stride=dil1), o, lse)
        return carry

    lax.fori_loop(0, dil1 // per_iter1, g1_body, 0)

    dil2 = DIL_PAIRS[2][1]
    assert SEQ // dil2 == DIL_QB
    for n in range(DIL_UNROLL):
        base = n * DIL_KW
        for ref, zeros in ((kpad_ref, zeros_k), (vpad_ref, zeros_v)):
            ref[base:base + pad, :] = zeros
            ref[base + pad + DIL_QB:base + DIL_KW, :] = zeros
        vpad_ref[base + pad:base + pad + DIL_QB, LANES:] = jnp.ones((DIL_QB, LANES), BF16)

    def g2_body(i, carry):
        for n in range(DIL_UNROLL):
            r = i * DIL_UNROLL + n
            src = g2_ref.at[0, r]
            base = n * DIL_KW
            kpad_ref[base + pad:base + pad + DIL_QB, :] = src[:, k_sl]
            vpad_ref[base + pad:base + pad + DIL_QB, :LANES] = src[:, v_sl]
            o, lse = _dil_block(src[:, q_sl], kpad_ref[base:base + DIL_KW, :],
                                vpad_ref[base:base + DIL_KW, :],
                                [bias_ref[2, EDGE_BOTH, 0], bias_ref[2, EDGE_BOTH, 1]],
                                head0_lanes)
            store(2, pl.ds(r, DIL_QB, stride=dil2), o, lse)
        return carry

    lax.fori_loop(0, dil2 // DIL_UNROLL, g2_body, 0)

    chunk = 256
    for c in range(SEQ // chunk):
        rows = slice(c * chunk, (c + 1) * chunk)
        l0, l1, l2 = lse_ref[0, rows, :], lse_ref[1, rows, :], lse_ref[2, rows, :]
        mx = jnp.maximum(jnp.maximum(l0, l1), l2)
        e0, e1, e2 = jnp.exp2(l0 - mx), jnp.exp2(l1 - mx), jnp.exp2(l2 - mx)
        mixed = (e0 * o_ref[0, rows, :] + e1 * o_ref[1, rows, :] + e2 * o_ref[2, rows, :]) / (e0 + e1 + e2)
        out_ref[0, rows, :] = mixed.astype(out_ref.dtype)


def _dilated_mixture(pb0, pb1, pb2, biases):
    d1, d2 = DIL_PAIRS[1][1], DIL_PAIRS[2][1]
    bias_spec = _resident((DIL_GROUPS, 4, 2, DIL_QB, DIL_KW))
    max_len = SEQ + 2 * DIL_RADIUS
    return pl.pallas_call(
        _dil_kernel,
        grid=(BATCH,),
        in_specs=[pl.BlockSpec((1, SEQ, DIL_QKV), lambda b: (b, 0, 0)),
                  pl.BlockSpec((1, d1, SEQ // d1, DIL_QKV), lambda b: (b, 0, 0, 0)),
                  pl.BlockSpec((1, d2, SEQ // d2, DIL_QKV), lambda b: (b, 0, 0, 0)),
                  bias_spec],
        out_specs=pl.BlockSpec((1, SEQ, LANES), lambda b: (b, 0, 0)),
        out_shape=jax.ShapeDtypeStruct((BATCH, SEQ, DIL_OUT_WIDTH), BF16),
        scratch_shapes=[pltpu.VMEM((max_len, LANES), BF16), pltpu.VMEM((max_len, 2 * LANES), BF16),
                        pltpu.VMEM((DIL_GROUPS, SEQ, LANES), F32),
                        pltpu.VMEM((DIL_GROUPS, SEQ, LANES), F32)],
        compiler_params=_params("parallel"),
        name="dilated_attn",
    )(pb0, pb1, pb2, biases)


def _diff_kernel(q_ref, k_ref, v_ref, bias_ref, lq1_ref, lk1_ref, lq2_ref, lk2_ref, sg_ref,
                 o_ref, vaug_ref, logit_ref, rmax_ref, *, lam_init):
    nq = SEQ // DIFF_TQ
    n_tiles = SEQ // DIFF_KT
    vaug_ref[:, :LANES] = v_ref[0]
    vaug_ref[:, LANES:] = jnp.ones((SEQ, LANES), BF16)
    lam = (jnp.exp(jnp.sum(lq1_ref[...] * lk1_ref[...], axis=-1, keepdims=True))
           - jnp.exp(jnp.sum(lq2_ref[...] * lk2_ref[...], axis=-1, keepdims=True))
           + lam_init)

    def key_tile(t):
        return slice(t * DIFF_KT, (t + 1) * DIFF_KT)

    def rows(j):
        return pl.ds(pl.multiple_of(j * DIFF_TQ, DIFF_TQ), DIFF_TQ)

    def pass1(j):
        slot = j % 2
        off = (nq - 1 - j) * DIFF_TQ
        q = q_ref[0, rows(j), :]
        first_half = lax.broadcasted_iota(jnp.int32, q.shape, 1) < DIFF_HEAD_DIM
        zero = jnp.zeros_like(q)
        for m in range(2):
            q_m = jnp.where(first_half, q, zero) if m == 0 else jnp.where(first_half, zero, q)
            lane_max = None
            for t in range(n_tiles):
                bias = bias_ref[0, :, pl.ds(pl.multiple_of(off + t * DIFF_KT, LANES), DIFF_KT)]
                logits = lax.dot_general(q_m, k_ref[0, key_tile(t), :], (((1,), (1,)), ((), ())),
                                         preferred_element_type=F32) + bias
                logit_ref[slot, m, :, key_tile(t)] = logits
                for c in range(DIFF_KT // LANES):
                    part = logits[:, c * LANES:(c + 1) * LANES]
                    lane_max = part if lane_max is None else jnp.maximum(lane_max, part)
            rmax_ref[slot, m] = jnp.broadcast_to(jnp.max(lane_max, axis=-1, keepdims=True),
                                                 (DIFF_TQ, LANES))

    def pass2(j):
        slot = j % 2
        outs = []
        for m in range(2):
            row_max = rmax_ref[slot, m]
            pv = jnp.zeros((DIFF_TQ, 2 * LANES), F32)
            for t in range(n_tiles):
                old = logit_ref[slot, m, :, key_tile(t)]
                e = jnp.concatenate([jnp.exp2(old[:, c * LANES:(c + 1) * LANES] - row_max)
                                     for c in range(DIFF_KT // LANES)], axis=1).astype(BF16)
                pv = pv + jnp.dot(e, vaug_ref[key_tile(t), :], preferred_element_type=F32)
            outs.append(pv[:, :LANES] / pv[:, LANES:])
        o = outs[0] - lam * outs[1]
        o = _rms(o, sg_ref[...], SUBLN_EPS) * (1.0 - lam_init)
        o_ref[0, rows(j), :] = o.astype(o_ref.dtype)

    def steady(j, carry):
        pass2(j - 1)
        pass1(j)
        return carry

    pass1(0)
    lax.fori_loop(1, nq, steady, 0)
    pass2(nq - 1)


def _diff_attention(pc, strip, lq1, lk1, lq2, lk2, subln_g, lam_init):
    pcv = pc.reshape(BATCH, SEQ, C_WIDTH)
    vec = _resident((1, DIFF_HEAD_DIM))

    def lane_block(part):
        return pl.BlockSpec((1, SEQ, LANES), lambda h, b: (b, 0, part * DIFF_HEADS + h))

    return pl.pallas_call(
        functools.partial(_diff_kernel, lam_init=lam_init),
        grid=(DIFF_HEADS, BATCH),
        in_specs=[lane_block(0), lane_block(1), lane_block(2),
                  pl.BlockSpec((1, DIFF_TQ, DIFF_STRIP), lambda h, b: (h, 0, 0)),
                  vec, vec, vec, vec, _resident((1, DIFF_V_DIM))],
        out_specs=lane_block(0),
        out_shape=jax.ShapeDtypeStruct((BATCH, SEQ, DIFF_V_WIDTH), BF16),
        scratch_shapes=[pltpu.VMEM((SEQ, 2 * LANES), BF16),
                        pltpu.VMEM((2, 2, DIFF_TQ, SEQ), F32),
                        pltpu.VMEM((2, 2, DIFF_TQ, LANES), F32)],
        compiler_params=_params("parallel", "parallel"),
        name="diff_attn",
    )(pcv, pcv, pcv, strip, lq1.reshape(1, -1), lk1.reshape(1, -1), lq2.reshape(1, -1),
      lk2.reshape(1, -1), subln_g.reshape(1, -1)).reshape(TOKENS, DIFF_V_WIDTH)


def _merge_ffn_kernel(x_ref, f_ref, b_ref, c_ref, gate_ref, wa_ref, wb_ref, wc_ref, wo_ref,
                      g_ref, wg_ref, wu_ref, wd_ref, *rest, final_norm):
    rest = list(rest)
    gf_ref = rest.pop(0) if final_norm else None
    out_ref, a_ref = rest
    y_a = jnp.dot(f_ref[...], wa_ref[...], preferred_element_type=F32)
    y_b = jnp.dot(b_ref[...], wb_ref[...], preferred_element_type=F32)
    y_c = jnp.dot(c_ref[...], wc_ref[...], preferred_element_type=F32)
    merged = (gate_ref[:, :D_MODEL].astype(F32) * y_a
              + gate_ref[:, D_MODEL:2 * D_MODEL].astype(F32) * y_b
              + gate_ref[:, 2 * D_MODEL:].astype(F32) * y_c)
    x = x_ref[...] + jnp.dot(merged.astype(BF16), wo_ref[...], preferred_element_type=F32)
    y = _swiglu_residual(x, g_ref, wg_ref, wu_ref, wd_ref, a_ref)
    if final_norm:
        y = _rms(y, gf_ref[...], EPS)
    out_ref[...] = y


def _merge_ffn(x, f, b, c, gates, wa, wb, wc, wo, g, wg, wu, wd, g_final=None):
    def tile(w):
        return pl.BlockSpec((MERGE_TM, w), lambda i: (i, 0))

    final_norm = g_final is not None
    weights = [wa, wb, wc, wo]
    gain = _resident((1, D_MODEL))
    in_specs = ([tile(D_MODEL), tile(FOURIER_WIDTH), tile(DIL_OUT_WIDTH), tile(DIFF_V_WIDTH),
                 tile(GATE_WIDTH)] + [_resident(w.shape) for w in weights]
                + [gain, _resident(wg.shape), _resident(wu.shape), _resident(wd.shape)])
    args = [x, f, b, c, gates, *weights, g.reshape(1, D_MODEL), wg, wu, wd]
    if final_norm:
        in_specs.append(gain)
        args.append(g_final.reshape(1, D_MODEL))
    return pl.pallas_call(
        functools.partial(_merge_ffn_kernel, final_norm=final_norm),
        grid=(TOKENS // MERGE_TM,),
        in_specs=in_specs,
        out_specs=tile(D_MODEL),
        out_shape=jax.ShapeDtypeStruct((TOKENS, D_MODEL), F32),
        scratch_shapes=[pltpu.VMEM((MERGE_TM, D_FF), BF16)],
        compiler_params=_params("parallel"),
        name="merge_ffn_final" if final_norm else "merge_ffn",
    )(*args)


def kernel(x, g_ffn1, w_ffn1_gate, w_ffn1_up, w_ffn1_down, g_mix, w_in, w_gate, b_gate,
           w_br_a, w_br_b, w_br_c, w_out, lam_q1, lam_k1, lam_q2, lam_k2, subln_g,
           rel_bias, g_ffn2, w_ffn2_gate, w_ffn2_up, w_ffn2_down, g_final):
    assert x.shape == (BATCH, SEQ, D_MODEL) and x.dtype == F32
    assert w_in.shape == (DEPTH, D_MODEL, IN_WIDTH)
    assert rel_bias.shape == (NUM_BUCKETS, N_BIAS_HEADS)
    bf = lambda w: w.astype(BF16)

    dil_bias = _dil_bias_tiles(rel_bias)
    diff_strip = _diff_bias_strip(rel_bias)
    m_seq = _dft_matrices()
    d_chan = _channel_dft()
    in_plan = _in_weight_plan()

    h = x.reshape(TOKENS, D_MODEL)
    ffn1_w = [bf(w_ffn1_gate[0]), bf(w_ffn1_up[0]), bf(w_ffn1_down[0])]
    for l in range(DEPTH):
        last = l == DEPTH - 1
        h, wa, wc_br, wo, gate_w, in_w = _ffn(
            h, g_ffn1[l], *ffn1_w,
            cast_jobs=[(w_br_a, l, None), (w_br_c, l, None), (w_out, l, None),
                       (w_gate, l, None), (w_in, l, in_plan)])

        next_ffn1 = [] if last else [(w_ffn1_gate, l + 1, None), (w_ffn1_up, l + 1, None),
                                     (w_ffn1_down, l + 1, None)]
        pf, pb0, pb1, pb2, pc, gates, *ffn_w = _proj(
            h, g_mix[l], in_w, gate_w, b_gate[l],
            cast_jobs=[(w_ffn2_gate, l, None), (w_ffn2_up, l, None), (w_ffn2_down, l, None)]
            + next_ffn1)
        ffn2_w, ffn1_w = ffn_w[:3], ffn_w[3:]

        f = _fourier(pf, m_seq, d_chan).reshape(TOKENS, FOURIER_WIDTH)
        mixed = _dilated_mixture(pb0.reshape(BATCH, SEQ, DIL_QKV), pb1, pb2, dil_bias)
        lam_init = 0.8 - 0.6 * math.exp(-0.3 * l)
        c = _diff_attention(pc, diff_strip, lam_q1[l], lam_k1[l], lam_q2[l], lam_k2[l],
                            subln_g[l], lam_init)

        h = _merge_ffn(h, f, mixed.reshape(TOKENS, DIL_OUT_WIDTH), c, gates, wa, bf(w_br_b[l]),
                       wc_br, wo, g_ffn2[l], *ffn2_w, g_final if last else None)
    return h.reshape(BATCH, SEQ, D_MODEL)
```

```python
import functools
import math

import numpy as np
import jax
import jax.numpy as jnp
from jax import lax
from jax.experimental import pallas as pl
from jax.experimental.pallas import tpu as pltpu

F32 = jnp.float32
BF16 = jnp.bfloat16

D_MODEL = 1024
BATCH = 8
SEQ = 2048
DEPTH = 2
TOKENS = BATCH * SEQ
D_FF = 2816
EPS = 1e-6
FOURIER_GROUPS = 8
FOURIER_GROUP_DIM = 128
FOURIER_WIDTH = FOURIER_GROUPS * FOURIER_GROUP_DIM
DIL_PAIRS = ((128, 1), (512, 4), (2048, 16))
DIL_GROUPS = len(DIL_PAIRS)
DIL_HEAD_DIM = 64
DIL_HEADS = 2 * DIL_GROUPS
DIL_WIDTH = DIL_HEADS * DIL_HEAD_DIM
DIL_OUT_WIDTH = 2 * DIL_HEAD_DIM
DIL_RADIUS = 64
DIFF_HEADS = 4
DIFF_HEAD_DIM = 64
DIFF_QK_WIDTH = DIFF_HEADS * 2 * DIFF_HEAD_DIM
DIFF_V_DIM = 2 * DIFF_HEAD_DIM
DIFF_V_WIDTH = DIFF_HEADS * DIFF_V_DIM
SUBLN_EPS = 1e-5
NUM_BUCKETS = 32
MAX_DISTANCE = 1024
N_BIAS_HEADS = DIL_HEADS + DIFF_HEADS
NEG_INF = -1e30
QK_SCALE = DIL_HEAD_DIM ** -0.5
LOG2_E = math.log2(math.e)
B_WIDTH = 3 * DIL_WIDTH
C_WIDTH = 2 * DIFF_QK_WIDTH + DIFF_V_WIDTH
IN_WIDTH = FOURIER_WIDTH + B_WIDTH + C_WIDTH
GATE_WIDTH = 3 * D_MODEL

LANES = 128
BF16_SUBLANES = 16
VMEM_BYTES_V7X = 64 * 1024 * 1024
VMEM_LIMIT = VMEM_BYTES_V7X - 8 * 1024 * 1024

FFN_TM = 1024
FF_CHUNK = 256
PROJ_TM = 512
PROJ_CHUNK = 512
NORM_SPLIT = 2
MERGE_TM = 512
FOURIER_RADIX = 4
QUARTER_SEQ = SEQ // FOURIER_RADIX
FOURIER_CN = 1024
DIL_QKV = 3 * LANES
DIL_QB = 128
DIL_KW = DIL_QB + 2 * DIL_RADIUS
DIL_UNROLL = 8
DIFF_TQ = 512
DIFF_STRIP = 2 * SEQ - DIFF_TQ
DIFF_RING = 2 * SEQ
DIFF_KT = 512


def _params(*semantics):
    return pltpu.CompilerParams(dimension_semantics=semantics, vmem_limit_bytes=VMEM_LIMIT)


def _resident(shape):
    nd = len(shape)
    return pl.BlockSpec(shape, lambda *_: (0,) * nd, pipeline_mode=pl.Buffered(1))


def _rms(x, g, eps):
    return x * lax.rsqrt(jnp.mean(x * x, axis=-1, keepdims=True) + eps) * g


def _cast_specs(jobs, steps, step_of=lambda i: i):
    in_specs, out_specs, out_shapes = [], [], []
    for src, layer, plan in jobs:
        _, rows, cols = src.shape
        out_cols = cols if plan is None else sum(width for _, width, _ in plan)
        share = 1
        while rows % (steps // share * BF16_SUBLANES):
            share *= 2
        blk = rows // (steps // share)
        in_specs.append(pl.BlockSpec(
            (None, blk, cols),
            lambda *idx, layer=layer, share=share: (layer, step_of(*idx) // share, 0)))
        out_specs.append(pl.BlockSpec(
            (blk, out_cols), lambda *idx, share=share: (step_of(*idx) // share, 0)))
        out_shapes.append(jax.ShapeDtypeStruct((rows, out_cols), BF16))
    return in_specs, out_specs, out_shapes


def _cast_blocks(src_refs, dst_refs, plans):
    for src, dst, plan in zip(src_refs, dst_refs, plans):
        if plan is None:
            dst[...] = src[...].astype(BF16)
            continue
        col = 0
        for src_col, width, scale in plan:
            part = src[:, src_col:src_col + width]
            dst[:, col:col + width] = (part if scale == 1.0 else part * scale).astype(BF16)
            col += width


def _cast_args(jobs):
    return [src for src, _, _ in jobs], tuple(plan for _, _, plan in jobs)


def _bucket_breakpoints():
    rel = np.arange(-(SEQ - 1), SEQ)
    half = NUM_BUCKETS // 2
    max_exact = half // 2
    n = np.abs(rel)
    nf = np.maximum(n, 1).astype(np.float64)
    large = max_exact + (np.log(nf / max_exact) / math.log(MAX_DISTANCE / max_exact)
                         * (half - max_exact)).astype(np.int32)
    large = np.minimum(large, half - 1)
    b = np.where(rel > 0, half, 0) + np.where(n < max_exact, n, large)
    pts = [(int(rel[i]), int(b[i])) for i in range(1, len(rel)) if b[i] != b[i - 1]]
    return int(b[0]), pts


def _bias_values(tbl_ref, head, rel):
    first, pts = _bucket_breakpoints()
    val = jnp.full(rel.shape, tbl_ref[first, head], F32)
    for thr, bucket in pts:
        val = jnp.where(rel >= thr, tbl_ref[bucket, head], val)
    return val * LOG2_E


def _diff_bias_kernel(tbl_ref, o_ref):
    rows, cols = o_ref.shape[1], o_ref.shape[2]
    ring = DIFF_RING
    u = lax.broadcasted_iota(jnp.int32, (8, ring), 1)
    offset = jnp.where(u < cols, u, u - ring)
    line = _bias_values(tbl_ref, DIL_HEADS + pl.program_id(0), offset - (SEQ - rows))
    tiled = jnp.broadcast_to(line[:1], (rows, ring))
    o_ref[0] = pltpu.roll(tiled, 0, 1, stride=1, stride_axis=0)[:, :cols]


def _diff_bias_strip(rel_bias):
    return pl.pallas_call(
        _diff_bias_kernel,
        grid=(DIFF_HEADS,),
        in_specs=[pl.BlockSpec(memory_space=pltpu.SMEM)],
        out_specs=pl.BlockSpec((1, DIFF_TQ, DIFF_STRIP), lambda h: (h, 0, 0)),
        out_shape=jax.ShapeDtypeStruct((DIFF_HEADS, DIFF_TQ, DIFF_STRIP), F32),
        compiler_params=_params("arbitrary"),
        name="diff_bias_strip",
    )(rel_bias)


EDGE_FIRST, EDGE_NONE, EDGE_LAST, EDGE_BOTH = range(4)


def _dil_bias_kernel(tbl_ref, o_ref):
    head = pl.program_id(0)
    group = head // 2
    dil = DIL_PAIRS[0][1]
    for g in range(1, DIL_GROUPS):
        dil = jnp.where(group == g, DIL_PAIRS[g][1], dil)
    rows, cols = o_ref.shape[3], o_ref.shape[4]
    col = lax.broadcasted_iota(jnp.int32, (rows, cols), 1)
    rel_t = col - lax.broadcasted_iota(jnp.int32, (rows, cols), 0) - DIL_RADIUS
    val = _bias_values(tbl_ref, head, rel_t * dil)
    val = jnp.where(jnp.abs(rel_t) <= DIL_RADIUS, val, NEG_INF)
    lo = jnp.where(col >= DIL_RADIUS, val, NEG_INF)
    o_ref[0, EDGE_FIRST, 0] = lo
    o_ref[0, EDGE_NONE, 0] = val
    o_ref[0, EDGE_LAST, 0] = jnp.where(col < cols - DIL_RADIUS, val, NEG_INF)
    o_ref[0, EDGE_BOTH, 0] = jnp.where(col < cols - DIL_RADIUS, lo, NEG_INF)


def _dil_bias_tiles(rel_bias):
    return pl.pallas_call(
        _dil_bias_kernel,
        grid=(DIL_HEADS,),
        in_specs=[pl.BlockSpec(memory_space=pltpu.SMEM)],
        out_specs=pl.BlockSpec((1, 4, 1, DIL_QB, DIL_KW), lambda h: (h // 2, 0, h % 2, 0, 0)),
        out_shape=jax.ShapeDtypeStruct((DIL_GROUPS, 4, 2, DIL_QB, DIL_KW), F32),
        compiler_params=_params("arbitrary"),
        name="dil_bias_tiles",
    )(rel_bias)


def _swiglu_residual(x, g_ref, wg_ref, wu_ref, wd_ref, a_ref):
    def hidden(h, sl):
        gate = jnp.dot(h, wg_ref[:, sl], preferred_element_type=F32)
        up = jnp.dot(h, wu_ref[:, sl], preferred_element_type=F32)
        return (gate * jax.nn.sigmoid(gate) * up).astype(BF16)

    part = x.shape[0] // NORM_SPLIT
    h_parts = [_rms(x[i * part:(i + 1) * part], g_ref[...], EPS).astype(BF16)
               for i in range(NORM_SPLIT)]
    h = jnp.concatenate(h_parts, axis=0)
    for i, h_part in enumerate(h_parts):
        a_ref[i * part:(i + 1) * part, :FF_CHUNK] = hidden(h_part, slice(0, FF_CHUNK))
    for c in range(1, D_FF // FF_CHUNK):
        sl = slice(c * FF_CHUNK, (c + 1) * FF_CHUNK)
        a_ref[:, sl] = hidden(h, sl)
    return x + 0.5 * jnp.dot(a_ref[...], wd_ref[...], preferred_element_type=F32)


def _ffn_kernel(x_ref, g_ref, wg_ref, wu_ref, wd_ref, *rest, cast_plans):
    n_cast = len(cast_plans)
    cast_src, o_ref, cast_dst, a_ref = (rest[:n_cast], rest[n_cast], rest[n_cast + 1:-1], rest[-1])
    _cast_blocks(cast_src, cast_dst, cast_plans)
    o_ref[...] = _swiglu_residual(x_ref[...], g_ref, wg_ref, wu_ref, wd_ref, a_ref)


def _ffn(x, g, wg, wu, wd, cast_jobs=()):
    steps = TOKENS // FFN_TM
    tile = pl.BlockSpec((FFN_TM, D_MODEL), lambda i: (i, 0))
    cast_in, cast_out, cast_shapes = _cast_specs(cast_jobs, steps)
    cast_srcs, cast_plans = _cast_args(cast_jobs)
    return pl.pallas_call(
        functools.partial(_ffn_kernel, cast_plans=cast_plans),
        grid=(steps,),
        in_specs=[tile, _resident((1, D_MODEL)), _resident((D_MODEL, D_FF)),
                  _resident((D_MODEL, D_FF)), _resident((D_FF, D_MODEL))] + cast_in,
        out_specs=[tile] + cast_out,
        out_shape=[jax.ShapeDtypeStruct((TOKENS, D_MODEL), F32)] + cast_shapes,
        scratch_shapes=[pltpu.VMEM((FFN_TM, D_FF), BF16)],
        compiler_params=_params("arbitrary"),
        name="ffn",
    )(x, g.reshape(1, D_MODEL), wg, wu, wd, *cast_srcs)


def _proj_kernel(x_ref, g_ref, w_ref, wg_ref, bg_ref, *rest, cast_plans):
    n_cast = len(cast_plans)
    cast_src, rest = rest[:n_cast], rest[n_cast:]
    pf_ref, pb0_ref, pb1_ref, pb2_ref, pc_ref, gate_ref = rest[:6]
    cast_dst, y_ref = rest[6:-1], rest[-1]
    _cast_blocks(cast_src, cast_dst, cast_plans)
    rows = x_ref.shape[0]
    half = rows // NORM_SPLIT
    u_halves = [_rms(x_ref[i * half:(i + 1) * half, :], g_ref[...], EPS).astype(BF16)
                for i in range(NORM_SPLIT)]
    u = jnp.concatenate(u_halves, axis=0)

    def chunks(width):
        return [slice(c0, min(c0 + PROJ_CHUNK, width)) for c0 in range(0, width, PROJ_CHUNK)]

    def project(w, col0, sl, lhs=None):
        return jnp.dot(u if lhs is None else lhs, w[:, col0 + sl.start:col0 + sl.stop],
                       preferred_element_type=F32)

    def by_residue(y, o_ref, dil, col0):
        for j in range(y.shape[1] // LANES):
            y_ref[j] = y[:, j * LANES:(j + 1) * LANES]
            cols = slice(col0 + j * LANES, col0 + (j + 1) * LANES)
            for r in range(dil):
                o_ref[0, r, :, cols] = y_ref[j, pl.ds(r, rows // dil, stride=dil), :].astype(o_ref.dtype)

    gate_chunks = chunks(GATE_WIDTH)
    for i, u_half in enumerate(u_halves):
        sl = gate_chunks[0]
        gate_ref[i * half:(i + 1) * half, sl] = jax.nn.sigmoid(
            project(wg_ref, 0, sl, u_half) + bg_ref[:, sl]).astype(gate_ref.dtype)
    for sl in gate_chunks[1:]:
        gate_ref[:, sl] = jax.nn.sigmoid(project(wg_ref, 0, sl) + bg_ref[:, sl]).astype(gate_ref.dtype)
    for sl in chunks(FOURIER_WIDTH):
        by_residue(project(w_ref, 0, sl), pf_ref, FOURIER_RADIX, sl.start)
    y = project(w_ref, FOURIER_WIDTH, slice(0, B_WIDTH))
    pb0_ref[...] = y[:, :DIL_QKV].astype(pb0_ref.dtype)
    by_residue(y[:, DIL_QKV:2 * DIL_QKV], pb1_ref, DIL_PAIRS[1][1], 0)
    by_residue(y[:, 2 * DIL_QKV:], pb2_ref, DIL_PAIRS[2][1], 0)
    for sl in chunks(C_WIDTH):
        pc_ref[:, sl] = project(w_ref, FOURIER_WIDTH + B_WIDTH, sl).astype(pc_ref.dtype)


def _in_weight_plan():
    q_scale = QK_SCALE * LOG2_E
    plan = [(0, FOURIER_WIDTH, 1.0)]
    for g in range(DIL_GROUPS):
        lo = FOURIER_WIDTH + g * LANES
        plan += [(lo, LANES, q_scale), (lo + DIL_WIDTH, LANES, 1.0), (lo + 2 * DIL_WIDTH, LANES, 1.0)]
    c3 = FOURIER_WIDTH + B_WIDTH
    plan += [(c3, DIFF_QK_WIDTH, q_scale), (c3 + DIFF_QK_WIDTH, DIFF_QK_WIDTH + DIFF_V_WIDTH, 1.0)]
    return plan


def _proj(x, g, w, wg, bg, cast_jobs=()):
    tiles_per_seq = SEQ // PROJ_TM
    steps = TOKENS // PROJ_TM
    cast_in, cast_out, cast_shapes = _cast_specs(cast_jobs, steps)
    cast_srcs, cast_plans = _cast_args(cast_jobs)

    def tile(w):
        return pl.BlockSpec((PROJ_TM, w), lambda i: (i, 0))

    def residue_tile(dil, w):
        return pl.BlockSpec((1, dil, PROJ_TM // dil, w),
                            lambda i: (i // tiles_per_seq, 0, i % tiles_per_seq, 0))

    def residue_shape(dil, w):
        return jax.ShapeDtypeStruct((BATCH, dil, SEQ // dil, w), BF16)

    d1, d2 = DIL_PAIRS[1][1], DIL_PAIRS[2][1]
    return pl.pallas_call(
        functools.partial(_proj_kernel, cast_plans=cast_plans),
        grid=(steps,),
        in_specs=[tile(D_MODEL), _resident((1, D_MODEL)), _resident((D_MODEL, IN_WIDTH)),
                  _resident((D_MODEL, GATE_WIDTH)), _resident((1, GATE_WIDTH))] + cast_in,
        out_specs=[residue_tile(FOURIER_RADIX, FOURIER_WIDTH), tile(DIL_QKV),
                   residue_tile(d1, DIL_QKV),
                   residue_tile(d2, DIL_QKV), tile(C_WIDTH), tile(GATE_WIDTH)] + cast_out,
        out_shape=[residue_shape(FOURIER_RADIX, FOURIER_WIDTH),
                   jax.ShapeDtypeStruct((TOKENS, DIL_QKV), BF16),
                   residue_shape(d1, DIL_QKV), residue_shape(d2, DIL_QKV),
                   jax.ShapeDtypeStruct((TOKENS, C_WIDTH), BF16),
                   jax.ShapeDtypeStruct((TOKENS, GATE_WIDTH), BF16)] + cast_shapes,
        scratch_shapes=[pltpu.VMEM((max(PROJ_CHUNK, DIL_QKV) // LANES, PROJ_TM, LANES), F32)],
        compiler_params=_params("arbitrary"),
        name="proj_in",
    )(x, g.reshape(1, D_MODEL), w, wg, bg.reshape(1, GATE_WIDTH), *cast_srcs)


def _fourier_kernel(a_ref, m_ref, d_ref, o_ref):
    q = QUARTER_SEQ
    tc, ts = [], []
    for r in range(FOURIER_RADIX):
        t = jnp.dot(m_ref[r], a_ref[0, r], preferred_element_type=F32)
        tc.append(t[:q])
        ts.append(t[q:])
    c02p, c02m, c13p, c13m = tc[0] + tc[2], tc[0] - tc[2], tc[1] + tc[3], tc[1] - tc[3]
    s02p, s02m, s13p, s13m = ts[0] + ts[2], ts[0] - ts[2], ts[1] + ts[3], ts[1] - ts[3]
    cos_sums = (c02p + c13p, c02m - s13m, c02p - c13p, c02m + s13m)
    sin_sums = (s02p + s13p, s02m + c13m, s02p - s13p, s02m - c13m)
    for j in range(FOURIER_RADIX):
        p, s = cos_sums[j].astype(BF16), sin_sums[j].astype(BF16)
        for g in range(o_ref.shape[2] // LANES):
            cols = slice(g * LANES, (g + 1) * LANES)
            lhs = jnp.concatenate([p[:, cols], s[:, cols]], axis=1)
            o_ref[0, j * q:(j + 1) * q, cols] = jnp.dot(
                lhs, d_ref[...], preferred_element_type=F32).astype(o_ref.dtype)


def _fourier(pf, m_seq, d_chan):
    cn = FOURIER_CN
    return pl.pallas_call(
        _fourier_kernel,
        grid=(BATCH, FOURIER_WIDTH // cn),
        in_specs=[pl.BlockSpec((1, FOURIER_RADIX, QUARTER_SEQ, cn), lambda b, j: (b, 0, 0, j)),
                  _resident((FOURIER_RADIX, 2 * QUARTER_SEQ, QUARTER_SEQ)),
                  _resident((2 * FOURIER_GROUP_DIM, FOURIER_GROUP_DIM))],
        out_specs=pl.BlockSpec((1, SEQ, cn), lambda b, j: (b, 0, j)),
        out_shape=jax.ShapeDtypeStruct((BATCH, SEQ, FOURIER_WIDTH), BF16),
        compiler_params=_params("parallel", "parallel"),
        name="fourier_mix",
    )(pf, m_seq, d_chan)


def _dft_matrices():
    q = QUARTER_SEQ
    split = 32

    def table(mult, width):
        k = lax.broadcasted_iota(jnp.int32, (q, width), 0)
        j = lax.broadcasted_iota(jnp.int32, (q, width), 1)
        ang = ((k * j * mult) % q).astype(F32) * (2.0 * math.pi / q)
        return jnp.cos(ang), jnp.sin(ang)

    (ca, sa), (cb, sb) = table(split, q // split), table(1, split)
    cos_0 = (ca[:, :, None] * cb[:, None, :] - sa[:, :, None] * sb[:, None, :]).reshape(q, q)
    sin_0 = (sa[:, :, None] * cb[:, None, :] + ca[:, :, None] * sb[:, None, :]).reshape(q, q)
    mats = []
    for r in range(FOURIER_RADIX):
        phi = lax.broadcasted_iota(jnp.int32, (q, 1), 0).astype(F32) * (2.0 * math.pi * r / SEQ)
        cos_r = cos_0 * jnp.cos(phi) - sin_0 * jnp.sin(phi)
        sin_r = sin_0 * jnp.cos(phi) + cos_0 * jnp.sin(phi)
        mats.append(jnp.concatenate([cos_r, sin_r], axis=0))
    return jnp.stack(mats).astype(BF16)


def _channel_dft():
    gd = FOURIER_GROUP_DIM
    idx = np.arange(gd)
    ang = 2.0 * np.pi * ((idx[:, None] * idx[None, :]) % gd) / gd
    scale = 1.0 / math.sqrt(SEQ * gd)
    return jnp.asarray(np.concatenate([np.cos(ang), -np.sin(ang)]) * scale, F32).astype(BF16)


def _dil_block(q, kw, vw_aug, bias, head0_lanes):
    zero = jnp.zeros_like(q)
    pv, lse = [], []
    for hh in range(2):
        q_h = jnp.where(head0_lanes, q, zero) if hh == 0 else jnp.where(head0_lanes, zero, q)
        logits = lax.dot_general(q_h, kw, (((1,), (1,)), ((), ())),
                                 preferred_element_type=F32) + bias[hh]
        mx = jnp.max(logits, axis=-1, keepdims=True)
        e = jnp.exp2(logits - mx).astype(BF16)
        r = jnp.dot(e, vw_aug, preferred_element_type=F32)
        s = r[:, LANES:]
        pv.append(r[:, :LANES] / s)
        lse.append(mx + jnp.log2(s))
    return jnp.where(head0_lanes, pv[0], pv[1]), jnp.where(head0_lanes, lse[0], lse[1])


def _dil_kernel(g0_ref, g1_ref, g2_ref, bias_ref, out_ref,
                kpad_ref, vpad_ref, o_ref, lse_ref):
    pad = DIL_RADIUS
    head0_lanes = lax.broadcasted_iota(jnp.int32, (DIL_QB, LANES), 1) < DIL_HEAD_DIM
    zeros_k = jnp.zeros((pad, LANES), BF16)
    zeros_v = jnp.zeros((pad, 2 * LANES), BF16)
    q_sl, k_sl, v_sl = (slice(i * LANES, (i + 1) * LANES) for i in range(3))

    vpad_ref[pl.ds(pad, SEQ), LANES:] = jnp.ones((SEQ, LANES), BF16)

    def fill_padded(src, sub_len):
        kpad_ref[:pad, :] = zeros_k
        kpad_ref[pl.ds(pad, sub_len), :] = src[:, k_sl]
        kpad_ref[pl.ds(pad + sub_len, pad), :] = zeros_k
        vpad_ref[:pad, :] = zeros_v
        vpad_ref[pl.ds(pad, sub_len), :LANES] = src[:, v_sl]
        vpad_ref[pl.ds(pad + sub_len, pad), :] = zeros_v

    def store(group, rows, o, lse):
        o_ref[group, rows, :] = o
        lse_ref[group, rows, :] = lse

    def padded_block(group, q_src, blk, edge, dil, residue):
        r0 = pl.multiple_of(blk * DIL_QB, DIL_QB)
        o, lse = _dil_block(q_src[pl.ds(r0, DIL_QB), q_sl], kpad_ref[pl.ds(r0, DIL_KW), :],
                            vpad_ref[pl.ds(r0, DIL_KW), :],
                            [bias_ref[group, edge, 0], bias_ref[group, edge, 1]],
                            head0_lanes)
        rows = pl.ds(residue + r0 * dil, DIL_QB, stride=dil) if dil > 1 else pl.ds(r0, DIL_QB)
        store(group, rows, o, lse)

    fill_padded(g0_ref.at[0], SEQ)
    n_blocks0 = SEQ // DIL_QB

    def g0_body(i, carry):
        for n in range(DIL_UNROLL):
            blk = i * DIL_UNROLL + n
            edge = jnp.where(blk == 0, EDGE_FIRST,
                             jnp.where(blk == n_blocks0 - 1, EDGE_LAST, EDGE_NONE))
            padded_block(0, g0_ref.at[0], blk, edge, 1, 0)
        return carry

    lax.fori_loop(0, n_blocks0 // DIL_UNROLL, g0_body, 0)

    dil1 = DIL_PAIRS[1][1]
    len1 = SEQ // dil1
    n_blocks1 = len1 // DIL_QB
    per_iter1 = DIL_UNROLL // n_blocks1
    slot1 = len1 + 2 * pad
    for n in range(per_iter1):
        for ref, zeros in ((kpad_ref, zeros_k), (vpad_ref, zeros_v)):
            ref[n * slot1:n * slot1 + pad, :] = zeros
            ref[n * slot1 + pad + len1:(n + 1) * slot1, :] = zeros
        vpad_ref[n * slot1 + pad:n * slot1 + pad + len1, LANES:] = jnp.ones((len1, LANES), BF16)

    def g1_body(i, carry):
        for n in range(per_iter1):
            r = i * per_iter1 + n
            src = g1_ref.at[0, r]
            base = n * slot1
            kpad_ref[base + pad:base + pad + len1, :] = src[:, k_sl]
            vpad_ref[base + pad:base + pad + len1, :LANES] = src[:, v_sl]
            for blk in range(n_blocks1):
                edge = EDGE_FIRST if blk == 0 else EDGE_LAST if blk == n_blocks1 - 1 else EDGE_NONE
                r0 = blk * DIL_QB
                o, lse = _dil_block(src[r0:r0 + DIL_QB, q_sl],
                                    kpad_ref[base + r0:base + r0 + DIL_KW, :],
                                    vpad_ref[base + r0:base + r0 + DIL_KW, :],
                                    [bias_ref[1, edge, 0], bias_ref[1, edge, 1]], head0_lanes)
                store(1, pl.ds(r + r0 * dil1, DIL_QB, stride=dil1), o, lse)
        return carry

    lax.fori_loop(0, dil1 // per_iter1, g1_body, 0)

    dil2 = DIL_PAIRS[2][1]
    assert SEQ // dil2 == DIL_QB
    for n in range(DIL_UNROLL):
        base = n * DIL_KW
        for ref, zeros in ((kpad_ref, zeros_k), (vpad_ref, zeros_v)):
            ref[base:base + pad, :] = zeros
            ref[base + pad + DIL_QB:base + DIL_KW, :] = zeros
        vpad_ref[base + pad:base + pad + DIL_QB, LANES:] = jnp.ones((DIL_QB, LANES), BF16)

    def g2_body(i, carry):
        for n in range(DIL_UNROLL):
            r = i * DIL_UNROLL + n
            src = g2_ref.at[0, r]
            base = n * DIL_KW
            kpad_ref[base + pad:base + pad + DIL_QB, :] = src[:, k_sl]
            vpad_ref[base + pad:base + pad + DIL_QB, :LANES] = src[:, v_sl]
            o, lse = _dil_block(src[:, q_sl], kpad_ref[base:base + DIL_KW, :],
                                vpad_ref[base:base + DIL_KW, :],
                                [bias_ref[2, EDGE_BOTH, 0], bias_ref[2, EDGE_BOTH, 1]],
                                head0_lanes)
            store(2, pl.ds(r, DIL_QB, stride=dil2), o, lse)
        return carry

    lax.fori_loop(0, dil2 // DIL_UNROLL, g2_body, 0)

    chunk = 256
    for c in range(SEQ // chunk):
        rows = slice(c * chunk, (c + 1) * chunk)
        l0, l1, l2 = lse_ref[0, rows, :], lse_ref[1, rows, :], lse_ref[2, rows, :]
        mx = jnp.maximum(jnp.maximum(l0, l1), l2)
        e0, e1, e2 = jnp.exp2(l0 - mx), jnp.exp2(l1 - mx), jnp.exp2(l2 - mx)
        mixed = (e0 * o_ref[0, rows, :] + e1 * o_ref[1, rows, :] + e2 * o_ref[2, rows, :]) / (e0 + e1 + e2)
        out_ref[0, rows, :] = mixed.astype(out_ref.dtype)


def _dilated_mixture(pb0, pb1, pb2, biases):
    d1, d2 = DIL_PAIRS[1][1], DIL_PAIRS[2][1]
    bias_spec = _resident((DIL_GROUPS, 4, 2, DIL_QB, DIL_KW))
    max_len = SEQ + 2 * DIL_RADIUS
    return pl.pallas_call(
        _dil_kernel,
        grid=(BATCH,),
        in_specs=[pl.BlockSpec((1, SEQ, DIL_QKV), lambda b: (b, 0, 0)),
                  pl.BlockSpec((1, d1, SEQ // d1, DIL_QKV), lambda b: (b, 0, 0, 0)),
                  pl.BlockSpec((1, d2, SEQ // d2, DIL_QKV), lambda b: (b, 0, 0, 0)),
                  bias_spec],
        out_specs=pl.BlockSpec((1, SEQ, LANES), lambda b: (b, 0, 0)),
        out_shape=jax.ShapeDtypeStruct((BATCH, SEQ, DIL_OUT_WIDTH), BF16),
        scratch_shapes=[pltpu.VMEM((max_len, LANES), BF16), pltpu.VMEM((max_len, 2 * LANES), BF16),
                        pltpu.VMEM((DIL_GROUPS, SEQ, LANES), F32),
                        pltpu.VMEM((DIL_GROUPS, SEQ, LANES), F32)],
        compiler_params=_params("parallel"),
        name="dilated_attn",
    )(pb0, pb1, pb2, biases)


def _diff_kernel(q_ref, k_ref, v_ref, bias_ref, lq1_ref, lk1_ref, lq2_ref, lk2_ref, sg_ref,
                 *rest, lam_init, cast_plans):
    n_cast = len(cast_plans)
    cast_src, o_ref, cast_dst = rest[:n_cast], rest[n_cast], rest[n_cast + 1:-3]
    vaug_ref, logit_ref, rmax_ref = rest[-3:]
    _cast_blocks(cast_src, cast_dst, cast_plans)
    nq = SEQ // DIFF_TQ
    n_tiles = SEQ // DIFF_KT
    vaug_ref[:, :LANES] = v_ref[0]
    vaug_ref[:, LANES:] = jnp.ones((SEQ, LANES), BF16)
    lam = (jnp.exp(jnp.sum(lq1_ref[...] * lk1_ref[...], axis=-1, keepdims=True))
           - jnp.exp(jnp.sum(lq2_ref[...] * lk2_ref[...], axis=-1, keepdims=True))
           + lam_init)

    def key_tile(t):
        return slice(t * DIFF_KT, (t + 1) * DIFF_KT)

    def rows(j):
        return pl.ds(pl.multiple_of(j * DIFF_TQ, DIFF_TQ), DIFF_TQ)

    def pass1(j):
        slot = j % 2
        off = (nq - 1 - j) * DIFF_TQ
        q = q_ref[0, rows(j), :]
        first_half = lax.broadcasted_iota(jnp.int32, q.shape, 1) < DIFF_HEAD_DIM
        zero = jnp.zeros_like(q)
        for m in range(2):
            q_m = jnp.where(first_half, q, zero) if m == 0 else jnp.where(first_half, zero, q)
            lane_max = None
            for t in range(n_tiles):
                bias = bias_ref[0, :, pl.ds(pl.multiple_of(off + t * DIFF_KT, LANES), DIFF_KT)]
                logits = lax.dot_general(q_m, k_ref[0, key_tile(t), :], (((1,), (1,)), ((), ())),
                                         preferred_element_type=F32) + bias
                logit_ref[slot, m, :, key_tile(t)] = logits
                for c in range(DIFF_KT // LANES):
                    part = logits[:, c * LANES:(c + 1) * LANES]
                    lane_max = part if lane_max is None else jnp.maximum(lane_max, part)
            rmax_ref[slot, m] = jnp.broadcast_to(jnp.max(lane_max, axis=-1, keepdims=True),
                                                 (DIFF_TQ, LANES))

    def pass2(j):
        slot = j % 2
        outs = []
        for m in range(2):
            row_max = rmax_ref[slot, m]
            pv = jnp.zeros((DIFF_TQ, 2 * LANES), F32)
            for t in range(n_tiles):
                old = logit_ref[slot, m, :, key_tile(t)]
                e = jnp.concatenate([jnp.exp2(old[:, c * LANES:(c + 1) * LANES] - row_max)
                                     for c in range(DIFF_KT // LANES)], axis=1).astype(BF16)
                pv = pv + jnp.dot(e, vaug_ref[key_tile(t), :], preferred_element_type=F32)
            outs.append(pv[:, :LANES] / pv[:, LANES:])
        o = outs[0] - lam * outs[1]
        o = _rms(o, sg_ref[...], SUBLN_EPS) * (1.0 - lam_init)
        o_ref[0, rows(j), :] = o.astype(o_ref.dtype)

    def steady(j, carry):
        pass2(j - 1)
        pass1(j)
        return carry

    pass1(0)
    lax.fori_loop(1, nq, steady, 0)
    pass2(nq - 1)


def _diff_attention(pc, strip, lq1, lk1, lq2, lk2, subln_g, lam_init, cast_jobs=()):
    pcv = pc.reshape(BATCH, SEQ, C_WIDTH)
    vec = _resident((1, DIFF_HEAD_DIM))

    def lane_block(part):
        return pl.BlockSpec((1, SEQ, LANES), lambda h, b: (b, 0, part * DIFF_HEADS + h))

    cast_in, cast_out, cast_shapes = _cast_specs(cast_jobs, DIFF_HEADS * BATCH,
                                                 lambda h, b: h * BATCH + b)
    cast_srcs, cast_plans = _cast_args(cast_jobs)
    out, *casts = pl.pallas_call(
        functools.partial(_diff_kernel, lam_init=lam_init, cast_plans=cast_plans),
        grid=(DIFF_HEADS, BATCH),
        in_specs=[lane_block(0), lane_block(1), lane_block(2),
                  pl.BlockSpec((1, DIFF_TQ, DIFF_STRIP), lambda h, b: (h, 0, 0)),
                  vec, vec, vec, vec, _resident((1, DIFF_V_DIM))] + cast_in,
        out_specs=[lane_block(0)] + cast_out,
        out_shape=[jax.ShapeDtypeStruct((BATCH, SEQ, DIFF_V_WIDTH), BF16)] + cast_shapes,
        scratch_shapes=[pltpu.VMEM((SEQ, 2 * LANES), BF16),
                        pltpu.VMEM((2, 2, DIFF_TQ, SEQ), F32),
                        pltpu.VMEM((2, 2, DIFF_TQ, LANES), F32)],
        compiler_params=_params("arbitrary", "arbitrary"),
        name="diff_attn",
    )(pcv, pcv, pcv, strip, lq1.reshape(1, -1), lk1.reshape(1, -1), lq2.reshape(1, -1),
      lk2.reshape(1, -1), subln_g.reshape(1, -1), *cast_srcs)
    return [out.reshape(TOKENS, DIFF_V_WIDTH), *casts]


def _merge_ffn_kernel(x_ref, f_ref, b_ref, c_ref, gate_ref, wa_ref, wb_ref, wc_ref, wo_ref,
                      g_ref, wg_ref, wu_ref, wd_ref, *rest, final_norm):
    rest = list(rest)
    gf_ref = rest.pop(0) if final_norm else None
    out_ref, a_ref = rest
    y_a = jnp.dot(f_ref[...], wa_ref[...], preferred_element_type=F32)
    y_b = jnp.dot(b_ref[...], wb_ref[...], preferred_element_type=F32)
    y_c = jnp.dot(c_ref[...], wc_ref[...], preferred_element_type=F32)
    merged = (gate_ref[:, :D_MODEL].astype(F32) * y_a
              + gate_ref[:, D_MODEL:2 * D_MODEL].astype(F32) * y_b
              + gate_ref[:, 2 * D_MODEL:].astype(F32) * y_c)
    x = x_ref[...] + jnp.dot(merged.astype(BF16), wo_ref[...], preferred_element_type=F32)
    y = _swiglu_residual(x, g_ref, wg_ref, wu_ref, wd_ref, a_ref)
    if final_norm:
        y = _rms(y, gf_ref[...], EPS)
    out_ref[...] = y


def _merge_ffn(x, f, b, c, gates, wa, wb, wc, wo, g, wg, wu, wd, g_final=None):
    def tile(w):
        return pl.BlockSpec((MERGE_TM, w), lambda i: (i, 0))

    final_norm = g_final is not None
    weights = [wa, wb, wc, wo]
    gain = _resident((1, D_MODEL))
    in_specs = ([tile(D_MODEL), tile(FOURIER_WIDTH), tile(DIL_OUT_WIDTH), tile(DIFF_V_WIDTH),
                 tile(GATE_WIDTH)] + [_resident(w.shape) for w in weights]
                + [gain, _resident(wg.shape), _resident(wu.shape), _resident(wd.shape)])
    args = [x, f, b, c, gates, *weights, g.reshape(1, D_MODEL), wg, wu, wd]
    if final_norm:
        in_specs.append(gain)
        args.append(g_final.reshape(1, D_MODEL))
    return pl.pallas_call(
        functools.partial(_merge_ffn_kernel, final_norm=final_norm),
        grid=(TOKENS // MERGE_TM,),
        in_specs=in_specs,
        out_specs=tile(D_MODEL),
        out_shape=jax.ShapeDtypeStruct((TOKENS, D_MODEL), F32),
        scratch_shapes=[pltpu.VMEM((MERGE_TM, D_FF), BF16)],
        compiler_params=_params("parallel"),
        name="merge_ffn_final" if final_norm else "merge_ffn",
    )(*args)


def kernel(x, g_ffn1, w_ffn1_gate, w_ffn1_up, w_ffn1_down, g_mix, w_in, w_gate, b_gate,
           w_br_a, w_br_b, w_br_c, w_out, lam_q1, lam_k1, lam_q2, lam_k2, subln_g,
           rel_bias, g_ffn2, w_ffn2_gate, w_ffn2_up, w_ffn2_down, g_final):
    assert x.shape == (BATCH, SEQ, D_MODEL) and x.dtype == F32
    assert w_in.shape == (DEPTH, D_MODEL, IN_WIDTH)
    assert rel_bias.shape == (NUM_BUCKETS, N_BIAS_HEADS)
    bf = lambda w: w.astype(BF16)

    dil_bias = _dil_bias_tiles(rel_bias)
    diff_strip = _diff_bias_strip(rel_bias)
    m_seq = _dft_matrices()
    d_chan = _channel_dft()
    in_plan = _in_weight_plan()

    h = x.reshape(TOKENS, D_MODEL)
    ffn1_w = [bf(w_ffn1_gate[0]), bf(w_ffn1_up[0]), bf(w_ffn1_down[0])]
    for l in range(DEPTH):
        last = l == DEPTH - 1
        h, wa, wc_br, wo, gate_w, in_w = _ffn(
            h, g_ffn1[l], *ffn1_w,
            cast_jobs=[(w_br_a, l, None), (w_br_c, l, None), (w_out, l, None),
                       (w_gate, l, None), (w_in, l, in_plan)])

        pf, pb0, pb1, pb2, pc, gates = _proj(h, g_mix[l], in_w, gate_w, b_gate[l])

        f = _fourier(pf, m_seq, d_chan).reshape(TOKENS, FOURIER_WIDTH)
        mixed = _dilated_mixture(pb0.reshape(BATCH, SEQ, DIL_QKV), pb1, pb2, dil_bias)
        lam_init = 0.8 - 0.6 * math.exp(-0.3 * l)
        next_ffn1 = [] if last else [(w_ffn1_gate, l + 1, None), (w_ffn1_up, l + 1, None),
                                     (w_ffn1_down, l + 1, None)]
        c, *ffn_w = _diff_attention(
            pc, diff_strip, lam_q1[l], lam_k1[l], lam_q2[l], lam_k2[l], subln_g[l], lam_init,
            cast_jobs=[(w_ffn2_gate, l, None), (w_ffn2_up, l, None), (w_ffn2_down, l, None)]
            + next_ffn1)
        ffn2_w, ffn1_w = ffn_w[:3], ffn_w[3:]

        h = _merge_ffn(h, f, mixed.reshape(TOKENS, DIL_OUT_WIDTH), c, gates, wa, bf(w_br_b[l]),
                       wc_br, wo, g_ffn2[l], *ffn2_w, g_final if last else None)
    return h.reshape(BATCH, SEQ, D_MODEL)
```
